```python
import math
import jax, jax.numpy as jnp
from jax import lax
import numpy as np

D_MODEL = 1024
BATCH = 2
SEQ = 16384
DEPTH = 4

CTX_LEN = 256
GRID_W = 64
BLOCK_Q = 128
ROPE_THETA = 10000.0
NORM_EPS = 1e-6
NEG_INF = -1e30
HEAD_DIM = 64

MLA_HEADS = 8
MLA_Q_LORA = 384
MLA_KV_LORA = 256
MLA_NOPE = 64
MLA_ROPE = 32
MLA_V = 64
MLA_SCALE = (MLA_NOPE + MLA_ROPE) ** -0.5
MLA_IN_COLS = MLA_Q_LORA + MLA_KV_LORA + MLA_ROPE

DIFF_HEADS = 8
DIFF_D = 32
DIFF_V = 2 * DIFF_D
DIFF_SCALE = DIFF_D ** -0.5
EVEN_IN = MLA_IN_COLS + 2 * DIFF_HEADS * 2 * DIFF_D + DIFF_HEADS * DIFF_V
EVEN_OUT = MLA_HEADS * MLA_V + DIFF_HEADS * DIFF_V

WIN_HEADS = 8
WIN_KV_HEADS = 2
WINDOW = 128
GLB_HEADS = 8
GLB_KV_HEADS = 2
ATTN_SCALE = HEAD_DIM ** -0.5
ODD_IN = (WIN_HEADS + 2 * WIN_KV_HEADS + GLB_HEADS + 2 * GLB_KV_HEADS) * HEAD_DIM
ODD_OUT = (WIN_HEADS + GLB_HEADS) * HEAD_DIM

N_EXPERTS = 16
N_GROUPS = 4
EXPERTS_PER_GROUP = N_EXPERTS // N_GROUPS
GROUP_SCORE_TOPK = 2
TOP_K = 2
D_FF_EXPERT = 512

N_EVEN = (DEPTH + 1) // 2
N_ODD = DEPTH // 2

kernel_name = "hybrid_mla_diff_window_axial_moe_dit"


def rmsnorm(x, g):
    xf = x.astype(jnp.float32)
    y = xf * lax.rsqrt(jnp.mean(xf * xf, axis=-1, keepdims=True) + NORM_EPS)
    return (y * g.astype(jnp.float32)).astype(x.dtype)


def modulate(h, shift, scale):
    return h * (1.0 + scale) + shift


def split_heads(z, n_heads, hd):
    B, T, _ = z.shape
    return z.reshape(B, T, n_heads, hd).transpose(0, 2, 1, 3)


def merge_heads(o):
    B, H, T, hd = o.shape
    return o.transpose(0, 2, 1, 3).reshape(B, T, H * hd)


def axial_tables(rows, rot_dim):
    half = rot_dim // 2
    inv_freq = ROPE_THETA ** (-jnp.arange(0, half, 2, dtype=jnp.float32) / half)
    row = jnp.broadcast_to(jnp.arange(rows, dtype=jnp.float32)[:, None], (rows, GRID_W)).reshape(-1)
    col = jnp.broadcast_to(jnp.arange(GRID_W, dtype=jnp.float32)[None, :], (rows, GRID_W)).reshape(-1)

    def tab(pos):
        ang = pos[:, None] * inv_freq[None, :]
        ang = jnp.concatenate([ang, ang], axis=-1)
        return jnp.cos(ang), jnp.sin(ang)

    return tab(row) + tab(col)


def rotate_half(x):
    h = x.shape[-1] // 2
    return jnp.concatenate([-x[..., h:], x[..., :h]], axis=-1)


def apply_rope(x, tabs):
    cos_r, sin_r, cos_c, sin_c = tabs
    h = x.shape[-1] // 2
    xr, xc = x[..., :h], x[..., h:]
    out = jnp.concatenate([xr * cos_r + rotate_half(xr) * sin_r,
                           xc * cos_c + rotate_half(xc) * sin_c], axis=-1)
    return out.astype(x.dtype)


def softmax_with_sink(s, sink):
    if sink is None:
        return jax.nn.softmax(s, axis=-1)
    snk = sink.astype(jnp.float32).reshape((1, s.shape[1], s.shape[2]) + (1,) * (s.ndim - 3))
    m = jnp.maximum(jnp.max(s, axis=-1, keepdims=True), snk)
    e = jnp.exp(s - m)
    return e / (jnp.sum(e, axis=-1, keepdims=True) + jnp.exp(snk - m))


def dense_attend(q, k, v, scale, sink=None):
    B, Hq, T, dk = q.shape
    Hk, dv = k.shape[1], v.shape[-1]
    G = Hq // Hk
    nb = T // BLOCK_Q
    qb = q.reshape(B, Hk, G, nb, BLOCK_Q, dk).transpose(3, 0, 1, 2, 4, 5)

    def one(qi):
        s = jnp.einsum("bhgqd,bhkd->bhgqk", qi, k).astype(jnp.float32) * scale
        p = softmax_with_sink(s, sink)
        return jnp.einsum("bhgqk,bhkd->bhgqd", p.astype(v.dtype), v)

    o = lax.map(one, qb)
    return o.transpose(1, 2, 3, 0, 4, 5).reshape(B, Hq, T, dv)


def window_attend(q, k, v, k_ctx, v_ctx, sink, scale):
    B, Hq, T, d = q.shape
    Hk = k.shape[1]
    G = Hq // Hk
    nb = T // BLOCK_Q
    pad = ((0, 0), (0, 0), (BLOCK_Q, BLOCK_Q), (0, 0))
    kp, vp = jnp.pad(k, pad), jnp.pad(v, pad)
    qb = q.reshape(B, Hk, G, nb, BLOCK_Q, d).transpose(3, 0, 1, 2, 4, 5)
    band = 3 * BLOCK_Q
    offs = jnp.arange(band) - BLOCK_Q
    qoff = jnp.arange(BLOCK_Q)

    def one(args):
        i, qi = args
        start = i * BLOCK_Q
        ki = lax.dynamic_slice_in_dim(kp, start, band, axis=2)
        vi = lax.dynamic_slice_in_dim(vp, start, band, axis=2)
        kpos = start + offs
        qpos = start + qoff
        valid = ((jnp.abs(kpos[None, :] - qpos[:, None]) <= WINDOW)
                 & (kpos >= 0)[None, :] & (kpos < T)[None, :])
        s_loc = jnp.einsum("bhgqd,bhkd->bhgqk", qi, ki).astype(jnp.float32) * scale
        s_loc = jnp.where(valid, s_loc, NEG_INF)
        s_ctx = jnp.einsum("bhgqd,bhcd->bhgqc", qi, k_ctx).astype(jnp.float32) * scale
        p = softmax_with_sink(jnp.concatenate([s_loc, s_ctx], axis=-1), sink).astype(v.dtype)
        return (jnp.einsum("bhgqk,bhkd->bhgqd", p[..., :band], vi)
                + jnp.einsum("bhgqc,bhcd->bhgqd", p[..., band:], v_ctx))

    o = lax.map(one, (jnp.arange(nb), qb))
    return o.transpose(1, 2, 3, 0, 4, 5).reshape(B, Hq, T, d)


def cat_tokens(a, b):
    return jnp.concatenate([a, b], axis=2)


def mla_qkv(z, norm_q, norm_kv, w_uq, w_ukv, tabs):
    o1 = MLA_Q_LORA
    o2 = o1 + MLA_KV_LORA
    o3 = o2 + MLA_ROPE
    c_q = rmsnorm(z[..., :o1], norm_q)
    c_kv = rmsnorm(z[..., o1:o2], norm_kv)
    k_rope = z[..., o2:o3][:, None]
    q = split_heads(c_q @ w_uq, MLA_HEADS, MLA_NOPE + MLA_ROPE)
    kv = split_heads(c_kv @ w_ukv, MLA_HEADS, MLA_NOPE + MLA_V)
    q_nope, q_rope = q[..., :MLA_NOPE], q[..., MLA_NOPE:]
    k_nope, v = kv[..., :MLA_NOPE], kv[..., MLA_NOPE:]
    if tabs is not None:
        q_rope = apply_rope(q_rope, tabs)
        k_rope = apply_rope(k_rope, tabs)
    k_rope = jnp.broadcast_to(k_rope, k_nope.shape[:-1] + (MLA_ROPE,))
    return (jnp.concatenate([q_nope, q_rope], axis=-1),
            jnp.concatenate([k_nope, k_rope], axis=-1), v)


def diff_qkv(z, tabs):
    B, T, _ = z.shape
    o0 = MLA_IN_COLS
    nq = DIFF_HEADS * 2 * DIFF_D
    q = z[..., o0:o0 + nq].reshape(B, T, DIFF_HEADS, 2, DIFF_D).transpose(3, 0, 2, 1, 4)
    k = z[..., o0 + nq:o0 + 2 * nq].reshape(B, T, DIFF_HEADS, 2, DIFF_D).transpose(3, 0, 2, 1, 4)
    v = split_heads(z[..., o0 + 2 * nq:o0 + 2 * nq + DIFF_HEADS * DIFF_V], DIFF_HEADS, DIFF_V)
    if tabs is not None:
        q = apply_rope(q, tabs)
        k = apply_rope(k, tabs)
    return q[0], q[1], k[0], k[1], v


def even_mixer(h_lat, h_ctx, w_in, norm_q, norm_kv, w_uq, w_ukv, lam, subln, w_out,
               tabs_mla, tabs_diff, lam_init, with_ctx):
    z_lat, z_ctx = h_lat @ w_in, h_ctx @ w_in
    qa, ka, va = mla_qkv(z_lat, norm_q, norm_kv, w_uq, w_ukv, tabs_mla)
    qa_c, ka_c, va_c = mla_qkv(z_ctx, norm_q, norm_kv, w_uq, w_ukv, None)
    q1, q2, k1, k2, vb = diff_qkv(z_lat, tabs_diff)
    q1_c, q2_c, k1_c, k2_c, vb_c = diff_qkv(z_ctx, None)
    lam = lam.astype(jnp.float32)
    lam_full = jnp.exp(jnp.sum(lam[0] * lam[1])) - jnp.exp(jnp.sum(lam[2] * lam[3])) + lam_init

    def merge(oa, o1, o2):
        ob = rmsnorm((o1 - lam_full * o2).astype(o1.dtype), subln) * (1.0 - lam_init)
        return jnp.concatenate([merge_heads(oa), merge_heads(ob)], axis=-1) @ w_out

    vb_all = cat_tokens(vb, vb_c)
    y_lat = merge(dense_attend(qa, cat_tokens(ka, ka_c), cat_tokens(va, va_c), MLA_SCALE),
                  dense_attend(q1, cat_tokens(k1, k1_c), vb_all, DIFF_SCALE),
                  dense_attend(q2, cat_tokens(k2, k2_c), vb_all, DIFF_SCALE))
    y_ctx = None
    if with_ctx:
        y_ctx = merge(dense_attend(qa_c, ka_c, va_c, MLA_SCALE),
                      dense_attend(q1_c, k1_c, vb_c, DIFF_SCALE),
                      dense_attend(q2_c, k2_c, vb_c, DIFF_SCALE))
    return y_lat, y_ctx


def odd_mixer(h_lat, h_ctx, w_in, sink, q_norm, k_norm, w_out, tabs, with_ctx):
    sizes = (WIN_HEADS, WIN_KV_HEADS, WIN_KV_HEADS, GLB_HEADS, GLB_KV_HEADS, GLB_KV_HEADS)

    def proj(h, tb):
        z = h @ w_in
        parts, off = [], 0
        for n in sizes:
            parts.append(split_heads(z[..., off * HEAD_DIM:(off + n) * HEAD_DIM], n, HEAD_DIM))
            off += n
        qc, kc, vc, qd, kd, vd = parts
        qd, kd = rmsnorm(qd, q_norm), rmsnorm(kd, k_norm)
        if tb is not None:
            qc, kc, qd, kd = apply_rope(qc, tb), apply_rope(kc, tb), apply_rope(qd, tb), apply_rope(kd, tb)
        return qc, kc, vc, qd, kd, vd

    qc, kc, vc, qd, kd, vd = proj(h_lat, tabs)
    qc_c, kc_c, vc_c, qd_c, kd_c, vd_c = proj(h_ctx, None)

    def merge(oc, od):
        return jnp.concatenate([merge_heads(oc), merge_heads(od)], axis=-1) @ w_out

    y_lat = merge(window_attend(qc, kc, vc, kc_c, vc_c, sink, ATTN_SCALE),
                  dense_attend(qd, cat_tokens(kd, kd_c), cat_tokens(vd, vd_c), ATTN_SCALE))
    y_ctx = None
    if with_ctx:
        y_ctx = merge(dense_attend(qc_c, kc_c, vc_c, ATTN_SCALE, sink),
                      dense_attend(qd_c, kd_c, vd_c, ATTN_SCALE))
    return y_lat, y_ctx


def moe_ffn(h, w_router, b_router, w_gate, w_up, w_down):
    n = h.shape[0]
    s = jax.nn.sigmoid((h @ w_router).astype(jnp.float32))
    sel = (s + b_router.astype(jnp.float32)).reshape(n, N_GROUPS, EXPERTS_PER_GROUP)
    group_score = jnp.sum(lax.top_k(sel, GROUP_SCORE_TOPK)[0], axis=-1)
    g = jnp.argmax(group_score, axis=-1)
    in_group = jnp.take_along_axis(sel, g[:, None, None], axis=1)[:, 0]
    _, local = lax.top_k(in_group, TOP_K)
    idx = g[:, None] * EXPERTS_PER_GROUP + local
    w = jnp.take_along_axis(s, idx, axis=-1)
    w = w / jnp.sum(w, axis=-1, keepdims=True)
    combine = jnp.sum(jax.nn.one_hot(idx, N_EXPERTS, dtype=jnp.float32) * w[..., None], axis=1).astype(h.dtype)
    out = jnp.zeros_like(h)
    for e in range(N_EXPERTS):
        y = (jax.nn.silu(h @ w_gate[e]) * (h @ w_up[e])) @ w_down[e]
        out = out + combine[:, e:e + 1] * y
    return out


def setup_inputs(seed: int = 0) -> dict:
    key = jax.random.key(seed)
    ks = iter(jax.random.split(key, 32))
    f32 = jnp.float32
    D = D_MODEL

    def nrm(shape, scale):
        return jax.random.normal(next(ks), shape, f32) * scale

    def gain(shape):
        return 1.0 + nrm(shape, 0.02)

    return {
        "x": nrm((BATCH, SEQ, D), 1.0),
        "c": nrm((BATCH, D), 1.0),
        "ctx": nrm((BATCH, CTX_LEN, D), 1.0),
        "c_ctx": nrm((D,), 1.0),
        "w_mod": nrm((DEPTH, D, 6 * D), 0.5 * D ** -0.5),
        "b_mod": nrm((DEPTH, 6 * D), 0.01),
        "norm_mix": gain((DEPTH, D)),
        "norm_ffn": gain((DEPTH, D)),
        "even_w_in": nrm((N_EVEN, D, EVEN_IN), D ** -0.5),
        "even_norm_q": gain((N_EVEN, MLA_Q_LORA)),
        "even_norm_kv": gain((N_EVEN, MLA_KV_LORA)),
        "even_w_uq": nrm((N_EVEN, MLA_Q_LORA, MLA_HEADS * (MLA_NOPE + MLA_ROPE)), MLA_Q_LORA ** -0.5),
        "even_w_ukv": nrm((N_EVEN, MLA_KV_LORA, MLA_HEADS * (MLA_NOPE + MLA_V)), MLA_KV_LORA ** -0.5),
        "even_lambda": nrm((N_EVEN, 4, DIFF_D), 0.1),
        "even_subln": gain((N_EVEN, DIFF_V)),
        "even_w_out": nrm((N_EVEN, EVEN_OUT, D), EVEN_OUT ** -0.5),
        "odd_w_in": nrm((N_ODD, D, ODD_IN), D ** -0.5),
        "odd_sink": nrm((N_ODD, WIN_HEADS), 0.5),
        "odd_q_norm": gain((N_ODD, HEAD_DIM)),
        "odd_k_norm": gain((N_ODD, HEAD_DIM)),
        "odd_w_out": nrm((N_ODD, ODD_OUT, D), ODD_OUT ** -0.5),
        "w_router": nrm((D, N_EXPERTS), D ** -0.5),
        "b_router": nrm((N_EXPERTS,), 0.01),
        "w_gate": nrm((DEPTH, N_EXPERTS, D, D_FF_EXPERT), D ** -0.5),
        "w_up": nrm((DEPTH, N_EXPERTS, D, D_FF_EXPERT), D ** -0.5),
        "w_down": nrm((DEPTH, N_EXPERTS, D_FF_EXPERT, D), D_FF_EXPERT ** -0.5),
        "norm_final": gain((D,)),
    }


def reference(x, c, ctx, c_ctx, w_mod, b_mod, norm_mix, norm_ffn,
              even_w_in, even_norm_q, even_norm_kv, even_w_uq, even_w_ukv, even_lambda, even_subln, even_w_out,
              odd_w_in, odd_sink, odd_q_norm, odd_k_norm, odd_w_out,
              w_router, b_router, w_gate, w_up, w_down, norm_final):
    B, T, D = x.shape
    rows = T // GRID_W
    tabs_mla = axial_tables(rows, MLA_ROPE)
    tabs_diff = axial_tables(rows, DIFF_D)
    tabs_attn = axial_tables(rows, HEAD_DIM)
    silu_c = jax.nn.silu(c)
    silu_cc = jax.nn.silu(c_ctx)
    cx = ctx
    n_ctx_tok = B * CTX_LEN
    for l in range(DEPTH):
        with_ctx = l < DEPTH - 1
        i = l // 2
        mod = (silu_c @ w_mod[l] + b_mod[l])[:, None, :]
        mod_c = silu_cc @ w_mod[l] + b_mod[l]
        sh_a, sc_a, g_a, sh_f, sc_f, g_f = jnp.split(mod, 6, axis=-1)
        ch_a, cs_a, cg_a, ch_f, cs_f, cg_f = jnp.split(mod_c, 6, axis=-1)
        h_lat = modulate(rmsnorm(x, norm_mix[l]), sh_a, sc_a)
        h_ctx = modulate(rmsnorm(cx, norm_mix[l]), ch_a, cs_a)
        if l % 2 == 0:
            lam_init = 0.8 - 0.6 * math.exp(-0.3 * l)
            y_lat, y_ctx = even_mixer(h_lat, h_ctx, even_w_in[i], even_norm_q[i], even_norm_kv[i],
                                      even_w_uq[i], even_w_ukv[i], even_lambda[i], even_subln[i],
                                      even_w_out[i], tabs_mla, tabs_diff, lam_init, with_ctx)
        else:
            y_lat, y_ctx = odd_mixer(h_lat, h_ctx, odd_w_in[i], odd_sink[i], odd_q_norm[i],
                                     odd_k_norm[i], odd_w_out[i], tabs_attn, with_ctx)
        x = x + g_a * y_lat
        h_lat = modulate(rmsnorm(x, norm_ffn[l]), sh_f, sc_f)
        if with_ctx:
            cx = cx + cg_a * y_ctx
            h_ctx = modulate(rmsnorm(cx, norm_ffn[l]), ch_f, cs_f)
            tok = jnp.concatenate([h_ctx.reshape(-1, D), h_lat.reshape(-1, D)], axis=0)
            y = moe_ffn(tok, w_router, b_router, w_gate[l], w_up[l], w_down[l])
            cx = cx + cg_f * y[:n_ctx_tok].reshape(B, CTX_LEN, D)
            x = x + g_f * y[n_ctx_tok:].reshape(B, T, D)
        else:
            y = moe_ffn(h_lat.reshape(-1, D), w_router, b_router, w_gate[l], w_up[l], w_down[l])
            x = x + g_f * y.reshape(B, T, D)
    return rmsnorm(x, norm_final)
```

```python
import functools
import math

import jax
import jax.numpy as jnp
from jax import lax
from jax.experimental import pallas as pl
from jax.experimental.pallas import tpu as pltpu

F32 = jnp.float32
BF16 = jnp.bfloat16
LOG2E = 1.4426950408889634

GRID_W = 64
ROPE_THETA = 10000.0
NORM_EPS = 1e-6
NEG_INF = -1e30

MLA_HEADS = 8
MLA_Q_LORA = 384
MLA_KV_LORA = 256
MLA_NOPE = 64
MLA_ROPE = 32
MLA_V = 64
MLA_SCALE = (MLA_NOPE + MLA_ROPE) ** -0.5
MLA_IN_COLS = MLA_Q_LORA + MLA_KV_LORA + MLA_ROPE

DIFF_HEADS = 8
DIFF_D = 32
DIFF_V = 2 * DIFF_D
DIFF_SCALE = DIFF_D ** -0.5

HEAD_DIM = 64
WIN_HEADS = 8
WIN_KV_HEADS = 2
WINDOW = 128
GLB_HEADS = 8
GLB_KV_HEADS = 2
ATTN_SCALE = HEAD_DIM ** -0.5

N_EXPERTS = 16
N_GROUPS = 4
EXPERTS_PER_GROUP = N_EXPERTS // N_GROUPS

TOK_TILE = 256
KEY_TILE = 256
MOE_TILE = 1024
ONES_ROWS = 16
VMEM_LIMIT = 56 * 1024 * 1024


def _cparams(n_axes):
    return pltpu.CompilerParams(dimension_semantics=("arbitrary",) * n_axes,
                                vmem_limit_bytes=VMEM_LIMIT)


def _mod_kernel(a_ref, w_ref, b_ref, o_ref):
    a = a_ref[...]
    a = a * (1.0 / (1.0 + jnp.exp(-a)))
    o_ref[0] = jnp.dot(a.astype(BF16), w_ref[0].astype(BF16), preferred_element_type=F32) + b_ref[0]


def _mod_vectors(cond, w_mod, b_mod):
    L, D, N = w_mod.shape
    tn = 1536
    return pl.pallas_call(
        _mod_kernel,
        grid=(L, N // tn),
        in_specs=[pl.BlockSpec((8, D), lambda l, j: (0, 0)),
                  pl.BlockSpec((1, D, tn), lambda l, j: (l, 0, j)),
                  pl.BlockSpec((1, 1, tn), lambda l, j: (l, 0, j))],
        out_specs=pl.BlockSpec((1, 8, tn), lambda l, j: (l, 0, j)),
        out_shape=jax.ShapeDtypeStruct((L, 8, N), F32),
        compiler_params=_cparams(2),
        name="mod_vectors",
    )(cond, w_mod, b_mod.reshape(L, 1, N))


def _norm_mod(x, g, sh, sc):
    y = x * lax.rsqrt(jnp.mean(x * x, axis=-1, keepdims=True) + NORM_EPS)
    return (y * g) * (1.0 + sc) + sh


def _nm_matmul_kernel(x_ref, g_ref, sh_ref, sc_ref, w_ref, o_ref):
    h = _norm_mod(x_ref[0].astype(F32), g_ref[...], sh_ref[0, 0], sc_ref[0, 0])
    o_ref[0] = jnp.dot(h.astype(BF16), w_ref[...], preferred_element_type=F32).astype(o_ref.dtype)


def _nm_matmul(x, g, sh, sc, w, n_tiles, out_dtype):
    B, S, K = x.shape
    N = w.shape[1]
    ctx_tile = S // TOK_TILE - 1
    mod_map = lambda b, i: (b, (i == ctx_tile).astype(jnp.int32), 0, 0)
    return pl.pallas_call(
        _nm_matmul_kernel,
        grid=(B, n_tiles),
        in_specs=[pl.BlockSpec((1, TOK_TILE, K), lambda b, i: (b, i, 0)),
                  pl.BlockSpec((1, K), lambda b, i: (0, 0)),
                  pl.BlockSpec((1, 1, 1, K), mod_map),
                  pl.BlockSpec((1, 1, 1, K), mod_map),
                  pl.BlockSpec((K, N), lambda b, i: (0, 0))],
        out_specs=pl.BlockSpec((1, TOK_TILE, N), lambda b, i: (b, i, 0)),
        out_shape=jax.ShapeDtypeStruct((B, S, N), out_dtype),
        compiler_params=_cparams(2),
        name="norm_mod_matmul",
    )(x, g.reshape(1, K).astype(F32), sh, sc, w)


def _attn_step(j, m, acc, qT, k_ref, vT_ref):
    off = j * KEY_TILE if isinstance(j, int) else pl.multiple_of(j * KEY_TILE, KEY_TILE)
    s = jnp.dot(k_ref[pl.ds(off, KEY_TILE), :], qT, preferred_element_type=F32)
    m_new = jnp.maximum(m, jnp.max(s, axis=0, keepdims=True))
    p = jnp.exp2(s - m_new).astype(BF16)
    acc = acc * jnp.exp2(m - m_new) + jnp.dot(vT_ref[:, pl.ds(off, KEY_TILE)], p, preferred_element_type=F32)
    return m_new, acc


def _finish(acc, dv):
    return acc[:dv] / acc[dv:dv + 1]


def _flash_kernel(init_ref, qT_ref, k_ref, vT_ref, o_ref, m_scr, acc_scr, *, q0, n_lat, dv):
    i = pl.program_id(2) + q0
    qT = qT_ref[...]
    tq = qT.shape[1]
    dva = vT_ref.shape[0]
    m0 = jnp.broadcast_to(init_ref[0, 0:1, 0:1], (1, tq))
    row = lax.broadcasted_iota(jnp.int32, (dva, tq), 0)
    acc0 = jnp.where(row == dv, jnp.broadcast_to(init_ref[0, 1:2, 0:1], (dva, tq)), 0.0)
    m_scr[...] = m0
    acc_scr[...] = acc0

    @pl.when(i < n_lat)
    def _():
        m, acc = lax.fori_loop(0, n_lat, lambda j, c: _attn_step(j, c[0], c[1], qT, k_ref, vT_ref), (m0, acc0))
        m_scr[...] = m
        acc_scr[...] = acc

    _, acc = _attn_step(n_lat, m_scr[...], acc_scr[...], qT, k_ref, vT_ref)
    o_ref[...] = _finish(acc, dv).astype(o_ref.dtype)


def _flash(qT, k, vT, init, q0, nqb):
    B, Hq, dk, S = qT.shape
    Hk, Hv, dva = k.shape[1], vT.shape[1], vT.shape[2]
    dv = dva - ONES_ROWS
    gk, gv = Hq // Hk, Hq // Hv
    n_lat = S // KEY_TILE - 1
    kern = functools.partial(_flash_kernel, q0=q0, n_lat=n_lat, dv=dv)
    return pl.pallas_call(
        kern,
        grid=(B, Hq, nqb),
        in_specs=[pl.BlockSpec((1, 2, 128), lambda b, h, i: (h, 0, 0)),
                  pl.BlockSpec((None, None, dk, TOK_TILE), lambda b, h, i: (b, h, 0, i + q0)),
                  pl.BlockSpec((None, None, S, dk), lambda b, h, i: (b, h // gk, 0, 0)),
                  pl.BlockSpec((None, None, dva, S), lambda b, h, i: (b, h // gv, 0, 0))],
        out_specs=pl.BlockSpec((None, None, dv, TOK_TILE), lambda b, h, i: (b, h, 0, i)),
        out_shape=jax.ShapeDtypeStruct((B, Hq, dv, nqb * TOK_TILE), F32),
        scratch_shapes=[pltpu.VMEM((1, TOK_TILE), F32), pltpu.VMEM((dva, TOK_TILE), F32)],
        compiler_params=_cparams(3),
        name="dense_attention",
    )(init, qT, k, vT)


def _window_kernel(sink_ref, qT_ref, k_ref, vT_ref, o_ref, *, T, dv):
    i = pl.program_id(2)
    qT = qT_ref[...]
    tq = qT.shape[1]
    band = tq + 2 * WINDOW
    t0 = i * tq
    off = pl.multiple_of(t0, TOK_TILE)
    s_loc = jnp.dot(k_ref[pl.ds(off, band), :], qT, preferred_element_type=F32)
    r = lax.broadcasted_iota(jnp.int32, (band, tq), 0)
    c = lax.broadcasted_iota(jnp.int32, (band, tq), 1)
    kpos = r + (t0 - WINDOW)
    valid = (jnp.abs(r - WINDOW - c) <= WINDOW) & (kpos >= 0) & (kpos < T)
    s_loc = jnp.where(valid, s_loc, NEG_INF)
    ctx0 = T + WINDOW
    s_ctx = jnp.dot(k_ref[pl.ds(ctx0, TOK_TILE), :], qT, preferred_element_type=F32)
    snk = jnp.broadcast_to(sink_ref[0, 0:1, 0:1], (1, tq))
    m = jnp.maximum(jnp.maximum(jnp.max(s_loc, axis=0, keepdims=True), jnp.max(s_ctx, axis=0, keepdims=True)), snk)
    p_loc = jnp.exp2(s_loc - m).astype(BF16)
    p_ctx = jnp.exp2(s_ctx - m).astype(BF16)
    acc = (jnp.dot(vT_ref[:, pl.ds(off, band)], p_loc, preferred_element_type=F32)
           + jnp.dot(vT_ref[:, pl.ds(ctx0, TOK_TILE)], p_ctx, preferred_element_type=F32))
    o_ref[...] = (acc[:dv] / (acc[dv:dv + 1] + jnp.exp2(snk - m))).astype(o_ref.dtype)


def _window(qT, k_pad, vT_pad, sink, T):
    B, Hq, dk, _ = qT.shape
    Hk, dva, Sp = vT_pad.shape[1], vT_pad.shape[2], vT_pad.shape[3]
    dv = dva - ONES_ROWS
    g = Hq // Hk
    kern = functools.partial(_window_kernel, T=T, dv=dv)
    return pl.pallas_call(
        kern,
        grid=(B, Hq, T // TOK_TILE),
        in_specs=[pl.BlockSpec((1, 1, 128), lambda b, h, i: (h, 0, 0)),
                  pl.BlockSpec((None, None, dk, TOK_TILE), lambda b, h, i: (b, h, 0, i)),
                  pl.BlockSpec((None, None, Sp, dk), lambda b, h, i: (b, h // g, 0, 0)),
                  pl.BlockSpec((None, None, dva, Sp), lambda b, h, i: (b, h // g, 0, 0))],
        out_specs=pl.BlockSpec((None, None, dv, TOK_TILE), lambda b, h, i: (b, h, 0, i)),
        out_shape=jax.ShapeDtypeStruct((B, Hq, dv, T), F32),
        compiler_params=_cparams(3),
        name="window_attention",
    )(sink, qT, k_pad, vT_pad)


def _out_proj_kernel(x_ref, o_ref, w_ref, gate_ref, y_ref):
    y = jnp.dot(o_ref[0], w_ref[...], preferred_element_type=F32)
    y_ref[0] = x_ref[0] + gate_ref[0, 0] * y


def _out_proj(x, o, w, gate, n_tiles):
    B, S, D = x.shape
    C = o.shape[2]
    ctx_tile = S // TOK_TILE - 1
    return pl.pallas_call(
        _out_proj_kernel,
        grid=(B, n_tiles),
        in_specs=[pl.BlockSpec((1, TOK_TILE, D), lambda b, i: (b, i, 0)),
                  pl.BlockSpec((1, TOK_TILE, C), lambda b, i: (b, i, 0)),
                  pl.BlockSpec((C, D), lambda b, i: (0, 0)),
                  pl.BlockSpec((1, 1, 1, D), lambda b, i: (b, (i == ctx_tile).astype(jnp.int32), 0, 0))],
        out_specs=pl.BlockSpec((1, TOK_TILE, D), lambda b, i: (b, i, 0)),
        out_shape=jax.ShapeDtypeStruct((B, S, D), F32),
        input_output_aliases={0: 0},
        compiler_params=_cparams(2),
        name="out_proj_residual",
    )(x, o, w, gate)


def _first_argmax(vals):
    best, idx = vals[0], jnp.zeros(vals[0].shape, jnp.int32)
    for j in range(1, len(vals)):
        better = vals[j] > best
        idx = jnp.where(better, j, idx)
        best = jnp.where(better, vals[j], best)
    return idx, best


def _pick(idx, vals):
    out = vals[0]
    for j in range(1, len(vals)):
        out = jnp.where(idx == j, vals[j], out)
    return out


def _route(logits, bias):
    s = 1.0 / (1.0 + jnp.exp(-logits))
    sel = s + bias
    srow = [s[e:e + 1] for e in range(N_EXPERTS)]
    row = [sel[e:e + 1] for e in range(N_EXPERTS)]
    scores = []
    for g in range(N_GROUPS):
        a, b, c, d = row[4 * g:4 * g + 4]
        hi1, lo1, hi2, lo2 = jnp.maximum(a, b), jnp.minimum(a, b), jnp.maximum(c, d), jnp.minimum(c, d)
        top1 = jnp.maximum(hi1, hi2)
        top2 = jnp.maximum(jnp.maximum(lo1, lo2), jnp.minimum(hi1, hi2))
        scores.append(top1 + top2)
    gi, _ = _first_argmax(scores)
    v = [_pick(gi, [row[4 * g + j] for g in range(N_GROUPS)]) for j in range(EXPERTS_PER_GROUP)]
    sv = [_pick(gi, [srow[4 * g + j] for g in range(N_GROUPS)]) for j in range(EXPERTS_PER_GROUP)]
    i1, _ = _first_argmax(v)
    i2, _ = _first_argmax([jnp.where(i1 == j, -jnp.inf, v[j]) for j in range(EXPERTS_PER_GROUP)])
    w1, w2 = _pick(i1, sv), _pick(i2, sv)
    tot = w1 + w2
    w1, w2 = w1 / tot, w2 / tot
    rows = []
    for e in range(N_EXPERTS):
        g, j = divmod(e, EXPERTS_PER_GROUP)
        in_g = gi == g
        rows.append(jnp.where(in_g & (i1 == j), w1, 0.0) + jnp.where(in_g & (i2 == j), w2, 0.0))
    return jnp.concatenate(rows, axis=0)


def _moe_kernel(x_ref, g_ref, sh_ref, sc_ref, gate_ref, wrT_ref, br_ref, wg_ref, wu_ref, wd_ref, y_ref,
                h_scr, comb_scr, acc_scr):
    e = pl.program_id(2)
    tm = x_ref.shape[1]

    @pl.when(e == 0)
    def _():
        h = _norm_mod(x_ref[0], g_ref[...], sh_ref[0, 0], sc_ref[0, 0]).astype(BF16)
        h_scr[...] = h
        logits = lax.dot_general(wrT_ref[...], h, (((1,), (1,)), ((), ())), preferred_element_type=F32)
        comb = _route(logits, br_ref[...])
        comb = jnp.concatenate([comb, jnp.zeros((128 - N_EXPERTS, tm), F32)], axis=0)
        comb_scr[...] = comb.T
        acc_scr[...] = jnp.zeros_like(acc_scr)

    h = h_scr[...]
    a = jnp.dot(h, wg_ref[0], preferred_element_type=F32)
    u = jnp.dot(h, wu_ref[0], preferred_element_type=F32)
    act = (a * (1.0 / (1.0 + jnp.exp(-a)))) * u
    y = jnp.dot(act.astype(BF16), wd_ref[0], preferred_element_type=F32)
    lane = lax.broadcasted_iota(jnp.int32, (tm, 128), 1)
    c = jnp.sum(jnp.where(lane == e, comb_scr[...], 0.0), axis=1, keepdims=True)
    acc_scr[...] += c * y

    @pl.when(e == N_EXPERTS - 1)
    def _():
        y_ref[0] = x_ref[0] + gate_ref[0, 0] * acc_scr[...]


def _moe(x, g, sh, sc, gate, wrT, br, wg, wu, wd, tm, tile0, n_tiles, kind):
    B, S, D = x.shape
    E, _, F = wg.shape
    mod_map = lambda b, i, e: (b, kind, 0, 0)
    row_map = lambda b, i, e: (b, i + tile0, 0)
    return pl.pallas_call(
        _moe_kernel,
        grid=(B, n_tiles, E),
        in_specs=[pl.BlockSpec((1, tm, D), row_map),
                  pl.BlockSpec((1, D), lambda b, i, e: (0, 0)),
                  pl.BlockSpec((1, 1, 1, D), mod_map),
                  pl.BlockSpec((1, 1, 1, D), mod_map),
                  pl.BlockSpec((1, 1, 1, D), mod_map),
                  pl.BlockSpec((E, D), lambda b, i, e: (0, 0)),
                  pl.BlockSpec((E, 1), lambda b, i, e: (0, 0)),
                  pl.BlockSpec((1, D, F), lambda b, i, e: (e, 0, 0)),
                  pl.BlockSpec((1, D, F), lambda b, i, e: (e, 0, 0)),
                  pl.BlockSpec((1, F, D), lambda b, i, e: (e, 0, 0))],
        out_specs=pl.BlockSpec((1, tm, D), row_map),
        out_shape=jax.ShapeDtypeStruct((B, S, D), F32),
        scratch_shapes=[pltpu.VMEM((tm, D), BF16), pltpu.VMEM((tm, 128), F32), pltpu.VMEM((tm, D), F32)],
        input_output_aliases={0: 0},
        compiler_params=_cparams(3),
        name="moe_experts",
    )(x, g.reshape(1, D).astype(F32), sh, sc, gate, wrT, br, wg, wu, wd)


def _final_norm_kernel(x_ref, g_ref, o_ref):
    x = x_ref[0]
    o_ref[0] = (x * lax.rsqrt(jnp.mean(x * x, axis=-1, keepdims=True) + NORM_EPS)) * g_ref[...]


def _final_norm(x, g, T):
    B, S, D = x.shape
    return pl.pallas_call(
        _final_norm_kernel,
        grid=(B, T // TOK_TILE),
        in_specs=[pl.BlockSpec((1, TOK_TILE, D), lambda b, i: (b, i, 0)),
                  pl.BlockSpec((1, D), lambda b, i: (0, 0))],
        out_specs=pl.BlockSpec((1, TOK_TILE, D), lambda b, i: (b, i, 0)),
        out_shape=jax.ShapeDtypeStruct((B, T, D), F32),
        compiler_params=_cparams(2),
        name="final_norm",
    )(x, g.reshape(1, D).astype(F32))


def _axial_tables(T, n_ctx, rot_dim):
    half = rot_dim // 2
    inv_freq = ROPE_THETA ** (-jnp.arange(0, half, 2, dtype=F32) / half)
    rows = T // GRID_W
    row = jnp.broadcast_to(jnp.arange(rows, dtype=F32)[:, None], (rows, GRID_W)).reshape(-1)
    col = jnp.broadcast_to(jnp.arange(GRID_W, dtype=F32)[None, :], (rows, GRID_W)).reshape(-1)

    def tab(pos):
        ang = pos[:, None] * inv_freq[None, :]
        ang = jnp.concatenate([ang, ang], axis=-1)
        one, zero = jnp.ones((n_ctx, half), F32), jnp.zeros((n_ctx, half), F32)
        return jnp.concatenate([jnp.cos(ang), one], axis=0), jnp.concatenate([jnp.sin(ang), zero], axis=0)

    return tab(row) + tab(col)


def _rot_half(x):
    h = x.shape[-1] // 2
    return jnp.concatenate([-x[..., h:], x[..., :h]], axis=-1)


def _rope(x, tabs):
    cos_r, sin_r, cos_c, sin_c = [t[None, :, None, :] for t in tabs]
    h = x.shape[-1] // 2
    xr, xc = x[..., :h], x[..., h:]
    return jnp.concatenate([xr * cos_r + _rot_half(xr) * sin_r, xc * cos_c + _rot_half(xc) * sin_c], axis=-1)


def _head_norm(x, g):
    return x * lax.rsqrt(jnp.mean(x * x, axis=-1, keepdims=True) + NORM_EPS) * g


def _to_qT(q, scale):
    return (q * (scale * LOG2E)).astype(BF16).transpose(0, 2, 3, 1)


def _to_k(k):
    return k.astype(BF16).transpose(0, 2, 1, 3)


def _to_vT(v):
    B, S, H, d = v.shape
    vT = v.astype(BF16).transpose(0, 2, 3, 1)
    extra = jnp.zeros((B, H, ONES_ROWS, S), BF16).at[:, :, 0, :].set(1.0)
    return jnp.concatenate([vT, extra], axis=2)


def _plain_init(n_heads):
    return jnp.broadcast_to(jnp.array([NEG_INF, 0.0], F32)[None, :, None], (n_heads, 2, 128))


def _sink_init(sink):
    s = sink.astype(F32) * LOG2E
    return jnp.broadcast_to(jnp.stack([s, jnp.ones_like(s)], axis=1)[:, :, None], (s.shape[0], 2, 128))


def _merge(oT):
    B, H, d, S = oT.shape
    return oT.reshape(B, H * d, S).transpose(0, 2, 1).astype(BF16)


def _even_mixer(xa, p, mods, T, nqb, n_tiles, lam_init, tabs_mla, tabs_diff):
    B, S, D = xa.shape
    z = _nm_matmul(xa, p["norm_mix"], mods["sh_a"], mods["sc_a"], p["w_in"], S // TOK_TILE, BF16)
    zero_q = jnp.zeros((B, 2, 1, MLA_Q_LORA), F32)
    zero_kv = jnp.zeros((B, 2, 1, MLA_KV_LORA), F32)
    o1 = MLA_Q_LORA
    o2 = o1 + MLA_KV_LORA
    o3 = o2 + MLA_ROPE
    q = _nm_matmul(z[..., :o1], p["norm_q"], zero_q, zero_q, p["w_uq"], S // TOK_TILE, F32)
    kv = _nm_matmul(z[..., o1:o2], p["norm_kv"], zero_kv, zero_kv, p["w_ukv"], S // TOK_TILE, F32)
    zf = z.astype(F32)
    q = q.reshape(B, S, MLA_HEADS, MLA_NOPE + MLA_ROPE)
    kv = kv.reshape(B, S, MLA_HEADS, MLA_NOPE + MLA_V)
    qa = jnp.concatenate([q[..., :MLA_NOPE], _rope(q[..., MLA_NOPE:], tabs_mla)], axis=-1)
    k_rope = _rope(zf[..., o2:o3][:, :, None, :], tabs_mla)
    ka = jnp.concatenate([kv[..., :MLA_NOPE], jnp.broadcast_to(k_rope, (B, S, MLA_HEADS, MLA_ROPE))], axis=-1)
    oa = _flash(_to_qT(qa, MLA_SCALE), _to_k(ka), _to_vT(kv[..., MLA_NOPE:]), _plain_init(MLA_HEADS), 0, nqb)

    nq = DIFF_HEADS * 2 * DIFF_D
    qd = _rope(zf[..., o3:o3 + nq].reshape(B, S, 2 * DIFF_HEADS, DIFF_D), tabs_diff)
    kd = _rope(zf[..., o3 + nq:o3 + 2 * nq].reshape(B, S, 2 * DIFF_HEADS, DIFF_D), tabs_diff)
    vd = zf[..., o3 + 2 * nq:].reshape(B, S, DIFF_HEADS, DIFF_V)
    od = _flash(_to_qT(qd, DIFF_SCALE), _to_k(kd), _to_vT(vd), _plain_init(2 * DIFF_HEADS), 0, nqb)

    lam = p["lam"].astype(F32)
    lam_full = jnp.exp(jnp.sum(lam[0] * lam[1])) - jnp.exp(jnp.sum(lam[2] * lam[3])) + lam_init
    diff = od[:, 0::2] - lam_full * od[:, 1::2]
    ob = diff * lax.rsqrt(jnp.mean(diff * diff, axis=2, keepdims=True) + NORM_EPS)
    ob = ob * p["subln"].astype(F32)[None, None, :, None] * (1.0 - lam_init)
    o = _merge(jnp.concatenate([oa, ob], axis=1))
    return _out_proj(xa, o, p["w_out"], mods["g_a"], n_tiles)


def _odd_mixer(xa, p, mods, T, nqb, n_tiles, tabs):
    B, S, D = xa.shape
    z = _nm_matmul(xa, p["norm_mix"], mods["sh_a"], mods["sc_a"], p["w_in"], S // TOK_TILE, BF16).astype(F32)
    parts, off = [], 0
    for n in (WIN_HEADS, WIN_KV_HEADS, WIN_KV_HEADS, GLB_HEADS, GLB_KV_HEADS, GLB_KV_HEADS):
        parts.append(z[..., off * HEAD_DIM:(off + n) * HEAD_DIM].reshape(B, S, n, HEAD_DIM))
        off += n
    qc, kc, vc, qd, kd, vd = parts
    qd, kd = _head_norm(qd, p["q_norm"].astype(F32)), _head_norm(kd, p["k_norm"].astype(F32))
    qc, kc, qd, kd = _rope(qc, tabs), _rope(kc, tabs), _rope(qd, tabs), _rope(kd, tabs)

    od = _flash(_to_qT(qd, ATTN_SCALE), _to_k(kd), _to_vT(vd), _plain_init(GLB_HEADS), 0, nqb)

    qcT, kcs, vcT = _to_qT(qc, ATTN_SCALE), _to_k(kc), _to_vT(vc)
    sink = p["sink"].astype(F32) * LOG2E
    sink_w = jnp.broadcast_to(sink[:, None, None], (WIN_HEADS, 1, 128))
    k_pad = jnp.pad(kcs, ((0, 0), (0, 0), (WINDOW, 0), (0, 0)))
    vT_pad = jnp.pad(vcT, ((0, 0), (0, 0), (0, 0), (WINDOW, 0)))
    oc = _window(qcT, k_pad, vT_pad, sink_w, T)
    if nqb > T // TOK_TILE:
        oc_ctx = _flash(qcT, kcs, vcT, _sink_init(p["sink"]), T // TOK_TILE, 1)
        oc = jnp.concatenate([oc, oc_ctx], axis=-1)
    o = _merge(jnp.concatenate([oc, od], axis=1))
    return _out_proj(xa, o, p["w_out"], mods["g_a"], n_tiles)


def kernel(x, c, ctx, c_ctx, w_mod, b_mod, norm_mix, norm_ffn, even_w_in, even_norm_q, even_norm_kv, even_w_uq, even_w_ukv, even_lambda, even_subln, even_w_out, odd_w_in, odd_sink, odd_q_norm, odd_k_norm, odd_w_out, w_router, b_router, w_gate, w_up, w_down, norm_final):
    B, T, D = x.shape
    n_ctx = ctx.shape[1]
    depth = w_mod.shape[0]
    assert n_ctx == TOK_TILE == KEY_TILE and T % MOE_TILE == 0 and B <= 7
    S = T + n_ctx
    tabs_mla = _axial_tables(T, n_ctx, MLA_ROPE)
    tabs_diff = _axial_tables(T, n_ctx, DIFF_D)
    tabs_attn = _axial_tables(T, n_ctx, HEAD_DIM)

    cond = jnp.zeros((8, D), F32).at[:B].set(c.astype(F32)).at[B].set(c_ctx.astype(F32))
    mod_all = _mod_vectors(cond, w_mod, b_mod)
    wrT = w_router.astype(BF16).T
    br = b_router.astype(F32).reshape(N_EXPERTS, 1)

    xa = jnp.concatenate([x, ctx], axis=1).astype(F32)
    for l in range(depth):
        with_ctx = l < depth - 1
        i = l // 2
        lat = mod_all[l, :B].reshape(B, 6, D)
        cx = jnp.broadcast_to(mod_all[l, B].reshape(1, 6, D), (B, 6, D))
        both = jnp.stack([lat, cx], axis=1)
        names = ("sh_a", "sc_a", "g_a", "sh_f", "sc_f", "g_f")
        mods = {n: both[:, :, j:j + 1, :] for j, n in enumerate(names)}
        n_tiles = S // TOK_TILE if with_ctx else T // TOK_TILE
        nqb = n_tiles
        if l % 2 == 0:
            lam_init = 0.8 - 0.6 * math.exp(-0.3 * l)
            p = dict(norm_mix=norm_mix[l], w_in=even_w_in[i].astype(BF16), norm_q=even_norm_q[i],
                     norm_kv=even_norm_kv[i], w_uq=even_w_uq[i].astype(BF16), w_ukv=even_w_ukv[i].astype(BF16),
                     lam=even_lambda[i], subln=even_subln[i], w_out=even_w_out[i].astype(BF16))
            xa = _even_mixer(xa, p, mods, T, nqb, n_tiles, lam_init, tabs_mla, tabs_diff)
        else:
            p = dict(norm_mix=norm_mix[l], w_in=odd_w_in[i].astype(BF16), sink=odd_sink[i], q_norm=odd_q_norm[i],
                     k_norm=odd_k_norm[i], w_out=odd_w_out[i].astype(BF16))
            xa = _odd_mixer(xa, p, mods, T, nqb, n_tiles, tabs_attn)
        wg, wu, wd = w_gate[l].astype(BF16), w_up[l].astype(BF16), w_down[l].astype(BF16)
        ffn = (norm_ffn[l], mods["sh_f"], mods["sc_f"], mods["g_f"], wrT, br, wg, wu, wd)
        xa = _moe(xa, *ffn, MOE_TILE, 0, T // MOE_TILE, 0)
        if with_ctx:
            xa = _moe(xa, *ffn, TOK_TILE, T // TOK_TILE, 1, 1)
    return _final_norm(xa, norm_final, T)
```

```python
import functools
import math

import jax
import jax.numpy as jnp
from jax import lax
from jax.experimental import pallas as pl
from jax.experimental.pallas import tpu as pltpu

F32 = jnp.float32
BF16 = jnp.bfloat16
LOG2E = 1.4426950408889634

GRID_W = 64
ROPE_THETA = 10000.0
NORM_EPS = 1e-6
NEG_INF = -1e30

MLA_HEADS = 8
MLA_Q_LORA = 384
MLA_KV_LORA = 256
MLA_NOPE = 64
MLA_ROPE = 32
MLA_V = 64
MLA_SCALE = (MLA_NOPE + MLA_ROPE) ** -0.5
MLA_IN_COLS = MLA_Q_LORA + MLA_KV_LORA + MLA_ROPE

DIFF_HEADS = 8
DIFF_D = 32
DIFF_V = 2 * DIFF_D
DIFF_SCALE = DIFF_D ** -0.5

HEAD_DIM = 64
WIN_HEADS = 8
WIN_KV_HEADS = 2
WINDOW = 128
GLB_HEADS = 8
GLB_KV_HEADS = 2
ATTN_SCALE = HEAD_DIM ** -0.5

N_EXPERTS = 16
N_GROUPS = 4
EXPERTS_PER_GROUP = N_EXPERTS // N_GROUPS

TOK_TILE = 256
Q_TILE = 512
KEY_TILE = 1024
PV_TILE = 256
MOE_TILE = 1024
ONES_ROWS = 16
VMEM_LIMIT = 56 * 1024 * 1024


def _cparams(n_axes):
    return pltpu.CompilerParams(dimension_semantics=("arbitrary",) * n_axes,
                                vmem_limit_bytes=VMEM_LIMIT)


def _mod_kernel(a_ref, w_ref, b_ref, o_ref):
    a = a_ref[...]
    a = a * (1.0 / (1.0 + jnp.exp(-a)))
    o_ref[0] = jnp.dot(a.astype(BF16), w_ref[0].astype(BF16), preferred_element_type=F32) + b_ref[0]


def _mod_vectors(cond, w_mod, b_mod):
    L, D, N = w_mod.shape
    tn = 1536
    return pl.pallas_call(
        _mod_kernel,
        grid=(L, N // tn),
        in_specs=[pl.BlockSpec((8, D), lambda l, j: (0, 0)),
                  pl.BlockSpec((1, D, tn), lambda l, j: (l, 0, j)),
                  pl.BlockSpec((1, 1, tn), lambda l, j: (l, 0, j))],
        out_specs=pl.BlockSpec((1, 8, tn), lambda l, j: (l, 0, j)),
        out_shape=jax.ShapeDtypeStruct((L, 8, N), F32),
        compiler_params=_cparams(2),
        name="mod_vectors",
    )(cond, w_mod, b_mod.reshape(L, 1, N))


def _norm_mod(x, g, sh, sc):
    y = x * lax.rsqrt(jnp.mean(x * x, axis=-1, keepdims=True) + NORM_EPS)
    return (y * g) * (1.0 + sc) + sh


def _nm_matmul_kernel(x_ref, g_ref, sh_ref, sc_ref, w_ref, o_ref):
    h = _norm_mod(x_ref[0].astype(F32), g_ref[...], sh_ref[0, 0], sc_ref[0, 0])
    o_ref[0] = jnp.dot(h.astype(BF16), w_ref[...], preferred_element_type=F32).astype(o_ref.dtype)


def _nm_matmul(x, g, sh, sc, w, n_tiles, out_dtype):
    B, S, K = x.shape
    N = w.shape[1]
    ctx_tile = S // TOK_TILE - 1
    mod_map = lambda b, i: (b, (i == ctx_tile).astype(jnp.int32), 0, 0)
    return pl.pallas_call(
        _nm_matmul_kernel,
        grid=(B, n_tiles),
        in_specs=[pl.BlockSpec((1, TOK_TILE, K), lambda b, i: (b, i, 0)),
                  pl.BlockSpec((1, K), lambda b, i: (0, 0)),
                  pl.BlockSpec((1, 1, 1, K), mod_map),
                  pl.BlockSpec((1, 1, 1, K), mod_map),
                  pl.BlockSpec((K, N), lambda b, i: (0, 0))],
        out_specs=pl.BlockSpec((1, TOK_TILE, N), lambda b, i: (b, i, 0)),
        out_shape=jax.ShapeDtypeStruct((B, S, N), out_dtype),
        compiler_params=_cparams(2),
        name="norm_mod_matmul",
    )(x, g.reshape(1, K).astype(F32), sh, sc, w)


def _scores(k_ref, qT, off, n, s_ref):
    s = jnp.dot(k_ref[pl.ds(off, n), :], qT, preferred_element_type=F32)
    s_ref[...] = s
    return jnp.max(s, axis=0, keepdims=True)


def _accumulate(s_ref, n, m, cmax, acc_ref, vT_ref, off):
    m_new = jnp.maximum(m, cmax)
    pv = None
    for kk in range(0, n, PV_TILE):
        p = jnp.exp2(s_ref[kk:kk + PV_TILE, :] - m_new).astype(BF16)
        start = off + kk if isinstance(off, int) else pl.multiple_of(off + kk, PV_TILE)
        d = jnp.dot(vT_ref[:, pl.ds(start, PV_TILE)], p, preferred_element_type=F32)
        pv = d if pv is None else pv + d
    acc_ref[...] = acc_ref[...] * jnp.exp2(m - m_new) + pv
    return m_new


def _softmax_init(init_ref, dva, dv, tq):
    m0 = jnp.broadcast_to(init_ref[0, 0:1, 0:1], (1, tq))
    row = lax.broadcasted_iota(jnp.int32, (dva, tq), 0)
    acc0 = jnp.where(row == dv, jnp.broadcast_to(init_ref[0, 1:2, 0:1], (dva, tq)), 0.0)
    return m0, acc0


def _flash_kernel(init_ref, qT_ref, k_ref, vT_ref, o_ref, s0_ref, s1_ref, sc_ref, acc_ref, *, T, kt, dv):
    qT = qT_ref[...]
    tq = qT.shape[1]
    dva = vT_ref.shape[0]
    n_ctx = sc_ref.shape[0]
    m, acc0 = _softmax_init(init_ref, dva, dv, tq)
    acc_ref[...] = acc0
    cm = _scores(k_ref, qT, T, n_ctx, sc_ref)
    m = _accumulate(sc_ref, n_ctx, m, cm, acc_ref, vT_ref, T)
    cm_a = _scores(k_ref, qT, 0, kt, s0_ref)

    def body(t, carry):
        m, cm_a = carry
        off = pl.multiple_of(t * (2 * kt), kt)
        cm_b = _scores(k_ref, qT, off + kt, kt, s1_ref)
        m = _accumulate(s0_ref, kt, m, cm_a, acc_ref, vT_ref, off)
        cm_a = _scores(k_ref, qT, off + 2 * kt, kt, s0_ref)
        m = _accumulate(s1_ref, kt, m, cm_b, acc_ref, vT_ref, off + kt)
        return m, cm_a

    m, cm_a = lax.fori_loop(0, T // (2 * kt) - 1, body, (m, cm_a))
    cm_b = _scores(k_ref, qT, T - kt, kt, s1_ref)
    m = _accumulate(s0_ref, kt, m, cm_a, acc_ref, vT_ref, T - 2 * kt)
    m = _accumulate(s1_ref, kt, m, cm_b, acc_ref, vT_ref, T - kt)
    acc = acc_ref[...]
    o_ref[...] = acc[:dv] / acc[dv:dv + 1]


def _flash(qT, k, vT, init, T):
    B, Hq, dk, S = qT.shape
    Hk, Hv, dva = k.shape[1], vT.shape[1], vT.shape[2]
    dv = dva - ONES_ROWS
    gk, gv = Hq // Hk, Hq // Hv
    kt = min(KEY_TILE, T // 2)
    tq = min(Q_TILE, T)
    assert T % (2 * kt) == 0 and T % tq == 0 and kt % PV_TILE == 0
    kern = functools.partial(_flash_kernel, T=T, kt=kt, dv=dv)
    return pl.pallas_call(
        kern,
        grid=(B, Hq, T // tq),
        in_specs=[pl.BlockSpec((1, 2, 128), lambda b, h, i: (h, 0, 0)),
                  pl.BlockSpec((None, None, dk, tq), lambda b, h, i: (b, h, 0, i)),
                  pl.BlockSpec((None, None, S, dk), lambda b, h, i: (b, h // gk, 0, 0)),
                  pl.BlockSpec((None, None, dva, S), lambda b, h, i: (b, h // gv, 0, 0))],
        out_specs=pl.BlockSpec((None, None, dv, tq), lambda b, h, i: (b, h, 0, i)),
        out_shape=jax.ShapeDtypeStruct((B, Hq, dv, T), F32),
        scratch_shapes=[pltpu.VMEM((kt, tq), F32), pltpu.VMEM((kt, tq), F32), pltpu.VMEM((S - T, tq), F32),
                        pltpu.VMEM((dva, tq), F32)],
        compiler_params=_cparams(3),
        name="dense_attention",
    )(init, qT, k, vT)


def _ctx_attn_kernel(init_ref, qT_ref, k_ref, vT_ref, o_ref, *, dv):
    qT = qT_ref[...]
    m0, acc0 = _softmax_init(init_ref, vT_ref.shape[0], dv, qT.shape[1])
    s = jnp.dot(k_ref[...], qT, preferred_element_type=F32)
    m = jnp.maximum(m0, jnp.max(s, axis=0, keepdims=True))
    acc = acc0 * jnp.exp2(m0 - m) + jnp.dot(vT_ref[...], jnp.exp2(s - m).astype(BF16), preferred_element_type=F32)
    o_ref[...] = acc[:dv] / acc[dv:dv + 1]


def _ctx_attn(qT, k, vT, init, T):
    B, Hq, dk, S = qT.shape
    Hk, Hv, dva = k.shape[1], vT.shape[1], vT.shape[2]
    dv = dva - ONES_ROWS
    gk, gv = Hq // Hk, Hq // Hv
    n = S - T
    blk = T // n
    return pl.pallas_call(
        functools.partial(_ctx_attn_kernel, dv=dv),
        grid=(B, Hq),
        in_specs=[pl.BlockSpec((1, 2, 128), lambda b, h: (h, 0, 0)),
                  pl.BlockSpec((None, None, dk, n), lambda b, h: (b, h, 0, blk)),
                  pl.BlockSpec((None, None, n, dk), lambda b, h: (b, h // gk, blk, 0)),
                  pl.BlockSpec((None, None, dva, n), lambda b, h: (b, h // gv, 0, blk))],
        out_specs=pl.BlockSpec((None, None, dv, n), lambda b, h: (b, h, 0, 0)),
        out_shape=jax.ShapeDtypeStruct((B, Hq, dv, n), F32),
        compiler_params=_cparams(2),
        name="context_attention",
    )(init, qT, k, vT)


def _dense_attend(qT, k, vT, init, T, with_ctx):
    o = _flash(qT, k, vT, init, T)
    if with_ctx:
        o = jnp.concatenate([o, _ctx_attn(qT, k, vT, init, T)], axis=-1)
    return o


def _window_kernel(sink_ref, qT_ref, k_ref, vT_ref, o_ref, *, T, dv):
    i = pl.program_id(2)
    qT = qT_ref[...]
    tq = qT.shape[1]
    band = tq + 2 * WINDOW
    t0 = i * tq
    off = pl.multiple_of(t0, TOK_TILE)
    s_loc = jnp.dot(k_ref[pl.ds(off, band), :], qT, preferred_element_type=F32)
    r = lax.broadcasted_iota(jnp.int32, (band, tq), 0)
    c = lax.broadcasted_iota(jnp.int32, (band, tq), 1)
    kpos = r + (t0 - WINDOW)
    valid = (jnp.abs(r - WINDOW - c) <= WINDOW) & (kpos >= 0) & (kpos < T)
    s_loc = jnp.where(valid, s_loc, NEG_INF)
    ctx0 = T + WINDOW
    s_ctx = jnp.dot(k_ref[pl.ds(ctx0, TOK_TILE), :], qT, preferred_element_type=F32)
    snk = jnp.broadcast_to(sink_ref[0, 0:1, 0:1], (1, tq))
    m = jnp.maximum(jnp.maximum(jnp.max(s_loc, axis=0, keepdims=True), jnp.max(s_ctx, axis=0, keepdims=True)), snk)
    p_loc = jnp.exp2(s_loc - m).astype(BF16)
    p_ctx = jnp.exp2(s_ctx - m).astype(BF16)
    acc = (jnp.dot(vT_ref[:, pl.ds(off, band)], p_loc, preferred_element_type=F32)
           + jnp.dot(vT_ref[:, pl.ds(ctx0, TOK_TILE)], p_ctx, preferred_element_type=F32))
    o_ref[...] = (acc[:dv] / (acc[dv:dv + 1] + jnp.exp2(snk - m))).astype(o_ref.dtype)


def _window(qT, k_pad, vT_pad, sink, T):
    B, Hq, dk, _ = qT.shape
    Hk, dva, Sp = vT_pad.shape[1], vT_pad.shape[2], vT_pad.shape[3]
    dv = dva - ONES_ROWS
    g = Hq // Hk
    kern = functools.partial(_window_kernel, T=T, dv=dv)
    return pl.pallas_call(
        kern,
        grid=(B, Hq, T // TOK_TILE),
        in_specs=[pl.BlockSpec((1, 1, 128), lambda b, h, i: (h, 0, 0)),
                  pl.BlockSpec((None, None, dk, TOK_TILE), lambda b, h, i: (b, h, 0, i)),
                  pl.BlockSpec((None, None, Sp, dk), lambda b, h, i: (b, h // g, 0, 0)),
                  pl.BlockSpec((None, None, dva, Sp), lambda b, h, i: (b, h // g, 0, 0))],
        out_specs=pl.BlockSpec((None, None, dv, TOK_TILE), lambda b, h, i: (b, h, 0, i)),
        out_shape=jax.ShapeDtypeStruct((B, Hq, dv, T), F32),
        compiler_params=_cparams(3),
        name="window_attention",
    )(sink, qT, k_pad, vT_pad)


def _out_proj_kernel(x_ref, o_ref, w_ref, gate_ref, y_ref):
    y = jnp.dot(o_ref[0], w_ref[...], preferred_element_type=F32)
    y_ref[0] = x_ref[0] + gate_ref[0, 0] * y


def _out_proj(x, o, w, gate, n_tiles):
    B, S, D = x.shape
    C = o.shape[2]
    ctx_tile = S // TOK_TILE - 1
    return pl.pallas_call(
        _out_proj_kernel,
        grid=(B, n_tiles),
        in_specs=[pl.BlockSpec((1, TOK_TILE, D), lambda b, i: (b, i, 0)),
                  pl.BlockSpec((1, TOK_TILE, C), lambda b, i: (b, i, 0)),
                  pl.BlockSpec((C, D), lambda b, i: (0, 0)),
                  pl.BlockSpec((1, 1, 1, D), lambda b, i: (b, (i == ctx_tile).astype(jnp.int32), 0, 0))],
        out_specs=pl.BlockSpec((1, TOK_TILE, D), lambda b, i: (b, i, 0)),
        out_shape=jax.ShapeDtypeStruct((B, S, D), F32),
        input_output_aliases={0: 0},
        compiler_params=_cparams(2),
        name="out_proj_residual",
    )(x, o, w, gate)


def _first_argmax(vals):
    best, idx = vals[0], jnp.zeros(vals[0].shape, jnp.int32)
    for j in range(1, len(vals)):
        better = vals[j] > best
        idx = jnp.where(better, j, idx)
        best = jnp.where(better, vals[j], best)
    return idx, best


def _pick(idx, vals):
    out = vals[0]
    for j in range(1, len(vals)):
        out = jnp.where(idx == j, vals[j], out)
    return out


def _route(logits, bias):
    s = 1.0 / (1.0 + jnp.exp(-logits))
    sel = s + bias
    srow = [s[e:e + 1] for e in range(N_EXPERTS)]
    row = [sel[e:e + 1] for e in range(N_EXPERTS)]
    scores = []
    for g in range(N_GROUPS):
        a, b, c, d = row[4 * g:4 * g + 4]
        hi1, lo1, hi2, lo2 = jnp.maximum(a, b), jnp.minimum(a, b), jnp.maximum(c, d), jnp.minimum(c, d)
        top1 = jnp.maximum(hi1, hi2)
        top2 = jnp.maximum(jnp.maximum(lo1, lo2), jnp.minimum(hi1, hi2))
        scores.append(top1 + top2)
    gi, _ = _first_argmax(scores)
    v = [_pick(gi, [row[4 * g + j] for g in range(N_GROUPS)]) for j in range(EXPERTS_PER_GROUP)]
    sv = [_pick(gi, [srow[4 * g + j] for g in range(N_GROUPS)]) for j in range(EXPERTS_PER_GROUP)]
    i1, _ = _first_argmax(v)
    i2, _ = _first_argmax([jnp.where(i1 == j, -jnp.inf, v[j]) for j in range(EXPERTS_PER_GROUP)])
    w1, w2 = _pick(i1, sv), _pick(i2, sv)
    tot = w1 + w2
    w1, w2 = w1 / tot, w2 / tot
    rows = []
    for e in range(N_EXPERTS):
        g, j = divmod(e, EXPERTS_PER_GROUP)
        in_g = gi == g
        rows.append(jnp.where(in_g & (i1 == j), w1, 0.0) + jnp.where(in_g & (i2 == j), w2, 0.0))
    return jnp.concatenate(rows, axis=0)


def _moe_kernel(x_ref, g_ref, sh_ref, sc_ref, gate_ref, wrT_ref, br_ref, wg_ref, wu_ref, wd_ref, y_ref,
                h_scr, comb_scr, acc_scr):
    e = pl.program_id(2)
    tm = x_ref.shape[1]

    @pl.when(e == 0)
    def _():
        h = _norm_mod(x_ref[0], g_ref[...], sh_ref[0, 0], sc_ref[0, 0]).astype(BF16)
        h_scr[...] = h
        logits = lax.dot_general(wrT_ref[...], h, (((1,), (1,)), ((), ())), preferred_element_type=F32)
        comb = _route(logits, br_ref[...])
        comb = jnp.concatenate([comb, jnp.zeros((128 - N_EXPERTS, tm), F32)], axis=0)
        comb_scr[...] = comb.T
        acc_scr[...] = jnp.zeros_like(acc_scr)

    h = h_scr[...]
    a = jnp.dot(h, wg_ref[0], preferred_element_type=F32)
    u = jnp.dot(h, wu_ref[0], preferred_element_type=F32)
    act = (a * (1.0 / (1.0 + jnp.exp(-a)))) * u
    y = jnp.dot(act.astype(BF16), wd_ref[0], preferred_element_type=F32)
    lane = lax.broadcasted_iota(jnp.int32, (tm, 128), 1)
    c = jnp.sum(jnp.where(lane == e, comb_scr[...], 0.0), axis=1, keepdims=True)
    acc_scr[...] += c * y

    @pl.when(e == N_EXPERTS - 1)
    def _():
        y_ref[0] = x_ref[0] + gate_ref[0, 0] * acc_scr[...]


def _moe(x, g, sh, sc, gate, wrT, br, wg, wu, wd, tm, tile0, n_tiles, kind):
    B, S, D = x.shape
    E, _, F = wg.shape
    mod_map = lambda b, i, e: (b, kind, 0, 0)
    row_map = lambda b, i, e: (b, i + tile0, 0)
    return pl.pallas_call(
        _moe_kernel,
        grid=(B, n_tiles, E),
        in_specs=[pl.BlockSpec((1, tm, D), row_map),
                  pl.BlockSpec((1, D), lambda b, i, e: (0, 0)),
                  pl.BlockSpec((1, 1, 1, D), mod_map),
                  pl.BlockSpec((1, 1, 1, D), mod_map),
                  pl.BlockSpec((1, 1, 1, D), mod_map),
                  pl.BlockSpec((E, D), lambda b, i, e: (0, 0)),
                  pl.BlockSpec((E, 1), lambda b, i, e: (0, 0)),
                  pl.BlockSpec((1, D, F), lambda b, i, e: (e, 0, 0)),
                  pl.BlockSpec((1, D, F), lambda b, i, e: (e, 0, 0)),
                  pl.BlockSpec((1, F, D), lambda b, i, e: (e, 0, 0))],
        out_specs=pl.BlockSpec((1, tm, D), row_map),
        out_shape=jax.ShapeDtypeStruct((B, S, D), F32),
        scratch_shapes=[pltpu.VMEM((tm, D), BF16), pltpu.VMEM((tm, 128), F32), pltpu.VMEM((tm, D), F32)],
        input_output_aliases={0: 0},
        compiler_params=_cparams(3),
        name="moe_experts",
    )(x, g.reshape(1, D).astype(F32), sh, sc, gate, wrT, br, wg, wu, wd)


def _final_norm_kernel(x_ref, g_ref, o_ref):
    x = x_ref[0]
    o_ref[0] = (x * lax.rsqrt(jnp.mean(x * x, axis=-1, keepdims=True) + NORM_EPS)) * g_ref[...]


def _final_norm(x, g, T):
    B, S, D = x.shape
    return pl.pallas_call(
        _final_norm_kernel,
        grid=(B, T // TOK_TILE),
        in_specs=[pl.BlockSpec((1, TOK_TILE, D), lambda b, i: (b, i, 0)),
                  pl.BlockSpec((1, D), lambda b, i: (0, 0))],
        out_specs=pl.BlockSpec((1, TOK_TILE, D), lambda b, i: (b, i, 0)),
        out_shape=jax.ShapeDtypeStruct((B, T, D), F32),
        compiler_params=_cparams(2),
        name="final_norm",
    )(x, g.reshape(1, D).astype(F32))


def _axial_tables(T, n_ctx, rot_dim):
    half = rot_dim // 2
    inv_freq = ROPE_THETA ** (-jnp.arange(0, half, 2, dtype=F32) / half)
    rows = T // GRID_W
    row = jnp.broadcast_to(jnp.arange(rows, dtype=F32)[:, None], (rows, GRID_W)).reshape(-1)
    col = jnp.broadcast_to(jnp.arange(GRID_W, dtype=F32)[None, :], (rows, GRID_W)).reshape(-1)

    def tab(pos):
        ang = pos[:, None] * inv_freq[None, :]
        ang = jnp.concatenate([ang, ang], axis=-1)
        one, zero = jnp.ones((n_ctx, half), F32), jnp.zeros((n_ctx, half), F32)
        return jnp.concatenate([jnp.cos(ang), one], axis=0), jnp.concatenate([jnp.sin(ang), zero], axis=0)

    return tab(row) + tab(col)


def _rot_half(x):
    h = x.shape[-1] // 2
    return jnp.concatenate([-x[..., h:], x[..., :h]], axis=-1)


def _rope(x, tabs):
    cos_r, sin_r, cos_c, sin_c = [t[None, :, None, :] for t in tabs]
    h = x.shape[-1] // 2
    xr, xc = x[..., :h], x[..., h:]
    return jnp.concatenate([xr * cos_r + _rot_half(xr) * sin_r, xc * cos_c + _rot_half(xc) * sin_c], axis=-1)


def _head_norm(x, g):
    return x * lax.rsqrt(jnp.mean(x * x, axis=-1, keepdims=True) + NORM_EPS) * g


def _to_qT(q, scale):
    return (q * (scale * LOG2E)).astype(BF16).transpose(0, 2, 3, 1)


def _to_k(k):
    return k.astype(BF16).transpose(0, 2, 1, 3)


def _to_vT(v):
    B, S, H, d = v.shape
    vT = v.astype(BF16).transpose(0, 2, 3, 1)
    extra = jnp.zeros((B, H, ONES_ROWS, S), BF16).at[:, :, 0, :].set(1.0)
    return jnp.concatenate([vT, extra], axis=2)


def _plain_init(n_heads):
    return jnp.broadcast_to(jnp.array([NEG_INF, 0.0], F32)[None, :, None], (n_heads, 2, 128))


def _sink_init(sink):
    s = sink.astype(F32) * LOG2E
    return jnp.broadcast_to(jnp.stack([s, jnp.ones_like(s)], axis=1)[:, :, None], (s.shape[0], 2, 128))


def _merge(oT):
    B, H, d, S = oT.shape
    return oT.reshape(B, H * d, S).transpose(0, 2, 1).astype(BF16)


def _even_mixer(xa, p, mods, T, with_ctx, n_tiles, lam_init, tabs_mla, tabs_diff):
    B, S, D = xa.shape
    z = _nm_matmul(xa, p["norm_mix"], mods["sh_a"], mods["sc_a"], p["w_in"], S // TOK_TILE, BF16)
    zero_q = jnp.zeros((B, 2, 1, MLA_Q_LORA), F32)
    zero_kv = jnp.zeros((B, 2, 1, MLA_KV_LORA), F32)
    o1 = MLA_Q_LORA
    o2 = o1 + MLA_KV_LORA
    o3 = o2 + MLA_ROPE
    q = _nm_matmul(z[..., :o1], p["norm_q"], zero_q, zero_q, p["w_uq"], S // TOK_TILE, F32)
    kv = _nm_matmul(z[..., o1:o2], p["norm_kv"], zero_kv, zero_kv, p["w_ukv"], S // TOK_TILE, F32)
    zf = z.astype(F32)
    q = q.reshape(B, S, MLA_HEADS, MLA_NOPE + MLA_ROPE)
    kv = kv.reshape(B, S, MLA_HEADS, MLA_NOPE + MLA_V)
    qa = jnp.concatenate([q[..., :MLA_NOPE], _rope(q[..., MLA_NOPE:], tabs_mla)], axis=-1)
    k_rope = _rope(zf[..., o2:o3][:, :, None, :], tabs_mla)
    ka = jnp.concatenate([kv[..., :MLA_NOPE], jnp.broadcast_to(k_rope, (B, S, MLA_HEADS, MLA_ROPE))], axis=-1)
    oa = _dense_attend(_to_qT(qa, MLA_SCALE), _to_k(ka), _to_vT(kv[..., MLA_NOPE:]), _plain_init(MLA_HEADS), T,
                       with_ctx)

    nq = DIFF_HEADS * 2 * DIFF_D
    qd = _rope(zf[..., o3:o3 + nq].reshape(B, S, 2 * DIFF_HEADS, DIFF_D), tabs_diff)
    kd = _rope(zf[..., o3 + nq:o3 + 2 * nq].reshape(B, S, 2 * DIFF_HEADS, DIFF_D), tabs_diff)
    vd = zf[..., o3 + 2 * nq:].reshape(B, S, DIFF_HEADS, DIFF_V)
    od = _dense_attend(_to_qT(qd, DIFF_SCALE), _to_k(kd), _to_vT(vd), _plain_init(2 * DIFF_HEADS), T, with_ctx)

    lam = p["lam"].astype(F32)
    lam_full = jnp.exp(jnp.sum(lam[0] * lam[1])) - jnp.exp(jnp.sum(lam[2] * lam[3])) + lam_init
    diff = od[:, 0::2] - lam_full * od[:, 1::2]
    ob = diff * lax.rsqrt(jnp.mean(diff * diff, axis=2, keepdims=True) + NORM_EPS)
    ob = ob * p["subln"].astype(F32)[None, None, :, None] * (1.0 - lam_init)
    o = _merge(jnp.concatenate([oa, ob], axis=1))
    return _out_proj(xa, o, p["w_out"], mods["g_a"], n_tiles)


def _odd_mixer(xa, p, mods, T, with_ctx, n_tiles, tabs):
    B, S, D = xa.shape
    z = _nm_matmul(xa, p["norm_mix"], mods["sh_a"], mods["sc_a"], p["w_in"], S // TOK_TILE, BF16).astype(F32)
    parts, off = [], 0
    for n in (WIN_HEADS, WIN_KV_HEADS, WIN_KV_HEADS, GLB_HEADS, GLB_KV_HEADS, GLB_KV_HEADS):
        parts.append(z[..., off * HEAD_DIM:(off + n) * HEAD_DIM].reshape(B, S, n, HEAD_DIM))
        off += n
    qc, kc, vc, qd, kd, vd = parts
    qd, kd = _head_norm(qd, p["q_norm"].astype(F32)), _head_norm(kd, p["k_norm"].astype(F32))
    qc, kc, qd, kd = _rope(qc, tabs), _rope(kc, tabs), _rope(qd, tabs), _rope(kd, tabs)

    od = _dense_attend(_to_qT(qd, ATTN_SCALE), _to_k(kd), _to_vT(vd), _plain_init(GLB_HEADS), T, with_ctx)

    qcT, kcs, vcT = _to_qT(qc, ATTN_SCALE), _to_k(kc), _to_vT(vc)
    sink = p["sink"].astype(F32) * LOG2E
    sink_w = jnp.broadcast_to(sink[:, None, None], (WIN_HEADS, 1, 128))
    k_pad = jnp.pad(kcs, ((0, 0), (0, 0), (WINDOW, 0), (0, 0)))
    vT_pad = jnp.pad(vcT, ((0, 0), (0, 0), (0, 0), (WINDOW, 0)))
    oc = _window(qcT, k_pad, vT_pad, sink_w, T)
    if with_ctx:
        oc = jnp.concatenate([oc, _ctx_attn(qcT, kcs, vcT, _sink_init(p["sink"]), T)], axis=-1)
    o = _merge(jnp.concatenate([oc, od], axis=1))
    return _out_proj(xa, o, p["w_out"], mods["g_a"], n_tiles)


def kernel(x, c, ctx, c_ctx, w_mod, b_mod, norm_mix, norm_ffn, even_w_in, even_norm_q, even_norm_kv, even_w_uq, even_w_ukv, even_lambda, even_subln, even_w_out, odd_w_in, odd_sink, odd_q_norm, odd_k_norm, odd_w_out, w_router, b_router, w_gate, w_up, w_down, norm_final):
    B, T, D = x.shape
    n_ctx = ctx.shape[1]
    depth = w_mod.shape[0]
    assert n_ctx == TOK_TILE == PV_TILE and T % MOE_TILE == 0 and B <= 7
    S = T + n_ctx
    tabs_mla = _axial_tables(T, n_ctx, MLA_ROPE)
    tabs_diff = _axial_tables(T, n_ctx, DIFF_D)
    tabs_attn = _axial_tables(T, n_ctx, HEAD_DIM)

    cond = jnp.zeros((8, D), F32).at[:B].set(c.astype(F32)).at[B].set(c_ctx.astype(F32))
    mod_all = _mod_vectors(cond, w_mod, b_mod)
    wrT = w_router.astype(BF16).T
    br = b_router.astype(F32).reshape(N_EXPERTS, 1)

    xa = jnp.concatenate([x, ctx], axis=1).astype(F32)
    for l in range(depth):
        with_ctx = l < depth - 1
        i = l // 2
        lat = mod_all[l, :B].reshape(B, 6, D)
        cx = jnp.broadcast_to(mod_all[l, B].reshape(1, 6, D), (B, 6, D))
        both = jnp.stack([lat, cx], axis=1)
        names = ("sh_a", "sc_a", "g_a", "sh_f", "sc_f", "g_f")
        mods = {n: both[:, :, j:j + 1, :] for j, n in enumerate(names)}
        n_tiles = S // TOK_TILE if with_ctx else T // TOK_TILE
        if l % 2 == 0:
            lam_init = 0.8 - 0.6 * math.exp(-0.3 * l)
            p = dict(norm_mix=norm_mix[l], w_in=even_w_in[i].astype(BF16), norm_q=even_norm_q[i],
                     norm_kv=even_norm_kv[i], w_uq=even_w_uq[i].astype(BF16), w_ukv=even_w_ukv[i].astype(BF16),
                     lam=even_lambda[i], subln=even_subln[i], w_out=even_w_out[i].astype(BF16))
            xa = _even_mixer(xa, p, mods, T, with_ctx, n_tiles, lam_init, tabs_mla, tabs_diff)
        else:
            p = dict(norm_mix=norm_mix[l], w_in=odd_w_in[i].astype(BF16), sink=odd_sink[i], q_norm=odd_q_norm[i],
                     k_norm=odd_k_norm[i], w_out=odd_w_out[i].astype(BF16))
            xa = _odd_mixer(xa, p, mods, T, with_ctx, n_tiles, tabs_attn)
        wg, wu, wd = w_gate[l].astype(BF16), w_up[l].astype(BF16), w_down[l].astype(BF16)
        ffn = (norm_ffn[l], mods["sh_f"], mods["sc_f"], mods["g_f"], wrT, br, wg, wu, wd)
        xa = _moe(xa, *ffn, MOE_TILE, 0, T // MOE_TILE, 0)
        if with_ctx:
            xa = _moe(xa, *ffn, TOK_TILE, T // TOK_TILE, 1, 1)
    return _final_norm(xa, norm_final, T)
```

```python
import functools
import math

import jax
import jax.numpy as jnp
from jax import lax
from jax.experimental import pallas as pl
from jax.experimental.pallas import tpu as pltpu

F32 = jnp.float32
BF16 = jnp.bfloat16
LOG2E = 1.4426950408889634

GRID_W = 64
ROPE_THETA = 10000.0
NORM_EPS = 1e-6
NEG_INF = -1e30

MLA_HEADS = 8
MLA_Q_LORA = 384
MLA_KV_LORA = 256
MLA_NOPE = 64
MLA_ROPE = 32
MLA_V = 64
MLA_SCALE = (MLA_NOPE + MLA_ROPE) ** -0.5
MLA_IN_COLS = MLA_Q_LORA + MLA_KV_LORA + MLA_ROPE
MLA_PAD = 128

DIFF_HEADS = 8
DIFF_D = 32
DIFF_V = 2 * DIFF_D
DIFF_SCALE = DIFF_D ** -0.5

HEAD_DIM = 64
WIN_HEADS = 8
WIN_KV_HEADS = 2
WINDOW = 128
GLB_HEADS = 8
GLB_KV_HEADS = 2
ATTN_SCALE = HEAD_DIM ** -0.5

N_EXPERTS = 16
N_GROUPS = 4
EXPERTS_PER_GROUP = N_EXPERTS // N_GROUPS

TOK_TILE = 256
Q_TILE = 1024
KEY_TILE = 1024
PV_TILE = 256
MOE_TILE = 1024
ONES_ROWS = 16
LANES = 128
VMEM_LIMIT = 56 * 1024 * 1024


def _cparams(n_axes):
    return pltpu.CompilerParams(dimension_semantics=("arbitrary",) * n_axes,
                                vmem_limit_bytes=VMEM_LIMIT)


def _full(shape):
    return pl.BlockSpec(shape, lambda *_: (0,) * len(shape))


def _mod_kernel(a_ref, w_ref, b_ref, o_ref):
    a = a_ref[...]
    a = a * (1.0 / (1.0 + jnp.exp(-a)))
    o_ref[0] = jnp.dot(a.astype(BF16), w_ref[0].astype(BF16), preferred_element_type=F32) + b_ref[0]


def _mod_vectors(cond, w_mod, b_mod):
    L, D, N = w_mod.shape
    tn = 1536
    return pl.pallas_call(
        _mod_kernel,
        grid=(L, N // tn),
        in_specs=[pl.BlockSpec((8, D), lambda l, j: (0, 0)),
                  pl.BlockSpec((1, D, tn), lambda l, j: (l, 0, j)),
                  pl.BlockSpec((1, 1, tn), lambda l, j: (l, 0, j))],
        out_specs=pl.BlockSpec((1, 8, tn), lambda l, j: (l, 0, j)),
        out_shape=jax.ShapeDtypeStruct((L, 8, N), F32),
        compiler_params=_cparams(2),
        name="mod_vectors",
    )(cond, w_mod, b_mod.reshape(L, 1, N))


def _norm_mod(x, g, sh, sc):
    y = x * lax.rsqrt(jnp.mean(x * x, axis=-1, keepdims=True) + NORM_EPS)
    return (y * g) * (1.0 + sc) + sh


def _dot_t(w, h):
    return lax.dot_general(w, h, (((1,), (1,)), ((), ())), preferred_element_type=F32)


def _rms_rows(x):
    return lax.rsqrt(jnp.mean(x * x, axis=0, keepdims=True) + NORM_EPS)


def _rope_rows(x, cosT, sinT):
    q = x.shape[0] // 4
    rot = jnp.concatenate([-x[q:2 * q], x[0:q], -x[3 * q:4 * q], x[2 * q:3 * q]], axis=0)
    return x * cosT + rot * sinT


def _ones_rows(n):
    row = lax.broadcasted_iota(jnp.int32, (ONES_ROWS, n), 0)
    return jnp.where(row == 0, 1.0, 0.0).astype(BF16)


def _tile_lanes(x, reps):
    return x if reps == 1 else jnp.concatenate([x] * reps, axis=1)


def _even_prep_kernel(x_ref, g_ref, sh_ref, sc_ref, w1T_ref, wckv_ref, wkr_ref, wkrr_ref, wdk_ref, wdkr_ref,
                      wuqT_ref, wuvT_ref, wuk_ref, place_ref, gq_ref, gkvc_ref, gkvr_ref, cosT_ref, sinT_ref,
                      cosK_ref, sinK_ref,
                      qa_ref, ka_ref, va_ref, qd_ref, kd_ref, vd_ref):
    h = _norm_mod(x_ref[0], g_ref[...], sh_ref[0, 0], sc_ref[0, 0]).astype(BF16)
    n = h.shape[0]
    cosT, sinT, cosK, sinK = cosT_ref[...], sinT_ref[...], cosK_ref[...], sinK_ref[...]
    ones = _ones_rows(n)
    zT = _dot_t(w1T_ref[...], h)
    o1, o2 = MLA_Q_LORA, MLA_Q_LORA + MLA_KV_LORA
    o3 = o2 + 2 * DIFF_HEADS * DIFF_D

    c_q = zT[:o1]
    c_q = (c_q * _rms_rows(c_q) * gq_ref[...]).astype(BF16)
    qT = jnp.dot(wuqT_ref[...], c_q, preferred_element_type=F32)
    for hd in range(MLA_HEADS):
        r0 = hd * MLA_PAD
        rope = _rope_rows(qT[r0 + MLA_NOPE:r0 + MLA_NOPE + MLA_ROPE], cosT, sinT)
        head = jnp.concatenate([qT[r0:r0 + MLA_NOPE], rope, qT[r0 + MLA_NOPE + MLA_ROPE:r0 + MLA_PAD]], axis=0)
        qa_ref[hd] = (head * (MLA_SCALE * LOG2E)).astype(BF16)

    c_kvT = zT[o1:o2]
    c_kvT = (c_kvT * _rms_rows(c_kvT) * gkvc_ref[...]).astype(BF16)
    vT = jnp.dot(wuvT_ref[...], c_kvT, preferred_element_type=F32).astype(BF16)
    for hd in range(MLA_HEADS):
        va_ref[hd, 0:MLA_V, :] = vT[hd * MLA_V:(hd + 1) * MLA_V]
        va_ref[hd, MLA_V:, :] = ones

    for j in range(2 * DIFF_HEADS):
        qj = _rope_rows(zT[o2 + j * DIFF_D:o2 + (j + 1) * DIFF_D], cosT, sinT)
        qd_ref[j] = (qj * (DIFF_SCALE * LOG2E)).astype(BF16)
    for hd in range(DIFF_HEADS):
        vd_ref[hd, 0:DIFF_V, :] = zT[o3 + hd * DIFF_V:o3 + (hd + 1) * DIFF_V].astype(BF16)
        vd_ref[hd, DIFF_V:, :] = ones

    c_kv = jnp.dot(h, wckv_ref[...], preferred_element_type=F32)
    c_kv = (c_kv * lax.rsqrt(jnp.mean(c_kv * c_kv, axis=-1, keepdims=True) + NORM_EPS) * gkvr_ref[...]).astype(BF16)
    kr = (jnp.dot(h, wkr_ref[...], preferred_element_type=F32) * cosK
          + jnp.dot(h, wkrr_ref[...], preferred_element_type=F32) * sinK)
    ka = (jnp.dot(c_kv, wuk_ref[...], preferred_element_type=F32)
          + jnp.dot(kr.astype(BF16), place_ref[...], preferred_element_type=F32))
    ka_ref[...] = ka.astype(BF16)

    reps = wdk_ref.shape[1] // LANES
    kd = (jnp.dot(h, wdk_ref[...], preferred_element_type=F32) * _tile_lanes(cosK, reps)
          + jnp.dot(h, wdkr_ref[...], preferred_element_type=F32) * _tile_lanes(sinK, reps))
    kd_ref[...] = kd.astype(BF16)


def _rot_cols(w, width):
    d, n = w.shape
    w4 = w.reshape(d, n // width, 4, width // 4)
    return jnp.stack([-w4[:, :, 1], w4[:, :, 0], -w4[:, :, 3], w4[:, :, 2]], axis=2).reshape(d, n)


def _perm_cols(g, width):
    g4 = g.reshape(-1, 4, width // 4)
    return jnp.stack([g4[:, 1], g4[:, 0], g4[:, 3], g4[:, 2]], axis=1).reshape(-1)


def _even_prep(xa, p, mods, tabs):
    B, S, D = xa.shape
    w_in, w_uq, w_ukv = p["w_in"], p["w_uq"], p["w_ukv"]
    o1, o2, o3 = MLA_Q_LORA, MLA_Q_LORA + MLA_KV_LORA, MLA_IN_COLS
    nq = 2 * DIFF_HEADS * DIFF_D
    bf = lambda a: a.astype(BF16)
    w1T = bf(jnp.concatenate([w_in[:, :o2], w_in[:, o3:o3 + nq], w_in[:, o3 + 2 * nq:]], axis=1).T)
    wckv = bf(w_in[:, o1:o2])
    wkr = jnp.pad(w_in[:, o2:o3], ((0, 0), (0, LANES - MLA_ROPE)))
    wkrr = jnp.pad(_rot_cols(w_in[:, o2:o3], MLA_ROPE), ((0, 0), (0, LANES - MLA_ROPE)))
    wdk = w_in[:, o3 + nq:o3 + 2 * nq]
    wdkr = _rot_cols(wdk, DIFF_D)
    uq = w_uq.reshape(MLA_Q_LORA, MLA_HEADS, MLA_NOPE + MLA_ROPE)
    wuqT = bf(jnp.pad(uq, ((0, 0), (0, 0), (0, MLA_PAD - MLA_NOPE - MLA_ROPE))).reshape(MLA_Q_LORA, -1).T)
    ukv = w_ukv.reshape(MLA_KV_LORA, MLA_HEADS, MLA_NOPE + MLA_V)
    wuvT = bf(ukv[:, :, MLA_NOPE:].reshape(MLA_KV_LORA, -1).T)
    wuk = bf(jnp.pad(ukv[:, :, :MLA_NOPE], ((0, 0), (0, 0), (0, MLA_PAD - MLA_NOPE))).reshape(MLA_KV_LORA, -1))
    src = jnp.arange(LANES)[:, None]
    dst = jnp.arange(MLA_HEADS * MLA_PAD)[None, :]
    place = bf((src < MLA_ROPE) & (dst % MLA_PAD == src + MLA_NOPE))
    gq = jnp.broadcast_to(p["norm_q"].astype(F32)[:, None], (MLA_Q_LORA, TOK_TILE))
    gkvc = jnp.broadcast_to(p["norm_kv"].astype(F32)[:, None], (MLA_KV_LORA, TOK_TILE))
    gkvr = p["norm_kv"].astype(F32).reshape(1, MLA_KV_LORA)
    cosT, sinT, cosK, sinK = tabs
    weights = [w1T, wckv, bf(wkr), bf(wkrr), bf(wdk), bf(wdkr), wuqT, wuvT, wuk, place, gq, gkvc, gkvr]
    nt = S // TOK_TILE
    ctx_tile = nt - 1
    mod_map = lambda b, i: (b, (i == ctx_tile).astype(jnp.int32), 0, 0)
    HA, HD = MLA_HEADS, DIFF_HEADS
    dva = MLA_V + ONES_ROWS
    return pl.pallas_call(
        _even_prep_kernel,
        grid=(B, nt),
        in_specs=[pl.BlockSpec((1, TOK_TILE, D), lambda b, i: (b, i, 0)),
                  _full((1, D)),
                  pl.BlockSpec((1, 1, 1, D), mod_map),
                  pl.BlockSpec((1, 1, 1, D), mod_map)]
                 + [_full(w.shape) for w in weights]
                 + [pl.BlockSpec((MLA_ROPE, TOK_TILE), lambda b, i: (0, i)),
                    pl.BlockSpec((MLA_ROPE, TOK_TILE), lambda b, i: (0, i)),
                    pl.BlockSpec((TOK_TILE, LANES), lambda b, i: (i, 0)),
                    pl.BlockSpec((TOK_TILE, LANES), lambda b, i: (i, 0))],
        out_specs=[pl.BlockSpec((None, HA, MLA_PAD, TOK_TILE), lambda b, i: (b, 0, 0, i)),
                   pl.BlockSpec((None, TOK_TILE, HA * MLA_PAD), lambda b, i: (b, i, 0)),
                   pl.BlockSpec((None, HA, dva, TOK_TILE), lambda b, i: (b, 0, 0, i)),
                   pl.BlockSpec((None, 2 * HD, DIFF_D, TOK_TILE), lambda b, i: (b, 0, 0, i)),
                   pl.BlockSpec((None, TOK_TILE, nq), lambda b, i: (b, i, 0)),
                   pl.BlockSpec((None, HD, dva, TOK_TILE), lambda b, i: (b, 0, 0, i))],
        out_shape=[jax.ShapeDtypeStruct((B, HA, MLA_PAD, S), BF16),
                   jax.ShapeDtypeStruct((B, S, HA * MLA_PAD), BF16),
                   jax.ShapeDtypeStruct((B, HA, dva, S), BF16),
                   jax.ShapeDtypeStruct((B, 2 * HD, DIFF_D, S), BF16),
                   jax.ShapeDtypeStruct((B, S, nq), BF16),
                   jax.ShapeDtypeStruct((B, HD, dva, S), BF16)],
        compiler_params=_cparams(2),
        name="even_qkv_prep",
    )(xa, p["norm_mix"].reshape(1, D).astype(F32), mods["sh_a"], mods["sc_a"], *weights, cosT, sinT, cosK, sinK)


def _odd_prep_kernel(x_ref, g_ref, sh_ref, sc_ref, wqvT_ref, wk_ref, wkr_ref, gqn_ref, gk_ref, gkp_ref, bd_ref,
                     cosT_ref, sinT_ref, cosK_ref, sinK_ref,
                     qc_ref, qd_ref, k_ref, vc_ref, vd_ref):
    h = _norm_mod(x_ref[0], g_ref[...], sh_ref[0, 0], sc_ref[0, 0]).astype(BF16)
    n = h.shape[0]
    cosT, sinT, cosK, sinK = cosT_ref[...], sinT_ref[...], cosK_ref[...], sinK_ref[...]
    ones = _ones_rows(n)
    zT = _dot_t(wqvT_ref[...], h)
    d = HEAD_DIM
    for hd in range(WIN_HEADS):
        qc_ref[hd] = (_rope_rows(zT[hd * d:(hd + 1) * d], cosT, sinT) * (ATTN_SCALE * LOG2E)).astype(BF16)
    o1 = WIN_HEADS * d
    for hd in range(GLB_HEADS):
        q = zT[o1 + hd * d:o1 + (hd + 1) * d]
        q = _rope_rows(q * _rms_rows(q) * gqn_ref[...], cosT, sinT)
        qd_ref[hd] = (q * (ATTN_SCALE * LOG2E)).astype(BF16)
    o2 = o1 + GLB_HEADS * d
    for j in range(WIN_KV_HEADS):
        vc_ref[j, 0:d, :] = zT[o2 + j * d:o2 + (j + 1) * d].astype(BF16)
        vc_ref[j, d:, :] = ones
    o3 = o2 + WIN_KV_HEADS * d
    for j in range(GLB_KV_HEADS):
        vd_ref[j, 0:d, :] = zT[o3 + j * d:o3 + (j + 1) * d].astype(BF16)
        vd_ref[j, d:, :] = ones

    zk = jnp.dot(h, wk_ref[...], preferred_element_type=F32)
    zkr = jnp.dot(h, wkr_ref[...], preferred_element_type=F32)
    wc = WIN_KV_HEADS * d
    kc = zk[:, :wc] * cosK + zkr[:, :wc] * sinK
    z, zr = zk[:, wc:], zkr[:, wc:]
    sq = z * z
    hi = sq.astype(BF16)
    lo = (sq - hi.astype(F32)).astype(BF16)
    mean = (jnp.dot(hi, bd_ref[...], preferred_element_type=F32) + jnp.dot(lo, bd_ref[...], preferred_element_type=F32))
    kd = lax.rsqrt(mean + NORM_EPS) * (z * gk_ref[...] * cosK + zr * gkp_ref[...] * sinK)
    k_ref[...] = jnp.concatenate([kc, kd], axis=1).astype(BF16)


def _odd_prep(xa, p, mods, tabs):
    B, S, D = xa.shape
    w_in = p["w_in"]
    d = HEAD_DIM
    sizes = (WIN_HEADS, WIN_KV_HEADS, WIN_KV_HEADS, GLB_HEADS, GLB_KV_HEADS, GLB_KV_HEADS)
    offs = [0]
    for s in sizes:
        offs.append(offs[-1] + s * d)
    col = lambda j: w_in[:, offs[j]:offs[j + 1]]
    bf = lambda a: a.astype(BF16)
    wqvT = bf(jnp.concatenate([col(0), col(3), col(2), col(5)], axis=1).T)
    wk = jnp.concatenate([col(1), col(4)], axis=1)
    wkr = _rot_cols(wk, d)
    gqn = jnp.broadcast_to(p["q_norm"].astype(F32)[:, None], (d, TOK_TILE))
    gk1 = p["k_norm"].astype(F32)
    gk = jnp.tile(gk1, GLB_KV_HEADS).reshape(1, -1)
    gkp = jnp.tile(_perm_cols(gk1, d), GLB_KV_HEADS).reshape(1, -1)
    wd = GLB_KV_HEADS * d
    lane = jnp.arange(wd)
    bd = bf(jnp.where(lane[:, None] // d == lane[None, :] // d, 1.0 / d, 0.0))
    assert WIN_KV_HEADS * d == LANES and wd == LANES
    cosT, sinT, cosK, sinK = tabs
    weights = [wqvT, bf(wk), bf(wkr), gqn, gk, gkp, bd]
    nt = S // TOK_TILE
    ctx_tile = nt - 1
    mod_map = lambda b, i: (b, (i == ctx_tile).astype(jnp.int32), 0, 0)
    dva = d + ONES_ROWS
    return pl.pallas_call(
        _odd_prep_kernel,
        grid=(B, nt),
        in_specs=[pl.BlockSpec((1, TOK_TILE, D), lambda b, i: (b, i, 0)),
                  _full((1, D)),
                  pl.BlockSpec((1, 1, 1, D), mod_map),
                  pl.BlockSpec((1, 1, 1, D), mod_map)]
                 + [_full(w.shape) for w in weights]
                 + [pl.BlockSpec((d, TOK_TILE), lambda b, i: (0, i)),
                    pl.BlockSpec((d, TOK_TILE), lambda b, i: (0, i)),
                    pl.BlockSpec((TOK_TILE, LANES), lambda b, i: (i, 0)),
                    pl.BlockSpec((TOK_TILE, LANES), lambda b, i: (i, 0))],
        out_specs=[pl.BlockSpec((None, WIN_HEADS, d, TOK_TILE), lambda b, i: (b, 0, 0, i)),
                   pl.BlockSpec((None, GLB_HEADS, d, TOK_TILE), lambda b, i: (b, 0, 0, i)),
                   pl.BlockSpec((None, TOK_TILE, 2 * LANES), lambda b, i: (b, i, 0)),
                   pl.BlockSpec((None, WIN_KV_HEADS, dva, TOK_TILE), lambda b, i: (b, 0, 0, i)),
                   pl.BlockSpec((None, GLB_KV_HEADS, dva, TOK_TILE), lambda b, i: (b, 0, 0, i))],
        out_shape=[jax.ShapeDtypeStruct((B, WIN_HEADS, d, S), BF16),
                   jax.ShapeDtypeStruct((B, GLB_HEADS, d, S), BF16),
                   jax.ShapeDtypeStruct((B, S, 2 * LANES), BF16),
                   jax.ShapeDtypeStruct((B, WIN_KV_HEADS, dva, S), BF16),
                   jax.ShapeDtypeStruct((B, GLB_KV_HEADS, dva, S), BF16)],
        compiler_params=_cparams(2),
        name="odd_qkv_prep",
    )(xa, p["norm_mix"].reshape(1, D).astype(F32), mods["sh_a"], mods["sc_a"], *weights, cosT, sinT, cosK, sinK)


class _KeyPack:
    def __init__(self, kw, dk, block_of, slot_of):
        self.kw, self.dk, self.block_of, self.slot_of = kw, dk, block_of, slot_of


def _pad_queries(qT, pack, head):
    reps = pack.kw // qT.shape[0]
    if reps == 1:
        return qT
    rows = lax.broadcasted_iota(jnp.int32, (pack.kw, qT.shape[1]), 0)
    slot = pack.slot_of(head)
    keep = (rows >= slot * pack.dk) & (rows < (slot + 1) * pack.dk)
    return jnp.where(keep, jnp.concatenate([qT] * reps, axis=0), jnp.zeros((), qT.dtype))


def _scores(k_ref, qT, off, n, s_ref):
    s = jnp.dot(k_ref[pl.ds(off, n), :], qT, preferred_element_type=F32)
    s_ref[...] = s
    return jnp.max(s, axis=0, keepdims=True)


def _accumulate(s_ref, n, m, cmax, acc_ref, vT_ref, off):
    m_new = jnp.maximum(m, cmax)
    pv = None
    for kk in range(0, n, PV_TILE):
        p = jnp.exp2(s_ref[kk:kk + PV_TILE, :] - m_new).astype(BF16)
        start = off + kk if isinstance(off, int) else pl.multiple_of(off + kk, PV_TILE)
        d = jnp.dot(vT_ref[:, pl.ds(start, PV_TILE)], p, preferred_element_type=F32)
        pv = d if pv is None else pv + d
    acc_ref[...] = acc_ref[...] * jnp.exp2(m - m_new) + pv
    return m_new


def _softmax_init(init_ref, dva, dv, tq):
    m0 = jnp.broadcast_to(init_ref[0, 0:1, 0:1], (1, tq))
    row = lax.broadcasted_iota(jnp.int32, (dva, tq), 0)
    acc0 = jnp.where(row == dv, jnp.broadcast_to(init_ref[0, 1:2, 0:1], (dva, tq)), 0.0)
    return m0, acc0


def _flash_kernel(init_ref, qT_ref, qTn_ref, k_ref, vT_ref, o_ref, s0_ref, s1_ref, sc_ref, acc_ref, cm_ref, *,
                  T, kt, dv, pack):
    head = pl.program_id(1)
    qT = _pad_queries(qT_ref[...], pack, head)
    tq = qT.shape[1]
    dva = vT_ref.shape[0]
    n_ctx = sc_ref.shape[0]

    def first_scores(q):
        cm_ref[0:1, :] = _scores(k_ref, q, T, n_ctx, sc_ref)
        cm_ref[1:2, :] = _scores(k_ref, q, 0, kt, s0_ref)

    @pl.when(pl.program_id(2) == 0)
    def _():
        first_scores(qT)

    m, acc0 = _softmax_init(init_ref, dva, dv, tq)
    acc_ref[...] = acc0
    cm_c, cm_a = cm_ref[0:1, :], cm_ref[1:2, :]
    cm_b = _scores(k_ref, qT, kt, kt, s1_ref)
    m = _accumulate(sc_ref, n_ctx, m, cm_c, acc_ref, vT_ref, T)
    m = _accumulate(s0_ref, kt, m, cm_a, acc_ref, vT_ref, 0)

    def body(t, carry):
        m, cm_b = carry
        off = pl.multiple_of(t * (2 * kt), kt)
        cm_a = _scores(k_ref, qT, off + 2 * kt, kt, s0_ref)
        m = _accumulate(s1_ref, kt, m, cm_b, acc_ref, vT_ref, off + kt)
        cm_b = _scores(k_ref, qT, off + 3 * kt, kt, s1_ref)
        m = _accumulate(s0_ref, kt, m, cm_a, acc_ref, vT_ref, off + 2 * kt)
        return m, cm_b

    m, cm_b = lax.fori_loop(0, T // (2 * kt) - 1, body, (m, cm_b))
    first_scores(_pad_queries(qTn_ref[...], pack, head))
    m = _accumulate(s1_ref, kt, m, cm_b, acc_ref, vT_ref, T - kt)
    acc = acc_ref[...]
    o_ref[...] = acc[:dv] / acc[dv:dv + 1]


def _flash(qT, k, vT, init, T, pack):
    B, Hq, dk, S = qT.shape
    Hv, dva = vT.shape[1], vT.shape[2]
    dv = dva - ONES_ROWS
    gv = Hq // Hv
    kt = min(KEY_TILE, T // 2)
    tq = min(Q_TILE, T)
    assert T % (2 * kt) == 0 and T % tq == 0 and kt % PV_TILE == 0
    kern = functools.partial(_flash_kernel, T=T, kt=kt, dv=dv, pack=pack)
    nq = T // tq
    return pl.pallas_call(
        kern,
        grid=(B, Hq, nq),
        in_specs=[pl.BlockSpec((1, 2, 128), lambda b, h, i: (h, 0, 0)),
                  pl.BlockSpec((None, None, dk, tq), lambda b, h, i: (b, h, 0, i)),
                  pl.BlockSpec((None, None, dk, tq), lambda b, h, i: (b, h, 0, jnp.minimum(i + 1, nq - 1))),
                  pl.BlockSpec((None, S, pack.kw), lambda b, h, i: (b, 0, pack.block_of(h))),
                  pl.BlockSpec((None, None, dva, S), lambda b, h, i: (b, h // gv, 0, 0))],
        out_specs=pl.BlockSpec((None, None, dv, tq), lambda b, h, i: (b, h, 0, i)),
        out_shape=jax.ShapeDtypeStruct((B, Hq, dv, T), F32),
        scratch_shapes=[pltpu.VMEM((kt, tq), F32), pltpu.VMEM((kt, tq), F32), pltpu.VMEM((S - T, tq), F32),
                        pltpu.VMEM((dva, tq), F32), pltpu.VMEM((8, tq), F32)],
        compiler_params=_cparams(3),
        name="dense_attention",
    )(init, qT, qT, k, vT)


def _ctx_attn_kernel(init_ref, qT_ref, k_ref, vT_ref, o_ref, *, dv, pack):
    qT = _pad_queries(qT_ref[...], pack, pl.program_id(1))
    m0, acc0 = _softmax_init(init_ref, vT_ref.shape[0], dv, qT.shape[1])
    s = jnp.dot(k_ref[...], qT, preferred_element_type=F32)
    m = jnp.maximum(m0, jnp.max(s, axis=0, keepdims=True))
    acc = acc0 * jnp.exp2(m0 - m) + jnp.dot(vT_ref[...], jnp.exp2(s - m).astype(BF16), preferred_element_type=F32)
    o_ref[...] = acc[:dv] / acc[dv:dv + 1]


def _ctx_attn(qT, k, vT, init, T, pack):
    B, Hq, dk, S = qT.shape
    Hv, dva = vT.shape[1], vT.shape[2]
    dv = dva - ONES_ROWS
    gv = Hq // Hv
    n = S - T
    blk = T // n
    return pl.pallas_call(
        functools.partial(_ctx_attn_kernel, dv=dv, pack=pack),
        grid=(B, Hq),
        in_specs=[pl.BlockSpec((1, 2, 128), lambda b, h: (h, 0, 0)),
                  pl.BlockSpec((None, None, dk, n), lambda b, h: (b, h, 0, blk)),
                  pl.BlockSpec((None, n, pack.kw), lambda b, h: (b, blk, pack.block_of(h))),
                  pl.BlockSpec((None, None, dva, n), lambda b, h: (b, h // gv, 0, blk))],
        out_specs=pl.BlockSpec((None, None, dv, n), lambda b, h: (b, h, 0, 0)),
        out_shape=jax.ShapeDtypeStruct((B, Hq, dv, n), F32),
        compiler_params=_cparams(2),
        name="context_attention",
    )(init, qT, k, vT)


def _window_kernel(sink_ref, qT_ref, k_ref, kc_ref, vT_ref, o_ref, *, T, dv, pack):
    i = pl.program_id(2)
    G, _, tq = qT_ref.shape
    head = pl.program_id(1) * G
    qT = jnp.concatenate([_pad_queries(qT_ref[g], pack, head) for g in range(G)], axis=1)
    snk = jnp.concatenate([jnp.broadcast_to(sink_ref[g, 0:1, 0:1], (1, tq)) for g in range(G)], axis=1)
    band = 3 * tq
    kband = jnp.concatenate([k_ref[0][...], k_ref[1][...], k_ref[2][...]], axis=0)
    s_loc = jnp.dot(kband, qT, preferred_element_type=F32)
    r = lax.broadcasted_iota(jnp.int32, (band, G * tq), 0)
    c = lax.broadcasted_iota(jnp.int32, (band, G * tq), 1) & (tq - 1)
    kpos = r + (i - 1) * tq
    valid = (jnp.abs(r - tq - c) <= WINDOW) & (kpos >= 0) & (kpos < T)
    s_loc = jnp.where(valid, s_loc, NEG_INF)
    s_ctx = jnp.dot(kc_ref[...], qT, preferred_element_type=F32)
    m = jnp.maximum(jnp.maximum(jnp.max(s_loc, axis=0, keepdims=True), jnp.max(s_ctx, axis=0, keepdims=True)), snk)
    p_loc = jnp.exp2(s_loc - m).astype(BF16)
    p_ctx = jnp.exp2(s_ctx - m).astype(BF16)
    vband = jnp.concatenate([vT_ref[0][...], vT_ref[1][...], vT_ref[2][...]], axis=1)
    acc = (jnp.dot(vband, p_loc, preferred_element_type=F32)
           + jnp.dot(vT_ref[3][...], p_ctx, preferred_element_type=F32))
    o = acc[:dv] / (acc[dv:dv + 1] + jnp.exp2(snk - m))
    for g in range(G):
        o_ref[g] = o[:, g * tq:(g + 1) * tq]


def _window(qT, k, vT, sink, T, pack):
    B, Hq, dk, S = qT.shape
    Hk, dva = vT.shape[1], vT.shape[2]
    dv = dva - ONES_ROWS
    g = Hq // Hk
    tq = TOK_TILE
    nt = T // tq
    ctx_blk = T // tq
    clip = lambda j: jnp.clip(j, 0, nt - 1)
    kern = functools.partial(_window_kernel, T=T, dv=dv, pack=pack)
    kspec = lambda f: pl.BlockSpec((None, tq, pack.kw), lambda b, h, i: (b, f(i), pack.block_of(h * g)))
    vspec = lambda f: pl.BlockSpec((None, None, dva, tq), lambda b, h, i: (b, h, 0, f(i)))

    def body(sink_ref, qT_ref, k0, k1, k2, kc, v0, v1, v2, vc, o_ref):
        kern(sink_ref, qT_ref, (k0, k1, k2), kc, (v0, v1, v2, vc), o_ref)

    return pl.pallas_call(
        body,
        grid=(B, Hk, nt),
        in_specs=[pl.BlockSpec((g, 1, 128), lambda b, h, i: (h, 0, 0)),
                  pl.BlockSpec((None, g, dk, tq), lambda b, h, i: (b, h, 0, i)),
                  kspec(lambda i: clip(i - 1)), kspec(lambda i: i), kspec(lambda i: clip(i + 1)),
                  kspec(lambda i: ctx_blk),
                  vspec(lambda i: clip(i - 1)), vspec(lambda i: i), vspec(lambda i: clip(i + 1)),
                  vspec(lambda i: ctx_blk)],
        out_specs=pl.BlockSpec((None, g, dv, tq), lambda b, h, i: (b, h, 0, i)),
        out_shape=jax.ShapeDtypeStruct((B, Hq, dv, T), F32),
        compiler_params=_cparams(3),
        name="window_attention",
    )(sink, qT, k, k, k, k, vT, vT, vT, vT)


def _project_out(x_ref, oT, w_ref, gate_ref, y_ref):
    y = jnp.dot(oT.T.astype(BF16), w_ref[...], preferred_element_type=F32)
    y_ref[0] = x_ref[0] + gate_ref[0, 0] * y


def _even_out_kernel(x_ref, oa_ref, od_ref, w_ref, gate_ref, lam_ref, sub_ref, y_ref, *, post_scale):
    lam = lam_ref[...]
    parts = [oa_ref[hd] for hd in range(MLA_HEADS)]
    for hd in range(DIFF_HEADS):
        diff = od_ref[2 * hd] - lam * od_ref[2 * hd + 1]
        parts.append(diff * _rms_rows(diff) * sub_ref[...] * post_scale)
    _project_out(x_ref, jnp.concatenate(parts, axis=0), w_ref, gate_ref, y_ref)


def _odd_out_kernel(x_ref, oc_ref, od_ref, w_ref, gate_ref, y_ref):
    parts = [oc_ref[hd] for hd in range(WIN_HEADS)] + [od_ref[hd] for hd in range(GLB_HEADS)]
    _project_out(x_ref, jnp.concatenate(parts, axis=0), w_ref, gate_ref, y_ref)


def _out_proj(kern, x, heads, w, gate, extra, tile0, n_tiles, kind):
    B, S, D = x.shape
    row_map = lambda b, i: (b, i + tile0, 0)
    return pl.pallas_call(
        kern,
        grid=(B, n_tiles),
        in_specs=[pl.BlockSpec((1, TOK_TILE, D), row_map)]
                 + [pl.BlockSpec((None,) + o.shape[1:3] + (TOK_TILE,), lambda b, i: (b, 0, 0, i)) for o in heads]
                 + [_full(w.shape), pl.BlockSpec((1, 1, 1, D), lambda b, i: (b, kind, 0, 0))]
                 + [_full(e.shape) for e in extra],
        out_specs=pl.BlockSpec((1, TOK_TILE, D), row_map),
        out_shape=jax.ShapeDtypeStruct((B, S, D), F32),
        input_output_aliases={0: 0},
        compiler_params=_cparams(2),
        name="merge_out_proj_residual",
    )(x, *heads, w, gate, *extra)


def _first_argmax(vals):
    best, idx = vals[0], jnp.zeros(vals[0].shape, jnp.int32)
    for j in range(1, len(vals)):
        better = vals[j] > best
        idx = jnp.where(better, j, idx)
        best = jnp.where(better, vals[j], best)
    return idx, best


def _pick(idx, vals):
    out = vals[0]
    for j in range(1, len(vals)):
        out = jnp.where(idx == j, vals[j], out)
    return out


def _route(logits, bias):
    s = 1.0 / (1.0 + jnp.exp(-logits))
    sel = s + bias
    srow = [s[e:e + 1] for e in range(N_EXPERTS)]
    row = [sel[e:e + 1] for e in range(N_EXPERTS)]
    scores = []
    for g in range(N_GROUPS):
        a, b, c, d = row[4 * g:4 * g + 4]
        hi1, lo1, hi2, lo2 = jnp.maximum(a, b), jnp.minimum(a, b), jnp.maximum(c, d), jnp.minimum(c, d)
        top1 = jnp.maximum(hi1, hi2)
        top2 = jnp.maximum(jnp.maximum(lo1, lo2), jnp.minimum(hi1, hi2))
        scores.append(top1 + top2)
    gi, _ = _first_argmax(scores)
    v = [_pick(gi, [row[4 * g + j] for g in range(N_GROUPS)]) for j in range(EXPERTS_PER_GROUP)]
    sv = [_pick(gi, [srow[4 * g + j] for g in range(N_GROUPS)]) for j in range(EXPERTS_PER_GROUP)]
    i1, _ = _first_argmax(v)
    i2, _ = _first_argmax([jnp.where(i1 == j, -jnp.inf, v[j]) for j in range(EXPERTS_PER_GROUP)])
    w1, w2 = _pick(i1, sv), _pick(i2, sv)
    tot = w1 + w2
    w1, w2 = w1 / tot, w2 / tot
    rows = []
    for e in range(N_EXPERTS):
        g, j = divmod(e, EXPERTS_PER_GROUP)
        in_g = gi == g
        rows.append(jnp.where(in_g & (i1 == j), w1, 0.0) + jnp.where(in_g & (i2 == j), w2, 0.0))
    return jnp.concatenate(rows, axis=0)


def _moe_kernel(x_ref, g_ref, sh_ref, sc_ref, gate_ref, wrT_ref, br_ref, wg_ref, wu_ref, wd_ref, y_ref,
                h_scr, comb_scr, acc_scr):
    e = pl.program_id(2)
    tm = x_ref.shape[1]

    @pl.when(e == 0)
    def _():
        h = _norm_mod(x_ref[0], g_ref[...], sh_ref[0, 0], sc_ref[0, 0]).astype(BF16)
        h_scr[...] = h
        comb = _route(_dot_t(wrT_ref[...], h), br_ref[...])
        comb = jnp.concatenate([comb, jnp.zeros((LANES - N_EXPERTS, tm), F32)], axis=0)
        comb_scr[...] = comb.T
        acc_scr[...] = jnp.zeros_like(acc_scr)

    h = h_scr[...]
    a = jnp.dot(h, wg_ref[0], preferred_element_type=F32)
    u = jnp.dot(h, wu_ref[0], preferred_element_type=F32)
    act = (a * (1.0 / (1.0 + jnp.exp(-a)))) * u
    y = jnp.dot(act.astype(BF16), wd_ref[0], preferred_element_type=F32)
    lane = lax.broadcasted_iota(jnp.int32, (tm, LANES), 1)
    c = jnp.sum(jnp.where(lane == e, comb_scr[...], 0.0), axis=1, keepdims=True)
    acc_scr[...] += c * y

    @pl.when(e == N_EXPERTS - 1)
    def _():
        y_ref[0] = x_ref[0] + gate_ref[0, 0] * acc_scr[...]


def _moe(x, g, sh, sc, gate, wrT, br, wg, wu, wd, tm, tile0, n_tiles, kind):
    B, S, D = x.shape
    E, _, F = wg.shape
    mod_map = lambda b, i, e: (b, kind, 0, 0)
    row_map = lambda b, i, e: (b, i + tile0, 0)
    return pl.pallas_call(
        _moe_kernel,
        grid=(B, n_tiles, E),
        in_specs=[pl.BlockSpec((1, tm, D), row_map),
                  pl.BlockSpec((1, D), lambda b, i, e: (0, 0)),
                  pl.BlockSpec((1, 1, 1, D), mod_map),
                  pl.BlockSpec((1, 1, 1, D), mod_map),
                  pl.BlockSpec((1, 1, 1, D), mod_map),
                  pl.BlockSpec((E, D), lambda b, i, e: (0, 0)),
                  pl.BlockSpec((E, 1), lambda b, i, e: (0, 0)),
                  pl.BlockSpec((1, D, F), lambda b, i, e: (e, 0, 0)),
                  pl.BlockSpec((1, D, F), lambda b, i, e: (e, 0, 0)),
                  pl.BlockSpec((1, F, D), lambda b, i, e: (e, 0, 0))],
        out_specs=pl.BlockSpec((1, tm, D), row_map),
        out_shape=jax.ShapeDtypeStruct((B, S, D), F32),
        scratch_shapes=[pltpu.VMEM((tm, D), BF16), pltpu.VMEM((tm, LANES), F32), pltpu.VMEM((tm, D), F32)],
        input_output_aliases={0: 0},
        compiler_params=_cparams(3),
        name="moe_experts",
    )(x, g.reshape(1, D).astype(F32), sh, sc, gate, wrT, br, wg, wu, wd)


def _final_norm_kernel(x_ref, g_ref, o_ref):
    x = x_ref[0]
    o_ref[0] = (x * lax.rsqrt(jnp.mean(x * x, axis=-1, keepdims=True) + NORM_EPS)) * g_ref[...]


def _final_norm(x, g, T):
    B, S, D = x.shape
    return pl.pallas_call(
        _final_norm_kernel,
        grid=(B, T // TOK_TILE),
        in_specs=[pl.BlockSpec((1, TOK_TILE, D), lambda b, i: (b, i, 0)),
                  pl.BlockSpec((1, D), lambda b, i: (0, 0))],
        out_specs=pl.BlockSpec((1, TOK_TILE, D), lambda b, i: (b, i, 0)),
        out_shape=jax.ShapeDtypeStruct((B, T, D), F32),
        compiler_params=_cparams(2),
        name="final_norm",
    )(x, g.reshape(1, D).astype(F32))


def _axial_tables(T, n_ctx, rot_dim):
    half = rot_dim // 2
    inv_freq = ROPE_THETA ** (-jnp.arange(0, half, 2, dtype=F32) / half)
    rows = T // GRID_W
    row = jnp.broadcast_to(jnp.arange(rows, dtype=F32)[:, None], (rows, GRID_W)).reshape(-1)
    col = jnp.broadcast_to(jnp.arange(GRID_W, dtype=F32)[None, :], (rows, GRID_W)).reshape(-1)

    def ang(pos):
        a = pos[:, None] * inv_freq[None, :]
        return jnp.concatenate([a, a], axis=-1)

    a = jnp.concatenate([ang(row), ang(col)], axis=-1)
    cos = jnp.concatenate([jnp.cos(a), jnp.ones((n_ctx, rot_dim), F32)], axis=0)
    sin = jnp.concatenate([jnp.sin(a), jnp.zeros((n_ctx, rot_dim), F32)], axis=0)
    reps = LANES // rot_dim
    return cos.T, sin.T, jnp.tile(cos, (1, reps)), jnp.tile(sin, (1, reps))


def _plain_init(n_heads):
    return jnp.broadcast_to(jnp.array([NEG_INF, 0.0], F32)[None, :, None], (n_heads, 2, 128))


def _sink_init(sink):
    s = sink.astype(F32) * LOG2E
    return jnp.broadcast_to(jnp.stack([s, jnp.ones_like(s)], axis=1)[:, :, None], (s.shape[0], 2, 128))


def _even_mixer(xa, p, mods, T, with_ctx, lam_init, tabs):
    B, S, D = xa.shape
    qa, ka, va, qd, kd, vd = _even_prep(xa, p, mods, tabs)
    pack_a = _KeyPack(MLA_PAD, MLA_PAD, lambda h: h, lambda h: 0)
    per = LANES // DIFF_D
    pack_d = _KeyPack(LANES, DIFF_D, lambda h: h // per, lambda h: h % per)
    init_a, init_d = _plain_init(MLA_HEADS), _plain_init(2 * DIFF_HEADS)
    lam = p["lam"].astype(F32)
    lam_full = (jnp.exp(jnp.sum(lam[0] * lam[1])) - jnp.exp(jnp.sum(lam[2] * lam[3])) + lam_init).reshape(1, 1)
    sub = jnp.broadcast_to(p["subln"].astype(F32)[:, None], (DIFF_V, TOK_TILE))
    kern = functools.partial(_even_out_kernel, post_scale=1.0 - lam_init)
    oa = _flash(qa, ka, va, init_a, T, pack_a)
    od = _flash(qd, kd, vd, init_d, T, pack_d)
    xa = _out_proj(kern, xa, [oa, od], p["w_out"], mods["g_a"], [lam_full, sub], 0, T // TOK_TILE, 0)
    if with_ctx:
        oa = _ctx_attn(qa, ka, va, init_a, T, pack_a)
        od = _ctx_attn(qd, kd, vd, init_d, T, pack_d)
        xa = _out_proj(kern, xa, [oa, od], p["w_out"], mods["g_a"], [lam_full, sub], T // TOK_TILE, 1, 1)
    return xa


def _odd_mixer(xa, p, mods, T, with_ctx, tabs):
    B, S, D = xa.shape
    qc, qd, k, vc, vd = _odd_prep(xa, p, mods, tabs)
    gw, gd = WIN_HEADS // WIN_KV_HEADS, GLB_HEADS // GLB_KV_HEADS
    pack_c = _KeyPack(2 * LANES, HEAD_DIM, lambda h: 0, lambda h: h // gw)
    pack_d = _KeyPack(2 * LANES, HEAD_DIM, lambda h: 0, lambda h: WIN_KV_HEADS + h // gd)
    init_d = _plain_init(GLB_HEADS)
    sink = p["sink"].astype(F32) * LOG2E
    sink_w = jnp.broadcast_to(sink[:, None, None], (WIN_HEADS, 1, 128))
    od = _flash(qd, k, vd, init_d, T, pack_d)
    oc = _window(qc, k, vc, sink_w, T, pack_c)
    xa = _out_proj(_odd_out_kernel, xa, [oc, od], p["w_out"], mods["g_a"], [], 0, T // TOK_TILE, 0)
    if with_ctx:
        oc = _ctx_attn(qc, k, vc, _sink_init(p["sink"]), T, pack_c)
        od = _ctx_attn(qd, k, vd, init_d, T, pack_d)
        xa = _out_proj(_odd_out_kernel, xa, [oc, od], p["w_out"], mods["g_a"], [], T // TOK_TILE, 1, 1)
    return xa


def kernel(x, c, ctx, c_ctx, w_mod, b_mod, norm_mix, norm_ffn, even_w_in, even_norm_q, even_norm_kv, even_w_uq, even_w_ukv, even_lambda, even_subln, even_w_out, odd_w_in, odd_sink, odd_q_norm, odd_k_norm, odd_w_out, w_router, b_router, w_gate, w_up, w_down, norm_final):
    B, T, D = x.shape
    n_ctx = ctx.shape[1]
    depth = w_mod.shape[0]
    assert n_ctx == TOK_TILE == PV_TILE and T % MOE_TILE == 0 and B <= 7
    assert MLA_ROPE == DIFF_D
    S = T + n_ctx
    tabs32 = _axial_tables(T, n_ctx, MLA_ROPE)
    tabs64 = _axial_tables(T, n_ctx, HEAD_DIM)

    cond = jnp.zeros((8, D), F32).at[:B].set(c.astype(F32)).at[B].set(c_ctx.astype(F32))
    mod_all = _mod_vectors(cond, w_mod, b_mod)
    wrT = w_router.astype(BF16).T
    br = b_router.astype(F32).reshape(N_EXPERTS, 1)

    xa = jnp.concatenate([x, ctx], axis=1).astype(F32)
    for l in range(depth):
        with_ctx = l < depth - 1
        i = l // 2
        lat = mod_all[l, :B].reshape(B, 6, D)
        cx = jnp.broadcast_to(mod_all[l, B].reshape(1, 6, D), (B, 6, D))
        both = jnp.stack([lat, cx], axis=1)
        names = ("sh_a", "sc_a", "g_a", "sh_f", "sc_f", "g_f")
        mods = {n: both[:, :, j:j + 1, :] for j, n in enumerate(names)}
        if l % 2 == 0:
            lam_init = 0.8 - 0.6 * math.exp(-0.3 * l)
            p = dict(norm_mix=norm_mix[l], w_in=even_w_in[i], norm_q=even_norm_q[i], norm_kv=even_norm_kv[i],
                     w_uq=even_w_uq[i], w_ukv=even_w_ukv[i], lam=even_lambda[i], subln=even_subln[i],
                     w_out=even_w_out[i].astype(BF16))
            xa = _even_mixer(xa, p, mods, T, with_ctx, lam_init, tabs32)
        else:
            p = dict(norm_mix=norm_mix[l], w_in=odd_w_in[i], sink=odd_sink[i], q_norm=odd_q_norm[i],
                     k_norm=odd_k_norm[i], w_out=odd_w_out[i].astype(BF16))
            xa = _odd_mixer(xa, p, mods, T, with_ctx, tabs64)
        wg, wu, wd = w_gate[l].astype(BF16), w_up[l].astype(BF16), w_down[l].astype(BF16)
        ffn = (norm_ffn[l], mods["sh_f"], mods["sc_f"], mods["g_f"], wrT, br, wg, wu, wd)
        xa = _moe(xa, *ffn, MOE_TILE, 0, T // MOE_TILE, 0)
        if with_ctx:
            xa = _moe(xa, *ffn, TOK_TILE, T // TOK_TILE, 1, 1)
    return _final_norm(xa, norm_final, T)
```

```python
import functools
import math

import jax
import jax.numpy as jnp
from jax import lax
from jax.experimental import pallas as pl
from jax.experimental.pallas import tpu as pltpu

F32 = jnp.float32
BF16 = jnp.bfloat16
LOG2E = 1.4426950408889634

GRID_W = 64
ROPE_THETA = 10000.0
NORM_EPS = 1e-6
NEG_INF = -1e30

MLA_HEADS = 8
MLA_Q_LORA = 384
MLA_KV_LORA = 256
MLA_NOPE = 64
MLA_ROPE = 32
MLA_V = 64
MLA_SCALE = (MLA_NOPE + MLA_ROPE) ** -0.5
MLA_IN_COLS = MLA_Q_LORA + MLA_KV_LORA + MLA_ROPE
MLA_PAD = 128

DIFF_HEADS = 8
DIFF_D = 32
DIFF_V = 2 * DIFF_D
DIFF_SCALE = DIFF_D ** -0.5

HEAD_DIM = 64
WIN_HEADS = 8
WIN_KV_HEADS = 2
WINDOW = 128
GLB_HEADS = 8
GLB_KV_HEADS = 2
ATTN_SCALE = HEAD_DIM ** -0.5

N_EXPERTS = 16
N_GROUPS = 4
EXPERTS_PER_GROUP = N_EXPERTS // N_GROUPS

TOK_TILE = 256
Q_TILE = 1024
KEY_TILE = 1024
PV_TILE = 256
MOE_TILE = 1024
MOE_CHUNK = 128
COMB_TERMS = 3
ONES_ROWS = 16
LANES = 128
VMEM_LIMIT = 56 * 1024 * 1024


def _cparams(n_axes):
    return pltpu.CompilerParams(dimension_semantics=("arbitrary",) * n_axes,
                                vmem_limit_bytes=VMEM_LIMIT)


def _full(shape):
    return pl.BlockSpec(shape, lambda *_: (0,) * len(shape))


def _mod_kernel(a_ref, w_ref, b_ref, o_ref):
    a = a_ref[...]
    a = a * (1.0 / (1.0 + jnp.exp(-a)))
    o_ref[0] = jnp.dot(a.astype(BF16), w_ref[0].astype(BF16), preferred_element_type=F32) + b_ref[0]


def _mod_vectors(cond, w_mod, b_mod):
    L, D, N = w_mod.shape
    tn = 1536
    return pl.pallas_call(
        _mod_kernel,
        grid=(L, N // tn),
        in_specs=[pl.BlockSpec((8, D), lambda l, j: (0, 0)),
                  pl.BlockSpec((1, D, tn), lambda l, j: (l, 0, j)),
                  pl.BlockSpec((1, 1, tn), lambda l, j: (l, 0, j))],
        out_specs=pl.BlockSpec((1, 8, tn), lambda l, j: (l, 0, j)),
        out_shape=jax.ShapeDtypeStruct((L, 8, N), F32),
        compiler_params=_cparams(2),
        name="mod_vectors",
    )(cond, w_mod, b_mod.reshape(L, 1, N))


def _norm_mod(x, g, sh, sc):
    y = x * lax.rsqrt(jnp.mean(x * x, axis=-1, keepdims=True) + NORM_EPS)
    return (y * g) * (1.0 + sc) + sh


def _dot_t(w, h):
    return lax.dot_general(w, h, (((1,), (1,)), ((), ())), preferred_element_type=F32)


def _rms_rows(x):
    return lax.rsqrt(jnp.mean(x * x, axis=0, keepdims=True) + NORM_EPS)


def _rope_rows(x, cosT, sinT):
    q = x.shape[0] // 4
    rot = jnp.concatenate([-x[q:2 * q], x[0:q], -x[3 * q:4 * q], x[2 * q:3 * q]], axis=0)
    return x * cosT + rot * sinT


def _ones_rows(n):
    row = lax.broadcasted_iota(jnp.int32, (ONES_ROWS, n), 0)
    return jnp.where(row == 0, 1.0, 0.0).astype(BF16)


def _tile_lanes(x, reps):
    return x if reps == 1 else jnp.concatenate([x] * reps, axis=1)


def _even_prep_kernel(x_ref, g_ref, sh_ref, sc_ref, w1T_ref, wckv_ref, wkr_ref, wkrr_ref, wdk_ref, wdkr_ref,
                      wuqT_ref, wuvT_ref, wuk_ref, place_ref, gq_ref, gkvc_ref, gkvr_ref, cosT_ref, sinT_ref,
                      cosK_ref, sinK_ref,
                      qa_ref, ka_ref, va_ref, qd_ref, kd_ref, vd_ref):
    h = _norm_mod(x_ref[0], g_ref[...], sh_ref[0, 0], sc_ref[0, 0]).astype(BF16)
    n = h.shape[0]
    cosT, sinT, cosK, sinK = cosT_ref[...], sinT_ref[...], cosK_ref[...], sinK_ref[...]
    ones = _ones_rows(n)
    zT = _dot_t(w1T_ref[...], h)
    o1, o2 = MLA_Q_LORA, MLA_Q_LORA + MLA_KV_LORA
    o3 = o2 + 2 * DIFF_HEADS * DIFF_D

    c_q = zT[:o1]
    c_q = (c_q * _rms_rows(c_q) * gq_ref[...]).astype(BF16)
    qT = jnp.dot(wuqT_ref[...], c_q, preferred_element_type=F32)
    for hd in range(MLA_HEADS):
        r0 = hd * MLA_PAD
        rope = _rope_rows(qT[r0 + MLA_NOPE:r0 + MLA_NOPE + MLA_ROPE], cosT, sinT)
        head = jnp.concatenate([qT[r0:r0 + MLA_NOPE], rope, qT[r0 + MLA_NOPE + MLA_ROPE:r0 + MLA_PAD]], axis=0)
        qa_ref[hd] = (head * (MLA_SCALE * LOG2E)).astype(BF16)

    c_kvT = zT[o1:o2]
    c_kvT = (c_kvT * _rms_rows(c_kvT) * gkvc_ref[...]).astype(BF16)
    vT = jnp.dot(wuvT_ref[...], c_kvT, preferred_element_type=F32).astype(BF16)
    for hd in range(MLA_HEADS):
        va_ref[hd, 0:MLA_V, :] = vT[hd * MLA_V:(hd + 1) * MLA_V]
        va_ref[hd, MLA_V:, :] = ones

    for j in range(2 * DIFF_HEADS):
        qj = _rope_rows(zT[o2 + j * DIFF_D:o2 + (j + 1) * DIFF_D], cosT, sinT)
        qd_ref[j] = (qj * (DIFF_SCALE * LOG2E)).astype(BF16)
    for hd in range(DIFF_HEADS):
        vd_ref[hd, 0:DIFF_V, :] = zT[o3 + hd * DIFF_V:o3 + (hd + 1) * DIFF_V].astype(BF16)
        vd_ref[hd, DIFF_V:, :] = ones

    c_kv = jnp.dot(h, wckv_ref[...], preferred_element_type=F32)
    c_kv = (c_kv * lax.rsqrt(jnp.mean(c_kv * c_kv, axis=-1, keepdims=True) + NORM_EPS) * gkvr_ref[...]).astype(BF16)
    kr = (jnp.dot(h, wkr_ref[...], preferred_element_type=F32) * cosK
          + jnp.dot(h, wkrr_ref[...], preferred_element_type=F32) * sinK)
    ka = (jnp.dot(c_kv, wuk_ref[...], preferred_element_type=F32)
          + jnp.dot(kr.astype(BF16), place_ref[...], preferred_element_type=F32))
    ka_ref[...] = ka.astype(BF16)

    reps = wdk_ref.shape[1] // LANES
    kd = (jnp.dot(h, wdk_ref[...], preferred_element_type=F32) * _tile_lanes(cosK, reps)
          + jnp.dot(h, wdkr_ref[...], preferred_element_type=F32) * _tile_lanes(sinK, reps))
    kd_ref[...] = kd.astype(BF16)


def _rot_cols(w, width):
    d, n = w.shape
    w4 = w.reshape(d, n // width, 4, width // 4)
    return jnp.stack([-w4[:, :, 1], w4[:, :, 0], -w4[:, :, 3], w4[:, :, 2]], axis=2).reshape(d, n)


def _perm_cols(g, width):
    g4 = g.reshape(-1, 4, width // 4)
    return jnp.stack([g4[:, 1], g4[:, 0], g4[:, 3], g4[:, 2]], axis=1).reshape(-1)


def _even_prep(xa, p, mods, tabs):
    B, S, D = xa.shape
    w_in, w_uq, w_ukv = p["w_in"], p["w_uq"], p["w_ukv"]
    o1, o2, o3 = MLA_Q_LORA, MLA_Q_LORA + MLA_KV_LORA, MLA_IN_COLS
    nq = 2 * DIFF_HEADS * DIFF_D
    bf = lambda a: a.astype(BF16)
    w1T = bf(jnp.concatenate([w_in[:, :o2], w_in[:, o3:o3 + nq], w_in[:, o3 + 2 * nq:]], axis=1).T)
    wckv = bf(w_in[:, o1:o2])
    wkr = jnp.pad(w_in[:, o2:o3], ((0, 0), (0, LANES - MLA_ROPE)))
    wkrr = jnp.pad(_rot_cols(w_in[:, o2:o3], MLA_ROPE), ((0, 0), (0, LANES - MLA_ROPE)))
    wdk = w_in[:, o3 + nq:o3 + 2 * nq]
    wdkr = _rot_cols(wdk, DIFF_D)
    uq = w_uq.reshape(MLA_Q_LORA, MLA_HEADS, MLA_NOPE + MLA_ROPE)
    wuqT = bf(jnp.pad(uq, ((0, 0), (0, 0), (0, MLA_PAD - MLA_NOPE - MLA_ROPE))).reshape(MLA_Q_LORA, -1).T)
    ukv = w_ukv.reshape(MLA_KV_LORA, MLA_HEADS, MLA_NOPE + MLA_V)
    wuvT = bf(ukv[:, :, MLA_NOPE:].reshape(MLA_KV_LORA, -1).T)
    wuk = bf(jnp.pad(ukv[:, :, :MLA_NOPE], ((0, 0), (0, 0), (0, MLA_PAD - MLA_NOPE))).reshape(MLA_KV_LORA, -1))
    src = jnp.arange(LANES)[:, None]
    dst = jnp.arange(MLA_HEADS * MLA_PAD)[None, :]
    place = bf((src < MLA_ROPE) & (dst % MLA_PAD == src + MLA_NOPE))
    gq = jnp.broadcast_to(p["norm_q"].astype(F32)[:, None], (MLA_Q_LORA, TOK_TILE))
    gkvc = jnp.broadcast_to(p["norm_kv"].astype(F32)[:, None], (MLA_KV_LORA, TOK_TILE))
    gkvr = p["norm_kv"].astype(F32).reshape(1, MLA_KV_LORA)
    cosT, sinT, cosK, sinK = tabs
    weights = [w1T, wckv, bf(wkr), bf(wkrr), bf(wdk), bf(wdkr), wuqT, wuvT, wuk, place, gq, gkvc, gkvr]
    nt = S // TOK_TILE
    ctx_tile = nt - 1
    mod_map = lambda b, i: (b, (i == ctx_tile).astype(jnp.int32), 0, 0)
    HA, HD = MLA_HEADS, DIFF_HEADS
    dva = MLA_V + ONES_ROWS
    return pl.pallas_call(
        _even_prep_kernel,
        grid=(B, nt),
        in_specs=[pl.BlockSpec((1, TOK_TILE, D), lambda b, i: (b, i, 0)),
                  _full((1, D)),
                  pl.BlockSpec((1, 1, 1, D), mod_map),
                  pl.BlockSpec((1, 1, 1, D), mod_map)]
                 + [_full(w.shape) for w in weights]
                 + [pl.BlockSpec((MLA_ROPE, TOK_TILE), lambda b, i: (0, i)),
                    pl.BlockSpec((MLA_ROPE, TOK_TILE), lambda b, i: (0, i)),
                    pl.BlockSpec((TOK_TILE, LANES), lambda b, i: (i, 0)),
                    pl.BlockSpec((TOK_TILE, LANES), lambda b, i: (i, 0))],
        out_specs=[pl.BlockSpec((None, HA, MLA_PAD, TOK_TILE), lambda b, i: (b, 0, 0, i)),
                   pl.BlockSpec((None, TOK_TILE, HA * MLA_PAD), lambda b, i: (b, i, 0)),
                   pl.BlockSpec((None, HA, dva, TOK_TILE), lambda b, i: (b, 0, 0, i)),
                   pl.BlockSpec((None, 2 * HD, DIFF_D, TOK_TILE), lambda b, i: (b, 0, 0, i)),
                   pl.BlockSpec((None, TOK_TILE, nq), lambda b, i: (b, i, 0)),
                   pl.BlockSpec((None, HD, dva, TOK_TILE), lambda b, i: (b, 0, 0, i))],
        out_shape=[jax.ShapeDtypeStruct((B, HA, MLA_PAD, S), BF16),
                   jax.ShapeDtypeStruct((B, S, HA * MLA_PAD), BF16),
                   jax.ShapeDtypeStruct((B, HA, dva, S), BF16),
                   jax.ShapeDtypeStruct((B, 2 * HD, DIFF_D, S), BF16),
                   jax.ShapeDtypeStruct((B, S, nq), BF16),
                   jax.ShapeDtypeStruct((B, HD, dva, S), BF16)],
        compiler_params=_cparams(2),
        name="even_qkv_prep",
    )(xa, p["norm_mix"].reshape(1, D).astype(F32), mods["sh_a"], mods["sc_a"], *weights, cosT, sinT, cosK, sinK)


def _odd_prep_kernel(x_ref, g_ref, sh_ref, sc_ref, wqvT_ref, wk_ref, wkr_ref, gqn_ref, gk_ref, gkp_ref, bd_ref,
                     cosT_ref, sinT_ref, cosK_ref, sinK_ref,
                     qc_ref, qd_ref, k_ref, vc_ref, vd_ref):
    h = _norm_mod(x_ref[0], g_ref[...], sh_ref[0, 0], sc_ref[0, 0]).astype(BF16)
    n = h.shape[0]
    cosT, sinT, cosK, sinK = cosT_ref[...], sinT_ref[...], cosK_ref[...], sinK_ref[...]
    ones = _ones_rows(n)
    zT = _dot_t(wqvT_ref[...], h)
    d = HEAD_DIM
    for hd in range(WIN_HEADS):
        qc_ref[hd] = (_rope_rows(zT[hd * d:(hd + 1) * d], cosT, sinT) * (ATTN_SCALE * LOG2E)).astype(BF16)
    o1 = WIN_HEADS * d
    for hd in range(GLB_HEADS):
        q = zT[o1 + hd * d:o1 + (hd + 1) * d]
        q = _rope_rows(q * _rms_rows(q) * gqn_ref[...], cosT, sinT)
        qd_ref[hd] = (q * (ATTN_SCALE * LOG2E)).astype(BF16)
    o2 = o1 + GLB_HEADS * d
    for j in range(WIN_KV_HEADS):
        vc_ref[j, 0:d, :] = zT[o2 + j * d:o2 + (j + 1) * d].astype(BF16)
        vc_ref[j, d:, :] = ones
    o3 = o2 + WIN_KV_HEADS * d
    for j in range(GLB_KV_HEADS):
        vd_ref[j, 0:d, :] = zT[o3 + j * d:o3 + (j + 1) * d].astype(BF16)
        vd_ref[j, d:, :] = ones

    zk = jnp.dot(h, wk_ref[...], preferred_element_type=F32)
    zkr = jnp.dot(h, wkr_ref[...], preferred_element_type=F32)
    wc = WIN_KV_HEADS * d
    kc = zk[:, :wc] * cosK + zkr[:, :wc] * sinK
    z, zr = zk[:, wc:], zkr[:, wc:]
    sq = z * z
    hi = sq.astype(BF16)
    lo = (sq - hi.astype(F32)).astype(BF16)
    mean = (jnp.dot(hi, bd_ref[...], preferred_element_type=F32) + jnp.dot(lo, bd_ref[...], preferred_element_type=F32))
    kd = lax.rsqrt(mean + NORM_EPS) * (z * gk_ref[...] * cosK + zr * gkp_ref[...] * sinK)
    k_ref[...] = jnp.concatenate([kc, kd], axis=1).astype(BF16)


def _odd_prep(xa, p, mods, tabs):
    B, S, D = xa.shape
    w_in = p["w_in"]
    d = HEAD_DIM
    sizes = (WIN_HEADS, WIN_KV_HEADS, WIN_KV_HEADS, GLB_HEADS, GLB_KV_HEADS, GLB_KV_HEADS)
    offs = [0]
    for s in sizes:
        offs.append(offs[-1] + s * d)
    col = lambda j: w_in[:, offs[j]:offs[j + 1]]
    bf = lambda a: a.astype(BF16)
    wqvT = bf(jnp.concatenate([col(0), col(3), col(2), col(5)], axis=1).T)
    wk = jnp.concatenate([col(1), col(4)], axis=1)
    wkr = _rot_cols(wk, d)
    gqn = jnp.broadcast_to(p["q_norm"].astype(F32)[:, None], (d, TOK_TILE))
    gk1 = p["k_norm"].astype(F32)
    gk = jnp.tile(gk1, GLB_KV_HEADS).reshape(1, -1)
    gkp = jnp.tile(_perm_cols(gk1, d), GLB_KV_HEADS).reshape(1, -1)
    wd = GLB_KV_HEADS * d
    lane = jnp.arange(wd)
    bd = bf(jnp.where(lane[:, None] // d == lane[None, :] // d, 1.0 / d, 0.0))
    assert WIN_KV_HEADS * d == LANES and wd == LANES
    cosT, sinT, cosK, sinK = tabs
    weights = [wqvT, bf(wk), bf(wkr), gqn, gk, gkp, bd]
    nt = S // TOK_TILE
    ctx_tile = nt - 1
    mod_map = lambda b, i: (b, (i == ctx_tile).astype(jnp.int32), 0, 0)
    dva = d + ONES_ROWS
    return pl.pallas_call(
        _odd_prep_kernel,
        grid=(B, nt),
        in_specs=[pl.BlockSpec((1, TOK_TILE, D), lambda b, i: (b, i, 0)),
                  _full((1, D)),
                  pl.BlockSpec((1, 1, 1, D), mod_map),
                  pl.BlockSpec((1, 1, 1, D), mod_map)]
                 + [_full(w.shape) for w in weights]
                 + [pl.BlockSpec((d, TOK_TILE), lambda b, i: (0, i)),
                    pl.BlockSpec((d, TOK_TILE), lambda b, i: (0, i)),
                    pl.BlockSpec((TOK_TILE, LANES), lambda b, i: (i, 0)),
                    pl.BlockSpec((TOK_TILE, LANES), lambda b, i: (i, 0))],
        out_specs=[pl.BlockSpec((None, WIN_HEADS, d, TOK_TILE), lambda b, i: (b, 0, 0, i)),
                   pl.BlockSpec((None, GLB_HEADS, d, TOK_TILE), lambda b, i: (b, 0, 0, i)),
                   pl.BlockSpec((None, TOK_TILE, 2 * LANES), lambda b, i: (b, i, 0)),
                   pl.BlockSpec((None, WIN_KV_HEADS, dva, TOK_TILE), lambda b, i: (b, 0, 0, i)),
                   pl.BlockSpec((None, GLB_KV_HEADS, dva, TOK_TILE), lambda b, i: (b, 0, 0, i))],
        out_shape=[jax.ShapeDtypeStruct((B, WIN_HEADS, d, S), BF16),
                   jax.ShapeDtypeStruct((B, GLB_HEADS, d, S), BF16),
                   jax.ShapeDtypeStruct((B, S, 2 * LANES), BF16),
                   jax.ShapeDtypeStruct((B, WIN_KV_HEADS, dva, S), BF16),
                   jax.ShapeDtypeStruct((B, GLB_KV_HEADS, dva, S), BF16)],
        compiler_params=_cparams(2),
        name="odd_qkv_prep",
    )(xa, p["norm_mix"].reshape(1, D).astype(F32), mods["sh_a"], mods["sc_a"], *weights, cosT, sinT, cosK, sinK)


class _KeyPack:
    def __init__(self, kw, dk, block_of, slot_of):
        self.kw, self.dk, self.block_of, self.slot_of = kw, dk, block_of, slot_of


def _pad_queries(qT, pack, head):
    reps = pack.kw // qT.shape[0]
    if reps == 1:
        return qT
    rows = lax.broadcasted_iota(jnp.int32, (pack.kw, qT.shape[1]), 0)
    slot = pack.slot_of(head)
    keep = (rows >= slot * pack.dk) & (rows < (slot + 1) * pack.dk)
    return jnp.where(keep, jnp.concatenate([qT] * reps, axis=0), jnp.zeros((), qT.dtype))


def _scores(k_ref, qT, off, n, s_ref):
    s = jnp.dot(k_ref[pl.ds(off, n), :], qT, preferred_element_type=F32)
    s_ref[...] = s
    return jnp.max(s, axis=0, keepdims=True)


def _accumulate(s_ref, n, m, cmax, acc_ref, vT_ref, off):
    m_new = jnp.maximum(m, cmax)
    pv = None
    for kk in range(0, n, PV_TILE):
        p = jnp.exp2(s_ref[kk:kk + PV_TILE, :] - m_new).astype(BF16)
        start = off + kk if isinstance(off, int) else pl.multiple_of(off + kk, PV_TILE)
        d = jnp.dot(vT_ref[:, pl.ds(start, PV_TILE)], p, preferred_element_type=F32)
        pv = d if pv is None else pv + d
    acc_ref[...] = acc_ref[...] * jnp.exp2(m - m_new) + pv
    return m_new


def _softmax_init(init_ref, dva, dv, tq):
    m0 = jnp.broadcast_to(init_ref[0, 0:1, 0:1], (1, tq))
    row = lax.broadcasted_iota(jnp.int32, (dva, tq), 0)
    acc0 = jnp.where(row == dv, jnp.broadcast_to(init_ref[0, 1:2, 0:1], (dva, tq)), 0.0)
    return m0, acc0


def _flash_kernel(init_ref, qT_ref, qTn_ref, k_ref, vT_ref, o_ref, s0_ref, s1_ref, sc_ref, acc_ref, cm_ref, *,
                  T, kt, dv, pack):
    head = pl.program_id(1)
    qT = _pad_queries(qT_ref[...], pack, head)
    tq = qT.shape[1]
    dva = vT_ref.shape[0]
    n_ctx = sc_ref.shape[0]

    def first_scores(q):
        cm_ref[0:1, :] = _scores(k_ref, q, T, n_ctx, sc_ref)
        cm_ref[1:2, :] = _scores(k_ref, q, 0, kt, s0_ref)

    @pl.when(pl.program_id(2) == 0)
    def _():
        first_scores(qT)

    m, acc0 = _softmax_init(init_ref, dva, dv, tq)
    acc_ref[...] = acc0
    cm_c, cm_a = cm_ref[0:1, :], cm_ref[1:2, :]
    cm_b = _scores(k_ref, qT, kt, kt, s1_ref)
    m = _accumulate(sc_ref, n_ctx, m, cm_c, acc_ref, vT_ref, T)
    m = _accumulate(s0_ref, kt, m, cm_a, acc_ref, vT_ref, 0)

    def body(t, carry):
        m, cm_b = carry
        off = pl.multiple_of(t * (2 * kt), kt)
        cm_a = _scores(k_ref, qT, off + 2 * kt, kt, s0_ref)
        m = _accumulate(s1_ref, kt, m, cm_b, acc_ref, vT_ref, off + kt)
        cm_b = _scores(k_ref, qT, off + 3 * kt, kt, s1_ref)
        m = _accumulate(s0_ref, kt, m, cm_a, acc_ref, vT_ref, off + 2 * kt)
        return m, cm_b

    m, cm_b = lax.fori_loop(0, T // (2 * kt) - 1, body, (m, cm_b))
    first_scores(_pad_queries(qTn_ref[...], pack, head))
    m = _accumulate(s1_ref, kt, m, cm_b, acc_ref, vT_ref, T - kt)
    acc = acc_ref[...]
    o_ref[...] = acc[:dv] / acc[dv:dv + 1]


def _flash(qT, k, vT, init, T, pack):
    B, Hq, dk, S = qT.shape
    Hv, dva = vT.shape[1], vT.shape[2]
    dv = dva - ONES_ROWS
    gv = Hq // Hv
    kt = min(KEY_TILE, T // 2)
    tq = min(Q_TILE, T)
    assert T % (2 * kt) == 0 and T % tq == 0 and kt % PV_TILE == 0
    kern = functools.partial(_flash_kernel, T=T, kt=kt, dv=dv, pack=pack)
    nq = T // tq
    return pl.pallas_call(
        kern,
        grid=(B, Hq, nq),
        in_specs=[pl.BlockSpec((1, 2, 128), lambda b, h, i: (h, 0, 0)),
                  pl.BlockSpec((None, None, dk, tq), lambda b, h, i: (b, h, 0, i)),
                  pl.BlockSpec((None, None, dk, tq), lambda b, h, i: (b, h, 0, jnp.minimum(i + 1, nq - 1))),
                  pl.BlockSpec((None, S, pack.kw), lambda b, h, i: (b, 0, pack.block_of(h))),
                  pl.BlockSpec((None, None, dva, S), lambda b, h, i: (b, h // gv, 0, 0))],
        out_specs=pl.BlockSpec((None, None, dv, tq), lambda b, h, i: (b, h, 0, i)),
        out_shape=jax.ShapeDtypeStruct((B, Hq, dv, T), F32),
        scratch_shapes=[pltpu.VMEM((kt, tq), F32), pltpu.VMEM((kt, tq), F32), pltpu.VMEM((S - T, tq), F32),
                        pltpu.VMEM((dva, tq), F32), pltpu.VMEM((8, tq), F32)],
        compiler_params=_cparams(3),
        name="dense_attention",
    )(init, qT, qT, k, vT)


def _ctx_attn_kernel(init_ref, qT_ref, k_ref, vT_ref, o_ref, *, dv, pack):
    qT = _pad_queries(qT_ref[...], pack, pl.program_id(1))
    m0, acc0 = _softmax_init(init_ref, vT_ref.shape[0], dv, qT.shape[1])
    s = jnp.dot(k_ref[...], qT, preferred_element_type=F32)
    m = jnp.maximum(m0, jnp.max(s, axis=0, keepdims=True))
    acc = acc0 * jnp.exp2(m0 - m) + jnp.dot(vT_ref[...], jnp.exp2(s - m).astype(BF16), preferred_element_type=F32)
    o_ref[...] = acc[:dv] / acc[dv:dv + 1]


def _ctx_attn(qT, k, vT, init, T, pack):
    B, Hq, dk, S = qT.shape
    Hv, dva = vT.shape[1], vT.shape[2]
    dv = dva - ONES_ROWS
    gv = Hq // Hv
    n = S - T
    blk = T // n
    return pl.pallas_call(
        functools.partial(_ctx_attn_kernel, dv=dv, pack=pack),
        grid=(B, Hq),
        in_specs=[pl.BlockSpec((1, 2, 128), lambda b, h: (h, 0, 0)),
                  pl.BlockSpec((None, None, dk, n), lambda b, h: (b, h, 0, blk)),
                  pl.BlockSpec((None, n, pack.kw), lambda b, h: (b, blk, pack.block_of(h))),
                  pl.BlockSpec((None, None, dva, n), lambda b, h: (b, h // gv, 0, blk))],
        out_specs=pl.BlockSpec((None, None, dv, n), lambda b, h: (b, h, 0, 0)),
        out_shape=jax.ShapeDtypeStruct((B, Hq, dv, n), F32),
        compiler_params=_cparams(2),
        name="context_attention",
    )(init, qT, k, vT)


def _window_kernel(sink_ref, qT_ref, k_ref, kc_ref, vT_ref, o_ref, *, T, dv, pack):
    i = pl.program_id(2)
    G, _, tq = qT_ref.shape
    head = pl.program_id(1) * G
    qT = jnp.concatenate([_pad_queries(qT_ref[g], pack, head) for g in range(G)], axis=1)
    snk = jnp.concatenate([jnp.broadcast_to(sink_ref[g, 0:1, 0:1], (1, tq)) for g in range(G)], axis=1)
    band = 3 * tq
    kband = jnp.concatenate([k_ref[0][...], k_ref[1][...], k_ref[2][...]], axis=0)
    s_loc = jnp.dot(kband, qT, preferred_element_type=F32)
    r = lax.broadcasted_iota(jnp.int32, (band, G * tq), 0)
    c = lax.broadcasted_iota(jnp.int32, (band, G * tq), 1) & (tq - 1)
    kpos = r + (i - 1) * tq
    valid = (jnp.abs(r - tq - c) <= WINDOW) & (kpos >= 0) & (kpos < T)
    s_loc = jnp.where(valid, s_loc, NEG_INF)
    s_ctx = jnp.dot(kc_ref[...], qT, preferred_element_type=F32)
    m = jnp.maximum(jnp.maximum(jnp.max(s_loc, axis=0, keepdims=True), jnp.max(s_ctx, axis=0, keepdims=True)), snk)
    p_loc = jnp.exp2(s_loc - m).astype(BF16)
    p_ctx = jnp.exp2(s_ctx - m).astype(BF16)
    vband = jnp.concatenate([vT_ref[0][...], vT_ref[1][...], vT_ref[2][...]], axis=1)
    acc = (jnp.dot(vband, p_loc, preferred_element_type=F32)
           + jnp.dot(vT_ref[3][...], p_ctx, preferred_element_type=F32))
    o = acc[:dv] / (acc[dv:dv + 1] + jnp.exp2(snk - m))
    for g in range(G):
        o_ref[g] = o[:, g * tq:(g + 1) * tq]


def _window(qT, k, vT, sink, T, pack):
    B, Hq, dk, S = qT.shape
    Hk, dva = vT.shape[1], vT.shape[2]
    dv = dva - ONES_ROWS
    g = Hq // Hk
    tq = TOK_TILE
    nt = T // tq
    ctx_blk = T // tq
    clip = lambda j: jnp.clip(j, 0, nt - 1)
    kern = functools.partial(_window_kernel, T=T, dv=dv, pack=pack)
    kspec = lambda f: pl.BlockSpec((None, tq, pack.kw), lambda b, h, i: (b, f(i), pack.block_of(h * g)))
    vspec = lambda f: pl.BlockSpec((None, None, dva, tq), lambda b, h, i: (b, h, 0, f(i)))

    def body(sink_ref, qT_ref, k0, k1, k2, kc, v0, v1, v2, vc, o_ref):
        kern(sink_ref, qT_ref, (k0, k1, k2), kc, (v0, v1, v2, vc), o_ref)

    return pl.pallas_call(
        body,
        grid=(B, Hk, nt),
        in_specs=[pl.BlockSpec((g, 1, 128), lambda b, h, i: (h, 0, 0)),
                  pl.BlockSpec((None, g, dk, tq), lambda b, h, i: (b, h, 0, i)),
                  kspec(lambda i: clip(i - 1)), kspec(lambda i: i), kspec(lambda i: clip(i + 1)),
                  kspec(lambda i: ctx_blk),
                  vspec(lambda i: clip(i - 1)), vspec(lambda i: i), vspec(lambda i: clip(i + 1)),
                  vspec(lambda i: ctx_blk)],
        out_specs=pl.BlockSpec((None, g, dv, tq), lambda b, h, i: (b, h, 0, i)),
        out_shape=jax.ShapeDtypeStruct((B, Hq, dv, T), F32),
        compiler_params=_cparams(3),
        name="window_attention",
    )(sink, qT, k, k, k, k, vT, vT, vT, vT)


def _project_out(x_ref, oT, w_ref, gate_ref, y_ref):
    y = jnp.dot(oT.T.astype(BF16), w_ref[...], preferred_element_type=F32)
    y_ref[0] = x_ref[0] + gate_ref[0, 0] * y


def _even_out_kernel(x_ref, oa_ref, od_ref, w_ref, gate_ref, lam_ref, sub_ref, y_ref, *, post_scale):
    lam = lam_ref[...]
    parts = [oa_ref[hd] for hd in range(MLA_HEADS)]
    for hd in range(DIFF_HEADS):
        diff = od_ref[2 * hd] - lam * od_ref[2 * hd + 1]
        parts.append(diff * _rms_rows(diff) * sub_ref[...] * post_scale)
    _project_out(x_ref, jnp.concatenate(parts, axis=0), w_ref, gate_ref, y_ref)


def _odd_out_kernel(x_ref, oc_ref, od_ref, w_ref, gate_ref, y_ref):
    parts = [oc_ref[hd] for hd in range(WIN_HEADS)] + [od_ref[hd] for hd in range(GLB_HEADS)]
    _project_out(x_ref, jnp.concatenate(parts, axis=0), w_ref, gate_ref, y_ref)


def _out_proj(kern, x, heads, w, gate, extra, tile0, n_tiles, kind):
    B, S, D = x.shape
    row_map = lambda b, i: (b, i + tile0, 0)
    return pl.pallas_call(
        kern,
        grid=(B, n_tiles),
        in_specs=[pl.BlockSpec((1, TOK_TILE, D), row_map)]
                 + [pl.BlockSpec((None,) + o.shape[1:3] + (TOK_TILE,), lambda b, i: (b, 0, 0, i)) for o in heads]
                 + [_full(w.shape), pl.BlockSpec((1, 1, 1, D), lambda b, i: (b, kind, 0, 0))]
                 + [_full(e.shape) for e in extra],
        out_specs=pl.BlockSpec((1, TOK_TILE, D), row_map),
        out_shape=jax.ShapeDtypeStruct((B, S, D), F32),
        input_output_aliases={0: 0},
        compiler_params=_cparams(2),
        name="merge_out_proj_residual",
    )(x, *heads, w, gate, *extra)


def _first_argmax(vals):
    best, idx = vals[0], jnp.zeros(vals[0].shape, jnp.int32)
    for j in range(1, len(vals)):
        better = vals[j] > best
        idx = jnp.where(better, j, idx)
        best = jnp.where(better, vals[j], best)
    return idx, best


def _pick(idx, vals):
    out = vals[0]
    for j in range(1, len(vals)):
        out = jnp.where(idx == j, vals[j], out)
    return out


def _route(logits, bias):
    s = 1.0 / (1.0 + jnp.exp(-logits))
    sel = s + bias
    srow = [s[e:e + 1] for e in range(N_EXPERTS)]
    row = [sel[e:e + 1] for e in range(N_EXPERTS)]
    scores = []
    for g in range(N_GROUPS):
        a, b, c, d = row[4 * g:4 * g + 4]
        hi1, lo1, hi2, lo2 = jnp.maximum(a, b), jnp.minimum(a, b), jnp.maximum(c, d), jnp.minimum(c, d)
        top1 = jnp.maximum(hi1, hi2)
        top2 = jnp.maximum(jnp.maximum(lo1, lo2), jnp.minimum(hi1, hi2))
        scores.append(top1 + top2)
    gi, _ = _first_argmax(scores)
    v = [_pick(gi, [row[4 * g + j] for g in range(N_GROUPS)]) for j in range(EXPERTS_PER_GROUP)]
    sv = [_pick(gi, [srow[4 * g + j] for g in range(N_GROUPS)]) for j in range(EXPERTS_PER_GROUP)]
    i1, _ = _first_argmax(v)
    i2, _ = _first_argmax([jnp.where(i1 == j, -jnp.inf, v[j]) for j in range(EXPERTS_PER_GROUP)])
    w1, w2 = _pick(i1, sv), _pick(i2, sv)
    tot = w1 + w2
    w1, w2 = w1 / tot, w2 / tot
    rows = []
    for e in range(N_EXPERTS):
        g, j = divmod(e, EXPERTS_PER_GROUP)
        in_g = gi == g
        rows.append(jnp.where(in_g & (i1 == j), w1, 0.0) + jnp.where(in_g & (i2 == j), w2, 0.0))
    return jnp.concatenate(rows, axis=0), gi


def _split(x, terms):
    out = []
    for _ in range(terms):
        part = x.astype(BF16).astype(F32)
        out.append(part)
        x = x - part
    return out


def _to_column(row):
    n = row.shape[1]
    return jnp.concatenate([row, jnp.zeros((LANES - 1, n), F32)], axis=0).T[:, 0:1]


def _moe_kernel(x_ref, g_ref, sh_ref, sc_ref, gate_ref, wrT_ref, br_ref, before_ref, wg_ref, wu_ref, wd_ref, y_ref,
                hs_scr, cs_scr, ys_scr, q_scr, info_ref):
    e = pl.program_id(2)
    tm = x_ref.shape[1]
    R = hs_scr.shape[0]

    @pl.when(e == 0)
    def _():
        h = _norm_mod(x_ref[0], g_ref[...], sh_ref[0, 0], sc_ref[0, 0]).astype(BF16)
        comb, gi = _route(_dot_t(wrT_ref[...], h), br_ref[...])
        sel = jnp.concatenate([(gi == g).astype(F32) for g in range(N_GROUPS)]
                              + [jnp.zeros((16 - N_GROUPS, tm), F32)], axis=0)
        rank = jnp.dot(sel.astype(BF16), before_ref[...], preferred_element_type=F32)
        cnt = jnp.sum(sel, axis=1, keepdims=True)
        seg = jnp.ceil(cnt * (1.0 / 16.0)) * 16.0
        pos = jnp.zeros((1, tm), F32)
        start = jnp.zeros((1, 1), F32)
        for g in range(N_GROUPS):
            pos = pos + sel[g:g + 1] * (start + rank[g:g + 1])
            info_ref[g] = start[0, 0].astype(jnp.int32)
            info_ref[N_GROUPS + g] = jnp.ceil(cnt[g:g + 1] * (1.0 / MOE_CHUNK))[0, 0].astype(jnp.int32)
            start = start + seg[g:g + 1]
        slot = lax.broadcasted_iota(jnp.int32, (R, tm), 0).astype(F32)
        P = jnp.where(slot == pos, 1.0, 0.0).astype(BF16)
        hs_scr[...] = jnp.dot(P, h, preferred_element_type=F32).astype(BF16)
        parts = _split(comb, COMB_TERMS)
        combT = jnp.concatenate(parts + [jnp.zeros((LANES - COMB_TERMS * N_EXPERTS, tm), F32)], axis=0).T
        cs_scr[...] = jnp.dot(P, combT.astype(BF16), preferred_element_type=F32)
        lane = lax.broadcasted_iota(jnp.int32, (tm, R), 1).astype(F32)
        q_scr[...] = jnp.where(lane == _to_column(pos), 1.0, 0.0).astype(BF16)
        ys_scr[...] = jnp.zeros_like(ys_scr)

    grp = e // EXPERTS_PER_GROUP
    start = info_ref[grp]
    lane = lax.broadcasted_iota(jnp.int32, (MOE_CHUNK, LANES), 1)

    def chunk(c, carry):
        r0 = pl.multiple_of(start + c * MOE_CHUNK, 16)
        h = hs_scr[pl.ds(r0, MOE_CHUNK), :]
        a = jnp.dot(h, wg_ref[0], preferred_element_type=F32)
        u = jnp.dot(h, wu_ref[0], preferred_element_type=F32)
        act = (a * (1.0 / (1.0 + jnp.exp(-a)))) * u
        y = jnp.dot(act.astype(BF16), wd_ref[0], preferred_element_type=F32)
        mine = (lane & (N_EXPERTS - 1)) == e
        w = jnp.sum(jnp.where(mine, cs_scr[pl.ds(r0, MOE_CHUNK), :], 0.0), axis=1, keepdims=True)
        ys_scr[pl.ds(r0, MOE_CHUNK), :] += w * y
        return carry

    lax.fori_loop(0, info_ref[N_GROUPS + grp], chunk, 0)

    @pl.when(e == N_EXPERTS - 1)
    def _():
        q = q_scr[...]
        out = sum(jnp.dot(q, part.astype(BF16), preferred_element_type=F32) for part in _split(ys_scr[...], 2))
        y_ref[0] = x_ref[0] + gate_ref[0, 0] * out


def _moe(x, g, sh, sc, gate, wrT, br, wg, wu, wd, tm, tile0, n_tiles, kind):
    B, S, D = x.shape
    E, _, F = wg.shape
    R = -(-(tm + 16 * N_GROUPS + MOE_CHUNK) // 256) * 256
    idx = jnp.arange(tm)
    before = (idx[:, None] < idx[None, :]).astype(BF16)
    mod_map = lambda b, i, e: (b, kind, 0, 0)
    row_map = lambda b, i, e: (b, i + tile0, 0)
    return pl.pallas_call(
        _moe_kernel,
        grid=(B, n_tiles, E),
        in_specs=[pl.BlockSpec((1, tm, D), row_map),
                  pl.BlockSpec((1, D), lambda b, i, e: (0, 0)),
                  pl.BlockSpec((1, 1, 1, D), mod_map),
                  pl.BlockSpec((1, 1, 1, D), mod_map),
                  pl.BlockSpec((1, 1, 1, D), mod_map),
                  pl.BlockSpec((E, D), lambda b, i, e: (0, 0)),
                  pl.BlockSpec((E, 1), lambda b, i, e: (0, 0)),
                  pl.BlockSpec((tm, tm), lambda b, i, e: (0, 0)),
                  pl.BlockSpec((1, D, F), lambda b, i, e: (e, 0, 0)),
                  pl.BlockSpec((1, D, F), lambda b, i, e: (e, 0, 0)),
                  pl.BlockSpec((1, F, D), lambda b, i, e: (e, 0, 0))],
        out_specs=pl.BlockSpec((1, tm, D), row_map),
        out_shape=jax.ShapeDtypeStruct((B, S, D), F32),
        scratch_shapes=[pltpu.VMEM((R, D), BF16), pltpu.VMEM((R, LANES), F32), pltpu.VMEM((R, D), F32),
                        pltpu.VMEM((tm, R), BF16), pltpu.SMEM((2 * N_GROUPS,), jnp.int32)],
        input_output_aliases={0: 0},
        compiler_params=_cparams(3),
        name="moe_experts",
    )(x, g.reshape(1, D).astype(F32), sh, sc, gate, wrT, br, before, wg, wu, wd)


def _final_norm_kernel(x_ref, g_ref, o_ref):
    x = x_ref[0]
    o_ref[0] = (x * lax.rsqrt(jnp.mean(x * x, axis=-1, keepdims=True) + NORM_EPS)) * g_ref[...]


def _final_norm(x, g, T):
    B, S, D = x.shape
    return pl.pallas_call(
        _final_norm_kernel,
        grid=(B, T // TOK_TILE),
        in_specs=[pl.BlockSpec((1, TOK_TILE, D), lambda b, i: (b, i, 0)),
                  pl.BlockSpec((1, D), lambda b, i: (0, 0))],
        out_specs=pl.BlockSpec((1, TOK_TILE, D), lambda b, i: (b, i, 0)),
        out_shape=jax.ShapeDtypeStruct((B, T, D), F32),
        compiler_params=_cparams(2),
        name="final_norm",
    )(x, g.reshape(1, D).astype(F32))


def _axial_tables(T, n_ctx, rot_dim):
    half = rot_dim // 2
    inv_freq = ROPE_THETA ** (-jnp.arange(0, half, 2, dtype=F32) / half)
    rows = T // GRID_W
    row = jnp.broadcast_to(jnp.arange(rows, dtype=F32)[:, None], (rows, GRID_W)).reshape(-1)
    col = jnp.broadcast_to(jnp.arange(GRID_W, dtype=F32)[None, :], (rows, GRID_W)).reshape(-1)

    def ang(pos):
        a = pos[:, None] * inv_freq[None, :]
        return jnp.concatenate([a, a], axis=-1)

    a = jnp.concatenate([ang(row), ang(col)], axis=-1)
    cos = jnp.concatenate([jnp.cos(a), jnp.ones((n_ctx, rot_dim), F32)], axis=0)
    sin = jnp.concatenate([jnp.sin(a), jnp.zeros((n_ctx, rot_dim), F32)], axis=0)
    reps = LANES // rot_dim
    return cos.T, sin.T, jnp.tile(cos, (1, reps)), jnp.tile(sin, (1, reps))


def _plain_init(n_heads):
    return jnp.broadcast_to(jnp.array([NEG_INF, 0.0], F32)[None, :, None], (n_heads, 2, 128))


def _sink_init(sink):
    s = sink.astype(F32) * LOG2E
    return jnp.broadcast_to(jnp.stack([s, jnp.ones_like(s)], axis=1)[:, :, None], (s.shape[0], 2, 128))


def _even_mixer(xa, p, mods, T, with_ctx, lam_init, tabs):
    B, S, D = xa.shape
    qa, ka, va, qd, kd, vd = _even_prep(xa, p, mods, tabs)
    pack_a = _KeyPack(MLA_PAD, MLA_PAD, lambda h: h, lambda h: 0)
    per = LANES // DIFF_D
    pack_d = _KeyPack(LANES, DIFF_D, lambda h: h // per, lambda h: h % per)
    init_a, init_d = _plain_init(MLA_HEADS), _plain_init(2 * DIFF_HEADS)
    lam = p["lam"].astype(F32)
    lam_full = (jnp.exp(jnp.sum(lam[0] * lam[1])) - jnp.exp(jnp.sum(lam[2] * lam[3])) + lam_init).reshape(1, 1)
    sub = jnp.broadcast_to(p["subln"].astype(F32)[:, None], (DIFF_V, TOK_TILE))
    kern = functools.partial(_even_out_kernel, post_scale=1.0 - lam_init)
    oa = _flash(qa, ka, va, init_a, T, pack_a)
    od = _flash(qd, kd, vd, init_d, T, pack_d)
    xa = _out_proj(kern, xa, [oa, od], p["w_out"], mods["g_a"], [lam_full, sub], 0, T // TOK_TILE, 0)
    if with_ctx:
        oa = _ctx_attn(qa, ka, va, init_a, T, pack_a)
        od = _ctx_attn(qd, kd, vd, init_d, T, pack_d)
        xa = _out_proj(kern, xa, [oa, od], p["w_out"], mods["g_a"], [lam_full, sub], T // TOK_TILE, 1, 1)
    return xa


def _odd_mixer(xa, p, mods, T, with_ctx, tabs):
    B, S, D = xa.shape
    qc, qd, k, vc, vd = _odd_prep(xa, p, mods, tabs)
    gw, gd = WIN_HEADS // WIN_KV_HEADS, GLB_HEADS // GLB_KV_HEADS
    pack_c = _KeyPack(2 * LANES, HEAD_DIM, lambda h: 0, lambda h: h // gw)
    pack_d = _KeyPack(2 * LANES, HEAD_DIM, lambda h: 0, lambda h: WIN_KV_HEADS + h // gd)
    init_d = _plain_init(GLB_HEADS)
    sink = p["sink"].astype(F32) * LOG2E
    sink_w = jnp.broadcast_to(sink[:, None, None], (WIN_HEADS, 1, 128))
    od = _flash(qd, k, vd, init_d, T, pack_d)
    oc = _window(qc, k, vc, sink_w, T, pack_c)
    xa = _out_proj(_odd_out_kernel, xa, [oc, od], p["w_out"], mods["g_a"], [], 0, T // TOK_TILE, 0)
    if with_ctx:
        oc = _ctx_attn(qc, k, vc, _sink_init(p["sink"]), T, pack_c)
        od = _ctx_attn(qd, k, vd, init_d, T, pack_d)
        xa = _out_proj(_odd_out_kernel, xa, [oc, od], p["w_out"], mods["g_a"], [], T // TOK_TILE, 1, 1)
    return xa


def kernel(x, c, ctx, c_ctx, w_mod, b_mod, norm_mix, norm_ffn, even_w_in, even_norm_q, even_norm_kv, even_w_uq, even_w_ukv, even_lambda, even_subln, even_w_out, odd_w_in, odd_sink, odd_q_norm, odd_k_norm, odd_w_out, w_router, b_router, w_gate, w_up, w_down, norm_final):
    B, T, D = x.shape
    n_ctx = ctx.shape[1]
    depth = w_mod.shape[0]
    assert n_ctx == TOK_TILE == PV_TILE and T % MOE_TILE == 0 and B <= 7
    assert MLA_ROPE == DIFF_D
    S = T + n_ctx
    tabs32 = _axial_tables(T, n_ctx, MLA_ROPE)
    tabs64 = _axial_tables(T, n_ctx, HEAD_DIM)

    cond = jnp.zeros((8, D), F32).at[:B].set(c.astype(F32)).at[B].set(c_ctx.astype(F32))
    mod_all = _mod_vectors(cond, w_mod, b_mod)
    wrT = w_router.astype(BF16).T
    br = b_router.astype(F32).reshape(N_EXPERTS, 1)

    xa = jnp.concatenate([x, ctx], axis=1).astype(F32)
    for l in range(depth):
        with_ctx = l < depth - 1
        i = l // 2
        lat = mod_all[l, :B].reshape(B, 6, D)
        cx = jnp.broadcast_to(mod_all[l, B].reshape(1, 6, D), (B, 6, D))
        both = jnp.stack([lat, cx], axis=1)
        names = ("sh_a", "sc_a", "g_a", "sh_f", "sc_f", "g_f")
        mods = {n: both[:, :, j:j + 1, :] for j, n in enumerate(names)}
        if l % 2 == 0:
            lam_init = 0.8 - 0.6 * math.exp(-0.3 * l)
            p = dict(norm_mix=norm_mix[l], w_in=even_w_in[i], norm_q=even_norm_q[i], norm_kv=even_norm_kv[i],
                     w_uq=even_w_uq[i], w_ukv=even_w_ukv[i], lam=even_lambda[i], subln=even_subln[i],
                     w_out=even_w_out[i].astype(BF16))
            xa = _even_mixer(xa, p, mods, T, with_ctx, lam_init, tabs32)
        else:
            p = dict(norm_mix=norm_mix[l], w_in=odd_w_in[i], sink=odd_sink[i], q_norm=odd_q_norm[i],
                     k_norm=odd_k_norm[i], w_out=odd_w_out[i].astype(BF16))
            xa = _odd_mixer(xa, p, mods, T, with_ctx, tabs64)
        wg, wu, wd = w_gate[l].astype(BF16), w_up[l].astype(BF16), w_down[l].astype(BF16)
        ffn = (norm_ffn[l], mods["sh_f"], mods["sc_f"], mods["g_f"], wrT, br, wg, wu, wd)
        xa = _moe(xa, *ffn, MOE_TILE, 0, T // MOE_TILE, 0)
        if with_ctx:
            xa = _moe(xa, *ffn, TOK_TILE, T // TOK_TILE, 1, 1)
    return _final_norm(xa, norm_final, T)
```

```python
import functools
import math

import jax
import jax.numpy as jnp
from jax import lax
from jax.experimental import pallas as pl
from jax.experimental.pallas import tpu as pltpu

F32 = jnp.float32
BF16 = jnp.bfloat16
LOG2E = 1.4426950408889634

GRID_W = 64
ROPE_THETA = 10000.0
NORM_EPS = 1e-6
NEG_INF = -1e30

MLA_HEADS = 8
MLA_Q_LORA = 384
MLA_KV_LORA = 256
MLA_NOPE = 64
MLA_ROPE = 32
MLA_V = 64
MLA_SCALE = (MLA_NOPE + MLA_ROPE) ** -0.5
MLA_IN_COLS = MLA_Q_LORA + MLA_KV_LORA + MLA_ROPE
MLA_PAD = 128

DIFF_HEADS = 8
DIFF_D = 32
DIFF_V = 2 * DIFF_D
DIFF_SCALE = DIFF_D ** -0.5

HEAD_DIM = 64
WIN_HEADS = 8
WIN_KV_HEADS = 2
WINDOW = 128
GLB_HEADS = 8
GLB_KV_HEADS = 2
ATTN_SCALE = HEAD_DIM ** -0.5

N_EXPERTS = 16
N_GROUPS = 4
EXPERTS_PER_GROUP = N_EXPERTS // N_GROUPS

TOK_TILE = 256
Q_TILE = 1024
KEY_TILE = 2048
PV_TILE = 256
MOE_TILE = 1024
MOE_CHUNK = 128
COMB_TERMS = 3
ONES_ROWS = 16
LANES = 128
VMEM_LIMIT = 56 * 1024 * 1024


def _cparams(n_axes):
    return pltpu.CompilerParams(dimension_semantics=("arbitrary",) * n_axes,
                                vmem_limit_bytes=VMEM_LIMIT)


def _full(shape):
    return pl.BlockSpec(shape, lambda *_: (0,) * len(shape))


def _mod_kernel(a_ref, w_ref, b_ref, o_ref):
    a = a_ref[...]
    a = a * (1.0 / (1.0 + jnp.exp(-a)))
    o_ref[0] = jnp.dot(a.astype(BF16), w_ref[0].astype(BF16), preferred_element_type=F32) + b_ref[0]


def _mod_vectors(cond, w_mod, b_mod):
    L, D, N = w_mod.shape
    tn = 1536
    return pl.pallas_call(
        _mod_kernel,
        grid=(L, N // tn),
        in_specs=[pl.BlockSpec((8, D), lambda l, j: (0, 0)),
                  pl.BlockSpec((1, D, tn), lambda l, j: (l, 0, j)),
                  pl.BlockSpec((1, 1, tn), lambda l, j: (l, 0, j))],
        out_specs=pl.BlockSpec((1, 8, tn), lambda l, j: (l, 0, j)),
        out_shape=jax.ShapeDtypeStruct((L, 8, N), F32),
        compiler_params=_cparams(2),
        name="mod_vectors",
    )(cond, w_mod, b_mod.reshape(L, 1, N))


def _norm_mod(x, g, sh, sc):
    y = x * lax.rsqrt(jnp.mean(x * x, axis=-1, keepdims=True) + NORM_EPS)
    return (y * g) * (1.0 + sc) + sh


def _dot_t(w, h):
    return lax.dot_general(w, h, (((1,), (1,)), ((), ())), preferred_element_type=F32)


def _rms_rows(x):
    return lax.rsqrt(jnp.mean(x * x, axis=0, keepdims=True) + NORM_EPS)


def _rope_rows(x, cosT, sinT):
    q = x.shape[0] // 4
    rot = jnp.concatenate([-x[q:2 * q], x[0:q], -x[3 * q:4 * q], x[2 * q:3 * q]], axis=0)
    return x * cosT + rot * sinT


def _ones_rows(n):
    row = lax.broadcasted_iota(jnp.int32, (ONES_ROWS, n), 0)
    return jnp.where(row == 0, 1.0, 0.0).astype(BF16)


def _tile_lanes(x, reps):
    return x if reps == 1 else jnp.concatenate([x] * reps, axis=1)


def _even_prep_kernel(x_ref, g_ref, sh_ref, sc_ref, w1T_ref, wckv_ref, wkr_ref, wkrr_ref, wdk_ref, wdkr_ref,
                      wuqT_ref, wuvT_ref, wuk_ref, place_ref, gq_ref, gkvc_ref, gkvr_ref, cosT_ref, sinT_ref,
                      cosK_ref, sinK_ref,
                      qa_ref, ka_ref, va_ref, qd_ref, kd_ref, vd_ref):
    h = _norm_mod(x_ref[0], g_ref[...], sh_ref[0, 0], sc_ref[0, 0]).astype(BF16)
    n = h.shape[0]
    cosT, sinT, cosK, sinK = cosT_ref[...], sinT_ref[...], cosK_ref[...], sinK_ref[...]
    ones = _ones_rows(n)
    zT = _dot_t(w1T_ref[...], h)
    o1, o2 = MLA_Q_LORA, MLA_Q_LORA + MLA_KV_LORA
    o3 = o2 + 2 * DIFF_HEADS * DIFF_D

    c_q = zT[:o1]
    c_q = (c_q * _rms_rows(c_q) * gq_ref[...]).astype(BF16)
    qT = jnp.dot(wuqT_ref[...], c_q, preferred_element_type=F32)
    for hd in range(MLA_HEADS):
        r0 = hd * MLA_PAD
        rope = _rope_rows(qT[r0 + MLA_NOPE:r0 + MLA_NOPE + MLA_ROPE], cosT, sinT)
        head = jnp.concatenate([qT[r0:r0 + MLA_NOPE], rope, qT[r0 + MLA_NOPE + MLA_ROPE:r0 + MLA_PAD]], axis=0)
        qa_ref[hd] = (head * (MLA_SCALE * LOG2E)).astype(BF16)

    c_kvT = zT[o1:o2]
    c_kvT = (c_kvT * _rms_rows(c_kvT) * gkvc_ref[...]).astype(BF16)
    vT = jnp.dot(wuvT_ref[...], c_kvT, preferred_element_type=F32).astype(BF16)
    for hd in range(MLA_HEADS):
        va_ref[hd, 0:MLA_V, :] = vT[hd * MLA_V:(hd + 1) * MLA_V]
        va_ref[hd, MLA_V:, :] = ones

    for j in range(2 * DIFF_HEADS):
        qj = _rope_rows(zT[o2 + j * DIFF_D:o2 + (j + 1) * DIFF_D], cosT, sinT)
        qd_ref[j] = (qj * (DIFF_SCALE * LOG2E)).astype(BF16)
    for hd in range(DIFF_HEADS):
        vd_ref[hd, 0:DIFF_V, :] = zT[o3 + hd * DIFF_V:o3 + (hd + 1) * DIFF_V].astype(BF16)
        vd_ref[hd, DIFF_V:, :] = ones

    c_kv = jnp.dot(h, wckv_ref[...], preferred_element_type=F32)
    c_kv = (c_kv * lax.rsqrt(jnp.mean(c_kv * c_kv, axis=-1, keepdims=True) + NORM_EPS) * gkvr_ref[...]).astype(BF16)
    kr = (jnp.dot(h, wkr_ref[...], preferred_element_type=F32) * cosK
          + jnp.dot(h, wkrr_ref[...], preferred_element_type=F32) * sinK)
    ka = (jnp.dot(c_kv, wuk_ref[...], preferred_element_type=F32)
          + jnp.dot(kr.astype(BF16), place_ref[...], preferred_element_type=F32))
    ka_ref[...] = ka.astype(BF16)

    reps = wdk_ref.shape[1] // LANES
    kd = (jnp.dot(h, wdk_ref[...], preferred_element_type=F32) * _tile_lanes(cosK, reps)
          + jnp.dot(h, wdkr_ref[...], preferred_element_type=F32) * _tile_lanes(sinK, reps))
    kd_ref[...] = kd.astype(BF16)


def _rot_cols(w, width):
    d, n = w.shape
    w4 = w.reshape(d, n // width, 4, width // 4)
    return jnp.stack([-w4[:, :, 1], w4[:, :, 0], -w4[:, :, 3], w4[:, :, 2]], axis=2).reshape(d, n)


def _perm_cols(g, width):
    g4 = g.reshape(-1, 4, width // 4)
    return jnp.stack([g4[:, 1], g4[:, 0], g4[:, 3], g4[:, 2]], axis=1).reshape(-1)


def _even_prep(xa, p, mods, tabs):
    B, S, D = xa.shape
    w_in, w_uq, w_ukv = p["w_in"], p["w_uq"], p["w_ukv"]
    o1, o2, o3 = MLA_Q_LORA, MLA_Q_LORA + MLA_KV_LORA, MLA_IN_COLS
    nq = 2 * DIFF_HEADS * DIFF_D
    bf = lambda a: a.astype(BF16)
    w1T = bf(jnp.concatenate([w_in[:, :o2], w_in[:, o3:o3 + nq], w_in[:, o3 + 2 * nq:]], axis=1).T)
    wckv = bf(w_in[:, o1:o2])
    wkr = jnp.pad(w_in[:, o2:o3], ((0, 0), (0, LANES - MLA_ROPE)))
    wkrr = jnp.pad(_rot_cols(w_in[:, o2:o3], MLA_ROPE), ((0, 0), (0, LANES - MLA_ROPE)))
    wdk = w_in[:, o3 + nq:o3 + 2 * nq]
    wdkr = _rot_cols(wdk, DIFF_D)
    uq = w_uq.reshape(MLA_Q_LORA, MLA_HEADS, MLA_NOPE + MLA_ROPE)
    wuqT = bf(jnp.pad(uq, ((0, 0), (0, 0), (0, MLA_PAD - MLA_NOPE - MLA_ROPE))).reshape(MLA_Q_LORA, -1).T)
    ukv = w_ukv.reshape(MLA_KV_LORA, MLA_HEADS, MLA_NOPE + MLA_V)
    wuvT = bf(ukv[:, :, MLA_NOPE:].reshape(MLA_KV_LORA, -1).T)
    wuk = bf(jnp.pad(ukv[:, :, :MLA_NOPE], ((0, 0), (0, 0), (0, MLA_PAD - MLA_NOPE))).reshape(MLA_KV_LORA, -1))
    src = jnp.arange(LANES)[:, None]
    dst = jnp.arange(MLA_HEADS * MLA_PAD)[None, :]
    place = bf((src < MLA_ROPE) & (dst % MLA_PAD == src + MLA_NOPE))
    gq = jnp.broadcast_to(p["norm_q"].astype(F32)[:, None], (MLA_Q_LORA, TOK_TILE))
    gkvc = jnp.broadcast_to(p["norm_kv"].astype(F32)[:, None], (MLA_KV_LORA, TOK_TILE))
    gkvr = p["norm_kv"].astype(F32).reshape(1, MLA_KV_LORA)
    cosT, sinT, cosK, sinK = tabs
    weights = [w1T, wckv, bf(wkr), bf(wkrr), bf(wdk), bf(wdkr), wuqT, wuvT, wuk, place, gq, gkvc, gkvr]
    nt = S // TOK_TILE
    ctx_tile = nt - 1
    mod_map = lambda b, i: (b, (i == ctx_tile).astype(jnp.int32), 0, 0)
    HA, HD = MLA_HEADS, DIFF_HEADS
    dva = MLA_V + ONES_ROWS
    return pl.pallas_call(
        _even_prep_kernel,
        grid=(B, nt),
        in_specs=[pl.BlockSpec((1, TOK_TILE, D), lambda b, i: (b, i, 0)),
                  _full((1, D)),
                  pl.BlockSpec((1, 1, 1, D), mod_map),
                  pl.BlockSpec((1, 1, 1, D), mod_map)]
                 + [_full(w.shape) for w in weights]
                 + [pl.BlockSpec((MLA_ROPE, TOK_TILE), lambda b, i: (0, i)),
                    pl.BlockSpec((MLA_ROPE, TOK_TILE), lambda b, i: (0, i)),
                    pl.BlockSpec((TOK_TILE, LANES), lambda b, i: (i, 0)),
                    pl.BlockSpec((TOK_TILE, LANES), lambda b, i: (i, 0))],
        out_specs=[pl.BlockSpec((None, HA, MLA_PAD, TOK_TILE), lambda b, i: (b, 0, 0, i)),
                   pl.BlockSpec((None, TOK_TILE, HA * MLA_PAD), lambda b, i: (b, i, 0)),
                   pl.BlockSpec((None, HA, dva, TOK_TILE), lambda b, i: (b, 0, 0, i)),
                   pl.BlockSpec((None, 2 * HD, DIFF_D, TOK_TILE), lambda b, i: (b, 0, 0, i)),
                   pl.BlockSpec((None, TOK_TILE, nq), lambda b, i: (b, i, 0)),
                   pl.BlockSpec((None, HD, dva, TOK_TILE), lambda b, i: (b, 0, 0, i))],
        out_shape=[jax.ShapeDtypeStruct((B, HA, MLA_PAD, S), BF16),
                   jax.ShapeDtypeStruct((B, S, HA * MLA_PAD), BF16),
                   jax.ShapeDtypeStruct((B, HA, dva, S), BF16),
                   jax.ShapeDtypeStruct((B, 2 * HD, DIFF_D, S), BF16),
                   jax.ShapeDtypeStruct((B, S, nq), BF16),
                   jax.ShapeDtypeStruct((B, HD, dva, S), BF16)],
        compiler_params=_cparams(2),
        name="even_qkv_prep",
    )(xa, p["norm_mix"].reshape(1, D).astype(F32), mods["sh_a"], mods["sc_a"], *weights, cosT, sinT, cosK, sinK)


def _odd_prep_kernel(x_ref, g_ref, sh_ref, sc_ref, wqvT_ref, wk_ref, wkr_ref, gqn_ref, gk_ref, gkp_ref, bd_ref,
                     cosT_ref, sinT_ref, cosK_ref, sinK_ref,
                     qc_ref, qd_ref, k_ref, vc_ref, vd_ref):
    h = _norm_mod(x_ref[0], g_ref[...], sh_ref[0, 0], sc_ref[0, 0]).astype(BF16)
    n = h.shape[0]
    cosT, sinT, cosK, sinK = cosT_ref[...], sinT_ref[...], cosK_ref[...], sinK_ref[...]
    ones = _ones_rows(n)
    zT = _dot_t(wqvT_ref[...], h)
    d = HEAD_DIM
    for hd in range(WIN_HEADS):
        qc_ref[hd] = (_rope_rows(zT[hd * d:(hd + 1) * d], cosT, sinT) * (ATTN_SCALE * LOG2E)).astype(BF16)
    o1 = WIN_HEADS * d
    for hd in range(GLB_HEADS):
        q = zT[o1 + hd * d:o1 + (hd + 1) * d]
        q = _rope_rows(q * _rms_rows(q) * gqn_ref[...], cosT, sinT)
        qd_ref[hd] = (q * (ATTN_SCALE * LOG2E)).astype(BF16)
    o2 = o1 + GLB_HEADS * d
    for j in range(WIN_KV_HEADS):
        vc_ref[j, 0:d, :] = zT[o2 + j * d:o2 + (j + 1) * d].astype(BF16)
        vc_ref[j, d:, :] = ones
    o3 = o2 + WIN_KV_HEADS * d
    for j in range(GLB_KV_HEADS):
        vd_ref[j, 0:d, :] = zT[o3 + j * d:o3 + (j + 1) * d].astype(BF16)
        vd_ref[j, d:, :] = ones

    zk = jnp.dot(h, wk_ref[...], preferred_element_type=F32)
    zkr = jnp.dot(h, wkr_ref[...], preferred_element_type=F32)
    wc = WIN_KV_HEADS * d
    kc = zk[:, :wc] * cosK + zkr[:, :wc] * sinK
    z, zr = zk[:, wc:], zkr[:, wc:]
    sq = z * z
    hi = sq.astype(BF16)
    lo = (sq - hi.astype(F32)).astype(BF16)
    mean = (jnp.dot(hi, bd_ref[...], preferred_element_type=F32) + jnp.dot(lo, bd_ref[...], preferred_element_type=F32))
    kd = lax.rsqrt(mean + NORM_EPS) * (z * gk_ref[...] * cosK + zr * gkp_ref[...] * sinK)
    k_ref[...] = jnp.concatenate([kc, kd], axis=1).astype(BF16)


def _odd_prep(xa, p, mods, tabs):
    B, S, D = xa.shape
    w_in = p["w_in"]
    d = HEAD_DIM
    sizes = (WIN_HEADS, WIN_KV_HEADS, WIN_KV_HEADS, GLB_HEADS, GLB_KV_HEADS, GLB_KV_HEADS)
    offs = [0]
    for s in sizes:
        offs.append(offs[-1] + s * d)
    col = lambda j: w_in[:, offs[j]:offs[j + 1]]
    bf = lambda a: a.astype(BF16)
    wqvT = bf(jnp.concatenate([col(0), col(3), col(2), col(5)], axis=1).T)
    wk = jnp.concatenate([col(1), col(4)], axis=1)
    wkr = _rot_cols(wk, d)
    gqn = jnp.broadcast_to(p["q_norm"].astype(F32)[:, None], (d, TOK_TILE))
    gk1 = p["k_norm"].astype(F32)
    gk = jnp.tile(gk1, GLB_KV_HEADS).reshape(1, -1)
    gkp = jnp.tile(_perm_cols(gk1, d), GLB_KV_HEADS).reshape(1, -1)
    wd = GLB_KV_HEADS * d
    lane = jnp.arange(wd)
    bd = bf(jnp.where(lane[:, None] // d == lane[None, :] // d, 1.0 / d, 0.0))
    assert WIN_KV_HEADS * d == LANES and wd == LANES
    cosT, sinT, cosK, sinK = tabs
    weights = [wqvT, bf(wk), bf(wkr), gqn, gk, gkp, bd]
    nt = S // TOK_TILE
    ctx_tile = nt - 1
    mod_map = lambda b, i: (b, (i == ctx_tile).astype(jnp.int32), 0, 0)
    dva = d + ONES_ROWS
    return pl.pallas_call(
        _odd_prep_kernel,
        grid=(B, nt),
        in_specs=[pl.BlockSpec((1, TOK_TILE, D), lambda b, i: (b, i, 0)),
                  _full((1, D)),
                  pl.BlockSpec((1, 1, 1, D), mod_map),
                  pl.BlockSpec((1, 1, 1, D), mod_map)]
                 + [_full(w.shape) for w in weights]
                 + [pl.BlockSpec((d, TOK_TILE), lambda b, i: (0, i)),
                    pl.BlockSpec((d, TOK_TILE), lambda b, i: (0, i)),
                    pl.BlockSpec((TOK_TILE, LANES), lambda b, i: (i, 0)),
                    pl.BlockSpec((TOK_TILE, LANES), lambda b, i: (i, 0))],
        out_specs=[pl.BlockSpec((None, WIN_HEADS, d, TOK_TILE), lambda b, i: (b, 0, 0, i)),
                   pl.BlockSpec((None, GLB_HEADS, d, TOK_TILE), lambda b, i: (b, 0, 0, i)),
                   pl.BlockSpec((None, TOK_TILE, 2 * LANES), lambda b, i: (b, i, 0)),
                   pl.BlockSpec((None, WIN_KV_HEADS, dva, TOK_TILE), lambda b, i: (b, 0, 0, i)),
                   pl.BlockSpec((None, GLB_KV_HEADS, dva, TOK_TILE), lambda b, i: (b, 0, 0, i))],
        out_shape=[jax.ShapeDtypeStruct((B, WIN_HEADS, d, S), BF16),
                   jax.ShapeDtypeStruct((B, GLB_HEADS, d, S), BF16),
                   jax.ShapeDtypeStruct((B, S, 2 * LANES), BF16),
                   jax.ShapeDtypeStruct((B, WIN_KV_HEADS, dva, S), BF16),
                   jax.ShapeDtypeStruct((B, GLB_KV_HEADS, dva, S), BF16)],
        compiler_params=_cparams(2),
        name="odd_qkv_prep",
    )(xa, p["norm_mix"].reshape(1, D).astype(F32), mods["sh_a"], mods["sc_a"], *weights, cosT, sinT, cosK, sinK)


class _KeyPack:
    def __init__(self, kw, dk, block_of, slot_of):
        self.kw, self.dk, self.block_of, self.slot_of = kw, dk, block_of, slot_of


def _pad_queries(qT, pack, head):
    reps = pack.kw // qT.shape[0]
    if reps == 1:
        return qT
    rows = lax.broadcasted_iota(jnp.int32, (pack.kw, qT.shape[1]), 0)
    slot = pack.slot_of(head)
    keep = (rows >= slot * pack.dk) & (rows < (slot + 1) * pack.dk)
    return jnp.where(keep, jnp.concatenate([qT] * reps, axis=0), jnp.zeros((), qT.dtype))


def _aligned(start):
    return start if isinstance(start, int) else pl.multiple_of(start, PV_TILE)


def _scores(k_ref, qT, off, n, s_ref):
    s = jnp.dot(k_ref[pl.ds(off, n), :], qT, preferred_element_type=F32)
    s_ref[...] = s
    return jnp.max(s, axis=0, keepdims=True)


def _accumulate(s_ref, n, m, cmax, acc_ref, vT_ref, off):
    m_new = jnp.maximum(m, cmax)
    pv = None
    for kk in range(0, n, PV_TILE):
        p = jnp.exp2(s_ref[kk:kk + PV_TILE, :] - m_new).astype(BF16)
        d = jnp.dot(vT_ref[:, pl.ds(_aligned(off + kk), PV_TILE)], p, preferred_element_type=F32)
        pv = d if pv is None else pv + d
    acc_ref[...] = acc_ref[...] * jnp.exp2(m - m_new) + pv
    return m_new


def _scores_and_accumulate(k_ref, qT, noff, s_next, s_cur, n, m, cmax, acc_ref, vT_ref, off):
    m_new = jnp.maximum(m, cmax)
    pv = None
    cnext = None
    for kk in range(0, n, PV_TILE):
        s = jnp.dot(k_ref[pl.ds(_aligned(noff + kk), PV_TILE), :], qT, preferred_element_type=F32)
        s_next[kk:kk + PV_TILE, :] = s
        cm = jnp.max(s, axis=0, keepdims=True)
        cnext = cm if cnext is None else jnp.maximum(cnext, cm)
        p = jnp.exp2(s_cur[kk:kk + PV_TILE, :] - m_new).astype(BF16)
        d = jnp.dot(vT_ref[:, pl.ds(_aligned(off + kk), PV_TILE)], p, preferred_element_type=F32)
        pv = d if pv is None else pv + d
    acc_ref[...] = acc_ref[...] * jnp.exp2(m - m_new) + pv
    return m_new, cnext


def _softmax_init(init_ref, dva, dv, tq):
    m0 = jnp.broadcast_to(init_ref[0, 0:1, 0:1], (1, tq))
    row = lax.broadcasted_iota(jnp.int32, (dva, tq), 0)
    acc0 = jnp.where(row == dv, jnp.broadcast_to(init_ref[0, 1:2, 0:1], (dva, tq)), 0.0)
    return m0, acc0


def _flash_kernel(init_ref, qT_ref, qTn_ref, k_ref, vT_ref, o_ref, s0_ref, s1_ref, sc_ref, acc_ref, cm_ref, *,
                  T, kt, dv, pack):
    head = pl.program_id(1)
    qT = _pad_queries(qT_ref[...], pack, head)
    tq = qT.shape[1]
    dva = vT_ref.shape[0]
    n_ctx = sc_ref.shape[0]

    def first_scores(q):
        cm_ref[0:1, :] = _scores(k_ref, q, T, n_ctx, sc_ref)
        cm_ref[1:2, :] = _scores(k_ref, q, 0, kt, s0_ref)

    @pl.when(pl.program_id(2) == 0)
    def _():
        first_scores(qT)

    m, acc0 = _softmax_init(init_ref, dva, dv, tq)
    acc_ref[...] = acc0
    cm_c, cm_a = cm_ref[0:1, :], cm_ref[1:2, :]
    m = _accumulate(sc_ref, n_ctx, m, cm_c, acc_ref, vT_ref, T)
    m, cm_b = _scores_and_accumulate(k_ref, qT, kt, s1_ref, s0_ref, kt, m, cm_a, acc_ref, vT_ref, 0)

    def body(t, carry):
        m, cm_b = carry
        off = pl.multiple_of(t * (2 * kt), kt)
        m, cm_a = _scores_and_accumulate(k_ref, qT, off + 2 * kt, s0_ref, s1_ref, kt, m, cm_b, acc_ref, vT_ref, off + kt)
        m, cm_b = _scores_and_accumulate(k_ref, qT, off + 3 * kt, s1_ref, s0_ref, kt, m, cm_a, acc_ref, vT_ref,
                                         off + 2 * kt)
        return m, cm_b

    m, cm_b = lax.fori_loop(0, T // (2 * kt) - 1, body, (m, cm_b))
    qn = _pad_queries(qTn_ref[...], pack, head)
    cm_ref[0:1, :] = _scores(k_ref, qn, T, n_ctx, sc_ref)
    m, cm_ref[1:2, :] = _scores_and_accumulate(k_ref, qn, 0, s0_ref, s1_ref, kt, m, cm_b, acc_ref, vT_ref, T - kt)
    acc = acc_ref[...]
    o_ref[...] = acc[:dv] / acc[dv:dv + 1]


def _flash(qT, k, vT, init, T, pack):
    B, Hq, dk, S = qT.shape
    Hv, dva = vT.shape[1], vT.shape[2]
    dv = dva - ONES_ROWS
    gv = Hq // Hv
    kt = min(KEY_TILE, T // 2)
    tq = min(Q_TILE, T)
    assert T % (2 * kt) == 0 and T % tq == 0 and kt % PV_TILE == 0
    kern = functools.partial(_flash_kernel, T=T, kt=kt, dv=dv, pack=pack)
    nq = T // tq
    return pl.pallas_call(
        kern,
        grid=(B, Hq, nq),
        in_specs=[pl.BlockSpec((1, 2, 128), lambda b, h, i: (h, 0, 0)),
                  pl.BlockSpec((None, None, dk, tq), lambda b, h, i: (b, h, 0, i)),
                  pl.BlockSpec((None, None, dk, tq), lambda b, h, i: (b, h, 0, jnp.minimum(i + 1, nq - 1))),
                  pl.BlockSpec((None, S, pack.kw), lambda b, h, i: (b, 0, pack.block_of(h))),
                  pl.BlockSpec((None, None, dva, S), lambda b, h, i: (b, h // gv, 0, 0))],
        out_specs=pl.BlockSpec((None, None, dv, tq), lambda b, h, i: (b, h, 0, i)),
        out_shape=jax.ShapeDtypeStruct((B, Hq, dv, T), F32),
        scratch_shapes=[pltpu.VMEM((kt, tq), F32), pltpu.VMEM((kt, tq), F32), pltpu.VMEM((S - T, tq), F32),
                        pltpu.VMEM((dva, tq), F32), pltpu.VMEM((8, tq), F32)],
        compiler_params=_cparams(3),
        name="dense_attention",
    )(init, qT, qT, k, vT)


def _ctx_attn_kernel(init_ref, qT_ref, k_ref, vT_ref, o_ref, *, dv, pack):
    qT = _pad_queries(qT_ref[...], pack, pl.program_id(1))
    m0, acc0 = _softmax_init(init_ref, vT_ref.shape[0], dv, qT.shape[1])
    s = jnp.dot(k_ref[...], qT, preferred_element_type=F32)
    m = jnp.maximum(m0, jnp.max(s, axis=0, keepdims=True))
    acc = acc0 * jnp.exp2(m0 - m) + jnp.dot(vT_ref[...], jnp.exp2(s - m).astype(BF16), preferred_element_type=F32)
    o_ref[...] = acc[:dv] / acc[dv:dv + 1]


def _ctx_attn(qT, k, vT, init, T, pack):
    B, Hq, dk, S = qT.shape
    Hv, dva = vT.shape[1], vT.shape[2]
    dv = dva - ONES_ROWS
    gv = Hq // Hv
    n = S - T
    blk = T // n
    return pl.pallas_call(
        functools.partial(_ctx_attn_kernel, dv=dv, pack=pack),
        grid=(B, Hq),
        in_specs=[pl.BlockSpec((1, 2, 128), lambda b, h: (h, 0, 0)),
                  pl.BlockSpec((None, None, dk, n), lambda b, h: (b, h, 0, blk)),
                  pl.BlockSpec((None, n, pack.kw), lambda b, h: (b, blk, pack.block_of(h))),
                  pl.BlockSpec((None, None, dva, n), lambda b, h: (b, h // gv, 0, blk))],
        out_specs=pl.BlockSpec((None, None, dv, n), lambda b, h: (b, h, 0, 0)),
        out_shape=jax.ShapeDtypeStruct((B, Hq, dv, n), F32),
        compiler_params=_cparams(2),
        name="context_attention",
    )(init, qT, k, vT)


def _window_kernel(sink_ref, qT_ref, k_ref, kc_ref, vT_ref, o_ref, *, T, dv, pack):
    i = pl.program_id(2)
    G, _, tq = qT_ref.shape
    head = pl.program_id(1) * G
    qT = jnp.concatenate([_pad_queries(qT_ref[g], pack, head) for g in range(G)], axis=1)
    snk = jnp.concatenate([jnp.broadcast_to(sink_ref[g, 0:1, 0:1], (1, tq)) for g in range(G)], axis=1)
    band = 3 * tq
    kband = jnp.concatenate([k_ref[0][...], k_ref[1][...], k_ref[2][...]], axis=0)
    s_loc = jnp.dot(kband, qT, preferred_element_type=F32)
    r = lax.broadcasted_iota(jnp.int32, (band, G * tq), 0)
    c = lax.broadcasted_iota(jnp.int32, (band, G * tq), 1) & (tq - 1)
    kpos = r + (i - 1) * tq
    valid = (jnp.abs(r - tq - c) <= WINDOW) & (kpos >= 0) & (kpos < T)
    s_loc = jnp.where(valid, s_loc, NEG_INF)
    s_ctx = jnp.dot(kc_ref[...], qT, preferred_element_type=F32)
    m = jnp.maximum(jnp.maximum(jnp.max(s_loc, axis=0, keepdims=True), jnp.max(s_ctx, axis=0, keepdims=True)), snk)
    p_loc = jnp.exp2(s_loc - m).astype(BF16)
    p_ctx = jnp.exp2(s_ctx - m).astype(BF16)
    vband = jnp.concatenate([vT_ref[0][...], vT_ref[1][...], vT_ref[2][...]], axis=1)
    acc = (jnp.dot(vband, p_loc, preferred_element_type=F32)
           + jnp.dot(vT_ref[3][...], p_ctx, preferred_element_type=F32))
    o = acc[:dv] / (acc[dv:dv + 1] + jnp.exp2(snk - m))
    for g in range(G):
        o_ref[g] = o[:, g * tq:(g + 1) * tq]


def _window(qT, k, vT, sink, T, pack):
    B, Hq, dk, S = qT.shape
    Hk, dva = vT.shape[1], vT.shape[2]
    dv = dva - ONES_ROWS
    g = Hq // Hk
    tq = TOK_TILE
    nt = T // tq
    ctx_blk = T // tq
    clip = lambda j: jnp.clip(j, 0, nt - 1)
    kern = functools.partial(_window_kernel, T=T, dv=dv, pack=pack)
    kspec = lambda f: pl.BlockSpec((None, tq, pack.kw), lambda b, h, i: (b, f(i), pack.block_of(h * g)))
    vspec = lambda f: pl.BlockSpec((None, None, dva, tq), lambda b, h, i: (b, h, 0, f(i)))

    def body(sink_ref, qT_ref, k0, k1, k2, kc, v0, v1, v2, vc, o_ref):
        kern(sink_ref, qT_ref, (k0, k1, k2), kc, (v0, v1, v2, vc), o_ref)

    return pl.pallas_call(
        body,
        grid=(B, Hk, nt),
        in_specs=[pl.BlockSpec((g, 1, 128), lambda b, h, i: (h, 0, 0)),
                  pl.BlockSpec((None, g, dk, tq), lambda b, h, i: (b, h, 0, i)),
                  kspec(lambda i: clip(i - 1)), kspec(lambda i: i), kspec(lambda i: clip(i + 1)),
                  kspec(lambda i: ctx_blk),
                  vspec(lambda i: clip(i - 1)), vspec(lambda i: i), vspec(lambda i: clip(i + 1)),
                  vspec(lambda i: ctx_blk)],
        out_specs=pl.BlockSpec((None, g, dv, tq), lambda b, h, i: (b, h, 0, i)),
        out_shape=jax.ShapeDtypeStruct((B, Hq, dv, T), F32),
        compiler_params=_cparams(3),
        name="window_attention",
    )(sink, qT, k, k, k, k, vT, vT, vT, vT)


def _project_out(x_ref, oT, w_ref, gate_ref, y_ref):
    y = jnp.dot(oT.T.astype(BF16), w_ref[...], preferred_element_type=F32)
    y_ref[0] = x_ref[0] + gate_ref[0, 0] * y


def _even_out_kernel(x_ref, oa_ref, od_ref, w_ref, gate_ref, lam_ref, sub_ref, y_ref, *, post_scale):
    lam = lam_ref[...]
    parts = [oa_ref[hd] for hd in range(MLA_HEADS)]
    for hd in range(DIFF_HEADS):
        diff = od_ref[2 * hd] - lam * od_ref[2 * hd + 1]
        parts.append(diff * _rms_rows(diff) * sub_ref[...] * post_scale)
    _project_out(x_ref, jnp.concatenate(parts, axis=0), w_ref, gate_ref, y_ref)


def _odd_out_kernel(x_ref, oc_ref, od_ref, w_ref, gate_ref, y_ref):
    parts = [oc_ref[hd] for hd in range(WIN_HEADS)] + [od_ref[hd] for hd in range(GLB_HEADS)]
    _project_out(x_ref, jnp.concatenate(parts, axis=0), w_ref, gate_ref, y_ref)


def _out_proj(kern, x, heads, w, gate, extra, tile0, n_tiles, kind):
    B, S, D = x.shape
    row_map = lambda b, i: (b, i + tile0, 0)
    return pl.pallas_call(
        kern,
        grid=(B, n_tiles),
        in_specs=[pl.BlockSpec((1, TOK_TILE, D), row_map)]
                 + [pl.BlockSpec((None,) + o.shape[1:3] + (TOK_TILE,), lambda b, i: (b, 0, 0, i)) for o in heads]
                 + [_full(w.shape), pl.BlockSpec((1, 1, 1, D), lambda b, i: (b, kind, 0, 0))]
                 + [_full(e.shape) for e in extra],
        out_specs=pl.BlockSpec((1, TOK_TILE, D), row_map),
        out_shape=jax.ShapeDtypeStruct((B, S, D), F32),
        input_output_aliases={0: 0},
        compiler_params=_cparams(2),
        name="merge_out_proj_residual",
    )(x, *heads, w, gate, *extra)


def _first_argmax(vals):
    best, idx = vals[0], jnp.zeros(vals[0].shape, jnp.int32)
    for j in range(1, len(vals)):
        better = vals[j] > best
        idx = jnp.where(better, j, idx)
        best = jnp.where(better, vals[j], best)
    return idx, best


def _pick(idx, vals):
    out = vals[0]
    for j in range(1, len(vals)):
        out = jnp.where(idx == j, vals[j], out)
    return out


def _route(logits, bias):
    s = 1.0 / (1.0 + jnp.exp(-logits))
    sel = s + bias
    srow = [s[e:e + 1] for e in range(N_EXPERTS)]
    row = [sel[e:e + 1] for e in range(N_EXPERTS)]
    scores = []
    for g in range(N_GROUPS):
        a, b, c, d = row[4 * g:4 * g + 4]
        hi1, lo1, hi2, lo2 = jnp.maximum(a, b), jnp.minimum(a, b), jnp.maximum(c, d), jnp.minimum(c, d)
        top1 = jnp.maximum(hi1, hi2)
        top2 = jnp.maximum(jnp.maximum(lo1, lo2), jnp.minimum(hi1, hi2))
        scores.append(top1 + top2)
    gi, _ = _first_argmax(scores)
    v = [_pick(gi, [row[4 * g + j] for g in range(N_GROUPS)]) for j in range(EXPERTS_PER_GROUP)]
    sv = [_pick(gi, [srow[4 * g + j] for g in range(N_GROUPS)]) for j in range(EXPERTS_PER_GROUP)]
    i1, _ = _first_argmax(v)
    i2, _ = _first_argmax([jnp.where(i1 == j, -jnp.inf, v[j]) for j in range(EXPERTS_PER_GROUP)])
    w1, w2 = _pick(i1, sv), _pick(i2, sv)
    tot = w1 + w2
    w1, w2 = w1 / tot, w2 / tot
    rows = []
    for e in range(N_EXPERTS):
        g, j = divmod(e, EXPERTS_PER_GROUP)
        in_g = gi == g
        rows.append(jnp.where(in_g & (i1 == j), w1, 0.0) + jnp.where(in_g & (i2 == j), w2, 0.0))
    return jnp.concatenate(rows, axis=0), gi


def _split(x, terms):
    out = []
    for _ in range(terms):
        part = x.astype(BF16).astype(F32)
        out.append(part)
        x = x - part
    return out


def _to_column(row):
    n = row.shape[1]
    return jnp.concatenate([row, jnp.zeros((LANES - 1, n), F32)], axis=0).T[:, 0:1]


def _moe_kernel(x_ref, g_ref, sh_ref, sc_ref, gate_ref, wrT_ref, br_ref, before_ref, wg_ref, wu_ref, wd_ref, y_ref,
                hs_scr, cs_scr, ys_scr, q_scr, info_ref):
    e = pl.program_id(2)
    tm = x_ref.shape[1]
    R = hs_scr.shape[0]

    @pl.when(e == 0)
    def _():
        h = _norm_mod(x_ref[0], g_ref[...], sh_ref[0, 0], sc_ref[0, 0]).astype(BF16)
        comb, gi = _route(_dot_t(wrT_ref[...], h), br_ref[...])
        sel = jnp.concatenate([(gi == g).astype(F32) for g in range(N_GROUPS)]
                              + [jnp.zeros((16 - N_GROUPS, tm), F32)], axis=0)
        rank = jnp.dot(sel.astype(BF16), before_ref[...], preferred_element_type=F32)
        cnt = jnp.sum(sel, axis=1, keepdims=True)
        seg = jnp.ceil(cnt * (1.0 / 16.0)) * 16.0
        pos = jnp.zeros((1, tm), F32)
        start = jnp.zeros((1, 1), F32)
        for g in range(N_GROUPS):
            pos = pos + sel[g:g + 1] * (start + rank[g:g + 1])
            info_ref[g] = start[0, 0].astype(jnp.int32)
            info_ref[N_GROUPS + g] = jnp.ceil(cnt[g:g + 1] * (1.0 / MOE_CHUNK))[0, 0].astype(jnp.int32)
            start = start + seg[g:g + 1]
        slot = lax.broadcasted_iota(jnp.int32, (R, tm), 0).astype(F32)
        P = jnp.where(slot == pos, 1.0, 0.0).astype(BF16)
        hs_scr[...] = jnp.dot(P, h, preferred_element_type=F32).astype(BF16)
        parts = _split(comb, COMB_TERMS)
        combT = jnp.concatenate(parts + [jnp.zeros((LANES - COMB_TERMS * N_EXPERTS, tm), F32)], axis=0).T
        cs_scr[...] = jnp.dot(P, combT.astype(BF16), preferred_element_type=F32)
        lane = lax.broadcasted_iota(jnp.int32, (tm, R), 1).astype(F32)
        q_scr[...] = jnp.where(lane == _to_column(pos), 1.0, 0.0).astype(BF16)
        ys_scr[...] = jnp.zeros_like(ys_scr)

    grp = e // EXPERTS_PER_GROUP
    start = info_ref[grp]
    n_chunks = info_ref[N_GROUPS + grp]

    def expert(r0, rows):
        h = hs_scr[pl.ds(r0, rows), :]
        a = jnp.dot(h, wg_ref[0], preferred_element_type=F32)
        u = jnp.dot(h, wu_ref[0], preferred_element_type=F32)
        act = (a * (1.0 / (1.0 + jnp.exp(-a)))) * u
        y = jnp.dot(act.astype(BF16), wd_ref[0], preferred_element_type=F32)
        mine = (lax.broadcasted_iota(jnp.int32, (rows, LANES), 1) & (N_EXPERTS - 1)) == e
        w = jnp.sum(jnp.where(mine, cs_scr[pl.ds(r0, rows), :], 0.0), axis=1, keepdims=True)
        ys_scr[pl.ds(r0, rows), :] += w * y

    def pair(c, carry):
        expert(pl.multiple_of(start + c * (2 * MOE_CHUNK), 16), 2 * MOE_CHUNK)
        return carry

    lax.fori_loop(0, lax.shift_right_logical(n_chunks, 1), pair, 0)

    @pl.when((n_chunks & 1) == 1)
    def _():
        expert(pl.multiple_of(start + (n_chunks - 1) * MOE_CHUNK, 16), MOE_CHUNK)

    @pl.when(e == N_EXPERTS - 1)
    def _():
        q = q_scr[...]
        out = sum(jnp.dot(q, part.astype(BF16), preferred_element_type=F32) for part in _split(ys_scr[...], 2))
        y_ref[0] = x_ref[0] + gate_ref[0, 0] * out


def _moe(x, g, sh, sc, gate, wrT, br, wg, wu, wd, tm, tile0, n_tiles, kind):
    B, S, D = x.shape
    E, _, F = wg.shape
    R = -(-(tm + 16 * N_GROUPS + MOE_CHUNK) // 256) * 256
    idx = jnp.arange(tm)
    before = (idx[:, None] < idx[None, :]).astype(BF16)
    mod_map = lambda b, i, e: (b, kind, 0, 0)
    row_map = lambda b, i, e: (b, i + tile0, 0)
    return pl.pallas_call(
        _moe_kernel,
        grid=(B, n_tiles, E),
        in_specs=[pl.BlockSpec((1, tm, D), row_map),
                  pl.BlockSpec((1, D), lambda b, i, e: (0, 0)),
                  pl.BlockSpec((1, 1, 1, D), mod_map),
                  pl.BlockSpec((1, 1, 1, D), mod_map),
                  pl.BlockSpec((1, 1, 1, D), mod_map),
                  pl.BlockSpec((E, D), lambda b, i, e: (0, 0)),
                  pl.BlockSpec((E, 1), lambda b, i, e: (0, 0)),
                  pl.BlockSpec((tm, tm), lambda b, i, e: (0, 0)),
                  pl.BlockSpec((1, D, F), lambda b, i, e: (e, 0, 0)),
                  pl.BlockSpec((1, D, F), lambda b, i, e: (e, 0, 0)),
                  pl.BlockSpec((1, F, D), lambda b, i, e: (e, 0, 0))],
        out_specs=pl.BlockSpec((1, tm, D), row_map),
        out_shape=jax.ShapeDtypeStruct((B, S, D), F32),
        scratch_shapes=[pltpu.VMEM((R, D), BF16), pltpu.VMEM((R, LANES), F32), pltpu.VMEM((R, D), F32),
                        pltpu.VMEM((tm, R), BF16), pltpu.SMEM((2 * N_GROUPS,), jnp.int32)],
        input_output_aliases={0: 0},
        compiler_params=_cparams(3),
        name="moe_experts",
    )(x, g.reshape(1, D).astype(F32), sh, sc, gate, wrT, br, before, wg, wu, wd)


def _final_norm_kernel(x_ref, g_ref, o_ref):
    x = x_ref[0]
    o_ref[0] = (x * lax.rsqrt(jnp.mean(x * x, axis=-1, keepdims=True) + NORM_EPS)) * g_ref[...]


def _final_norm(x, g, T):
    B, S, D = x.shape
    return pl.pallas_call(
        _final_norm_kernel,
        grid=(B, T // TOK_TILE),
        in_specs=[pl.BlockSpec((1, TOK_TILE, D), lambda b, i: (b, i, 0)),
                  pl.BlockSpec((1, D), lambda b, i: (0, 0))],
        out_specs=pl.BlockSpec((1, TOK_TILE, D), lambda b, i: (b, i, 0)),
        out_shape=jax.ShapeDtypeStruct((B, T, D), F32),
        compiler_params=_cparams(2),
        name="final_norm",
    )(x, g.reshape(1, D).astype(F32))


def _axial_tables(T, n_ctx, rot_dim):
    half = rot_dim // 2
    inv_freq = ROPE_THETA ** (-jnp.arange(0, half, 2, dtype=F32) / half)
    rows = T // GRID_W
    row = jnp.broadcast_to(jnp.arange(rows, dtype=F32)[:, None], (rows, GRID_W)).reshape(-1)
    col = jnp.broadcast_to(jnp.arange(GRID_W, dtype=F32)[None, :], (rows, GRID_W)).reshape(-1)

    def ang(pos):
        a = pos[:, None] * inv_freq[None, :]
        return jnp.concatenate([a, a], axis=-1)

    a = jnp.concatenate([ang(row), ang(col)], axis=-1)
    cos = jnp.concatenate([jnp.cos(a), jnp.ones((n_ctx, rot_dim), F32)], axis=0)
    sin = jnp.concatenate([jnp.sin(a), jnp.zeros((n_ctx, rot_dim), F32)], axis=0)
    reps = LANES // rot_dim
    return cos.T, sin.T, jnp.tile(cos, (1, reps)), jnp.tile(sin, (1, reps))


def _plain_init(n_heads):
    return jnp.broadcast_to(jnp.array([NEG_INF, 0.0], F32)[None, :, None], (n_heads, 2, 128))


def _sink_init(sink):
    s = sink.astype(F32) * LOG2E
    return jnp.broadcast_to(jnp.stack([s, jnp.ones_like(s)], axis=1)[:, :, None], (s.shape[0], 2, 128))


def _even_mixer(xa, p, mods, T, with_ctx, lam_init, tabs):
    B, S, D = xa.shape
    qa, ka, va, qd, kd, vd = _even_prep(xa, p, mods, tabs)
    pack_a = _KeyPack(MLA_PAD, MLA_PAD, lambda h: h, lambda h: 0)
    per = LANES // DIFF_D
    pack_d = _KeyPack(LANES, DIFF_D, lambda h: h // per, lambda h: h % per)
    init_a, init_d = _plain_init(MLA_HEADS), _plain_init(2 * DIFF_HEADS)
    lam = p["lam"].astype(F32)
    lam_full = (jnp.exp(jnp.sum(lam[0] * lam[1])) - jnp.exp(jnp.sum(lam[2] * lam[3])) + lam_init).reshape(1, 1)
    sub = jnp.broadcast_to(p["subln"].astype(F32)[:, None], (DIFF_V, TOK_TILE))
    kern = functools.partial(_even_out_kernel, post_scale=1.0 - lam_init)
    oa = _flash(qa, ka, va, init_a, T, pack_a)
    od = _flash(qd, kd, vd, init_d, T, pack_d)
    xa = _out_proj(kern, xa, [oa, od], p["w_out"], mods["g_a"], [lam_full, sub], 0, T // TOK_TILE, 0)
    if with_ctx:
        oa = _ctx_attn(qa, ka, va, init_a, T, pack_a)
        od = _ctx_attn(qd, kd, vd, init_d, T, pack_d)
        xa = _out_proj(kern, xa, [oa, od], p["w_out"], mods["g_a"], [lam_full, sub], T // TOK_TILE, 1, 1)
    return xa


def _odd_mixer(xa, p, mods, T, with_ctx, tabs):
    B, S, D = xa.shape
    qc, qd, k, vc, vd = _odd_prep(xa, p, mods, tabs)
    gw, gd = WIN_HEADS // WIN_KV_HEADS, GLB_HEADS // GLB_KV_HEADS
    pack_c = _KeyPack(2 * LANES, HEAD_DIM, lambda h: 0, lambda h: h // gw)
    pack_d = _KeyPack(2 * LANES, HEAD_DIM, lambda h: 0, lambda h: WIN_KV_HEADS + h // gd)
    init_d = _plain_init(GLB_HEADS)
    sink = p["sink"].astype(F32) * LOG2E
    sink_w = jnp.broadcast_to(sink[:, None, None], (WIN_HEADS, 1, 128))
    od = _flash(qd, k, vd, init_d, T, pack_d)
    oc = _window(qc, k, vc, sink_w, T, pack_c)
    xa = _out_proj(_odd_out_kernel, xa, [oc, od], p["w_out"], mods["g_a"], [], 0, T // TOK_TILE, 0)
    if with_ctx:
        oc = _ctx_attn(qc, k, vc, _sink_init(p["sink"]), T, pack_c)
        od = _ctx_attn(qd, k, vd, init_d, T, pack_d)
        xa = _out_proj(_odd_out_kernel, xa, [oc, od], p["w_out"], mods["g_a"], [], T // TOK_TILE, 1, 1)
    return xa


def kernel(x, c, ctx, c_ctx, w_mod, b_mod, norm_mix, norm_ffn, even_w_in, even_norm_q, even_norm_kv, even_w_uq, even_w_ukv, even_lambda, even_subln, even_w_out, odd_w_in, odd_sink, odd_q_norm, odd_k_norm, odd_w_out, w_router, b_router, w_gate, w_up, w_down, norm_final):
    B, T, D = x.shape
    n_ctx = ctx.shape[1]
    depth = w_mod.shape[0]
    assert n_ctx == TOK_TILE == PV_TILE and T % MOE_TILE == 0 and B <= 7
    assert MLA_ROPE == DIFF_D
    S = T + n_ctx
    tabs32 = _axial_tables(T, n_ctx, MLA_ROPE)
    tabs64 = _axial_tables(T, n_ctx, HEAD_DIM)

    cond = jnp.zeros((8, D), F32).at[:B].set(c.astype(F32)).at[B].set(c_ctx.astype(F32))
    mod_all = _mod_vectors(cond, w_mod, b_mod)
    wrT = w_router.astype(BF16).T
    br = b_router.astype(F32).reshape(N_EXPERTS, 1)

    xa = jnp.concatenate([x, ctx], axis=1).astype(F32)
    for l in range(depth):
        with_ctx = l < depth - 1
        i = l // 2
        lat = mod_all[l, :B].reshape(B, 6, D)
        cx = jnp.broadcast_to(mod_all[l, B].reshape(1, 6, D), (B, 6, D))
        both = jnp.stack([lat, cx], axis=1)
        names = ("sh_a", "sc_a", "g_a", "sh_f", "sc_f", "g_f")
        mods = {n: both[:, :, j:j + 1, :] for j, n in enumerate(names)}
        if l % 2 == 0:
            lam_init = 0.8 - 0.6 * math.exp(-0.3 * l)
            p = dict(norm_mix=norm_mix[l], w_in=even_w_in[i], norm_q=even_norm_q[i], norm_kv=even_norm_kv[i],
                     w_uq=even_w_uq[i], w_ukv=even_w_ukv[i], lam=even_lambda[i], subln=even_subln[i],
                     w_out=even_w_out[i].astype(BF16))
            xa = _even_mixer(xa, p, mods, T, with_ctx, lam_init, tabs32)
        else:
            p = dict(norm_mix=norm_mix[l], w_in=odd_w_in[i], sink=odd_sink[i], q_norm=odd_q_norm[i],
                     k_norm=odd_k_norm[i], w_out=odd_w_out[i].astype(BF16))
            xa = _odd_mixer(xa, p, mods, T, with_ctx, tabs64)
        wg, wu, wd = w_gate[l].astype(BF16), w_up[l].astype(BF16), w_down[l].astype(BF16)
        ffn = (norm_ffn[l], mods["sh_f"], mods["sc_f"], mods["g_f"], wrT, br, wg, wu, wd)
        xa = _moe(xa, *ffn, MOE_TILE, 0, T // MOE_TILE, 0)
        if with_ctx:
            xa = _moe(xa, *ffn, TOK_TILE, T // TOK_TILE, 1, 1)
    return _final_norm(xa, norm_final, T)
```

```python
import functools
import math

import jax
import jax.numpy as jnp
from jax import lax
from jax.experimental import pallas as pl
from jax.experimental.pallas import tpu as pltpu

F32 = jnp.float32
BF16 = jnp.bfloat16
LOG2E = 1.4426950408889634

GRID_W = 64
ROPE_THETA = 10000.0
NORM_EPS = 1e-6
NEG_INF = -1e30

MLA_HEADS = 8
MLA_Q_LORA = 384
MLA_KV_LORA = 256
MLA_NOPE = 64
MLA_ROPE = 32
MLA_V = 64
MLA_SCALE = (MLA_NOPE + MLA_ROPE) ** -0.5
MLA_IN_COLS = MLA_Q_LORA + MLA_KV_LORA + MLA_ROPE
MLA_PAD = 128

DIFF_HEADS = 8
DIFF_D = 32
DIFF_V = 2 * DIFF_D
DIFF_SCALE = DIFF_D ** -0.5

HEAD_DIM = 64
WIN_HEADS = 8
WIN_KV_HEADS = 2
WINDOW = 128
GLB_HEADS = 8
GLB_KV_HEADS = 2
ATTN_SCALE = HEAD_DIM ** -0.5

N_EXPERTS = 16
N_GROUPS = 4
EXPERTS_PER_GROUP = N_EXPERTS // N_GROUPS

TOK_TILE = 256
Q_TILE = 1024
KEY_TILE = 2048
PV_TILE = 256
MOE_TILE = 1024
MOE_CHUNK = 128
COMB_TERMS = 3
ONES_ROWS = 16
LANES = 128
VMEM_LIMIT = 56 * 1024 * 1024


def _cparams(n_axes):
    return pltpu.CompilerParams(dimension_semantics=("arbitrary",) * n_axes,
                                vmem_limit_bytes=VMEM_LIMIT)


def _full(shape):
    return pl.BlockSpec(shape, lambda *_: (0,) * len(shape))


def _mod_kernel(a_ref, w_ref, b_ref, o_ref):
    a = a_ref[...]
    a = a * (1.0 / (1.0 + jnp.exp(-a)))
    o_ref[0] = jnp.dot(a.astype(BF16), w_ref[0].astype(BF16), preferred_element_type=F32) + b_ref[0]


def _mod_vectors(cond, w_mod, b_mod):
    L, D, N = w_mod.shape
    tn = 1536
    return pl.pallas_call(
        _mod_kernel,
        grid=(L, N // tn),
        in_specs=[pl.BlockSpec((8, D), lambda l, j: (0, 0)),
                  pl.BlockSpec((1, D, tn), lambda l, j: (l, 0, j)),
                  pl.BlockSpec((1, 1, tn), lambda l, j: (l, 0, j))],
        out_specs=pl.BlockSpec((1, 8, tn), lambda l, j: (l, 0, j)),
        out_shape=jax.ShapeDtypeStruct((L, 8, N), F32),
        compiler_params=_cparams(2),
        name="mod_vectors",
    )(cond, w_mod, b_mod.reshape(L, 1, N))


def _norm_mod(x, g, sh, sc):
    y = x * lax.rsqrt(jnp.mean(x * x, axis=-1, keepdims=True) + NORM_EPS)
    return (y * g) * (1.0 + sc) + sh


def _dot_t(w, h):
    return lax.dot_general(w, h, (((1,), (1,)), ((), ())), preferred_element_type=F32)


def _rms_rows(x):
    return lax.rsqrt(jnp.mean(x * x, axis=0, keepdims=True) + NORM_EPS)


def _rope_rows(x, cosT, sinT):
    q = x.shape[0] // 4
    rot = jnp.concatenate([-x[q:2 * q], x[0:q], -x[3 * q:4 * q], x[2 * q:3 * q]], axis=0)
    return x * cosT + rot * sinT


def _ones_rows(n):
    row = lax.broadcasted_iota(jnp.int32, (ONES_ROWS, n), 0)
    return jnp.where(row == 0, 1.0, 0.0).astype(BF16)


def _tile_lanes(x, reps):
    return x if reps == 1 else jnp.concatenate([x] * reps, axis=1)


def _even_prep_kernel(x_ref, g_ref, sh_ref, sc_ref, w1T_ref, wckv_ref, wkr_ref, wkrr_ref, wdk_ref, wdkr_ref,
                      wuqT_ref, wuvT_ref, wuk_ref, place_ref, gq_ref, gkvc_ref, gkvr_ref, cosT_ref, sinT_ref,
                      cosK_ref, sinK_ref,
                      qa_ref, ka_ref, va_ref, qd_ref, kd_ref, vd_ref):
    h = _norm_mod(x_ref[0], g_ref[...], sh_ref[0, 0], sc_ref[0, 0]).astype(BF16)
    n = h.shape[0]
    cosT, sinT, cosK, sinK = cosT_ref[...], sinT_ref[...], cosK_ref[...], sinK_ref[...]
    ones = _ones_rows(n)
    zT = _dot_t(w1T_ref[...], h)
    o1, o2 = MLA_Q_LORA, MLA_Q_LORA + MLA_KV_LORA
    o3 = o2 + 2 * DIFF_HEADS * DIFF_D

    c_q = zT[:o1]
    c_q = (c_q * _rms_rows(c_q) * gq_ref[...]).astype(BF16)
    qT = jnp.dot(wuqT_ref[...], c_q, preferred_element_type=F32)
    for hd in range(MLA_HEADS):
        r0 = hd * MLA_PAD
        rope = _rope_rows(qT[r0 + MLA_NOPE:r0 + MLA_NOPE + MLA_ROPE], cosT, sinT)
        head = jnp.concatenate([qT[r0:r0 + MLA_NOPE], rope, qT[r0 + MLA_NOPE + MLA_ROPE:r0 + MLA_PAD]], axis=0)
        qa_ref[hd] = (head * (MLA_SCALE * LOG2E)).astype(BF16)

    c_kvT = zT[o1:o2]
    c_kvT = (c_kvT * _rms_rows(c_kvT) * gkvc_ref[...]).astype(BF16)
    vT = jnp.dot(wuvT_ref[...], c_kvT, preferred_element_type=F32).astype(BF16)
    for hd in range(MLA_HEADS):
        va_ref[hd, 0:MLA_V, :] = vT[hd * MLA_V:(hd + 1) * MLA_V]
        va_ref[hd, MLA_V:, :] = ones

    for j in range(2 * DIFF_HEADS):
        qj = _rope_rows(zT[o2 + j * DIFF_D:o2 + (j + 1) * DIFF_D], cosT, sinT)
        qd_ref[j] = (qj * (DIFF_SCALE * LOG2E)).astype(BF16)
    for hd in range(DIFF_HEADS):
        vd_ref[hd, 0:DIFF_V, :] = zT[o3 + hd * DIFF_V:o3 + (hd + 1) * DIFF_V].astype(BF16)
        vd_ref[hd, DIFF_V:, :] = ones

    c_kv = jnp.dot(h, wckv_ref[...], preferred_element_type=F32)
    c_kv = (c_kv * lax.rsqrt(jnp.mean(c_kv * c_kv, axis=-1, keepdims=True) + NORM_EPS) * gkvr_ref[...]).astype(BF16)
    kr = (jnp.dot(h, wkr_ref[...], preferred_element_type=F32) * cosK
          + jnp.dot(h, wkrr_ref[...], preferred_element_type=F32) * sinK)
    ka = (jnp.dot(c_kv, wuk_ref[...], preferred_element_type=F32)
          + jnp.dot(kr.astype(BF16), place_ref[...], preferred_element_type=F32))
    ka_ref[...] = ka.astype(BF16)

    reps = wdk_ref.shape[1] // LANES
    kd = (jnp.dot(h, wdk_ref[...], preferred_element_type=F32) * _tile_lanes(cosK, reps)
          + jnp.dot(h, wdkr_ref[...], preferred_element_type=F32) * _tile_lanes(sinK, reps))
    kd_ref[...] = kd.astype(BF16)


def _rot_cols(w, width):
    d, n = w.shape
    w4 = w.reshape(d, n // width, 4, width // 4)
    return jnp.stack([-w4[:, :, 1], w4[:, :, 0], -w4[:, :, 3], w4[:, :, 2]], axis=2).reshape(d, n)


def _perm_cols(g, width):
    g4 = g.reshape(-1, 4, width // 4)
    return jnp.stack([g4[:, 1], g4[:, 0], g4[:, 3], g4[:, 2]], axis=1).reshape(-1)


def _even_prep(xa, p, mods, tabs):
    B, S, D = xa.shape
    w_in, w_uq, w_ukv = p["w_in"], p["w_uq"], p["w_ukv"]
    o1, o2, o3 = MLA_Q_LORA, MLA_Q_LORA + MLA_KV_LORA, MLA_IN_COLS
    nq = 2 * DIFF_HEADS * DIFF_D
    bf = lambda a: a.astype(BF16)
    w1T = bf(jnp.concatenate([w_in[:, :o2], w_in[:, o3:o3 + nq], w_in[:, o3 + 2 * nq:]], axis=1).T)
    wckv = bf(w_in[:, o1:o2])
    wkr = jnp.pad(w_in[:, o2:o3], ((0, 0), (0, LANES - MLA_ROPE)))
    wkrr = jnp.pad(_rot_cols(w_in[:, o2:o3], MLA_ROPE), ((0, 0), (0, LANES - MLA_ROPE)))
    wdk = w_in[:, o3 + nq:o3 + 2 * nq]
    wdkr = _rot_cols(wdk, DIFF_D)
    uq = w_uq.reshape(MLA_Q_LORA, MLA_HEADS, MLA_NOPE + MLA_ROPE)
    wuqT = bf(jnp.pad(uq, ((0, 0), (0, 0), (0, MLA_PAD - MLA_NOPE - MLA_ROPE))).reshape(MLA_Q_LORA, -1).T)
    ukv = w_ukv.reshape(MLA_KV_LORA, MLA_HEADS, MLA_NOPE + MLA_V)
    wuvT = bf(ukv[:, :, MLA_NOPE:].reshape(MLA_KV_LORA, -1).T)
    wuk = bf(jnp.pad(ukv[:, :, :MLA_NOPE], ((0, 0), (0, 0), (0, MLA_PAD - MLA_NOPE))).reshape(MLA_KV_LORA, -1))
    src = jnp.arange(LANES)[:, None]
    dst = jnp.arange(MLA_HEADS * MLA_PAD)[None, :]
    place = bf((src < MLA_ROPE) & (dst % MLA_PAD == src + MLA_NOPE))
    gq = jnp.broadcast_to(p["norm_q"].astype(F32)[:, None], (MLA_Q_LORA, TOK_TILE))
    gkvc = jnp.broadcast_to(p["norm_kv"].astype(F32)[:, None], (MLA_KV_LORA, TOK_TILE))
    gkvr = p["norm_kv"].astype(F32).reshape(1, MLA_KV_LORA)
    cosT, sinT, cosK, sinK = tabs
    weights = [w1T, wckv, bf(wkr), bf(wkrr), bf(wdk), bf(wdkr), wuqT, wuvT, wuk, place, gq, gkvc, gkvr]
    nt = S // TOK_TILE
    ctx_tile = nt - 1
    mod_map = lambda b, i: (b, (i == ctx_tile).astype(jnp.int32), 0, 0)
    HA, HD = MLA_HEADS, DIFF_HEADS
    dva = MLA_V + ONES_ROWS
    return pl.pallas_call(
        _even_prep_kernel,
        grid=(B, nt),
        in_specs=[pl.BlockSpec((1, TOK_TILE, D), lambda b, i: (b, i, 0)),
                  _full((1, D)),
                  pl.BlockSpec((1, 1, 1, D), mod_map),
                  pl.BlockSpec((1, 1, 1, D), mod_map)]
                 + [_full(w.shape) for w in weights]
                 + [pl.BlockSpec((MLA_ROPE, TOK_TILE), lambda b, i: (0, i)),
                    pl.BlockSpec((MLA_ROPE, TOK_TILE), lambda b, i: (0, i)),
                    pl.BlockSpec((TOK_TILE, LANES), lambda b, i: (i, 0)),
                    pl.BlockSpec((TOK_TILE, LANES), lambda b, i: (i, 0))],
        out_specs=[pl.BlockSpec((None, HA, MLA_PAD, TOK_TILE), lambda b, i: (b, 0, 0, i)),
                   pl.BlockSpec((None, TOK_TILE, HA * MLA_PAD), lambda b, i: (b, i, 0)),
                   pl.BlockSpec((None, HA, dva, TOK_TILE), lambda b, i: (b, 0, 0, i)),
                   pl.BlockSpec((None, 2 * HD, DIFF_D, TOK_TILE), lambda b, i: (b, 0, 0, i)),
                   pl.BlockSpec((None, TOK_TILE, nq), lambda b, i: (b, i, 0)),
                   pl.BlockSpec((None, HD, dva, TOK_TILE), lambda b, i: (b, 0, 0, i))],
        out_shape=[jax.ShapeDtypeStruct((B, HA, MLA_PAD, S), BF16),
                   jax.ShapeDtypeStruct((B, S, HA * MLA_PAD), BF16),
                   jax.ShapeDtypeStruct((B, HA, dva, S), BF16),
                   jax.ShapeDtypeStruct((B, 2 * HD, DIFF_D, S), BF16),
                   jax.ShapeDtypeStruct((B, S, nq), BF16),
                   jax.ShapeDtypeStruct((B, HD, dva, S), BF16)],
        compiler_params=_cparams(2),
        name="even_qkv_prep",
    )(xa, p["norm_mix"].reshape(1, D).astype(F32), mods["sh_a"], mods["sc_a"], *weights, cosT, sinT, cosK, sinK)


def _odd_prep_kernel(x_ref, g_ref, sh_ref, sc_ref, wqvT_ref, wk_ref, wkr_ref, gqn_ref, gk_ref, gkp_ref, bd_ref,
                     cosT_ref, sinT_ref, cosK_ref, sinK_ref,
                     qc_ref, qd_ref, k_ref, vc_ref, vd_ref):
    h = _norm_mod(x_ref[0], g_ref[...], sh_ref[0, 0], sc_ref[0, 0]).astype(BF16)
    n = h.shape[0]
    cosT, sinT, cosK, sinK = cosT_ref[...], sinT_ref[...], cosK_ref[...], sinK_ref[...]
    ones = _ones_rows(n)
    zT = _dot_t(wqvT_ref[...], h)
    d = HEAD_DIM
    for hd in range(WIN_HEADS):
        qc_ref[hd] = (_rope_rows(zT[hd * d:(hd + 1) * d], cosT, sinT) * (ATTN_SCALE * LOG2E)).astype(BF16)
    o1 = WIN_HEADS * d
    for hd in range(GLB_HEADS):
        q = zT[o1 + hd * d:o1 + (hd + 1) * d]
        q = _rope_rows(q * _rms_rows(q) * gqn_ref[...], cosT, sinT)
        qd_ref[hd] = (q * (ATTN_SCALE * LOG2E)).astype(BF16)
    o2 = o1 + GLB_HEADS * d
    for j in range(WIN_KV_HEADS):
        vc_ref[j, 0:d, :] = zT[o2 + j * d:o2 + (j + 1) * d].astype(BF16)
        vc_ref[j, d:, :] = ones
    o3 = o2 + WIN_KV_HEADS * d
    for j in range(GLB_KV_HEADS):
        vd_ref[j, 0:d, :] = zT[o3 + j * d:o3 + (j + 1) * d].astype(BF16)
        vd_ref[j, d:, :] = ones

    zk = jnp.dot(h, wk_ref[...], preferred_element_type=F32)
    zkr = jnp.dot(h, wkr_ref[...], preferred_element_type=F32)
    wc = WIN_KV_HEADS * d
    kc = zk[:, :wc] * cosK + zkr[:, :wc] * sinK
    z, zr = zk[:, wc:], zkr[:, wc:]
    sq = z * z
    hi = sq.astype(BF16)
    lo = (sq - hi.astype(F32)).astype(BF16)
    mean = (jnp.dot(hi, bd_ref[...], preferred_element_type=F32) + jnp.dot(lo, bd_ref[...], preferred_element_type=F32))
    kd = lax.rsqrt(mean + NORM_EPS) * (z * gk_ref[...] * cosK + zr * gkp_ref[...] * sinK)
    k_ref[...] = jnp.concatenate([kc, kd], axis=1).astype(BF16)


def _odd_prep(xa, p, mods, tabs):
    B, S, D = xa.shape
    w_in = p["w_in"]
    d = HEAD_DIM
    sizes = (WIN_HEADS, WIN_KV_HEADS, WIN_KV_HEADS, GLB_HEADS, GLB_KV_HEADS, GLB_KV_HEADS)
    offs = [0]
    for s in sizes:
        offs.append(offs[-1] + s * d)
    col = lambda j: w_in[:, offs[j]:offs[j + 1]]
    bf = lambda a: a.astype(BF16)
    wqvT = bf(jnp.concatenate([col(0), col(3), col(2), col(5)], axis=1).T)
    wk = jnp.concatenate([col(1), col(4)], axis=1)
    wkr = _rot_cols(wk, d)
    gqn = jnp.broadcast_to(p["q_norm"].astype(F32)[:, None], (d, TOK_TILE))
    gk1 = p["k_norm"].astype(F32)
    gk = jnp.tile(gk1, GLB_KV_HEADS).reshape(1, -1)
    gkp = jnp.tile(_perm_cols(gk1, d), GLB_KV_HEADS).reshape(1, -1)
    wd = GLB_KV_HEADS * d
    lane = jnp.arange(wd)
    bd = bf(jnp.where(lane[:, None] // d == lane[None, :] // d, 1.0 / d, 0.0))
    assert WIN_KV_HEADS * d == LANES and wd == LANES
    cosT, sinT, cosK, sinK = tabs
    weights = [wqvT, bf(wk), bf(wkr), gqn, gk, gkp, bd]
    nt = S // TOK_TILE
    ctx_tile = nt - 1
    mod_map = lambda b, i: (b, (i == ctx_tile).astype(jnp.int32), 0, 0)
    dva = d + ONES_ROWS
    return pl.pallas_call(
        _odd_prep_kernel,
        grid=(B, nt),
        in_specs=[pl.BlockSpec((1, TOK_TILE, D), lambda b, i: (b, i, 0)),
                  _full((1, D)),
                  pl.BlockSpec((1, 1, 1, D), mod_map),
                  pl.BlockSpec((1, 1, 1, D), mod_map)]
                 + [_full(w.shape) for w in weights]
                 + [pl.BlockSpec((d, TOK_TILE), lambda b, i: (0, i)),
                    pl.BlockSpec((d, TOK_TILE), lambda b, i: (0, i)),
                    pl.BlockSpec((TOK_TILE, LANES), lambda b, i: (i, 0)),
                    pl.BlockSpec((TOK_TILE, LANES), lambda b, i: (i, 0))],
        out_specs=[pl.BlockSpec((None, WIN_HEADS, d, TOK_TILE), lambda b, i: (b, 0, 0, i)),
                   pl.BlockSpec((None, GLB_HEADS, d, TOK_TILE), lambda b, i: (b, 0, 0, i)),
                   pl.BlockSpec((None, TOK_TILE, 2 * LANES), lambda b, i: (b, i, 0)),
                   pl.BlockSpec((None, WIN_KV_HEADS, dva, TOK_TILE), lambda b, i: (b, 0, 0, i)),
                   pl.BlockSpec((None, GLB_KV_HEADS, dva, TOK_TILE), lambda b, i: (b, 0, 0, i))],
        out_shape=[jax.ShapeDtypeStruct((B, WIN_HEADS, d, S), BF16),
                   jax.ShapeDtypeStruct((B, GLB_HEADS, d, S), BF16),
                   jax.ShapeDtypeStruct((B, S, 2 * LANES), BF16),
                   jax.ShapeDtypeStruct((B, WIN_KV_HEADS, dva, S), BF16),
                   jax.ShapeDtypeStruct((B, GLB_KV_HEADS, dva, S), BF16)],
        compiler_params=_cparams(2),
        name="odd_qkv_prep",
    )(xa, p["norm_mix"].reshape(1, D).astype(F32), mods["sh_a"], mods["sc_a"], *weights, cosT, sinT, cosK, sinK)


class _KeyPack:
    def __init__(self, kw, dk, block_of, slot_of):
        self.kw, self.dk, self.block_of, self.slot_of = kw, dk, block_of, slot_of


def _pad_queries(qT, pack, head):
    reps = pack.kw // qT.shape[0]
    if reps == 1:
        return qT
    rows = lax.broadcasted_iota(jnp.int32, (pack.kw, qT.shape[1]), 0)
    slot = pack.slot_of(head)
    keep = (rows >= slot * pack.dk) & (rows < (slot + 1) * pack.dk)
    return jnp.where(keep, jnp.concatenate([qT] * reps, axis=0), jnp.zeros((), qT.dtype))


def _aligned(start):
    return start if isinstance(start, int) else pl.multiple_of(start, PV_TILE)


def _attn_step(k_ref, vT_ref, acc_ref, q_next, next_offs, s_next, cur_offs, s_cur, m, cmax):
    m_new = jnp.maximum(m, cmax) if cur_offs else m
    pv = None
    cnext = None
    for i in range(max(len(next_offs), len(cur_offs))):
        rows = slice(i * PV_TILE, (i + 1) * PV_TILE)
        if i < len(next_offs):
            s = jnp.dot(k_ref[pl.ds(_aligned(next_offs[i]), PV_TILE), :], q_next, preferred_element_type=F32)
            s_next[rows, :] = s
            cm = jnp.max(s, axis=0, keepdims=True)
            cnext = cm if cnext is None else jnp.maximum(cnext, cm)
        if i < len(cur_offs):
            p = jnp.exp2(s_cur[rows, :] - m_new).astype(BF16)
            d = jnp.dot(vT_ref[:, pl.ds(_aligned(cur_offs[i]), PV_TILE)], p, preferred_element_type=F32)
            pv = d if pv is None else pv + d
    if cur_offs:
        acc_ref[...] = acc_ref[...] * jnp.exp2(m - m_new) + pv
    return m_new, cnext


def _softmax_init(init_ref, dva, dv, tq):
    m0 = jnp.broadcast_to(init_ref[0, 0:1, 0:1], (1, tq))
    row = lax.broadcasted_iota(jnp.int32, (dva, tq), 0)
    acc0 = jnp.where(row == dv, jnp.broadcast_to(init_ref[0, 1:2, 0:1], (dva, tq)), 0.0)
    return m0, acc0


def _flash_kernel(init_ref, qT_ref, qTn_ref, k_ref, vT_ref, o_ref, s0_ref, s1_ref, acc_ref, cm_ref, *,
                  T, n_ctx, kt, dv, pack):
    head = pl.program_id(1)
    qT = _pad_queries(qT_ref[...], pack, head)
    tq = qT.shape[1]
    dva = vT_ref.shape[0]
    pieces = lambda off: [off + kk for kk in range(0, kt, PV_TILE)]
    first = [T + kk for kk in range(0, n_ctx, PV_TILE)] + pieces(0)
    step = functools.partial(_attn_step, k_ref, vT_ref, acc_ref)

    @pl.when(pl.program_id(2) == 0)
    def _():
        _, cm_ref[0:1, :] = step(qT, first, s0_ref, [], None, None, None)

    m, acc0 = _softmax_init(init_ref, dva, dv, tq)
    acc_ref[...] = acc0
    m, cm_b = step(qT, pieces(kt), s1_ref, first, s0_ref, m, cm_ref[0:1, :])

    def body(t, carry):
        m, cm_b = carry
        off = pl.multiple_of(t * (2 * kt), kt)
        m, cm_a = step(qT, pieces(off + 2 * kt), s0_ref, pieces(off + kt), s1_ref, m, cm_b)
        m, cm_b = step(qT, pieces(off + 3 * kt), s1_ref, pieces(off + 2 * kt), s0_ref, m, cm_a)
        return m, cm_b

    m, cm_b = lax.fori_loop(0, T // (2 * kt) - 1, body, (m, cm_b))
    qn = _pad_queries(qTn_ref[...], pack, head)
    m, cm_ref[0:1, :] = step(qn, first, s0_ref, pieces(T - kt), s1_ref, m, cm_b)
    acc = acc_ref[...]
    o_ref[...] = acc[:dv] / acc[dv:dv + 1]


def _flash(qT, k, vT, init, T, pack):
    B, Hq, dk, S = qT.shape
    Hv, dva = vT.shape[1], vT.shape[2]
    dv = dva - ONES_ROWS
    gv = Hq // Hv
    kt = min(KEY_TILE, T // 2)
    tq = min(Q_TILE, T)
    assert T % (2 * kt) == 0 and T % tq == 0 and kt % PV_TILE == 0
    n_ctx = S - T
    assert n_ctx % PV_TILE == 0
    kern = functools.partial(_flash_kernel, T=T, n_ctx=n_ctx, kt=kt, dv=dv, pack=pack)
    nq = T // tq
    return pl.pallas_call(
        kern,
        grid=(B, Hq, nq),
        in_specs=[pl.BlockSpec((1, 2, 128), lambda b, h, i: (h, 0, 0)),
                  pl.BlockSpec((None, None, dk, tq), lambda b, h, i: (b, h, 0, i)),
                  pl.BlockSpec((None, None, dk, tq), lambda b, h, i: (b, h, 0, jnp.minimum(i + 1, nq - 1))),
                  pl.BlockSpec((None, S, pack.kw), lambda b, h, i: (b, 0, pack.block_of(h))),
                  pl.BlockSpec((None, None, dva, S), lambda b, h, i: (b, h // gv, 0, 0))],
        out_specs=pl.BlockSpec((None, None, dv, tq), lambda b, h, i: (b, h, 0, i)),
        out_shape=jax.ShapeDtypeStruct((B, Hq, dv, T), F32),
        scratch_shapes=[pltpu.VMEM((kt + n_ctx, tq), F32), pltpu.VMEM((kt, tq), F32),
                        pltpu.VMEM((dva, tq), F32), pltpu.VMEM((8, tq), F32)],
        compiler_params=_cparams(3),
        name="dense_attention",
    )(init, qT, qT, k, vT)


def _ctx_attn_kernel(init_ref, qT_ref, k_ref, vT_ref, o_ref, *, dv, pack):
    qT = _pad_queries(qT_ref[...], pack, pl.program_id(1))
    m0, acc0 = _softmax_init(init_ref, vT_ref.shape[0], dv, qT.shape[1])
    s = jnp.dot(k_ref[...], qT, preferred_element_type=F32)
    m = jnp.maximum(m0, jnp.max(s, axis=0, keepdims=True))
    acc = acc0 * jnp.exp2(m0 - m) + jnp.dot(vT_ref[...], jnp.exp2(s - m).astype(BF16), preferred_element_type=F32)
    o_ref[...] = acc[:dv] / acc[dv:dv + 1]


def _ctx_attn(qT, k, vT, init, T, pack):
    B, Hq, dk, S = qT.shape
    Hv, dva = vT.shape[1], vT.shape[2]
    dv = dva - ONES_ROWS
    gv = Hq // Hv
    n = S - T
    blk = T // n
    return pl.pallas_call(
        functools.partial(_ctx_attn_kernel, dv=dv, pack=pack),
        grid=(B, Hq),
        in_specs=[pl.BlockSpec((1, 2, 128), lambda b, h: (h, 0, 0)),
                  pl.BlockSpec((None, None, dk, n), lambda b, h: (b, h, 0, blk)),
                  pl.BlockSpec((None, n, pack.kw), lambda b, h: (b, blk, pack.block_of(h))),
                  pl.BlockSpec((None, None, dva, n), lambda b, h: (b, h // gv, 0, blk))],
        out_specs=pl.BlockSpec((None, None, dv, n), lambda b, h: (b, h, 0, 0)),
        out_shape=jax.ShapeDtypeStruct((B, Hq, dv, n), F32),
        compiler_params=_cparams(2),
        name="context_attention",
    )(init, qT, k, vT)


def _window_kernel(sink_ref, bias_ref, qT_ref, k_ref, kc_ref, vT_ref, o_ref, *, dv, pack):
    i = pl.program_id(2)
    G, _, tq = qT_ref.shape
    head = pl.program_id(1) * G
    qT = jnp.concatenate([_pad_queries(qT_ref[g], pack, head) for g in range(G)], axis=1)
    snk = jnp.concatenate([jnp.broadcast_to(sink_ref[g, 0:1, 0:1], (1, tq)) for g in range(G)], axis=1)
    W = WINDOW
    kband = jnp.concatenate([k_ref[0][tq - W:tq, :], k_ref[1][...], k_ref[2][0:W, :]], axis=0)
    before_first = jnp.where(i == 0, NEG_INF, 0.0)
    after_last = jnp.where(i == pl.num_programs(2) - 1, NEG_INF, 0.0)
    bias = jnp.concatenate([bias_ref[0:W, :] + before_first, bias_ref[W:W + tq, :],
                            bias_ref[W + tq:, :] + after_last], axis=0)
    s_loc = jnp.dot(kband, qT, preferred_element_type=F32) + bias
    s_ctx = jnp.dot(kc_ref[...], qT, preferred_element_type=F32)
    m = jnp.maximum(jnp.maximum(jnp.max(s_loc, axis=0, keepdims=True), jnp.max(s_ctx, axis=0, keepdims=True)), snk)
    p_loc = jnp.exp2(s_loc - m).astype(BF16)
    p_ctx = jnp.exp2(s_ctx - m).astype(BF16)
    vband = jnp.concatenate([vT_ref[0][:, tq - W:tq], vT_ref[1][...], vT_ref[2][:, 0:W]], axis=1)
    acc = (jnp.dot(vband, p_loc, preferred_element_type=F32)
           + jnp.dot(vT_ref[3][...], p_ctx, preferred_element_type=F32))
    o = acc[:dv] / (acc[dv:dv + 1] + jnp.exp2(snk - m))
    for g in range(G):
        o_ref[g] = o[:, g * tq:(g + 1) * tq]


def _window(qT, k, vT, sink, T, pack):
    B, Hq, dk, S = qT.shape
    Hk, dva = vT.shape[1], vT.shape[2]
    dv = dva - ONES_ROWS
    g = Hq // Hk
    tq = TOK_TILE
    nt = T // tq
    ctx_blk = T // tq
    clip = lambda j: jnp.clip(j, 0, nt - 1)
    kern = functools.partial(_window_kernel, dv=dv, pack=pack)
    kspec = lambda f: pl.BlockSpec((None, tq, pack.kw), lambda b, h, i: (b, f(i), pack.block_of(h * g)))
    vspec = lambda f: pl.BlockSpec((None, None, dva, tq), lambda b, h, i: (b, h, 0, f(i)))
    r = jnp.arange(tq + 2 * WINDOW)[:, None]
    c = jnp.arange(g * tq)[None, :] % tq
    bias = jnp.where(jnp.abs(r - WINDOW - c) <= WINDOW, 0.0, NEG_INF).astype(F32)

    def body(sink_ref, bias_ref, qT_ref, k0, k1, k2, kc, v0, v1, v2, vc, o_ref):
        kern(sink_ref, bias_ref, qT_ref, (k0, k1, k2), kc, (v0, v1, v2, vc), o_ref)

    return pl.pallas_call(
        body,
        grid=(B, Hk, nt),
        in_specs=[pl.BlockSpec((g, 1, 128), lambda b, h, i: (h, 0, 0)),
                  _full(bias.shape),
                  pl.BlockSpec((None, g, dk, tq), lambda b, h, i: (b, h, 0, i)),
                  kspec(lambda i: clip(i - 1)), kspec(lambda i: i), kspec(lambda i: clip(i + 1)),
                  kspec(lambda i: ctx_blk),
                  vspec(lambda i: clip(i - 1)), vspec(lambda i: i), vspec(lambda i: clip(i + 1)),
                  vspec(lambda i: ctx_blk)],
        out_specs=pl.BlockSpec((None, g, dv, tq), lambda b, h, i: (b, h, 0, i)),
        out_shape=jax.ShapeDtypeStruct((B, Hq, dv, T), F32),
        compiler_params=_cparams(3),
        name="window_attention",
    )(sink, bias, qT, k, k, k, k, vT, vT, vT, vT)


def _project_out(x_ref, oT, w_ref, gate_ref, y_ref):
    y = jnp.dot(oT.T.astype(BF16), w_ref[...], preferred_element_type=F32)
    y_ref[0] = x_ref[0] + gate_ref[0, 0] * y


def _even_out_kernel(x_ref, oa_ref, od_ref, w_ref, gate_ref, lam_ref, sub_ref, y_ref, *, post_scale):
    lam = lam_ref[...]
    parts = [oa_ref[hd] for hd in range(MLA_HEADS)]
    for hd in range(DIFF_HEADS):
        diff = od_ref[2 * hd] - lam * od_ref[2 * hd + 1]
        parts.append(diff * _rms_rows(diff) * sub_ref[...] * post_scale)
    _project_out(x_ref, jnp.concatenate(parts, axis=0), w_ref, gate_ref, y_ref)


def _odd_out_kernel(x_ref, oc_ref, od_ref, w_ref, gate_ref, y_ref):
    parts = [oc_ref[hd] for hd in range(WIN_HEADS)] + [od_ref[hd] for hd in range(GLB_HEADS)]
    _project_out(x_ref, jnp.concatenate(parts, axis=0), w_ref, gate_ref, y_ref)


def _out_proj(kern, x, heads, w, gate, extra, tile0, n_tiles, kind):
    B, S, D = x.shape
    row_map = lambda b, i: (b, i + tile0, 0)
    return pl.pallas_call(
        kern,
        grid=(B, n_tiles),
        in_specs=[pl.BlockSpec((1, TOK_TILE, D), row_map)]
                 + [pl.BlockSpec((None,) + o.shape[1:3] + (TOK_TILE,), lambda b, i: (b, 0, 0, i)) for o in heads]
                 + [_full(w.shape), pl.BlockSpec((1, 1, 1, D), lambda b, i: (b, kind, 0, 0))]
                 + [_full(e.shape) for e in extra],
        out_specs=pl.BlockSpec((1, TOK_TILE, D), row_map),
        out_shape=jax.ShapeDtypeStruct((B, S, D), F32),
        input_output_aliases={0: 0},
        compiler_params=_cparams(2),
        name="merge_out_proj_residual",
    )(x, *heads, w, gate, *extra)


def _first_argmax(vals):
    best, idx = vals[0], jnp.zeros(vals[0].shape, jnp.int32)
    for j in range(1, len(vals)):
        better = vals[j] > best
        idx = jnp.where(better, j, idx)
        best = jnp.where(better, vals[j], best)
    return idx, best


def _pick(idx, vals):
    out = vals[0]
    for j in range(1, len(vals)):
        out = jnp.where(idx == j, vals[j], out)
    return out


def _route(logits, bias):
    s = 1.0 / (1.0 + jnp.exp(-logits))
    sel = s + bias
    srow = [s[e:e + 1] for e in range(N_EXPERTS)]
    row = [sel[e:e + 1] for e in range(N_EXPERTS)]
    scores = []
    for g in range(N_GROUPS):
        a, b, c, d = row[4 * g:4 * g + 4]
        hi1, lo1, hi2, lo2 = jnp.maximum(a, b), jnp.minimum(a, b), jnp.maximum(c, d), jnp.minimum(c, d)
        top1 = jnp.maximum(hi1, hi2)
        top2 = jnp.maximum(jnp.maximum(lo1, lo2), jnp.minimum(hi1, hi2))
        scores.append(top1 + top2)
    gi, _ = _first_argmax(scores)
    v = [_pick(gi, [row[4 * g + j] for g in range(N_GROUPS)]) for j in range(EXPERTS_PER_GROUP)]
    sv = [_pick(gi, [srow[4 * g + j] for g in range(N_GROUPS)]) for j in range(EXPERTS_PER_GROUP)]
    i1, _ = _first_argmax(v)
    i2, _ = _first_argmax([jnp.where(i1 == j, -jnp.inf, v[j]) for j in range(EXPERTS_PER_GROUP)])
    w1, w2 = _pick(i1, sv), _pick(i2, sv)
    tot = w1 + w2
    w1, w2 = w1 / tot, w2 / tot
    rows = []
    for e in range(N_EXPERTS):
        g, j = divmod(e, EXPERTS_PER_GROUP)
        in_g = gi == g
        rows.append(jnp.where(in_g & (i1 == j), w1, 0.0) + jnp.where(in_g & (i2 == j), w2, 0.0))
    return jnp.concatenate(rows, axis=0), gi


def _split(x, terms):
    out = []
    for _ in range(terms):
        part = x.astype(BF16).astype(F32)
        out.append(part)
        x = x - part
    return out


def _to_column(row):
    n = row.shape[1]
    return jnp.concatenate([row, jnp.zeros((LANES - 1, n), F32)], axis=0).T[:, 0:1]


def _moe_kernel(x_ref, g_ref, sh_ref, sc_ref, gate_ref, wrT_ref, br_ref, before_ref, wg_ref, wu_ref, wd_ref, y_ref,
                hs_scr, cs_scr, ys_scr, q_scr, info_ref):
    e = pl.program_id(2)
    tm = x_ref.shape[1]
    R = hs_scr.shape[0]

    @pl.when(e == 0)
    def _():
        h = _norm_mod(x_ref[0], g_ref[...], sh_ref[0, 0], sc_ref[0, 0]).astype(BF16)
        comb, gi = _route(_dot_t(wrT_ref[...], h), br_ref[...])
        sel = jnp.concatenate([(gi == g).astype(F32) for g in range(N_GROUPS)]
                              + [jnp.zeros((16 - N_GROUPS, tm), F32)], axis=0)
        rank = jnp.dot(sel.astype(BF16), before_ref[...], preferred_element_type=F32)
        cnt = jnp.sum(sel, axis=1, keepdims=True)
        seg = jnp.ceil(cnt * (1.0 / 16.0)) * 16.0
        pos = jnp.zeros((1, tm), F32)
        start = jnp.zeros((1, 1), F32)
        for g in range(N_GROUPS):
            pos = pos + sel[g:g + 1] * (start + rank[g:g + 1])
            info_ref[g] = start[0, 0].astype(jnp.int32)
            info_ref[N_GROUPS + g] = jnp.ceil(cnt[g:g + 1] * (1.0 / MOE_CHUNK))[0, 0].astype(jnp.int32)
            start = start + seg[g:g + 1]
        slot = lax.broadcasted_iota(jnp.int32, (R, tm), 0).astype(F32)
        P = jnp.where(slot == pos, 1.0, 0.0).astype(BF16)
        hs_scr[...] = jnp.dot(P, h, preferred_element_type=F32).astype(BF16)
        parts = _split(comb, COMB_TERMS)
        combT = jnp.concatenate(parts + [jnp.zeros((LANES - COMB_TERMS * N_EXPERTS, tm), F32)], axis=0).T
        cs_scr[...] = jnp.dot(P, combT.astype(BF16), preferred_element_type=F32)
        lane = lax.broadcasted_iota(jnp.int32, (tm, R), 1).astype(F32)
        q_scr[...] = jnp.where(lane == _to_column(pos), 1.0, 0.0).astype(BF16)
        ys_scr[...] = jnp.zeros_like(ys_scr)

    grp = e // EXPERTS_PER_GROUP
    start = info_ref[grp]
    n_chunks = info_ref[N_GROUPS + grp]

    def expert(r0, rows):
        h = hs_scr[pl.ds(r0, rows), :]
        a = jnp.dot(h, wg_ref[0], preferred_element_type=F32)
        u = jnp.dot(h, wu_ref[0], preferred_element_type=F32)
        act = (a * (1.0 / (1.0 + jnp.exp(-a)))) * u
        y = jnp.dot(act.astype(BF16), wd_ref[0], preferred_element_type=F32)
        mine = (lax.broadcasted_iota(jnp.int32, (rows, LANES), 1) & (N_EXPERTS - 1)) == e
        w = jnp.sum(jnp.where(mine, cs_scr[pl.ds(r0, rows), :], 0.0), axis=1, keepdims=True)
        ys_scr[pl.ds(r0, rows), :] += w * y

    def pair(c, carry):
        expert(pl.multiple_of(start + c * (2 * MOE_CHUNK), 16), 2 * MOE_CHUNK)
        return carry

    lax.fori_loop(0, lax.shift_right_logical(n_chunks, 1), pair, 0)

    @pl.when((n_chunks & 1) == 1)
    def _():
        expert(pl.multiple_of(start + (n_chunks - 1) * MOE_CHUNK, 16), MOE_CHUNK)

    @pl.when(e == N_EXPERTS - 1)
    def _():
        q = q_scr[...]
        out = sum(jnp.dot(q, part.astype(BF16), preferred_element_type=F32) for part in _split(ys_scr[...], 2))
        y_ref[0] = x_ref[0] + gate_ref[0, 0] * out


def _moe(x, g, sh, sc, gate, wrT, br, wg, wu, wd, tm, tile0, n_tiles, kind):
    B, S, D = x.shape
    E, _, F = wg.shape
    R = -(-(tm + 16 * N_GROUPS + MOE_CHUNK) // 256) * 256
    idx = jnp.arange(tm)
    before = (idx[:, None] < idx[None, :]).astype(BF16)
    mod_map = lambda b, i, e: (b, kind, 0, 0)
    row_map = lambda b, i, e: (b, i + tile0, 0)
    return pl.pallas_call(
        _moe_kernel,
        grid=(B, n_tiles, E),
        in_specs=[pl.BlockSpec((1, tm, D), row_map),
                  pl.BlockSpec((1, D), lambda b, i, e: (0, 0)),
                  pl.BlockSpec((1, 1, 1, D), mod_map),
                  pl.BlockSpec((1, 1, 1, D), mod_map),
                  pl.BlockSpec((1, 1, 1, D), mod_map),
                  pl.BlockSpec((E, D), lambda b, i, e: (0, 0)),
                  pl.BlockSpec((E, 1), lambda b, i, e: (0, 0)),
                  pl.BlockSpec((tm, tm), lambda b, i, e: (0, 0)),
                  pl.BlockSpec((1, D, F), lambda b, i, e: (e, 0, 0)),
                  pl.BlockSpec((1, D, F), lambda b, i, e: (e, 0, 0)),
                  pl.BlockSpec((1, F, D), lambda b, i, e: (e, 0, 0))],
        out_specs=pl.BlockSpec((1, tm, D), row_map),
        out_shape=jax.ShapeDtypeStruct((B, S, D), F32),
        scratch_shapes=[pltpu.VMEM((R, D), BF16), pltpu.VMEM((R, LANES), F32), pltpu.VMEM((R, D), F32),
                        pltpu.VMEM((tm, R), BF16), pltpu.SMEM((2 * N_GROUPS,), jnp.int32)],
        input_output_aliases={0: 0},
        compiler_params=_cparams(3),
        name="moe_experts",
    )(x, g.reshape(1, D).astype(F32), sh, sc, gate, wrT, br, before, wg, wu, wd)


def _final_norm_kernel(x_ref, g_ref, o_ref):
    x = x_ref[0]
    o_ref[0] = (x * lax.rsqrt(jnp.mean(x * x, axis=-1, keepdims=True) + NORM_EPS)) * g_ref[...]


def _final_norm(x, g, T):
    B, S, D = x.shape
    return pl.pallas_call(
        _final_norm_kernel,
        grid=(B, T // TOK_TILE),
        in_specs=[pl.BlockSpec((1, TOK_TILE, D), lambda b, i: (b, i, 0)),
                  pl.BlockSpec((1, D), lambda b, i: (0, 0))],
        out_specs=pl.BlockSpec((1, TOK_TILE, D), lambda b, i: (b, i, 0)),
        out_shape=jax.ShapeDtypeStruct((B, T, D), F32),
        compiler_params=_cparams(2),
        name="final_norm",
    )(x, g.reshape(1, D).astype(F32))


def _axial_tables(T, n_ctx, rot_dim):
    half = rot_dim // 2
    inv_freq = ROPE_THETA ** (-jnp.arange(0, half, 2, dtype=F32) / half)
    rows = T // GRID_W
    row = jnp.broadcast_to(jnp.arange(rows, dtype=F32)[:, None], (rows, GRID_W)).reshape(-1)
    col = jnp.broadcast_to(jnp.arange(GRID_W, dtype=F32)[None, :], (rows, GRID_W)).reshape(-1)

    def ang(pos):
        a = pos[:, None] * inv_freq[None, :]
        return jnp.concatenate([a, a], axis=-1)

    a = jnp.concatenate([ang(row), ang(col)], axis=-1)
    cos = jnp.concatenate([jnp.cos(a), jnp.ones((n_ctx, rot_dim), F32)], axis=0)
    sin = jnp.concatenate([jnp.sin(a), jnp.zeros((n_ctx, rot_dim), F32)], axis=0)
    reps = LANES // rot_dim
    return cos.T, sin.T, jnp.tile(cos, (1, reps)), jnp.tile(sin, (1, reps))


def _plain_init(n_heads):
    return jnp.broadcast_to(jnp.array([NEG_INF, 0.0], F32)[None, :, None], (n_heads, 2, 128))


def _sink_init(sink):
    s = sink.astype(F32) * LOG2E
    return jnp.broadcast_to(jnp.stack([s, jnp.ones_like(s)], axis=1)[:, :, None], (s.shape[0], 2, 128))


def _even_mixer(xa, p, mods, T, with_ctx, lam_init, tabs):
    B, S, D = xa.shape
    qa, ka, va, qd, kd, vd = _even_prep(xa, p, mods, tabs)
    pack_a = _KeyPack(MLA_PAD, MLA_PAD, lambda h: h, lambda h: 0)
    per = LANES // DIFF_D
    pack_d = _KeyPack(LANES, DIFF_D, lambda h: h // per, lambda h: h % per)
    init_a, init_d = _plain_init(MLA_HEADS), _plain_init(2 * DIFF_HEADS)
    lam = p["lam"].astype(F32)
    lam_full = (jnp.exp(jnp.sum(lam[0] * lam[1])) - jnp.exp(jnp.sum(lam[2] * lam[3])) + lam_init).reshape(1, 1)
    sub = jnp.broadcast_to(p["subln"].astype(F32)[:, None], (DIFF_V, TOK_TILE))
    kern = functools.partial(_even_out_kernel, post_scale=1.0 - lam_init)
    oa = _flash(qa, ka, va, init_a, T, pack_a)
    od = _flash(qd, kd, vd, init_d, T, pack_d)
    xa = _out_proj(kern, xa, [oa, od], p["w_out"], mods["g_a"], [lam_full, sub], 0, T // TOK_TILE, 0)
    if with_ctx:
        oa = _ctx_attn(qa, ka, va, init_a, T, pack_a)
        od = _ctx_attn(qd, kd, vd, init_d, T, pack_d)
        xa = _out_proj(kern, xa, [oa, od], p["w_out"], mods["g_a"], [lam_full, sub], T // TOK_TILE, 1, 1)
    return xa


def _odd_mixer(xa, p, mods, T, with_ctx, tabs):
    B, S, D = xa.shape
    qc, qd, k, vc, vd = _odd_prep(xa, p, mods, tabs)
    gw, gd = WIN_HEADS // WIN_KV_HEADS, GLB_HEADS // GLB_KV_HEADS
    pack_c = _KeyPack(2 * LANES, HEAD_DIM, lambda h: 0, lambda h: h // gw)
    pack_d = _KeyPack(2 * LANES, HEAD_DIM, lambda h: 0, lambda h: WIN_KV_HEADS + h // gd)
    init_d = _plain_init(GLB_HEADS)
    sink = p["sink"].astype(F32) * LOG2E
    sink_w = jnp.broadcast_to(sink[:, None, None], (WIN_HEADS, 1, 128))
    od = _flash(qd, k, vd, init_d, T, pack_d)
    oc = _window(qc, k, vc, sink_w, T, pack_c)
    xa = _out_proj(_odd_out_kernel, xa, [oc, od], p["w_out"], mods["g_a"], [], 0, T // TOK_TILE, 0)
    if with_ctx:
        oc = _ctx_attn(qc, k, vc, _sink_init(p["sink"]), T, pack_c)
        od = _ctx_attn(qd, k, vd, init_d, T, pack_d)
        xa = _out_proj(_odd_out_kernel, xa, [oc, od], p["w_out"], mods["g_a"], [], T // TOK_TILE, 1, 1)
    return xa


def kernel(x, c, ctx, c_ctx, w_mod, b_mod, norm_mix, norm_ffn, even_w_in, even_norm_q, even_norm_kv, even_w_uq, even_w_ukv, even_lambda, even_subln, even_w_out, odd_w_in, odd_sink, odd_q_norm, odd_k_norm, odd_w_out, w_router, b_router, w_gate, w_up, w_down, norm_final):
    B, T, D = x.shape
    n_ctx = ctx.shape[1]
    depth = w_mod.shape[0]
    assert n_ctx == TOK_TILE == PV_TILE and T % MOE_TILE == 0 and B <= 7
    assert MLA_ROPE == DIFF_D
    S = T + n_ctx
    tabs32 = _axial_tables(T, n_ctx, MLA_ROPE)
    tabs64 = _axial_tables(T, n_ctx, HEAD_DIM)

    cond = jnp.zeros((8, D), F32).at[:B].set(c.astype(F32)).at[B].set(c_ctx.astype(F32))
    mod_all = _mod_vectors(cond, w_mod, b_mod)
    wrT = w_router.astype(BF16).T
    br = b_router.astype(F32).reshape(N_EXPERTS, 1)

    xa = jnp.concatenate([x, ctx], axis=1).astype(F32)
    for l in range(depth):
        with_ctx = l < depth - 1
        i = l // 2
        lat = mod_all[l, :B].reshape(B, 6, D)
        cx = jnp.broadcast_to(mod_all[l, B].reshape(1, 6, D), (B, 6, D))
        both = jnp.stack([lat, cx], axis=1)
        names = ("sh_a", "sc_a", "g_a", "sh_f", "sc_f", "g_f")
        mods = {n: both[:, :, j:j + 1, :] for j, n in enumerate(names)}
        if l % 2 == 0:
            lam_init = 0.8 - 0.6 * math.exp(-0.3 * l)
            p = dict(norm_mix=norm_mix[l], w_in=even_w_in[i], norm_q=even_norm_q[i], norm_kv=even_norm_kv[i],
                     w_uq=even_w_uq[i], w_ukv=even_w_ukv[i], lam=even_lambda[i], subln=even_subln[i],
                     w_out=even_w_out[i].astype(BF16))
            xa = _even_mixer(xa, p, mods, T, with_ctx, lam_init, tabs32)
        else:
            p = dict(norm_mix=norm_mix[l], w_in=odd_w_in[i], sink=odd_sink[i], q_norm=odd_q_norm[i],
                     k_norm=odd_k_norm[i], w_out=odd_w_out[i].astype(BF16))
            xa = _odd_mixer(xa, p, mods, T, with_ctx, tabs64)
        wg, wu, wd = w_gate[l].astype(BF16), w_up[l].astype(BF16), w_down[l].astype(BF16)
        ffn = (norm_ffn[l], mods["sh_f"], mods["sc_f"], mods["g_f"], wrT, br, wg, wu, wd)
        xa = _moe(xa, *ffn, MOE_TILE, 0, T // MOE_TILE, 0)
        if with_ctx:
            xa = _moe(xa, *ffn, TOK_TILE, T // TOK_TILE, 1, 1)
    return _final_norm(xa, norm_final, T)
```

```python
import functools
import math

import jax
import jax.numpy as jnp
from jax import lax
from jax.experimental import pallas as pl
from jax.experimental.pallas import tpu as pltpu

F32 = jnp.float32
BF16 = jnp.bfloat16
LOG2E = 1.4426950408889634

GRID_W = 64
ROPE_THETA = 10000.0
NORM_EPS = 1e-6
NEG_INF = -1e30

MLA_HEADS = 8
MLA_Q_LORA = 384
MLA_KV_LORA = 256
MLA_NOPE = 64
MLA_ROPE = 32
MLA_V = 64
MLA_SCALE = (MLA_NOPE + MLA_ROPE) ** -0.5
MLA_IN_COLS = MLA_Q_LORA + MLA_KV_LORA + MLA_ROPE
MLA_PAD = 128

DIFF_HEADS = 8
DIFF_D = 32
DIFF_V = 2 * DIFF_D
DIFF_SCALE = DIFF_D ** -0.5

HEAD_DIM = 64
WIN_HEADS = 8
WIN_KV_HEADS = 2
WINDOW = 128
GLB_HEADS = 8
GLB_KV_HEADS = 2
ATTN_SCALE = HEAD_DIM ** -0.5

N_EXPERTS = 16
N_GROUPS = 4
EXPERTS_PER_GROUP = N_EXPERTS // N_GROUPS

TOK_TILE = 256
Q_TILE = 1024
KEY_TILE = 2048
PV_TILE = 256
MOE_TILE = 1024
MOE_CHUNK = 128
MOE_STEP_EXPERTS = 2
COMB_TERMS = 3
ONES_ROWS = 16
LANES = 128
VMEM_LIMIT = 56 * 1024 * 1024


def _cparams(n_axes):
    return pltpu.CompilerParams(dimension_semantics=("arbitrary",) * n_axes,
                                vmem_limit_bytes=VMEM_LIMIT)


def _full(shape):
    return pl.BlockSpec(shape, lambda *_: (0,) * len(shape))


def _mod_kernel(a_ref, w_ref, b_ref, o_ref):
    a = a_ref[...]
    a = a * (1.0 / (1.0 + jnp.exp(-a)))
    o_ref[0] = jnp.dot(a.astype(BF16), w_ref[0].astype(BF16), preferred_element_type=F32) + b_ref[0]


def _mod_vectors(cond, w_mod, b_mod):
    L, D, N = w_mod.shape
    tn = 1536
    return pl.pallas_call(
        _mod_kernel,
        grid=(L, N // tn),
        in_specs=[pl.BlockSpec((8, D), lambda l, j: (0, 0)),
                  pl.BlockSpec((1, D, tn), lambda l, j: (l, 0, j)),
                  pl.BlockSpec((1, 1, tn), lambda l, j: (l, 0, j))],
        out_specs=pl.BlockSpec((1, 8, tn), lambda l, j: (l, 0, j)),
        out_shape=jax.ShapeDtypeStruct((L, 8, N), F32),
        compiler_params=_cparams(2),
        name="mod_vectors",
    )(cond, w_mod, b_mod.reshape(L, 1, N))


def _norm_mod(x, g, sh, sc):
    y = x * lax.rsqrt(jnp.mean(x * x, axis=-1, keepdims=True) + NORM_EPS)
    return (y * g) * (1.0 + sc) + sh


def _dot_t(w, h):
    return lax.dot_general(w, h, (((1,), (1,)), ((), ())), preferred_element_type=F32)


def _rms_rows(x):
    return lax.rsqrt(jnp.mean(x * x, axis=0, keepdims=True) + NORM_EPS)


def _rope_rows(x, cosT, sinT):
    q = x.shape[0] // 4
    rot = jnp.concatenate([-x[q:2 * q], x[0:q], -x[3 * q:4 * q], x[2 * q:3 * q]], axis=0)
    return x * cosT + rot * sinT


def _ones_rows(n):
    row = lax.broadcasted_iota(jnp.int32, (ONES_ROWS, n), 0)
    return jnp.where(row == 0, 1.0, 0.0).astype(BF16)


def _tile_lanes(x, reps):
    return x if reps == 1 else jnp.concatenate([x] * reps, axis=1)


def _even_prep_kernel(x_ref, g_ref, sh_ref, sc_ref, w1T_ref, wckv_ref, wkr_ref, wkrr_ref, wdk_ref, wdkr_ref,
                      wuqT_ref, wuvT_ref, wuk_ref, place_ref, gq_ref, gkvc_ref, gkvr_ref, cosT_ref, sinT_ref,
                      cosK_ref, sinK_ref,
                      qa_ref, ka_ref, va_ref, qd_ref, kd_ref, vd_ref):
    h = _norm_mod(x_ref[0], g_ref[...], sh_ref[0, 0], sc_ref[0, 0]).astype(BF16)
    n = h.shape[0]
    cosT, sinT, cosK, sinK = cosT_ref[...], sinT_ref[...], cosK_ref[...], sinK_ref[...]
    ones = _ones_rows(n)
    zT = _dot_t(w1T_ref[...], h)
    o1, o2 = MLA_Q_LORA, MLA_Q_LORA + MLA_KV_LORA
    o3 = o2 + 2 * DIFF_HEADS * DIFF_D

    c_q = zT[:o1]
    c_q = (c_q * _rms_rows(c_q) * gq_ref[...]).astype(BF16)
    qT = jnp.dot(wuqT_ref[...], c_q, preferred_element_type=F32)
    for hd in range(MLA_HEADS):
        r0 = hd * MLA_PAD
        rope = _rope_rows(qT[r0 + MLA_NOPE:r0 + MLA_NOPE + MLA_ROPE], cosT, sinT)
        head = jnp.concatenate([qT[r0:r0 + MLA_NOPE], rope, qT[r0 + MLA_NOPE + MLA_ROPE:r0 + MLA_PAD]], axis=0)
        qa_ref[hd] = (head * (MLA_SCALE * LOG2E)).astype(BF16)

    c_kvT = zT[o1:o2]
    c_kvT = (c_kvT * _rms_rows(c_kvT) * gkvc_ref[...]).astype(BF16)
    vT = jnp.dot(wuvT_ref[...], c_kvT, preferred_element_type=F32).astype(BF16)
    for hd in range(MLA_HEADS):
        va_ref[hd, 0:MLA_V, :] = vT[hd * MLA_V:(hd + 1) * MLA_V]
        va_ref[hd, MLA_V:, :] = ones

    for j in range(2 * DIFF_HEADS):
        qj = _rope_rows(zT[o2 + j * DIFF_D:o2 + (j + 1) * DIFF_D], cosT, sinT)
        qd_ref[j] = (qj * (DIFF_SCALE * LOG2E)).astype(BF16)
    for hd in range(DIFF_HEADS):
        vd_ref[hd, 0:DIFF_V, :] = zT[o3 + hd * DIFF_V:o3 + (hd + 1) * DIFF_V].astype(BF16)
        vd_ref[hd, DIFF_V:, :] = ones

    c_kv = jnp.dot(h, wckv_ref[...], preferred_element_type=F32)
    c_kv = (c_kv * lax.rsqrt(jnp.mean(c_kv * c_kv, axis=-1, keepdims=True) + NORM_EPS) * gkvr_ref[...]).astype(BF16)
    kr = (jnp.dot(h, wkr_ref[...], preferred_element_type=F32) * cosK
          + jnp.dot(h, wkrr_ref[...], preferred_element_type=F32) * sinK)
    ka = (jnp.dot(c_kv, wuk_ref[...], preferred_element_type=F32)
          + jnp.dot(kr.astype(BF16), place_ref[...], preferred_element_type=F32))
    ka_ref[...] = ka.astype(BF16)

    reps = wdk_ref.shape[1] // LANES
    kd = (jnp.dot(h, wdk_ref[...], preferred_element_type=F32) * _tile_lanes(cosK, reps)
          + jnp.dot(h, wdkr_ref[...], preferred_element_type=F32) * _tile_lanes(sinK, reps))
    kd_ref[...] = kd.astype(BF16)


def _rot_cols(w, width):
    d, n = w.shape
    w4 = w.reshape(d, n // width, 4, width // 4)
    return jnp.stack([-w4[:, :, 1], w4[:, :, 0], -w4[:, :, 3], w4[:, :, 2]], axis=2).reshape(d, n)


def _perm_cols(g, width):
    g4 = g.reshape(-1, 4, width // 4)
    return jnp.stack([g4[:, 1], g4[:, 0], g4[:, 3], g4[:, 2]], axis=1).reshape(-1)


def _even_prep(xa, p, mods, tabs):
    B, S, D = xa.shape
    w_in, w_uq, w_ukv = p["w_in"], p["w_uq"], p["w_ukv"]
    o1, o2, o3 = MLA_Q_LORA, MLA_Q_LORA + MLA_KV_LORA, MLA_IN_COLS
    nq = 2 * DIFF_HEADS * DIFF_D
    bf = lambda a: a.astype(BF16)
    w1T = bf(jnp.concatenate([w_in[:, :o2], w_in[:, o3:o3 + nq], w_in[:, o3 + 2 * nq:]], axis=1).T)
    wckv = bf(w_in[:, o1:o2])
    wkr = jnp.pad(w_in[:, o2:o3], ((0, 0), (0, LANES - MLA_ROPE)))
    wkrr = jnp.pad(_rot_cols(w_in[:, o2:o3], MLA_ROPE), ((0, 0), (0, LANES - MLA_ROPE)))
    wdk = w_in[:, o3 + nq:o3 + 2 * nq]
    wdkr = _rot_cols(wdk, DIFF_D)
    uq = w_uq.reshape(MLA_Q_LORA, MLA_HEADS, MLA_NOPE + MLA_ROPE)
    wuqT = bf(jnp.pad(uq, ((0, 0), (0, 0), (0, MLA_PAD - MLA_NOPE - MLA_ROPE))).reshape(MLA_Q_LORA, -1).T)
    ukv = w_ukv.reshape(MLA_KV_LORA, MLA_HEADS, MLA_NOPE + MLA_V)
    wuvT = bf(ukv[:, :, MLA_NOPE:].reshape(MLA_KV_LORA, -1).T)
    wuk = bf(jnp.pad(ukv[:, :, :MLA_NOPE], ((0, 0), (0, 0), (0, MLA_PAD - MLA_NOPE))).reshape(MLA_KV_LORA, -1))
    src = jnp.arange(LANES)[:, None]
    dst = jnp.arange(MLA_HEADS * MLA_PAD)[None, :]
    place = bf((src < MLA_ROPE) & (dst % MLA_PAD == src + MLA_NOPE))
    gq = jnp.broadcast_to(p["norm_q"].astype(F32)[:, None], (MLA_Q_LORA, TOK_TILE))
    gkvc = jnp.broadcast_to(p["norm_kv"].astype(F32)[:, None], (MLA_KV_LORA, TOK_TILE))
    gkvr = p["norm_kv"].astype(F32).reshape(1, MLA_KV_LORA)
    cosT, sinT, cosK, sinK = tabs
    weights = [w1T, wckv, bf(wkr), bf(wkrr), bf(wdk), bf(wdkr), wuqT, wuvT, wuk, place, gq, gkvc, gkvr]
    nt = S // TOK_TILE
    ctx_tile = nt - 1
    mod_map = lambda b, i: (b, (i == ctx_tile).astype(jnp.int32), 0, 0)
    HA, HD = MLA_HEADS, DIFF_HEADS
    dva = MLA_V + ONES_ROWS
    return pl.pallas_call(
        _even_prep_kernel,
        grid=(B, nt),
        in_specs=[pl.BlockSpec((1, TOK_TILE, D), lambda b, i: (b, i, 0)),
                  _full((1, D)),
                  pl.BlockSpec((1, 1, 1, D), mod_map),
                  pl.BlockSpec((1, 1, 1, D), mod_map)]
                 + [_full(w.shape) for w in weights]
                 + [pl.BlockSpec((MLA_ROPE, TOK_TILE), lambda b, i: (0, i)),
                    pl.BlockSpec((MLA_ROPE, TOK_TILE), lambda b, i: (0, i)),
                    pl.BlockSpec((TOK_TILE, LANES), lambda b, i: (i, 0)),
                    pl.BlockSpec((TOK_TILE, LANES), lambda b, i: (i, 0))],
        out_specs=[pl.BlockSpec((None, HA, MLA_PAD, TOK_TILE), lambda b, i: (b, 0, 0, i)),
                   pl.BlockSpec((None, TOK_TILE, HA * MLA_PAD), lambda b, i: (b, i, 0)),
                   pl.BlockSpec((None, HA, dva, TOK_TILE), lambda b, i: (b, 0, 0, i)),
                   pl.BlockSpec((None, 2 * HD, DIFF_D, TOK_TILE), lambda b, i: (b, 0, 0, i)),
                   pl.BlockSpec((None, TOK_TILE, nq), lambda b, i: (b, i, 0)),
                   pl.BlockSpec((None, HD, dva, TOK_TILE), lambda b, i: (b, 0, 0, i))],
        out_shape=[jax.ShapeDtypeStruct((B, HA, MLA_PAD, S), BF16),
                   jax.ShapeDtypeStruct((B, S, HA * MLA_PAD), BF16),
                   jax.ShapeDtypeStruct((B, HA, dva, S), BF16),
                   jax.ShapeDtypeStruct((B, 2 * HD, DIFF_D, S), BF16),
                   jax.ShapeDtypeStruct((B, S, nq), BF16),
                   jax.ShapeDtypeStruct((B, HD, dva, S), BF16)],
        compiler_params=_cparams(2),
        name="even_qkv_prep",
    )(xa, p["norm_mix"].reshape(1, D).astype(F32), mods["sh_a"], mods["sc_a"], *weights, cosT, sinT, cosK, sinK)


def _odd_prep_kernel(x_ref, g_ref, sh_ref, sc_ref, wqvT_ref, wk_ref, wkr_ref, gqn_ref, gk_ref, gkp_ref, bd_ref,
                     cosT_ref, sinT_ref, cosK_ref, sinK_ref,
                     qc_ref, qd_ref, k_ref, vc_ref, vd_ref):
    h = _norm_mod(x_ref[0], g_ref[...], sh_ref[0, 0], sc_ref[0, 0]).astype(BF16)
    n = h.shape[0]
    cosT, sinT, cosK, sinK = cosT_ref[...], sinT_ref[...], cosK_ref[...], sinK_ref[...]
    ones = _ones_rows(n)
    zT = _dot_t(wqvT_ref[...], h)
    d = HEAD_DIM
    for hd in range(WIN_HEADS):
        qc_ref[hd] = (_rope_rows(zT[hd * d:(hd + 1) * d], cosT, sinT) * (ATTN_SCALE * LOG2E)).astype(BF16)
    o1 = WIN_HEADS * d
    for hd in range(GLB_HEADS):
        q = zT[o1 + hd * d:o1 + (hd + 1) * d]
        q = _rope_rows(q * _rms_rows(q) * gqn_ref[...], cosT, sinT)
        qd_ref[hd] = (q * (ATTN_SCALE * LOG2E)).astype(BF16)
    o2 = o1 + GLB_HEADS * d
    for j in range(WIN_KV_HEADS):
        vc_ref[j, 0:d, :] = zT[o2 + j * d:o2 + (j + 1) * d].astype(BF16)
        vc_ref[j, d:, :] = ones
    o3 = o2 + WIN_KV_HEADS * d
    for j in range(GLB_KV_HEADS):
        vd_ref[j, 0:d, :] = zT[o3 + j * d:o3 + (j + 1) * d].astype(BF16)
        vd_ref[j, d:, :] = ones

    zk = jnp.dot(h, wk_ref[...], preferred_element_type=F32)
    zkr = jnp.dot(h, wkr_ref[...], preferred_element_type=F32)
    wc = WIN_KV_HEADS * d
    kc = zk[:, :wc] * cosK + zkr[:, :wc] * sinK
    z, zr = zk[:, wc:], zkr[:, wc:]
    sq = z * z
    hi = sq.astype(BF16)
    lo = (sq - hi.astype(F32)).astype(BF16)
    mean = (jnp.dot(hi, bd_ref[...], preferred_element_type=F32) + jnp.dot(lo, bd_ref[...], preferred_element_type=F32))
    kd = lax.rsqrt(mean + NORM_EPS) * (z * gk_ref[...] * cosK + zr * gkp_ref[...] * sinK)
    k_ref[...] = jnp.concatenate([kc, kd], axis=1).astype(BF16)


def _odd_prep(xa, p, mods, tabs):
    B, S, D = xa.shape
    w_in = p["w_in"]
    d = HEAD_DIM
    sizes = (WIN_HEADS, WIN_KV_HEADS, WIN_KV_HEADS, GLB_HEADS, GLB_KV_HEADS, GLB_KV_HEADS)
    offs = [0]
    for s in sizes:
        offs.append(offs[-1] + s * d)
    col = lambda j: w_in[:, offs[j]:offs[j + 1]]
    bf = lambda a: a.astype(BF16)
    wqvT = bf(jnp.concatenate([col(0), col(3), col(2), col(5)], axis=1).T)
    wk = jnp.concatenate([col(1), col(4)], axis=1)
    wkr = _rot_cols(wk, d)
    gqn = jnp.broadcast_to(p["q_norm"].astype(F32)[:, None], (d, TOK_TILE))
    gk1 = p["k_norm"].astype(F32)
    gk = jnp.tile(gk1, GLB_KV_HEADS).reshape(1, -1)
    gkp = jnp.tile(_perm_cols(gk1, d), GLB_KV_HEADS).reshape(1, -1)
    wd = GLB_KV_HEADS * d
    lane = jnp.arange(wd)
    bd = bf(jnp.where(lane[:, None] // d == lane[None, :] // d, 1.0 / d, 0.0))
    assert WIN_KV_HEADS * d == LANES and wd == LANES
    cosT, sinT, cosK, sinK = tabs
    weights = [wqvT, bf(wk), bf(wkr), gqn, gk, gkp, bd]
    nt = S // TOK_TILE
    ctx_tile = nt - 1
    mod_map = lambda b, i: (b, (i == ctx_tile).astype(jnp.int32), 0, 0)
    dva = d + ONES_ROWS
    return pl.pallas_call(
        _odd_prep_kernel,
        grid=(B, nt),
        in_specs=[pl.BlockSpec((1, TOK_TILE, D), lambda b, i: (b, i, 0)),
                  _full((1, D)),
                  pl.BlockSpec((1, 1, 1, D), mod_map),
                  pl.BlockSpec((1, 1, 1, D), mod_map)]
                 + [_full(w.shape) for w in weights]
                 + [pl.BlockSpec((d, TOK_TILE), lambda b, i: (0, i)),
                    pl.BlockSpec((d, TOK_TILE), lambda b, i: (0, i)),
                    pl.BlockSpec((TOK_TILE, LANES), lambda b, i: (i, 0)),
                    pl.BlockSpec((TOK_TILE, LANES), lambda b, i: (i, 0))],
        out_specs=[pl.BlockSpec((None, WIN_HEADS, d, TOK_TILE), lambda b, i: (b, 0, 0, i)),
                   pl.BlockSpec((None, GLB_HEADS, d, TOK_TILE), lambda b, i: (b, 0, 0, i)),
                   pl.BlockSpec((None, TOK_TILE, 2 * LANES), lambda b, i: (b, i, 0)),
                   pl.BlockSpec((None, WIN_KV_HEADS, dva, TOK_TILE), lambda b, i: (b, 0, 0, i)),
                   pl.BlockSpec((None, GLB_KV_HEADS, dva, TOK_TILE), lambda b, i: (b, 0, 0, i))],
        out_shape=[jax.ShapeDtypeStruct((B, WIN_HEADS, d, S), BF16),
                   jax.ShapeDtypeStruct((B, GLB_HEADS, d, S), BF16),
                   jax.ShapeDtypeStruct((B, S, 2 * LANES), BF16),
                   jax.ShapeDtypeStruct((B, WIN_KV_HEADS, dva, S), BF16),
                   jax.ShapeDtypeStruct((B, GLB_KV_HEADS, dva, S), BF16)],
        compiler_params=_cparams(2),
        name="odd_qkv_prep",
    )(xa, p["norm_mix"].reshape(1, D).astype(F32), mods["sh_a"], mods["sc_a"], *weights, cosT, sinT, cosK, sinK)


class _KeyPack:
    def __init__(self, kw, dk, block_of, slot_of):
        self.kw, self.dk, self.block_of, self.slot_of = kw, dk, block_of, slot_of


def _pad_queries(qT, pack, head):
    reps = pack.kw // qT.shape[0]
    if reps == 1:
        return qT
    rows = lax.broadcasted_iota(jnp.int32, (pack.kw, qT.shape[1]), 0)
    slot = pack.slot_of(head)
    keep = (rows >= slot * pack.dk) & (rows < (slot + 1) * pack.dk)
    return jnp.where(keep, jnp.concatenate([qT] * reps, axis=0), jnp.zeros((), qT.dtype))


def _aligned(start):
    return start if isinstance(start, int) else pl.multiple_of(start, PV_TILE)


def _attn_step(k_ref, vT_ref, acc_ref, q_next, next_offs, s_next, cur_offs, s_cur, m, cmax):
    m_new = jnp.maximum(m, cmax) if cur_offs else m
    pv = None
    cnext = None
    for i in range(max(len(next_offs), len(cur_offs))):
        rows = slice(i * PV_TILE, (i + 1) * PV_TILE)
        if i < len(next_offs):
            s = jnp.dot(k_ref[pl.ds(_aligned(next_offs[i]), PV_TILE), :], q_next, preferred_element_type=F32)
            s_next[rows, :] = s
            cm = jnp.max(s, axis=0, keepdims=True)
            cnext = cm if cnext is None else jnp.maximum(cnext, cm)
        if i < len(cur_offs):
            p = jnp.exp2(s_cur[rows, :] - m_new).astype(BF16)
            d = jnp.dot(vT_ref[:, pl.ds(_aligned(cur_offs[i]), PV_TILE)], p, preferred_element_type=F32)
            pv = d if pv is None else pv + d
    if cur_offs:
        acc_ref[...] = acc_ref[...] * jnp.exp2(m - m_new) + pv
    return m_new, cnext


def _softmax_init(init_ref, dva, dv, tq):
    m0 = jnp.broadcast_to(init_ref[0, 0:1, 0:1], (1, tq))
    row = lax.broadcasted_iota(jnp.int32, (dva, tq), 0)
    acc0 = jnp.where(row == dv, jnp.broadcast_to(init_ref[0, 1:2, 0:1], (dva, tq)), 0.0)
    return m0, acc0


def _flash_kernel(init_ref, qT_ref, qTn_ref, k_ref, vT_ref, o_ref, s0_ref, s1_ref, acc_ref, cm_ref, *,
                  T, n_ctx, kt, dv, pack):
    head = pl.program_id(1)
    qT = _pad_queries(qT_ref[...], pack, head)
    tq = qT.shape[1]
    dva = vT_ref.shape[0]
    pieces = lambda off: [off + kk for kk in range(0, kt, PV_TILE)]
    first = [T + kk for kk in range(0, n_ctx, PV_TILE)] + pieces(0)
    step = functools.partial(_attn_step, k_ref, vT_ref, acc_ref)

    @pl.when(pl.program_id(2) == 0)
    def _():
        _, cm_ref[0:1, :] = step(qT, first, s0_ref, [], None, None, None)

    m, acc0 = _softmax_init(init_ref, dva, dv, tq)
    acc_ref[...] = acc0
    m, cm_b = step(qT, pieces(kt), s1_ref, first, s0_ref, m, cm_ref[0:1, :])

    def body(t, carry):
        m, cm_b = carry
        off = pl.multiple_of(t * (2 * kt), kt)
        m, cm_a = step(qT, pieces(off + 2 * kt), s0_ref, pieces(off + kt), s1_ref, m, cm_b)
        m, cm_b = step(qT, pieces(off + 3 * kt), s1_ref, pieces(off + 2 * kt), s0_ref, m, cm_a)
        return m, cm_b

    m, cm_b = lax.fori_loop(0, T // (2 * kt) - 1, body, (m, cm_b))
    qn = _pad_queries(qTn_ref[...], pack, head)
    m, cm_ref[0:1, :] = step(qn, first, s0_ref, pieces(T - kt), s1_ref, m, cm_b)
    acc = acc_ref[...]
    o_ref[...] = acc[:dv] / acc[dv:dv + 1]


def _flash(qT, k, vT, init, T, pack):
    B, Hq, dk, S = qT.shape
    Hv, dva = vT.shape[1], vT.shape[2]
    dv = dva - ONES_ROWS
    gv = Hq // Hv
    kt = min(KEY_TILE, T // 2)
    tq = min(Q_TILE, T)
    assert T % (2 * kt) == 0 and T % tq == 0 and kt % PV_TILE == 0
    n_ctx = S - T
    assert n_ctx % PV_TILE == 0
    kern = functools.partial(_flash_kernel, T=T, n_ctx=n_ctx, kt=kt, dv=dv, pack=pack)
    nq = T // tq
    return pl.pallas_call(
        kern,
        grid=(B, Hq, nq),
        in_specs=[pl.BlockSpec((1, 2, 128), lambda b, h, i: (h, 0, 0)),
                  pl.BlockSpec((None, None, dk, tq), lambda b, h, i: (b, h, 0, i)),
                  pl.BlockSpec((None, None, dk, tq), lambda b, h, i: (b, h, 0, jnp.minimum(i + 1, nq - 1))),
                  pl.BlockSpec((None, S, pack.kw), lambda b, h, i: (b, 0, pack.block_of(h))),
                  pl.BlockSpec((None, None, dva, S), lambda b, h, i: (b, h // gv, 0, 0))],
        out_specs=pl.BlockSpec((None, None, dv, tq), lambda b, h, i: (b, h, 0, i)),
        out_shape=jax.ShapeDtypeStruct((B, Hq, dv, T), F32),
        scratch_shapes=[pltpu.VMEM((kt + n_ctx, tq), F32), pltpu.VMEM((kt, tq), F32),
                        pltpu.VMEM((dva, tq), F32), pltpu.VMEM((8, tq), F32)],
        compiler_params=_cparams(3),
        name="dense_attention",
    )(init, qT, qT, k, vT)


def _ctx_attn_kernel(init_ref, qT_ref, k_ref, vT_ref, o_ref, *, dv, pack):
    qT = _pad_queries(qT_ref[...], pack, pl.program_id(1))
    m0, acc0 = _softmax_init(init_ref, vT_ref.shape[0], dv, qT.shape[1])
    s = jnp.dot(k_ref[...], qT, preferred_element_type=F32)
    m = jnp.maximum(m0, jnp.max(s, axis=0, keepdims=True))
    acc = acc0 * jnp.exp2(m0 - m) + jnp.dot(vT_ref[...], jnp.exp2(s - m).astype(BF16), preferred_element_type=F32)
    o_ref[...] = acc[:dv] / acc[dv:dv + 1]


def _ctx_attn(qT, k, vT, init, T, pack):
    B, Hq, dk, S = qT.shape
    Hv, dva = vT.shape[1], vT.shape[2]
    dv = dva - ONES_ROWS
    gv = Hq // Hv
    n = S - T
    blk = T // n
    return pl.pallas_call(
        functools.partial(_ctx_attn_kernel, dv=dv, pack=pack),
        grid=(B, Hq),
        in_specs=[pl.BlockSpec((1, 2, 128), lambda b, h: (h, 0, 0)),
                  pl.BlockSpec((None, None, dk, n), lambda b, h: (b, h, 0, blk)),
                  pl.BlockSpec((None, n, pack.kw), lambda b, h: (b, blk, pack.block_of(h))),
                  pl.BlockSpec((None, None, dva, n), lambda b, h: (b, h // gv, 0, blk))],
        out_specs=pl.BlockSpec((None, None, dv, n), lambda b, h: (b, h, 0, 0)),
        out_shape=jax.ShapeDtypeStruct((B, Hq, dv, n), F32),
        compiler_params=_cparams(2),
        name="context_attention",
    )(init, qT, k, vT)


def _window_kernel(sink_ref, bias_ref, qT_ref, k_ref, kc_ref, vT_ref, o_ref, *, dv, pack):
    i = pl.program_id(2)
    G, _, tq = qT_ref.shape
    head = pl.program_id(1) * G
    qT = jnp.concatenate([_pad_queries(qT_ref[g], pack, head) for g in range(G)], axis=1)
    snk = jnp.concatenate([jnp.broadcast_to(sink_ref[g, 0:1, 0:1], (1, tq)) for g in range(G)], axis=1)
    W = WINDOW
    kband = jnp.concatenate([k_ref[0][tq - W:tq, :], k_ref[1][...], k_ref[2][0:W, :]], axis=0)
    before_first = jnp.where(i == 0, NEG_INF, 0.0)
    after_last = jnp.where(i == pl.num_programs(2) - 1, NEG_INF, 0.0)
    bias = jnp.concatenate([bias_ref[0:W, :] + before_first, bias_ref[W:W + tq, :],
                            bias_ref[W + tq:, :] + after_last], axis=0)
    s_loc = jnp.dot(kband, qT, preferred_element_type=F32) + bias
    s_ctx = jnp.dot(kc_ref[...], qT, preferred_element_type=F32)
    m = jnp.maximum(jnp.maximum(jnp.max(s_loc, axis=0, keepdims=True), jnp.max(s_ctx, axis=0, keepdims=True)), snk)
    p_loc = jnp.exp2(s_loc - m).astype(BF16)
    p_ctx = jnp.exp2(s_ctx - m).astype(BF16)
    vband = jnp.concatenate([vT_ref[0][:, tq - W:tq], vT_ref[1][...], vT_ref[2][:, 0:W]], axis=1)
    acc = (jnp.dot(vband, p_loc, preferred_element_type=F32)
           + jnp.dot(vT_ref[3][...], p_ctx, preferred_element_type=F32))
    o = acc[:dv] / (acc[dv:dv + 1] + jnp.exp2(snk - m))
    for g in range(G):
        o_ref[g] = o[:, g * tq:(g + 1) * tq]


def _window(qT, k, vT, sink, T, pack):
    B, Hq, dk, S = qT.shape
    Hk, dva = vT.shape[1], vT.shape[2]
    dv = dva - ONES_ROWS
    g = Hq // Hk
    tq = TOK_TILE
    nt = T // tq
    ctx_blk = T // tq
    clip = lambda j: jnp.clip(j, 0, nt - 1)
    kern = functools.partial(_window_kernel, dv=dv, pack=pack)
    kspec = lambda f: pl.BlockSpec((None, tq, pack.kw), lambda b, h, i: (b, f(i), pack.block_of(h * g)))
    vspec = lambda f: pl.BlockSpec((None, None, dva, tq), lambda b, h, i: (b, h, 0, f(i)))
    r = jnp.arange(tq + 2 * WINDOW)[:, None]
    c = jnp.arange(g * tq)[None, :] % tq
    bias = jnp.where(jnp.abs(r - WINDOW - c) <= WINDOW, 0.0, NEG_INF).astype(F32)

    def body(sink_ref, bias_ref, qT_ref, k0, k1, k2, kc, v0, v1, v2, vc, o_ref):
        kern(sink_ref, bias_ref, qT_ref, (k0, k1, k2), kc, (v0, v1, v2, vc), o_ref)

    return pl.pallas_call(
        body,
        grid=(B, Hk, nt),
        in_specs=[pl.BlockSpec((g, 1, 128), lambda b, h, i: (h, 0, 0)),
                  _full(bias.shape),
                  pl.BlockSpec((None, g, dk, tq), lambda b, h, i: (b, h, 0, i)),
                  kspec(lambda i: clip(i - 1)), kspec(lambda i: i), kspec(lambda i: clip(i + 1)),
                  kspec(lambda i: ctx_blk),
                  vspec(lambda i: clip(i - 1)), vspec(lambda i: i), vspec(lambda i: clip(i + 1)),
                  vspec(lambda i: ctx_blk)],
        out_specs=pl.BlockSpec((None, g, dv, tq), lambda b, h, i: (b, h, 0, i)),
        out_shape=jax.ShapeDtypeStruct((B, Hq, dv, T), F32),
        compiler_params=_cparams(3),
        name="window_attention",
    )(sink, bias, qT, k, k, k, k, vT, vT, vT, vT)


def _project_out(x_ref, oT, w_ref, gate_ref, y_ref):
    y = jnp.dot(oT.T.astype(BF16), w_ref[...], preferred_element_type=F32)
    y_ref[0] = x_ref[0] + gate_ref[0, 0] * y


def _even_out_kernel(x_ref, oa_ref, od_ref, w_ref, gate_ref, lam_ref, sub_ref, y_ref, *, post_scale):
    lam = lam_ref[...]
    parts = [oa_ref[hd] for hd in range(MLA_HEADS)]
    for hd in range(DIFF_HEADS):
        diff = od_ref[2 * hd] - lam * od_ref[2 * hd + 1]
        parts.append(diff * _rms_rows(diff) * sub_ref[...] * post_scale)
    _project_out(x_ref, jnp.concatenate(parts, axis=0), w_ref, gate_ref, y_ref)


def _odd_out_kernel(x_ref, oc_ref, od_ref, w_ref, gate_ref, y_ref):
    parts = [oc_ref[hd] for hd in range(WIN_HEADS)] + [od_ref[hd] for hd in range(GLB_HEADS)]
    _project_out(x_ref, jnp.concatenate(parts, axis=0), w_ref, gate_ref, y_ref)


def _out_proj(kern, x, heads, w, gate, extra, tile0, n_tiles, kind):
    B, S, D = x.shape
    row_map = lambda b, i: (b, i + tile0, 0)
    return pl.pallas_call(
        kern,
        grid=(B, n_tiles),
        in_specs=[pl.BlockSpec((1, TOK_TILE, D), row_map)]
                 + [pl.BlockSpec((None,) + o.shape[1:3] + (TOK_TILE,), lambda b, i: (b, 0, 0, i)) for o in heads]
                 + [_full(w.shape), pl.BlockSpec((1, 1, 1, D), lambda b, i: (b, kind, 0, 0))]
                 + [_full(e.shape) for e in extra],
        out_specs=pl.BlockSpec((1, TOK_TILE, D), row_map),
        out_shape=jax.ShapeDtypeStruct((B, S, D), F32),
        input_output_aliases={0: 0},
        compiler_params=_cparams(2),
        name="merge_out_proj_residual",
    )(x, *heads, w, gate, *extra)


def _first_argmax(vals):
    best, idx = vals[0], jnp.zeros(vals[0].shape, jnp.int32)
    for j in range(1, len(vals)):
        better = vals[j] > best
        idx = jnp.where(better, j, idx)
        best = jnp.where(better, vals[j], best)
    return idx, best


def _pick(idx, vals):
    out = vals[0]
    for j in range(1, len(vals)):
        out = jnp.where(idx == j, vals[j], out)
    return out


def _route(logits, bias):
    s = 1.0 / (1.0 + jnp.exp(-logits))
    sel = s + bias
    srow = [s[e:e + 1] for e in range(N_EXPERTS)]
    row = [sel[e:e + 1] for e in range(N_EXPERTS)]
    scores = []
    for g in range(N_GROUPS):
        a, b, c, d = row[4 * g:4 * g + 4]
        hi1, lo1, hi2, lo2 = jnp.maximum(a, b), jnp.minimum(a, b), jnp.maximum(c, d), jnp.minimum(c, d)
        top1 = jnp.maximum(hi1, hi2)
        top2 = jnp.maximum(jnp.maximum(lo1, lo2), jnp.minimum(hi1, hi2))
        scores.append(top1 + top2)
    gi, _ = _first_argmax(scores)
    v = [_pick(gi, [row[4 * g + j] for g in range(N_GROUPS)]) for j in range(EXPERTS_PER_GROUP)]
    sv = [_pick(gi, [srow[4 * g + j] for g in range(N_GROUPS)]) for j in range(EXPERTS_PER_GROUP)]
    i1, _ = _first_argmax(v)
    i2, _ = _first_argmax([jnp.where(i1 == j, -jnp.inf, v[j]) for j in range(EXPERTS_PER_GROUP)])
    w1, w2 = _pick(i1, sv), _pick(i2, sv)
    tot = w1 + w2
    w1, w2 = w1 / tot, w2 / tot
    rows = []
    for e in range(N_EXPERTS):
        g, j = divmod(e, EXPERTS_PER_GROUP)
        in_g = gi == g
        rows.append(jnp.where(in_g & (i1 == j), w1, 0.0) + jnp.where(in_g & (i2 == j), w2, 0.0))
    return jnp.concatenate(rows, axis=0), gi


def _split(x, terms):
    out = []
    for _ in range(terms):
        part = x.astype(BF16).astype(F32)
        out.append(part)
        x = x - part
    return out


def _to_column(row):
    n = row.shape[1]
    return jnp.concatenate([row, jnp.zeros((LANES - 1, n), F32)], axis=0).T[:, 0:1]


def _moe_kernel(x_ref, g_ref, sh_ref, sc_ref, gate_ref, wrT_ref, br_ref, before_ref, wg_ref, wu_ref, wd_ref, y_ref,
                hs_scr, cs_scr, ys_scr, q_scr, info_ref):
    step = pl.program_id(2)
    per_step = wg_ref.shape[0]
    tm = x_ref.shape[1]
    R = hs_scr.shape[0]

    @pl.when(step == 0)
    def _():
        h = _norm_mod(x_ref[0], g_ref[...], sh_ref[0, 0], sc_ref[0, 0]).astype(BF16)
        comb, gi = _route(_dot_t(wrT_ref[...], h), br_ref[...])
        sel = jnp.concatenate([(gi == g).astype(F32) for g in range(N_GROUPS)]
                              + [jnp.zeros((16 - N_GROUPS, tm), F32)], axis=0)
        rank = jnp.dot(sel.astype(BF16), before_ref[...], preferred_element_type=F32)
        cnt = jnp.sum(sel, axis=1, keepdims=True)
        seg = jnp.ceil(cnt * (1.0 / 16.0)) * 16.0
        pos = jnp.zeros((1, tm), F32)
        start = jnp.zeros((1, 1), F32)
        for g in range(N_GROUPS):
            pos = pos + sel[g:g + 1] * (start + rank[g:g + 1])
            info_ref[g] = start[0, 0].astype(jnp.int32)
            info_ref[N_GROUPS + g] = jnp.ceil(cnt[g:g + 1] * (1.0 / MOE_CHUNK))[0, 0].astype(jnp.int32)
            start = start + seg[g:g + 1]
        slot = lax.broadcasted_iota(jnp.int32, (R, tm), 0).astype(F32)
        P = jnp.where(slot == pos, 1.0, 0.0).astype(BF16)
        hs_scr[...] = jnp.dot(P, h, preferred_element_type=F32).astype(BF16)
        parts = _split(comb, COMB_TERMS)
        combT = jnp.concatenate(parts + [jnp.zeros((LANES - COMB_TERMS * N_EXPERTS, tm), F32)], axis=0).T
        cs_scr[...] = jnp.dot(P, combT.astype(BF16), preferred_element_type=F32)
        lane = lax.broadcasted_iota(jnp.int32, (tm, R), 1).astype(F32)
        q_scr[...] = jnp.where(lane == _to_column(pos), 1.0, 0.0).astype(BF16)
        ys_scr[...] = jnp.zeros_like(ys_scr)

    steps_per_group = EXPERTS_PER_GROUP // per_step
    grp = lax.shift_right_logical(step, steps_per_group.bit_length() - 1)
    start = info_ref[grp]
    n_chunks = info_ref[N_GROUPS + grp]

    def experts(r0, rows):
        h = hs_scr[pl.ds(r0, rows), :]
        cs = cs_scr[pl.ds(r0, rows), :]
        lane = lax.broadcasted_iota(jnp.int32, (rows, LANES), 1) & (N_EXPERTS - 1)
        total = None
        for j in range(per_step):
            a = jnp.dot(h, wg_ref[j], preferred_element_type=F32)
            u = jnp.dot(h, wu_ref[j], preferred_element_type=F32)
            act = (a * (1.0 / (1.0 + jnp.exp(-a)))) * u
            y = jnp.dot(act.astype(BF16), wd_ref[j], preferred_element_type=F32)
            w = jnp.sum(jnp.where(lane == step * per_step + j, cs, 0.0), axis=1, keepdims=True)
            total = w * y if total is None else total + w * y
        ys_scr[pl.ds(r0, rows), :] += total

    def pair(c, carry):
        experts(pl.multiple_of(start + c * (2 * MOE_CHUNK), 16), 2 * MOE_CHUNK)
        return carry

    lax.fori_loop(0, lax.shift_right_logical(n_chunks, 1), pair, 0)

    @pl.when((n_chunks & 1) == 1)
    def _():
        experts(pl.multiple_of(start + (n_chunks - 1) * MOE_CHUNK, 16), MOE_CHUNK)

    @pl.when(step == pl.num_programs(2) - 1)
    def _():
        q = q_scr[...]
        out = sum(jnp.dot(q, part.astype(BF16), preferred_element_type=F32) for part in _split(ys_scr[...], 2))
        y_ref[0] = x_ref[0] + gate_ref[0, 0] * out


def _moe(x, g, sh, sc, gate, wrT, br, wg, wu, wd, tm, tile0, n_tiles, kind):
    B, S, D = x.shape
    E, _, F = wg.shape
    R = -(-(tm + 16 * N_GROUPS + MOE_CHUNK) // 256) * 256
    idx = jnp.arange(tm)
    before = (idx[:, None] < idx[None, :]).astype(BF16)
    mod_map = lambda b, i, e: (b, kind, 0, 0)
    row_map = lambda b, i, e: (b, i + tile0, 0)
    return pl.pallas_call(
        _moe_kernel,
        grid=(B, n_tiles, E // MOE_STEP_EXPERTS),
        in_specs=[pl.BlockSpec((1, tm, D), row_map),
                  pl.BlockSpec((1, D), lambda b, i, e: (0, 0)),
                  pl.BlockSpec((1, 1, 1, D), mod_map),
                  pl.BlockSpec((1, 1, 1, D), mod_map),
                  pl.BlockSpec((1, 1, 1, D), mod_map),
                  pl.BlockSpec((E, D), lambda b, i, e: (0, 0)),
                  pl.BlockSpec((E, 1), lambda b, i, e: (0, 0)),
                  pl.BlockSpec((tm, tm), lambda b, i, e: (0, 0)),
                  pl.BlockSpec((MOE_STEP_EXPERTS, D, F), lambda b, i, e: (e, 0, 0)),
                  pl.BlockSpec((MOE_STEP_EXPERTS, D, F), lambda b, i, e: (e, 0, 0)),
                  pl.BlockSpec((MOE_STEP_EXPERTS, F, D), lambda b, i, e: (e, 0, 0))],
        out_specs=pl.BlockSpec((1, tm, D), row_map),
        out_shape=jax.ShapeDtypeStruct((B, S, D), F32),
        scratch_shapes=[pltpu.VMEM((R, D), BF16), pltpu.VMEM((R, LANES), F32), pltpu.VMEM((R, D), F32),
                        pltpu.VMEM((tm, R), BF16), pltpu.SMEM((2 * N_GROUPS,), jnp.int32)],
        input_output_aliases={0: 0},
        compiler_params=_cparams(3),
        name="moe_experts",
    )(x, g.reshape(1, D).astype(F32), sh, sc, gate, wrT, br, before, wg, wu, wd)


def _final_norm_kernel(x_ref, g_ref, o_ref):
    x = x_ref[0]
    o_ref[0] = (x * lax.rsqrt(jnp.mean(x * x, axis=-1, keepdims=True) + NORM_EPS)) * g_ref[...]


def _final_norm(x, g, T):
    B, S, D = x.shape
    return pl.pallas_call(
        _final_norm_kernel,
        grid=(B, T // TOK_TILE),
        in_specs=[pl.BlockSpec((1, TOK_TILE, D), lambda b, i: (b, i, 0)),
                  pl.BlockSpec((1, D), lambda b, i: (0, 0))],
        out_specs=pl.BlockSpec((1, TOK_TILE, D), lambda b, i: (b, i, 0)),
        out_shape=jax.ShapeDtypeStruct((B, T, D), F32),
        compiler_params=_cparams(2),
        name="final_norm",
    )(x, g.reshape(1, D).astype(F32))


def _axial_tables(T, n_ctx, rot_dim):
    half = rot_dim // 2
    inv_freq = ROPE_THETA ** (-jnp.arange(0, half, 2, dtype=F32) / half)
    rows = T // GRID_W
    row = jnp.broadcast_to(jnp.arange(rows, dtype=F32)[:, None], (rows, GRID_W)).reshape(-1)
    col = jnp.broadcast_to(jnp.arange(GRID_W, dtype=F32)[None, :], (rows, GRID_W)).reshape(-1)

    def ang(pos):
        a = pos[:, None] * inv_freq[None, :]
        return jnp.concatenate([a, a], axis=-1)

    a = jnp.concatenate([ang(row), ang(col)], axis=-1)
    cos = jnp.concatenate([jnp.cos(a), jnp.ones((n_ctx, rot_dim), F32)], axis=0)
    sin = jnp.concatenate([jnp.sin(a), jnp.zeros((n_ctx, rot_dim), F32)], axis=0)
    reps = LANES // rot_dim
    return cos.T, sin.T, jnp.tile(cos, (1, reps)), jnp.tile(sin, (1, reps))


def _plain_init(n_heads):
    return jnp.broadcast_to(jnp.array([NEG_INF, 0.0], F32)[None, :, None], (n_heads, 2, 128))


def _sink_init(sink):
    s = sink.astype(F32) * LOG2E
    return jnp.broadcast_to(jnp.stack([s, jnp.ones_like(s)], axis=1)[:, :, None], (s.shape[0], 2, 128))


def _even_mixer(xa, p, mods, T, with_ctx, lam_init, tabs):
    B, S, D = xa.shape
    qa, ka, va, qd, kd, vd = _even_prep(xa, p, mods, tabs)
    pack_a = _KeyPack(MLA_PAD, MLA_PAD, lambda h: h, lambda h: 0)
    per = LANES // DIFF_D
    pack_d = _KeyPack(LANES, DIFF_D, lambda h: h // per, lambda h: h % per)
    init_a, init_d = _plain_init(MLA_HEADS), _plain_init(2 * DIFF_HEADS)
    lam = p["lam"].astype(F32)
    lam_full = (jnp.exp(jnp.sum(lam[0] * lam[1])) - jnp.exp(jnp.sum(lam[2] * lam[3])) + lam_init).reshape(1, 1)
    sub = jnp.broadcast_to(p["subln"].astype(F32)[:, None], (DIFF_V, TOK_TILE))
    kern = functools.partial(_even_out_kernel, post_scale=1.0 - lam_init)
    oa = _flash(qa, ka, va, init_a, T, pack_a)
    od = _flash(qd, kd, vd, init_d, T, pack_d)
    xa = _out_proj(kern, xa, [oa, od], p["w_out"], mods["g_a"], [lam_full, sub], 0, T // TOK_TILE, 0)
    if with_ctx:
        oa = _ctx_attn(qa, ka, va, init_a, T, pack_a)
        od = _ctx_attn(qd, kd, vd, init_d, T, pack_d)
        xa = _out_proj(kern, xa, [oa, od], p["w_out"], mods["g_a"], [lam_full, sub], T // TOK_TILE, 1, 1)
    return xa


def _odd_mixer(xa, p, mods, T, with_ctx, tabs):
    B, S, D = xa.shape
    qc, qd, k, vc, vd = _odd_prep(xa, p, mods, tabs)
    gw, gd = WIN_HEADS // WIN_KV_HEADS, GLB_HEADS // GLB_KV_HEADS
    pack_c = _KeyPack(2 * LANES, HEAD_DIM, lambda h: 0, lambda h: h // gw)
    pack_d = _KeyPack(2 * LANES, HEAD_DIM, lambda h: 0, lambda h: WIN_KV_HEADS + h // gd)
    init_d = _plain_init(GLB_HEADS)
    sink = p["sink"].astype(F32) * LOG2E
    sink_w = jnp.broadcast_to(sink[:, None, None], (WIN_HEADS, 1, 128))
    od = _flash(qd, k, vd, init_d, T, pack_d)
    oc = _window(qc, k, vc, sink_w, T, pack_c)
    xa = _out_proj(_odd_out_kernel, xa, [oc, od], p["w_out"], mods["g_a"], [], 0, T // TOK_TILE, 0)
    if with_ctx:
        oc = _ctx_attn(qc, k, vc, _sink_init(p["sink"]), T, pack_c)
        od = _ctx_attn(qd, k, vd, init_d, T, pack_d)
        xa = _out_proj(_odd_out_kernel, xa, [oc, od], p["w_out"], mods["g_a"], [], T // TOK_TILE, 1, 1)
    return xa


def kernel(x, c, ctx, c_ctx, w_mod, b_mod, norm_mix, norm_ffn, even_w_in, even_norm_q, even_norm_kv, even_w_uq, even_w_ukv, even_lambda, even_subln, even_w_out, odd_w_in, odd_sink, odd_q_norm, odd_k_norm, odd_w_out, w_router, b_router, w_gate, w_up, w_down, norm_final):
    B, T, D = x.shape
    n_ctx = ctx.shape[1]
    depth = w_mod.shape[0]
    assert n_ctx == TOK_TILE == PV_TILE and T % MOE_TILE == 0 and B <= 7
    assert MLA_ROPE == DIFF_D
    S = T + n_ctx
    tabs32 = _axial_tables(T, n_ctx, MLA_ROPE)
    tabs64 = _axial_tables(T, n_ctx, HEAD_DIM)

    cond = jnp.zeros((8, D), F32).at[:B].set(c.astype(F32)).at[B].set(c_ctx.astype(F32))
    mod_all = _mod_vectors(cond, w_mod, b_mod)
    wrT = w_router.astype(BF16).T
    br = b_router.astype(F32).reshape(N_EXPERTS, 1)

    xa = jnp.concatenate([x, ctx], axis=1).astype(F32)
    for l in range(depth):
        with_ctx = l < depth - 1
        i = l // 2
        lat = mod_all[l, :B].reshape(B, 6, D)
        cx = jnp.broadcast_to(mod_all[l, B].reshape(1, 6, D), (B, 6, D))
        both = jnp.stack([lat, cx], axis=1)
        names = ("sh_a", "sc_a", "g_a", "sh_f", "sc_f", "g_f")
        mods = {n: both[:, :, j:j + 1, :] for j, n in enumerate(names)}
        if l % 2 == 0:
            lam_init = 0.8 - 0.6 * math.exp(-0.3 * l)
            p = dict(norm_mix=norm_mix[l], w_in=even_w_in[i], norm_q=even_norm_q[i], norm_kv=even_norm_kv[i],
                     w_uq=even_w_uq[i], w_ukv=even_w_ukv[i], lam=even_lambda[i], subln=even_subln[i],
                     w_out=even_w_out[i].astype(BF16))
            xa = _even_mixer(xa, p, mods, T, with_ctx, lam_init, tabs32)
        else:
            p = dict(norm_mix=norm_mix[l], w_in=odd_w_in[i], sink=odd_sink[i], q_norm=odd_q_norm[i],
                     k_norm=odd_k_norm[i], w_out=odd_w_out[i].astype(BF16))
            xa = _odd_mixer(xa, p, mods, T, with_ctx, tabs64)
        wg, wu, wd = w_gate[l].astype(BF16), w_up[l].astype(BF16), w_down[l].astype(BF16)
        ffn = (norm_ffn[l], mods["sh_f"], mods["sc_f"], mods["g_f"], wrT, br, wg, wu, wd)
        xa = _moe(xa, *ffn, MOE_TILE, 0, T // MOE_TILE, 0)
        if with_ctx:
            xa = _moe(xa, *ffn, TOK_TILE, T // TOK_TILE, 1, 1)
    return _final_norm(xa, norm_final, T)
```

```python
import functools
import math

import jax
import jax.numpy as jnp
from jax import lax
from jax.experimental import pallas as pl
from jax.experimental.pallas import tpu as pltpu

F32 = jnp.float32
BF16 = jnp.bfloat16
LOG2E = 1.4426950408889634

GRID_W = 64
ROPE_THETA = 10000.0
NORM_EPS = 1e-6
NEG_INF = -1e30

MLA_HEADS = 8
MLA_Q_LORA = 384
MLA_KV_LORA = 256
MLA_NOPE = 64
MLA_ROPE = 32
MLA_V = 64
MLA_SCALE = (MLA_NOPE + MLA_ROPE) ** -0.5
MLA_IN_COLS = MLA_Q_LORA + MLA_KV_LORA + MLA_ROPE
MLA_PAD = 128

DIFF_HEADS = 8
DIFF_D = 32
DIFF_V = 2 * DIFF_D
DIFF_SCALE = DIFF_D ** -0.5

HEAD_DIM = 64
WIN_HEADS = 8
WIN_KV_HEADS = 2
WINDOW = 128
GLB_HEADS = 8
GLB_KV_HEADS = 2
ATTN_SCALE = HEAD_DIM ** -0.5

N_EXPERTS = 16
N_GROUPS = 4
EXPERTS_PER_GROUP = N_EXPERTS // N_GROUPS

TOK_TILE = 256
Q_TILE = 1024
KEY_TILE = 2048
PV_TILE = 256
MOE_TILE = 1024
MOE_CHUNK = 128
MOE_STEP_EXPERTS = 2
COMB_TERMS = 3
ONES_ROWS = 16
LANES = 128
VMEM_LIMIT = 56 * 1024 * 1024


def _cparams(n_axes):
    return pltpu.CompilerParams(dimension_semantics=("arbitrary",) * n_axes,
                                vmem_limit_bytes=VMEM_LIMIT)


def _full(shape):
    return pl.BlockSpec(shape, lambda *_: (0,) * len(shape))


def _mod_kernel(a_ref, w_ref, b_ref, o_ref):
    a = a_ref[...]
    a = a * (1.0 / (1.0 + jnp.exp(-a)))
    o_ref[0] = jnp.dot(a.astype(BF16), w_ref[0].astype(BF16), preferred_element_type=F32) + b_ref[0]


def _mod_vectors(cond, w_mod, b_mod):
    L, D, N = w_mod.shape
    tn = 1536
    return pl.pallas_call(
        _mod_kernel,
        grid=(L, N // tn),
        in_specs=[pl.BlockSpec((8, D), lambda l, j: (0, 0)),
                  pl.BlockSpec((1, D, tn), lambda l, j: (l, 0, j)),
                  pl.BlockSpec((1, 1, tn), lambda l, j: (l, 0, j))],
        out_specs=pl.BlockSpec((1, 8, tn), lambda l, j: (l, 0, j)),
        out_shape=jax.ShapeDtypeStruct((L, 8, N), F32),
        compiler_params=_cparams(2),
        name="mod_vectors",
    )(cond, w_mod, b_mod.reshape(L, 1, N))


def _norm_mod(x, g, sh, sc):
    y = x * lax.rsqrt(jnp.mean(x * x, axis=-1, keepdims=True) + NORM_EPS)
    return (y * g) * (1.0 + sc) + sh


def _dot_t(w, h):
    return lax.dot_general(w, h, (((1,), (1,)), ((), ())), preferred_element_type=F32)


def _rms_rows(x):
    return lax.rsqrt(jnp.mean(x * x, axis=0, keepdims=True) + NORM_EPS)


def _rope_rows(x, cosT, sinT):
    q = x.shape[0] // 4
    rot = jnp.concatenate([-x[q:2 * q], x[0:q], -x[3 * q:4 * q], x[2 * q:3 * q]], axis=0)
    return x * cosT + rot * sinT


def _ones_rows(n):
    row = lax.broadcasted_iota(jnp.int32, (ONES_ROWS, n), 0)
    return jnp.where(row == 0, 1.0, 0.0).astype(BF16)


def _tile_lanes(x, reps):
    return x if reps == 1 else jnp.concatenate([x] * reps, axis=1)


def _even_prep_kernel(x_ref, g_ref, sh_ref, sc_ref, w1T_ref, wckv_ref, wkr_ref, wkrr_ref, wdk_ref, wdkr_ref,
                      wuqT_ref, wuvT_ref, wuk_ref, place_ref, gq_ref, gkvc_ref, gkvr_ref, cosT_ref, sinT_ref,
                      cosK_ref, sinK_ref,
                      qa_ref, ka_ref, va_ref, qd_ref, kd_ref, vd_ref):
    h = _norm_mod(x_ref[0], g_ref[...], sh_ref[0, 0], sc_ref[0, 0]).astype(BF16)
    n = h.shape[0]
    cosT, sinT, cosK, sinK = cosT_ref[...], sinT_ref[...], cosK_ref[...], sinK_ref[...]
    ones = _ones_rows(n)
    zT = _dot_t(w1T_ref[...], h)
    o1, o2 = MLA_Q_LORA, MLA_Q_LORA + MLA_KV_LORA
    o3 = o2 + 2 * DIFF_HEADS * DIFF_D

    c_q = zT[:o1]
    c_q = (c_q * _rms_rows(c_q) * gq_ref[...]).astype(BF16)
    qT = jnp.dot(wuqT_ref[...], c_q, preferred_element_type=F32)
    for hd in range(MLA_HEADS):
        r0 = hd * MLA_PAD
        rope = _rope_rows(qT[r0 + MLA_NOPE:r0 + MLA_NOPE + MLA_ROPE], cosT, sinT)
        head = jnp.concatenate([qT[r0:r0 + MLA_NOPE], rope, qT[r0 + MLA_NOPE + MLA_ROPE:r0 + MLA_PAD]], axis=0)
        qa_ref[hd] = (head * (MLA_SCALE * LOG2E)).astype(BF16)

    c_kvT = zT[o1:o2]
    c_kvT = (c_kvT * _rms_rows(c_kvT) * gkvc_ref[...]).astype(BF16)
    vT = jnp.dot(wuvT_ref[...], c_kvT, preferred_element_type=F32).astype(BF16)
    for hd in range(MLA_HEADS):
        va_ref[hd, 0:MLA_V, :] = vT[hd * MLA_V:(hd + 1) * MLA_V]
        va_ref[hd, MLA_V:, :] = ones

    for j in range(2 * DIFF_HEADS):
        qj = _rope_rows(zT[o2 + j * DIFF_D:o2 + (j + 1) * DIFF_D], cosT, sinT)
        qd_ref[j] = (qj * (DIFF_SCALE * LOG2E)).astype(BF16)
    for hd in range(DIFF_HEADS):
        vd_ref[hd, 0:DIFF_V, :] = zT[o3 + hd * DIFF_V:o3 + (hd + 1) * DIFF_V].astype(BF16)
        vd_ref[hd, DIFF_V:, :] = ones

    c_kv = jnp.dot(h, wckv_ref[...], preferred_element_type=F32)
    c_kv = (c_kv * lax.rsqrt(jnp.mean(c_kv * c_kv, axis=-1, keepdims=True) + NORM_EPS) * gkvr_ref[...]).astype(BF16)
    kr = (jnp.dot(h, wkr_ref[...], preferred_element_type=F32) * cosK
          + jnp.dot(h, wkrr_ref[...], preferred_element_type=F32) * sinK)
    ka = (jnp.dot(c_kv, wuk_ref[...], preferred_element_type=F32)
          + jnp.dot(kr.astype(BF16), place_ref[...], preferred_element_type=F32))
    ka_ref[...] = ka.astype(BF16)

    reps = wdk_ref.shape[1] // LANES
    kd = (jnp.dot(h, wdk_ref[...], preferred_element_type=F32) * _tile_lanes(cosK, reps)
          + jnp.dot(h, wdkr_ref[...], preferred_element_type=F32) * _tile_lanes(sinK, reps))
    kd_ref[...] = kd.astype(BF16)


def _rot_cols(w, width):
    d, n = w.shape
    w4 = w.reshape(d, n // width, 4, width // 4)
    return jnp.stack([-w4[:, :, 1], w4[:, :, 0], -w4[:, :, 3], w4[:, :, 2]], axis=2).reshape(d, n)


def _perm_cols(g, width):
    g4 = g.reshape(-1, 4, width // 4)
    return jnp.stack([g4[:, 1], g4[:, 0], g4[:, 3], g4[:, 2]], axis=1).reshape(-1)


def _even_prep(xa, p, mods, tabs):
    B, S, D = xa.shape
    w_in, w_uq, w_ukv = p["w_in"], p["w_uq"], p["w_ukv"]
    o1, o2, o3 = MLA_Q_LORA, MLA_Q_LORA + MLA_KV_LORA, MLA_IN_COLS
    nq = 2 * DIFF_HEADS * DIFF_D
    bf = lambda a: a.astype(BF16)
    w1T = bf(jnp.concatenate([w_in[:, :o2], w_in[:, o3:o3 + nq], w_in[:, o3 + 2 * nq:]], axis=1).T)
    wckv = bf(w_in[:, o1:o2])
    wkr = jnp.pad(w_in[:, o2:o3], ((0, 0), (0, LANES - MLA_ROPE)))
    wkrr = jnp.pad(_rot_cols(w_in[:, o2:o3], MLA_ROPE), ((0, 0), (0, LANES - MLA_ROPE)))
    wdk = w_in[:, o3 + nq:o3 + 2 * nq]
    wdkr = _rot_cols(wdk, DIFF_D)
    uq = w_uq.reshape(MLA_Q_LORA, MLA_HEADS, MLA_NOPE + MLA_ROPE)
    wuqT = bf(jnp.pad(uq, ((0, 0), (0, 0), (0, MLA_PAD - MLA_NOPE - MLA_ROPE))).reshape(MLA_Q_LORA, -1).T)
    ukv = w_ukv.reshape(MLA_KV_LORA, MLA_HEADS, MLA_NOPE + MLA_V)
    wuvT = bf(ukv[:, :, MLA_NOPE:].reshape(MLA_KV_LORA, -1).T)
    wuk = bf(jnp.pad(ukv[:, :, :MLA_NOPE], ((0, 0), (0, 0), (0, MLA_PAD - MLA_NOPE))).reshape(MLA_KV_LORA, -1))
    src = jnp.arange(LANES)[:, None]
    dst = jnp.arange(MLA_HEADS * MLA_PAD)[None, :]
    place = bf((src < MLA_ROPE) & (dst % MLA_PAD == src + MLA_NOPE))
    gq = jnp.broadcast_to(p["norm_q"].astype(F32)[:, None], (MLA_Q_LORA, TOK_TILE))
    gkvc = jnp.broadcast_to(p["norm_kv"].astype(F32)[:, None], (MLA_KV_LORA, TOK_TILE))
    gkvr = p["norm_kv"].astype(F32).reshape(1, MLA_KV_LORA)
    cosT, sinT, cosK, sinK = tabs
    weights = [w1T, wckv, bf(wkr), bf(wkrr), bf(wdk), bf(wdkr), wuqT, wuvT, wuk, place, gq, gkvc, gkvr]
    nt = S // TOK_TILE
    ctx_tile = nt - 1
    mod_map = lambda b, i: (b, (i == ctx_tile).astype(jnp.int32), 0, 0)
    HA, HD = MLA_HEADS, DIFF_HEADS
    dva = MLA_V + ONES_ROWS
    return pl.pallas_call(
        _even_prep_kernel,
        grid=(B, nt),
        in_specs=[pl.BlockSpec((1, TOK_TILE, D), lambda b, i: (b, i, 0)),
                  _full((1, D)),
                  pl.BlockSpec((1, 1, 1, D), mod_map),
                  pl.BlockSpec((1, 1, 1, D), mod_map)]
                 + [_full(w.shape) for w in weights]
                 + [pl.BlockSpec((MLA_ROPE, TOK_TILE), lambda b, i: (0, i)),
                    pl.BlockSpec((MLA_ROPE, TOK_TILE), lambda b, i: (0, i)),
                    pl.BlockSpec((TOK_TILE, LANES), lambda b, i: (i, 0)),
                    pl.BlockSpec((TOK_TILE, LANES), lambda b, i: (i, 0))],
        out_specs=[pl.BlockSpec((None, HA, MLA_PAD, TOK_TILE), lambda b, i: (b, 0, 0, i)),
                   pl.BlockSpec((None, TOK_TILE, HA * MLA_PAD), lambda b, i: (b, i, 0)),
                   pl.BlockSpec((None, HA, dva, TOK_TILE), lambda b, i: (b, 0, 0, i)),
                   pl.BlockSpec((None, 2 * HD, DIFF_D, TOK_TILE), lambda b, i: (b, 0, 0, i)),
                   pl.BlockSpec((None, TOK_TILE, nq), lambda b, i: (b, i, 0)),
                   pl.BlockSpec((None, HD, dva, TOK_TILE), lambda b, i: (b, 0, 0, i))],
        out_shape=[jax.ShapeDtypeStruct((B, HA, MLA_PAD, S), BF16),
                   jax.ShapeDtypeStruct((B, S, HA * MLA_PAD), BF16),
                   jax.ShapeDtypeStruct((B, HA, dva, S), BF16),
                   jax.ShapeDtypeStruct((B, 2 * HD, DIFF_D, S), BF16),
                   jax.ShapeDtypeStruct((B, S, nq), BF16),
                   jax.ShapeDtypeStruct((B, HD, dva, S), BF16)],
        compiler_params=_cparams(2),
        name="even_qkv_prep",
    )(xa, p["norm_mix"].reshape(1, D).astype(F32), mods["sh_a"], mods["sc_a"], *weights, cosT, sinT, cosK, sinK)


def _odd_prep_kernel(x_ref, g_ref, sh_ref, sc_ref, wqvT_ref, wk_ref, wkr_ref, gqn_ref, gk_ref, gkp_ref, bd_ref,
                     cosT_ref, sinT_ref, cosK_ref, sinK_ref,
                     qc_ref, qd_ref, k_ref, vc_ref, vd_ref):
    h = _norm_mod(x_ref[0], g_ref[...], sh_ref[0, 0], sc_ref[0, 0]).astype(BF16)
    n = h.shape[0]
    cosT, sinT, cosK, sinK = cosT_ref[...], sinT_ref[...], cosK_ref[...], sinK_ref[...]
    ones = _ones_rows(n)
    zT = _dot_t(wqvT_ref[...], h)
    d = HEAD_DIM
    for hd in range(WIN_HEADS):
        qc_ref[hd] = (_rope_rows(zT[hd * d:(hd + 1) * d], cosT, sinT) * (ATTN_SCALE * LOG2E)).astype(BF16)
    o1 = WIN_HEADS * d
    for hd in range(GLB_HEADS):
        q = zT[o1 + hd * d:o1 + (hd + 1) * d]
        q = _rope_rows(q * _rms_rows(q) * gqn_ref[...], cosT, sinT)
        qd_ref[hd] = (q * (ATTN_SCALE * LOG2E)).astype(BF16)
    o2 = o1 + GLB_HEADS * d
    for j in range(WIN_KV_HEADS):
        vc_ref[j, 0:d, :] = zT[o2 + j * d:o2 + (j + 1) * d].astype(BF16)
        vc_ref[j, d:, :] = ones
    o3 = o2 + WIN_KV_HEADS * d
    for j in range(GLB_KV_HEADS):
        vd_ref[j, 0:d, :] = zT[o3 + j * d:o3 + (j + 1) * d].astype(BF16)
        vd_ref[j, d:, :] = ones

    zk = jnp.dot(h, wk_ref[...], preferred_element_type=F32)
    zkr = jnp.dot(h, wkr_ref[...], preferred_element_type=F32)
    wc = WIN_KV_HEADS * d
    kc = zk[:, :wc] * cosK + zkr[:, :wc] * sinK
    z, zr = zk[:, wc:], zkr[:, wc:]
    sq = z * z
    hi = sq.astype(BF16)
    lo = (sq - hi.astype(F32)).astype(BF16)
    mean = (jnp.dot(hi, bd_ref[...], preferred_element_type=F32) + jnp.dot(lo, bd_ref[...], preferred_element_type=F32))
    kd = lax.rsqrt(mean + NORM_EPS) * (z * gk_ref[...] * cosK + zr * gkp_ref[...] * sinK)
    k_ref[...] = jnp.concatenate([kc, kd], axis=1).astype(BF16)


def _odd_prep(xa, p, mods, tabs):
    B, S, D = xa.shape
    w_in = p["w_in"]
    d = HEAD_DIM
    sizes = (WIN_HEADS, WIN_KV_HEADS, WIN_KV_HEADS, GLB_HEADS, GLB_KV_HEADS, GLB_KV_HEADS)
    offs = [0]
    for s in sizes:
        offs.append(offs[-1] + s * d)
    col = lambda j: w_in[:, offs[j]:offs[j + 1]]
    bf = lambda a: a.astype(BF16)
    wqvT = bf(jnp.concatenate([col(0), col(3), col(2), col(5)], axis=1).T)
    wk = jnp.concatenate([col(1), col(4)], axis=1)
    wkr = _rot_cols(wk, d)
    gqn = jnp.broadcast_to(p["q_norm"].astype(F32)[:, None], (d, TOK_TILE))
    gk1 = p["k_norm"].astype(F32)
    gk = jnp.tile(gk1, GLB_KV_HEADS).reshape(1, -1)
    gkp = jnp.tile(_perm_cols(gk1, d), GLB_KV_HEADS).reshape(1, -1)
    wd = GLB_KV_HEADS * d
    lane = jnp.arange(wd)
    bd = bf(jnp.where(lane[:, None] // d == lane[None, :] // d, 1.0 / d, 0.0))
    assert WIN_KV_HEADS * d == LANES and wd == LANES
    cosT, sinT, cosK, sinK = tabs
    weights = [wqvT, bf(wk), bf(wkr), gqn, gk, gkp, bd]
    nt = S // TOK_TILE
    ctx_tile = nt - 1
    mod_map = lambda b, i: (b, (i == ctx_tile).astype(jnp.int32), 0, 0)
    dva = d + ONES_ROWS
    return pl.pallas_call(
        _odd_prep_kernel,
        grid=(B, nt),
        in_specs=[pl.BlockSpec((1, TOK_TILE, D), lambda b, i: (b, i, 0)),
                  _full((1, D)),
                  pl.BlockSpec((1, 1, 1, D), mod_map),
                  pl.BlockSpec((1, 1, 1, D), mod_map)]
                 + [_full(w.shape) for w in weights]
                 + [pl.BlockSpec((d, TOK_TILE), lambda b, i: (0, i)),
                    pl.BlockSpec((d, TOK_TILE), lambda b, i: (0, i)),
                    pl.BlockSpec((TOK_TILE, LANES), lambda b, i: (i, 0)),
                    pl.BlockSpec((TOK_TILE, LANES), lambda b, i: (i, 0))],
        out_specs=[pl.BlockSpec((None, WIN_HEADS, d, TOK_TILE), lambda b, i: (b, 0, 0, i)),
                   pl.BlockSpec((None, GLB_HEADS, d, TOK_TILE), lambda b, i: (b, 0, 0, i)),
                   pl.BlockSpec((None, TOK_TILE, 2 * LANES), lambda b, i: (b, i, 0)),
                   pl.BlockSpec((None, WIN_KV_HEADS, dva, TOK_TILE), lambda b, i: (b, 0, 0, i)),
                   pl.BlockSpec((None, GLB_KV_HEADS, dva, TOK_TILE), lambda b, i: (b, 0, 0, i))],
        out_shape=[jax.ShapeDtypeStruct((B, WIN_HEADS, d, S), BF16),
                   jax.ShapeDtypeStruct((B, GLB_HEADS, d, S), BF16),
                   jax.ShapeDtypeStruct((B, S, 2 * LANES), BF16),
                   jax.ShapeDtypeStruct((B, WIN_KV_HEADS, dva, S), BF16),
                   jax.ShapeDtypeStruct((B, GLB_KV_HEADS, dva, S), BF16)],
        compiler_params=_cparams(2),
        name="odd_qkv_prep",
    )(xa, p["norm_mix"].reshape(1, D).astype(F32), mods["sh_a"], mods["sc_a"], *weights, cosT, sinT, cosK, sinK)


class _KeyPack:
    def __init__(self, kw, dk, block_of, slot_of):
        self.kw, self.dk, self.block_of, self.slot_of = kw, dk, block_of, slot_of


def _pad_queries(qT, pack, head):
    reps = pack.kw // qT.shape[0]
    if reps == 1:
        return qT
    rows = lax.broadcasted_iota(jnp.int32, (pack.kw, qT.shape[1]), 0)
    slot = pack.slot_of(head)
    keep = (rows >= slot * pack.dk) & (rows < (slot + 1) * pack.dk)
    return jnp.where(keep, jnp.concatenate([qT] * reps, axis=0), jnp.zeros((), qT.dtype))


def _aligned(start):
    return start if isinstance(start, int) else pl.multiple_of(start, PV_TILE)


def _attn_step(k_ref, vT_ref, acc_ref, q_next, next_offs, s_next, cur_offs, s_cur, m, cmax):
    tq = acc_ref.shape[1]
    fold = lambda x: x.reshape(x.shape[0] // 8, 8, tq)
    if cur_offs:
        for shift in (4, 2, 1):
            cmax = jnp.maximum(cmax, pltpu.roll(cmax, shift, 0))
        m_new = jnp.maximum(m, cmax)
    else:
        m_new = m
    pv = None
    cnext = None
    for i in range(max(len(next_offs), len(cur_offs))):
        rows = slice(i * PV_TILE, (i + 1) * PV_TILE)
        if i < len(next_offs):
            s = jnp.dot(k_ref[pl.ds(_aligned(next_offs[i]), PV_TILE), :], q_next[...], preferred_element_type=F32)
            s_next[rows, :] = s
            cm = jnp.max(fold(s), axis=0)
            cnext = cm if cnext is None else jnp.maximum(cnext, cm)
        if i < len(cur_offs):
            p = jnp.exp2(fold(s_cur[rows, :]) - m_new).reshape(PV_TILE, tq).astype(BF16)
            d = jnp.dot(vT_ref[:, pl.ds(_aligned(cur_offs[i]), PV_TILE)], p, preferred_element_type=F32)
            pv = d if pv is None else pv + d
    if cur_offs:
        dva = acc_ref.shape[0]
        acc_ref[...] = (fold(acc_ref[...]) * jnp.exp2(m - m_new) + fold(pv)).reshape(dva, tq)
    return m_new, cnext


def _softmax_init(init_ref, dva, dv, tq):
    m0 = jnp.broadcast_to(init_ref[0, 0:1, 0:1], (8, tq))
    row = lax.broadcasted_iota(jnp.int32, (dva, tq), 0)
    acc0 = jnp.where(row == dv, jnp.broadcast_to(init_ref[0, 1:2, 0:1], (dva, tq)), 0.0)
    return m0, acc0


def _flash_kernel(init_ref, qT_ref, qTn_ref, k_ref, vT_ref, o_ref, s0_ref, s1_ref, acc_ref, cm_ref, qp_ref, *,
                  T, n_ctx, kt, dv, pack):
    head = pl.program_id(1)
    qp_ref[0] = _pad_queries(qT_ref[...], pack, head)
    qp_ref[1] = _pad_queries(qTn_ref[...], pack, head)
    qT, qn = qp_ref.at[0], qp_ref.at[1]
    tq = qT_ref.shape[1]
    dva = vT_ref.shape[0]
    pieces = lambda off: [off + kk for kk in range(0, kt, PV_TILE)]
    first = [T + kk for kk in range(0, n_ctx, PV_TILE)] + pieces(0)
    step = functools.partial(_attn_step, k_ref, vT_ref, acc_ref)

    @pl.when(pl.program_id(2) == 0)
    def _():
        _, cm_ref[...] = step(qT, first, s0_ref, [], None, None, None)

    m, acc0 = _softmax_init(init_ref, dva, dv, tq)
    acc_ref[...] = acc0
    m, cm_b = step(qT, pieces(kt), s1_ref, first, s0_ref, m, cm_ref[...])

    def body(t, carry):
        m, cm_b = carry
        off = pl.multiple_of(t * (2 * kt), kt)
        m, cm_a = step(qT, pieces(off + 2 * kt), s0_ref, pieces(off + kt), s1_ref, m, cm_b)
        m, cm_b = step(qT, pieces(off + 3 * kt), s1_ref, pieces(off + 2 * kt), s0_ref, m, cm_a)
        return m, cm_b

    m, cm_b = lax.fori_loop(0, T // (2 * kt) - 1, body, (m, cm_b))
    m, cm_ref[...] = step(qn, first, s0_ref, pieces(T - kt), s1_ref, m, cm_b)
    acc = acc_ref[...]
    o_ref[...] = acc[:dv] / acc[dv:dv + 1]


def _flash(qT, k, vT, init, T, pack):
    B, Hq, dk, S = qT.shape
    Hv, dva = vT.shape[1], vT.shape[2]
    dv = dva - ONES_ROWS
    gv = Hq // Hv
    kt = min(KEY_TILE, T // 2)
    tq = min(Q_TILE, T)
    assert T % (2 * kt) == 0 and T % tq == 0 and kt % PV_TILE == 0
    n_ctx = S - T
    assert n_ctx % PV_TILE == 0
    kern = functools.partial(_flash_kernel, T=T, n_ctx=n_ctx, kt=kt, dv=dv, pack=pack)
    nq = T // tq
    return pl.pallas_call(
        kern,
        grid=(B, Hq, nq),
        in_specs=[pl.BlockSpec((1, 2, 128), lambda b, h, i: (h, 0, 0)),
                  pl.BlockSpec((None, None, dk, tq), lambda b, h, i: (b, h, 0, i)),
                  pl.BlockSpec((None, None, dk, tq), lambda b, h, i: (b, h, 0, jnp.minimum(i + 1, nq - 1))),
                  pl.BlockSpec((None, S, pack.kw), lambda b, h, i: (b, 0, pack.block_of(h))),
                  pl.BlockSpec((None, None, dva, S), lambda b, h, i: (b, h // gv, 0, 0))],
        out_specs=pl.BlockSpec((None, None, dv, tq), lambda b, h, i: (b, h, 0, i)),
        out_shape=jax.ShapeDtypeStruct((B, Hq, dv, T), F32),
        scratch_shapes=[pltpu.VMEM((kt + n_ctx, tq), F32), pltpu.VMEM((kt, tq), F32),
                        pltpu.VMEM((dva, tq), F32), pltpu.VMEM((8, tq), F32), pltpu.VMEM((2, pack.kw, tq), BF16)],
        compiler_params=_cparams(3),
        name="dense_attention",
    )(init, qT, qT, k, vT)


def _ctx_attn_kernel(init_ref, qT_ref, k_ref, vT_ref, o_ref, *, dv, pack):
    qT = _pad_queries(qT_ref[...], pack, pl.program_id(1))
    m0, acc0 = _softmax_init(init_ref, vT_ref.shape[0], dv, qT.shape[1])
    m0 = m0[0:1]
    s = jnp.dot(k_ref[...], qT, preferred_element_type=F32)
    m = jnp.maximum(m0, jnp.max(s, axis=0, keepdims=True))
    acc = acc0 * jnp.exp2(m0 - m) + jnp.dot(vT_ref[...], jnp.exp2(s - m).astype(BF16), preferred_element_type=F32)
    o_ref[...] = acc[:dv] / acc[dv:dv + 1]


def _ctx_attn(qT, k, vT, init, T, pack):
    B, Hq, dk, S = qT.shape
    Hv, dva = vT.shape[1], vT.shape[2]
    dv = dva - ONES_ROWS
    gv = Hq // Hv
    n = S - T
    blk = T // n
    return pl.pallas_call(
        functools.partial(_ctx_attn_kernel, dv=dv, pack=pack),
        grid=(B, Hq),
        in_specs=[pl.BlockSpec((1, 2, 128), lambda b, h: (h, 0, 0)),
                  pl.BlockSpec((None, None, dk, n), lambda b, h: (b, h, 0, blk)),
                  pl.BlockSpec((None, n, pack.kw), lambda b, h: (b, blk, pack.block_of(h))),
                  pl.BlockSpec((None, None, dva, n), lambda b, h: (b, h // gv, 0, blk))],
        out_specs=pl.BlockSpec((None, None, dv, n), lambda b, h: (b, h, 0, 0)),
        out_shape=jax.ShapeDtypeStruct((B, Hq, dv, n), F32),
        compiler_params=_cparams(2),
        name="context_attention",
    )(init, qT, k, vT)


def _window_kernel(sink_ref, bias_ref, qT_ref, k_ref, kc_ref, vT_ref, o_ref, *, dv, pack):
    i = pl.program_id(2)
    G, _, tq = qT_ref.shape
    head = pl.program_id(1) * G
    qT = jnp.concatenate([_pad_queries(qT_ref[g], pack, head) for g in range(G)], axis=1)
    snk = jnp.concatenate([jnp.broadcast_to(sink_ref[g, 0:1, 0:1], (1, tq)) for g in range(G)], axis=1)
    W = WINDOW
    kband = jnp.concatenate([k_ref[0][tq - W:tq, :], k_ref[1][...], k_ref[2][0:W, :]], axis=0)
    before_first = jnp.where(i == 0, NEG_INF, 0.0)
    after_last = jnp.where(i == pl.num_programs(2) - 1, NEG_INF, 0.0)
    bias = jnp.concatenate([bias_ref[0:W, :] + before_first, bias_ref[W:W + tq, :],
                            bias_ref[W + tq:, :] + after_last], axis=0)
    s_loc = jnp.dot(kband, qT, preferred_element_type=F32) + bias
    s_ctx = jnp.dot(kc_ref[...], qT, preferred_element_type=F32)
    m = jnp.maximum(jnp.maximum(jnp.max(s_loc, axis=0, keepdims=True), jnp.max(s_ctx, axis=0, keepdims=True)), snk)
    p_loc = jnp.exp2(s_loc - m).astype(BF16)
    p_ctx = jnp.exp2(s_ctx - m).astype(BF16)
    vband = jnp.concatenate([vT_ref[0][:, tq - W:tq], vT_ref[1][...], vT_ref[2][:, 0:W]], axis=1)
    acc = (jnp.dot(vband, p_loc, preferred_element_type=F32)
           + jnp.dot(vT_ref[3][...], p_ctx, preferred_element_type=F32))
    o = acc[:dv] / (acc[dv:dv + 1] + jnp.exp2(snk - m))
    for g in range(G):
        o_ref[g] = o[:, g * tq:(g + 1) * tq]


def _window(qT, k, vT, sink, T, pack):
    B, Hq, dk, S = qT.shape
    Hk, dva = vT.shape[1], vT.shape[2]
    dv = dva - ONES_ROWS
    g = Hq // Hk
    tq = TOK_TILE
    nt = T // tq
    ctx_blk = T // tq
    clip = lambda j: jnp.clip(j, 0, nt - 1)
    kern = functools.partial(_window_kernel, dv=dv, pack=pack)
    kspec = lambda f: pl.BlockSpec((None, tq, pack.kw), lambda b, h, i: (b, f(i), pack.block_of(h * g)))
    vspec = lambda f: pl.BlockSpec((None, None, dva, tq), lambda b, h, i: (b, h, 0, f(i)))
    r = jnp.arange(tq + 2 * WINDOW)[:, None]
    c = jnp.arange(g * tq)[None, :] % tq
    bias = jnp.where(jnp.abs(r - WINDOW - c) <= WINDOW, 0.0, NEG_INF).astype(F32)

    def body(sink_ref, bias_ref, qT_ref, k0, k1, k2, kc, v0, v1, v2, vc, o_ref):
        kern(sink_ref, bias_ref, qT_ref, (k0, k1, k2), kc, (v0, v1, v2, vc), o_ref)

    return pl.pallas_call(
        body,
        grid=(B, Hk, nt),
        in_specs=[pl.BlockSpec((g, 1, 128), lambda b, h, i: (h, 0, 0)),
                  _full(bias.shape),
                  pl.BlockSpec((None, g, dk, tq), lambda b, h, i: (b, h, 0, i)),
                  kspec(lambda i: clip(i - 1)), kspec(lambda i: i), kspec(lambda i: clip(i + 1)),
                  kspec(lambda i: ctx_blk),
                  vspec(lambda i: clip(i - 1)), vspec(lambda i: i), vspec(lambda i: clip(i + 1)),
                  vspec(lambda i: ctx_blk)],
        out_specs=pl.BlockSpec((None, g, dv, tq), lambda b, h, i: (b, h, 0, i)),
        out_shape=jax.ShapeDtypeStruct((B, Hq, dv, T), F32),
        compiler_params=_cparams(3),
        name="window_attention",
    )(sink, bias, qT, k, k, k, k, vT, vT, vT, vT)


def _project_out(x_ref, oT, w_ref, gate_ref, y_ref):
    y = jnp.dot(oT.T.astype(BF16), w_ref[...], preferred_element_type=F32)
    y_ref[0] = x_ref[0] + gate_ref[0, 0] * y


def _even_out_kernel(x_ref, oa_ref, od_ref, w_ref, gate_ref, lam_ref, sub_ref, y_ref, *, post_scale):
    lam = lam_ref[...]
    parts = [oa_ref[hd] for hd in range(MLA_HEADS)]
    for hd in range(DIFF_HEADS):
        diff = od_ref[2 * hd] - lam * od_ref[2 * hd + 1]
        parts.append(diff * _rms_rows(diff) * sub_ref[...] * post_scale)
    _project_out(x_ref, jnp.concatenate(parts, axis=0), w_ref, gate_ref, y_ref)


def _odd_out_kernel(x_ref, oc_ref, od_ref, w_ref, gate_ref, y_ref):
    parts = [oc_ref[hd] for hd in range(WIN_HEADS)] + [od_ref[hd] for hd in range(GLB_HEADS)]
    _project_out(x_ref, jnp.concatenate(parts, axis=0), w_ref, gate_ref, y_ref)


def _out_proj(kern, x, heads, w, gate, extra, tile0, n_tiles, kind):
    B, S, D = x.shape
    row_map = lambda b, i: (b, i + tile0, 0)
    return pl.pallas_call(
        kern,
        grid=(B, n_tiles),
        in_specs=[pl.BlockSpec((1, TOK_TILE, D), row_map)]
                 + [pl.BlockSpec((None,) + o.shape[1:3] + (TOK_TILE,), lambda b, i: (b, 0, 0, i)) for o in heads]
                 + [_full(w.shape), pl.BlockSpec((1, 1, 1, D), lambda b, i: (b, kind, 0, 0))]
                 + [_full(e.shape) for e in extra],
        out_specs=pl.BlockSpec((1, TOK_TILE, D), row_map),
        out_shape=jax.ShapeDtypeStruct((B, S, D), F32),
        input_output_aliases={0: 0},
        compiler_params=_cparams(2),
        name="merge_out_proj_residual",
    )(x, *heads, w, gate, *extra)


def _first_argmax(vals):
    best, idx = vals[0], jnp.zeros(vals[0].shape, jnp.int32)
    for j in range(1, len(vals)):
        better = vals[j] > best
        idx = jnp.where(better, j, idx)
        best = jnp.where(better, vals[j], best)
    return idx, best


def _pick(idx, vals):
    out = vals[0]
    for j in range(1, len(vals)):
        out = jnp.where(idx == j, vals[j], out)
    return out


def _route(logits, bias):
    s = 1.0 / (1.0 + jnp.exp(-logits))
    sel = s + bias
    srow = [s[e:e + 1] for e in range(N_EXPERTS)]
    row = [sel[e:e + 1] for e in range(N_EXPERTS)]
    scores = []
    for g in range(N_GROUPS):
        a, b, c, d = row[4 * g:4 * g + 4]
        hi1, lo1, hi2, lo2 = jnp.maximum(a, b), jnp.minimum(a, b), jnp.maximum(c, d), jnp.minimum(c, d)
        top1 = jnp.maximum(hi1, hi2)
        top2 = jnp.maximum(jnp.maximum(lo1, lo2), jnp.minimum(hi1, hi2))
        scores.append(top1 + top2)
    gi, _ = _first_argmax(scores)
    v = [_pick(gi, [row[4 * g + j] for g in range(N_GROUPS)]) for j in range(EXPERTS_PER_GROUP)]
    sv = [_pick(gi, [srow[4 * g + j] for g in range(N_GROUPS)]) for j in range(EXPERTS_PER_GROUP)]
    i1, _ = _first_argmax(v)
    i2, _ = _first_argmax([jnp.where(i1 == j, -jnp.inf, v[j]) for j in range(EXPERTS_PER_GROUP)])
    w1, w2 = _pick(i1, sv), _pick(i2, sv)
    tot = w1 + w2
    w1, w2 = w1 / tot, w2 / tot
    rows = []
    for e in range(N_EXPERTS):
        g, j = divmod(e, EXPERTS_PER_GROUP)
        in_g = gi == g
        rows.append(jnp.where(in_g & (i1 == j), w1, 0.0) + jnp.where(in_g & (i2 == j), w2, 0.0))
    return jnp.concatenate(rows, axis=0), gi


def _split(x, terms):
    out = []
    for _ in range(terms):
        part = x.astype(BF16).astype(F32)
        out.append(part)
        x = x - part
    return out


def _to_column(row):
    n = row.shape[1]
    return jnp.concatenate([row, jnp.zeros((LANES - 1, n), F32)], axis=0).T[:, 0:1]


def _moe_kernel(x_ref, g_ref, sh_ref, sc_ref, gate_ref, wrT_ref, br_ref, before_ref, wg_ref, wu_ref, wd_ref, y_ref,
                hs_scr, cs_scr, ys_scr, q_scr, info_ref):
    step = pl.program_id(2)
    per_step = wg_ref.shape[0]
    tm = x_ref.shape[1]
    R = hs_scr.shape[0]

    @pl.when(step == 0)
    def _():
        h = _norm_mod(x_ref[0], g_ref[...], sh_ref[0, 0], sc_ref[0, 0]).astype(BF16)
        comb, gi = _route(_dot_t(wrT_ref[...], h), br_ref[...])
        sel = jnp.concatenate([(gi == g).astype(F32) for g in range(N_GROUPS)]
                              + [jnp.zeros((16 - N_GROUPS, tm), F32)], axis=0)
        rank = jnp.dot(sel.astype(BF16), before_ref[...], preferred_element_type=F32)
        cnt = jnp.sum(sel, axis=1, keepdims=True)
        seg = jnp.ceil(cnt * (1.0 / 16.0)) * 16.0
        pos = jnp.zeros((1, tm), F32)
        start = jnp.zeros((1, 1), F32)
        for g in range(N_GROUPS):
            pos = pos + sel[g:g + 1] * (start + rank[g:g + 1])
            info_ref[g] = start[0, 0].astype(jnp.int32)
            info_ref[N_GROUPS + g] = jnp.ceil(cnt[g:g + 1] * (1.0 / MOE_CHUNK))[0, 0].astype(jnp.int32)
            start = start + seg[g:g + 1]
        slot = lax.broadcasted_iota(jnp.int32, (R, tm), 0).astype(F32)
        P = jnp.where(slot == pos, 1.0, 0.0).astype(BF16)
        hs_scr[...] = jnp.dot(P, h, preferred_element_type=F32).astype(BF16)
        parts = _split(comb, COMB_TERMS)
        combT = jnp.concatenate(parts + [jnp.zeros((LANES - COMB_TERMS * N_EXPERTS, tm), F32)], axis=0).T
        cs_scr[...] = jnp.dot(P, combT.astype(BF16), preferred_element_type=F32)
        lane = lax.broadcasted_iota(jnp.int32, (tm, R), 1).astype(F32)
        q_scr[...] = jnp.where(lane == _to_column(pos), 1.0, 0.0).astype(BF16)
        ys_scr[...] = jnp.zeros_like(ys_scr)

    steps_per_group = EXPERTS_PER_GROUP // per_step
    grp = lax.shift_right_logical(step, steps_per_group.bit_length() - 1)
    start = info_ref[grp]
    n_chunks = info_ref[N_GROUPS + grp]

    def experts(r0, rows):
        h = hs_scr[pl.ds(r0, rows), :]
        cs = cs_scr[pl.ds(r0, rows), :]
        lane = lax.broadcasted_iota(jnp.int32, (rows, LANES), 1) & (N_EXPERTS - 1)
        total = None
        for j in range(per_step):
            a = jnp.dot(h, wg_ref[j], preferred_element_type=F32)
            u = jnp.dot(h, wu_ref[j], preferred_element_type=F32)
            act = (a * (1.0 / (1.0 + jnp.exp(-a)))) * u
            y = jnp.dot(act.astype(BF16), wd_ref[j], preferred_element_type=F32)
            w = jnp.sum(jnp.where(lane == step * per_step + j, cs, 0.0), axis=1, keepdims=True)
            total = w * y if total is None else total + w * y
        ys_scr[pl.ds(r0, rows), :] += total

    def pair(c, carry):
        experts(pl.multiple_of(start + c * (2 * MOE_CHUNK), 16), 2 * MOE_CHUNK)
        return carry

    lax.fori_loop(0, lax.shift_right_logical(n_chunks, 1), pair, 0)

    @pl.when((n_chunks & 1) == 1)
    def _():
        experts(pl.multiple_of(start + (n_chunks - 1) * MOE_CHUNK, 16), MOE_CHUNK)

    @pl.when(step == pl.num_programs(2) - 1)
    def _():
        q = q_scr[...]
        out = sum(jnp.dot(q, part.astype(BF16), preferred_element_type=F32) for part in _split(ys_scr[...], 2))
        y_ref[0] = x_ref[0] + gate_ref[0, 0] * out


def _moe(x, g, sh, sc, gate, wrT, br, wg, wu, wd, tm, tile0, n_tiles, kind):
    B, S, D = x.shape
    E, _, F = wg.shape
    R = -(-(tm + 16 * N_GROUPS + MOE_CHUNK) // 256) * 256
    idx = jnp.arange(tm)
    before = (idx[:, None] < idx[None, :]).astype(BF16)
    mod_map = lambda b, i, e: (b, kind, 0, 0)
    row_map = lambda b, i, e: (b, i + tile0, 0)
    return pl.pallas_call(
        _moe_kernel,
        grid=(B, n_tiles, E // MOE_STEP_EXPERTS),
        in_specs=[pl.BlockSpec((1, tm, D), row_map),
                  pl.BlockSpec((1, D), lambda b, i, e: (0, 0)),
                  pl.BlockSpec((1, 1, 1, D), mod_map),
                  pl.BlockSpec((1, 1, 1, D), mod_map),
                  pl.BlockSpec((1, 1, 1, D), mod_map),
                  pl.BlockSpec((E, D), lambda b, i, e: (0, 0)),
                  pl.BlockSpec((E, 1), lambda b, i, e: (0, 0)),
                  pl.BlockSpec((tm, tm), lambda b, i, e: (0, 0)),
                  pl.BlockSpec((MOE_STEP_EXPERTS, D, F), lambda b, i, e: (e, 0, 0)),
                  pl.BlockSpec((MOE_STEP_EXPERTS, D, F), lambda b, i, e: (e, 0, 0)),
                  pl.BlockSpec((MOE_STEP_EXPERTS, F, D), lambda b, i, e: (e, 0, 0))],
        out_specs=pl.BlockSpec((1, tm, D), row_map),
        out_shape=jax.ShapeDtypeStruct((B, S, D), F32),
        scratch_shapes=[pltpu.VMEM((R, D), BF16), pltpu.VMEM((R, LANES), F32), pltpu.VMEM((R, D), F32),
                        pltpu.VMEM((tm, R), BF16), pltpu.SMEM((2 * N_GROUPS,), jnp.int32)],
        input_output_aliases={0: 0},
        compiler_params=_cparams(3),
        name="moe_experts",
    )(x, g.reshape(1, D).astype(F32), sh, sc, gate, wrT, br, before, wg, wu, wd)


def _final_norm_kernel(x_ref, g_ref, o_ref):
    x = x_ref[0]
    o_ref[0] = (x * lax.rsqrt(jnp.mean(x * x, axis=-1, keepdims=True) + NORM_EPS)) * g_ref[...]


def _final_norm(x, g, T):
    B, S, D = x.shape
    return pl.pallas_call(
        _final_norm_kernel,
        grid=(B, T // TOK_TILE),
        in_specs=[pl.BlockSpec((1, TOK_TILE, D), lambda b, i: (b, i, 0)),
                  pl.BlockSpec((1, D), lambda b, i: (0, 0))],
        out_specs=pl.BlockSpec((1, TOK_TILE, D), lambda b, i: (b, i, 0)),
        out_shape=jax.ShapeDtypeStruct((B, T, D), F32),
        compiler_params=_cparams(2),
        name="final_norm",
    )(x, g.reshape(1, D).astype(F32))


def _axial_tables(T, n_ctx, rot_dim):
    half = rot_dim // 2
    inv_freq = ROPE_THETA ** (-jnp.arange(0, half, 2, dtype=F32) / half)
    rows = T // GRID_W
    row = jnp.broadcast_to(jnp.arange(rows, dtype=F32)[:, None], (rows, GRID_W)).reshape(-1)
    col = jnp.broadcast_to(jnp.arange(GRID_W, dtype=F32)[None, :], (rows, GRID_W)).reshape(-1)

    def ang(pos):
        a = pos[:, None] * inv_freq[None, :]
        return jnp.concatenate([a, a], axis=-1)

    a = jnp.concatenate([ang(row), ang(col)], axis=-1)
    cos = jnp.concatenate([jnp.cos(a), jnp.ones((n_ctx, rot_dim), F32)], axis=0)
    sin = jnp.concatenate([jnp.sin(a), jnp.zeros((n_ctx, rot_dim), F32)], axis=0)
    reps = LANES // rot_dim
    return cos.T, sin.T, jnp.tile(cos, (1, reps)), jnp.tile(sin, (1, reps))


def _plain_init(n_heads):
    return jnp.broadcast_to(jnp.array([NEG_INF, 0.0], F32)[None, :, None], (n_heads, 2, 128))


def _sink_init(sink):
    s = sink.astype(F32) * LOG2E
    return jnp.broadcast_to(jnp.stack([s, jnp.ones_like(s)], axis=1)[:, :, None], (s.shape[0], 2, 128))


def _even_mixer(xa, p, mods, T, with_ctx, lam_init, tabs):
    B, S, D = xa.shape
    qa, ka, va, qd, kd, vd = _even_prep(xa, p, mods, tabs)
    pack_a = _KeyPack(MLA_PAD, MLA_PAD, lambda h: h, lambda h: 0)
    per = LANES // DIFF_D
    pack_d = _KeyPack(LANES, DIFF_D, lambda h: h // per, lambda h: h % per)
    init_a, init_d = _plain_init(MLA_HEADS), _plain_init(2 * DIFF_HEADS)
    lam = p["lam"].astype(F32)
    lam_full = (jnp.exp(jnp.sum(lam[0] * lam[1])) - jnp.exp(jnp.sum(lam[2] * lam[3])) + lam_init).reshape(1, 1)
    sub = jnp.broadcast_to(p["subln"].astype(F32)[:, None], (DIFF_V, TOK_TILE))
    kern = functools.partial(_even_out_kernel, post_scale=1.0 - lam_init)
    oa = _flash(qa, ka, va, init_a, T, pack_a)
    od = _flash(qd, kd, vd, init_d, T, pack_d)
    xa = _out_proj(kern, xa, [oa, od], p["w_out"], mods["g_a"], [lam_full, sub], 0, T // TOK_TILE, 0)
    if with_ctx:
        oa = _ctx_attn(qa, ka, va, init_a, T, pack_a)
        od = _ctx_attn(qd, kd, vd, init_d, T, pack_d)
        xa = _out_proj(kern, xa, [oa, od], p["w_out"], mods["g_a"], [lam_full, sub], T // TOK_TILE, 1, 1)
    return xa


def _odd_mixer(xa, p, mods, T, with_ctx, tabs):
    B, S, D = xa.shape
    qc, qd, k, vc, vd = _odd_prep(xa, p, mods, tabs)
    gw, gd = WIN_HEADS // WIN_KV_HEADS, GLB_HEADS // GLB_KV_HEADS
    pack_c = _KeyPack(2 * LANES, HEAD_DIM, lambda h: 0, lambda h: h // gw)
    pack_d = _KeyPack(2 * LANES, HEAD_DIM, lambda h: 0, lambda h: WIN_KV_HEADS + h // gd)
    init_d = _plain_init(GLB_HEADS)
    sink = p["sink"].astype(F32) * LOG2E
    sink_w = jnp.broadcast_to(sink[:, None, None], (WIN_HEADS, 1, 128))
    od = _flash(qd, k, vd, init_d, T, pack_d)
    oc = _window(qc, k, vc, sink_w, T, pack_c)
    xa = _out_proj(_odd_out_kernel, xa, [oc, od], p["w_out"], mods["g_a"], [], 0, T // TOK_TILE, 0)
    if with_ctx:
        oc = _ctx_attn(qc, k, vc, _sink_init(p["sink"]), T, pack_c)
        od = _ctx_attn(qd, k, vd, init_d, T, pack_d)
        xa = _out_proj(_odd_out_kernel, xa, [oc, od], p["w_out"], mods["g_a"], [], T // TOK_TILE, 1, 1)
    return xa


def kernel(x, c, ctx, c_ctx, w_mod, b_mod, norm_mix, norm_ffn, even_w_in, even_norm_q, even_norm_kv, even_w_uq, even_w_ukv, even_lambda, even_subln, even_w_out, odd_w_in, odd_sink, odd_q_norm, odd_k_norm, odd_w_out, w_router, b_router, w_gate, w_up, w_down, norm_final):
    B, T, D = x.shape
    n_ctx = ctx.shape[1]
    depth = w_mod.shape[0]
    assert n_ctx == TOK_TILE == PV_TILE and T % MOE_TILE == 0 and B <= 7
    assert MLA_ROPE == DIFF_D
    S = T + n_ctx
    tabs32 = _axial_tables(T, n_ctx, MLA_ROPE)
    tabs64 = _axial_tables(T, n_ctx, HEAD_DIM)

    cond = jnp.zeros((8, D), F32).at[:B].set(c.astype(F32)).at[B].set(c_ctx.astype(F32))
    mod_all = _mod_vectors(cond, w_mod, b_mod)
    wrT = w_router.astype(BF16).T
    br = b_router.astype(F32).reshape(N_EXPERTS, 1)

    xa = jnp.concatenate([x, ctx], axis=1).astype(F32)
    for l in range(depth):
        with_ctx = l < depth - 1
        i = l // 2
        lat = mod_all[l, :B].reshape(B, 6, D)
        cx = jnp.broadcast_to(mod_all[l, B].reshape(1, 6, D), (B, 6, D))
        both = jnp.stack([lat, cx], axis=1)
        names = ("sh_a", "sc_a", "g_a", "sh_f", "sc_f", "g_f")
        mods = {n: both[:, :, j:j + 1, :] for j, n in enumerate(names)}
        if l % 2 == 0:
            lam_init = 0.8 - 0.6 * math.exp(-0.3 * l)
            p = dict(norm_mix=norm_mix[l], w_in=even_w_in[i], norm_q=even_norm_q[i], norm_kv=even_norm_kv[i],
                     w_uq=even_w_uq[i], w_ukv=even_w_ukv[i], lam=even_lambda[i], subln=even_subln[i],
                     w_out=even_w_out[i].astype(BF16))
            xa = _even_mixer(xa, p, mods, T, with_ctx, lam_init, tabs32)
        else:
            p = dict(norm_mix=norm_mix[l], w_in=odd_w_in[i], sink=odd_sink[i], q_norm=odd_q_norm[i],
                     k_norm=odd_k_norm[i], w_out=odd_w_out[i].astype(BF16))
            xa = _odd_mixer(xa, p, mods, T, with_ctx, tabs64)
        wg, wu, wd = w_gate[l].astype(BF16), w_up[l].astype(BF16), w_down[l].astype(BF16)
        ffn = (norm_ffn[l], mods["sh_f"], mods["sc_f"], mods["g_f"], wrT, br, wg, wu, wd)
        xa = _moe(xa, *ffn, MOE_TILE, 0, T // MOE_TILE, 0)
        if with_ctx:
            xa = _moe(xa, *ffn, TOK_TILE, T // TOK_TILE, 1, 1)
    return _final_norm(xa, norm_final, T)
```

```python
import functools
import math

import jax
import jax.numpy as jnp
from jax import lax
from jax.experimental import pallas as pl
from jax.experimental.pallas import tpu as pltpu

F32 = jnp.float32
BF16 = jnp.bfloat16
LOG2E = 1.4426950408889634

GRID_W = 64
ROPE_THETA = 10000.0
NORM_EPS = 1e-6
NEG_INF = -1e30

MLA_HEADS = 8
MLA_Q_LORA = 384
MLA_KV_LORA = 256
MLA_NOPE = 64
MLA_ROPE = 32
MLA_V = 64
MLA_SCALE = (MLA_NOPE + MLA_ROPE) ** -0.5
MLA_IN_COLS = MLA_Q_LORA + MLA_KV_LORA + MLA_ROPE
MLA_PAD = 128

DIFF_HEADS = 8
DIFF_D = 32
DIFF_V = 2 * DIFF_D
DIFF_SCALE = DIFF_D ** -0.5

HEAD_DIM = 64
WIN_HEADS = 8
WIN_KV_HEADS = 2
WINDOW = 128
GLB_HEADS = 8
GLB_KV_HEADS = 2
ATTN_SCALE = HEAD_DIM ** -0.5

N_EXPERTS = 16
N_GROUPS = 4
EXPERTS_PER_GROUP = N_EXPERTS // N_GROUPS

TOK_TILE = 256
Q_TILE = 1024
KEY_TILE = 2048
PV_TILE = 256
MOE_TILE = 1024
MOE_CHUNK = 128
MOE_STEP_EXPERTS = 2
COMB_TERMS = 3
ONES_ROWS = 16
LANES = 128
BF16_ROWS = 16
MXU_DIM = 256
VMEM_LIMIT = 56 * 1024 * 1024


def _cparams(n_axes):
    return pltpu.CompilerParams(dimension_semantics=("arbitrary",) * n_axes,
                                vmem_limit_bytes=VMEM_LIMIT)


def _full(shape):
    return pl.BlockSpec(shape, lambda *_: (0,) * len(shape))


def _mod_kernel(a_ref, w_ref, b_ref, o_ref):
    a = a_ref[...]
    a = a * (1.0 / (1.0 + jnp.exp(-a)))
    o_ref[0] = jnp.dot(a.astype(BF16), w_ref[0].astype(BF16), preferred_element_type=F32) + b_ref[0]


def _mod_vectors(cond, w_mod, b_mod):
    L, D, N = w_mod.shape
    tn = 1536
    return pl.pallas_call(
        _mod_kernel,
        grid=(L, N // tn),
        in_specs=[pl.BlockSpec((8, D), lambda l, j: (0, 0)),
                  pl.BlockSpec((1, D, tn), lambda l, j: (l, 0, j)),
                  pl.BlockSpec((1, 1, tn), lambda l, j: (l, 0, j))],
        out_specs=pl.BlockSpec((1, 8, tn), lambda l, j: (l, 0, j)),
        out_shape=jax.ShapeDtypeStruct((L, 8, N), F32),
        compiler_params=_cparams(2),
        name="mod_vectors",
    )(cond, w_mod, b_mod.reshape(L, 1, N))


def _norm_mod(x, g, sh, sc):
    y = x * lax.rsqrt(jnp.mean(x * x, axis=-1, keepdims=True) + NORM_EPS)
    return (y * g) * (1.0 + sc) + sh


def _dot_t(w, h):
    return lax.dot_general(w, h, (((1,), (1,)), ((), ())), preferred_element_type=F32)


def _rms_rows(x):
    return lax.rsqrt(jnp.mean(x * x, axis=0, keepdims=True) + NORM_EPS)


def _rope_rows(x, cosT, sinT):
    q = x.shape[0] // 4
    rot = jnp.concatenate([-x[q:2 * q], x[0:q], -x[3 * q:4 * q], x[2 * q:3 * q]], axis=0)
    return x * cosT + rot * sinT


def _ones_rows(n):
    row = lax.broadcasted_iota(jnp.int32, (ONES_ROWS, n), 0)
    return jnp.where(row == 0, 1.0, 0.0).astype(BF16)


def _tile_lanes(x, reps):
    return x if reps == 1 else jnp.concatenate([x] * reps, axis=1)


def _even_prep_kernel(x_ref, g_ref, sh_ref, sc_ref, w1T_ref, wckv_ref, wkr_ref, wkrr_ref, wdk_ref, wdkr_ref,
                      wuqT_ref, wuvT_ref, wuk_ref, place_ref, gq_ref, gkvc_ref, gkvr_ref, cosT_ref, sinT_ref,
                      cosK_ref, sinK_ref,
                      qa_ref, ka_ref, va_ref, qd_ref, kd_ref, vd_ref):
    h = _norm_mod(x_ref[0], g_ref[...], sh_ref[0, 0], sc_ref[0, 0]).astype(BF16)
    n = h.shape[0]
    cosT, sinT, cosK, sinK = cosT_ref[...], sinT_ref[...], cosK_ref[...], sinK_ref[...]
    ones = _ones_rows(n)
    zT = _dot_t(w1T_ref[...], h)
    o1, o2 = MLA_Q_LORA, MLA_Q_LORA + MLA_KV_LORA
    o3 = o2 + 2 * DIFF_HEADS * DIFF_D

    c_q = zT[:o1]
    c_q = (c_q * _rms_rows(c_q) * gq_ref[...]).astype(BF16)
    qT = jnp.dot(wuqT_ref[...], c_q, preferred_element_type=F32)
    for hd in range(MLA_HEADS):
        r0 = hd * MLA_PAD
        rope = _rope_rows(qT[r0 + MLA_NOPE:r0 + MLA_NOPE + MLA_ROPE], cosT, sinT)
        head = jnp.concatenate([qT[r0:r0 + MLA_NOPE], rope, qT[r0 + MLA_NOPE + MLA_ROPE:r0 + MLA_PAD]], axis=0)
        qa_ref[hd] = (head * (MLA_SCALE * LOG2E)).astype(BF16)

    c_kvT = zT[o1:o2]
    c_kvT = (c_kvT * _rms_rows(c_kvT) * gkvc_ref[...]).astype(BF16)
    vT = jnp.dot(wuvT_ref[...], c_kvT, preferred_element_type=F32).astype(BF16)
    for hd in range(MLA_HEADS):
        va_ref[hd, 0:MLA_V, :] = vT[hd * MLA_V:(hd + 1) * MLA_V]
        va_ref[hd, MLA_V:, :] = ones

    for j in range(2 * DIFF_HEADS):
        qj = _rope_rows(zT[o2 + j * DIFF_D:o2 + (j + 1) * DIFF_D], cosT, sinT)
        qd_ref[j] = (qj * (DIFF_SCALE * LOG2E)).astype(BF16)
    for hd in range(DIFF_HEADS):
        vd_ref[hd, 0:DIFF_V, :] = zT[o3 + hd * DIFF_V:o3 + (hd + 1) * DIFF_V].astype(BF16)
        vd_ref[hd, DIFF_V:, :] = ones

    c_kv = jnp.dot(h, wckv_ref[...], preferred_element_type=F32)
    c_kv = (c_kv * lax.rsqrt(jnp.mean(c_kv * c_kv, axis=-1, keepdims=True) + NORM_EPS) * gkvr_ref[...]).astype(BF16)
    kr = (jnp.dot(h, wkr_ref[...], preferred_element_type=F32) * cosK
          + jnp.dot(h, wkrr_ref[...], preferred_element_type=F32) * sinK)
    ka = (jnp.dot(c_kv, wuk_ref[...], preferred_element_type=F32)
          + jnp.dot(kr.astype(BF16), place_ref[...], preferred_element_type=F32))
    ka_ref[...] = ka.astype(BF16)

    reps = wdk_ref.shape[1] // LANES
    kd = (jnp.dot(h, wdk_ref[...], preferred_element_type=F32) * _tile_lanes(cosK, reps)
          + jnp.dot(h, wdkr_ref[...], preferred_element_type=F32) * _tile_lanes(sinK, reps))
    kd_ref[...] = kd.astype(BF16)


def _rot_cols(w, width):
    d, n = w.shape
    w4 = w.reshape(d, n // width, 4, width // 4)
    return jnp.stack([-w4[:, :, 1], w4[:, :, 0], -w4[:, :, 3], w4[:, :, 2]], axis=2).reshape(d, n)


def _perm_cols(g, width):
    g4 = g.reshape(-1, 4, width // 4)
    return jnp.stack([g4[:, 1], g4[:, 0], g4[:, 3], g4[:, 2]], axis=1).reshape(-1)


def _even_prep(xa, p, mods, tabs):
    B, S, D = xa.shape
    w_in, w_uq, w_ukv = p["w_in"], p["w_uq"], p["w_ukv"]
    o1, o2, o3 = MLA_Q_LORA, MLA_Q_LORA + MLA_KV_LORA, MLA_IN_COLS
    nq = 2 * DIFF_HEADS * DIFF_D
    bf = lambda a: a.astype(BF16)
    w1T = bf(jnp.concatenate([w_in[:, :o2], w_in[:, o3:o3 + nq], w_in[:, o3 + 2 * nq:]], axis=1).T)
    wckv = bf(w_in[:, o1:o2])
    wkr = jnp.pad(w_in[:, o2:o3], ((0, 0), (0, LANES - MLA_ROPE)))
    wkrr = jnp.pad(_rot_cols(w_in[:, o2:o3], MLA_ROPE), ((0, 0), (0, LANES - MLA_ROPE)))
    wdk = w_in[:, o3 + nq:o3 + 2 * nq]
    wdkr = _rot_cols(wdk, DIFF_D)
    uq = w_uq.reshape(MLA_Q_LORA, MLA_HEADS, MLA_NOPE + MLA_ROPE)
    wuqT = bf(jnp.pad(uq, ((0, 0), (0, 0), (0, MLA_PAD - MLA_NOPE - MLA_ROPE))).reshape(MLA_Q_LORA, -1).T)
    ukv = w_ukv.reshape(MLA_KV_LORA, MLA_HEADS, MLA_NOPE + MLA_V)
    wuvT = bf(ukv[:, :, MLA_NOPE:].reshape(MLA_KV_LORA, -1).T)
    wuk = bf(jnp.pad(ukv[:, :, :MLA_NOPE], ((0, 0), (0, 0), (0, MLA_PAD - MLA_NOPE))).reshape(MLA_KV_LORA, -1))
    src = jnp.arange(LANES)[:, None]
    dst = jnp.arange(MLA_HEADS * MLA_PAD)[None, :]
    place = bf((src < MLA_ROPE) & (dst % MLA_PAD == src + MLA_NOPE))
    gq = jnp.broadcast_to(p["norm_q"].astype(F32)[:, None], (MLA_Q_LORA, TOK_TILE))
    gkvc = jnp.broadcast_to(p["norm_kv"].astype(F32)[:, None], (MLA_KV_LORA, TOK_TILE))
    gkvr = p["norm_kv"].astype(F32).reshape(1, MLA_KV_LORA)
    cosT, sinT, cosK, sinK = tabs
    weights = [w1T, wckv, bf(wkr), bf(wkrr), bf(wdk), bf(wdkr), wuqT, wuvT, wuk, place, gq, gkvc, gkvr]
    nt = S // TOK_TILE
    ctx_tile = nt - 1
    mod_map = lambda b, i: (b, (i == ctx_tile).astype(jnp.int32), 0, 0)
    HA, HD = MLA_HEADS, DIFF_HEADS
    dva = MLA_V + ONES_ROWS
    return pl.pallas_call(
        _even_prep_kernel,
        grid=(B, nt),
        in_specs=[pl.BlockSpec((1, TOK_TILE, D), lambda b, i: (b, i, 0)),
                  _full((1, D)),
                  pl.BlockSpec((1, 1, 1, D), mod_map),
                  pl.BlockSpec((1, 1, 1, D), mod_map)]
                 + [_full(w.shape) for w in weights]
                 + [pl.BlockSpec((MLA_ROPE, TOK_TILE), lambda b, i: (0, i)),
                    pl.BlockSpec((MLA_ROPE, TOK_TILE), lambda b, i: (0, i)),
                    pl.BlockSpec((TOK_TILE, LANES), lambda b, i: (i, 0)),
                    pl.BlockSpec((TOK_TILE, LANES), lambda b, i: (i, 0))],
        out_specs=[pl.BlockSpec((None, HA, MLA_PAD, TOK_TILE), lambda b, i: (b, 0, 0, i)),
                   pl.BlockSpec((None, TOK_TILE, HA * MLA_PAD), lambda b, i: (b, i, 0)),
                   pl.BlockSpec((None, HA, dva, TOK_TILE), lambda b, i: (b, 0, 0, i)),
                   pl.BlockSpec((None, 2 * HD, DIFF_D, TOK_TILE), lambda b, i: (b, 0, 0, i)),
                   pl.BlockSpec((None, TOK_TILE, nq), lambda b, i: (b, i, 0)),
                   pl.BlockSpec((None, HD, dva, TOK_TILE), lambda b, i: (b, 0, 0, i))],
        out_shape=[jax.ShapeDtypeStruct((B, HA, MLA_PAD, S), BF16),
                   jax.ShapeDtypeStruct((B, S, HA * MLA_PAD), BF16),
                   jax.ShapeDtypeStruct((B, HA, dva, S), BF16),
                   jax.ShapeDtypeStruct((B, 2 * HD, DIFF_D, S), BF16),
                   jax.ShapeDtypeStruct((B, S, nq), BF16),
                   jax.ShapeDtypeStruct((B, HD, dva, S), BF16)],
        compiler_params=_cparams(2),
        name="even_qkv_prep",
    )(xa, p["norm_mix"].reshape(1, D).astype(F32), mods["sh_a"], mods["sc_a"], *weights, cosT, sinT, cosK, sinK)


def _odd_prep_kernel(x_ref, g_ref, sh_ref, sc_ref, wqvT_ref, wk_ref, wkr_ref, gqn_ref, gk_ref, gkp_ref, bd_ref,
                     cosT_ref, sinT_ref, cosK_ref, sinK_ref,
                     qc_ref, qd_ref, k_ref, vc_ref, vd_ref):
    h = _norm_mod(x_ref[0], g_ref[...], sh_ref[0, 0], sc_ref[0, 0]).astype(BF16)
    n = h.shape[0]
    cosT, sinT, cosK, sinK = cosT_ref[...], sinT_ref[...], cosK_ref[...], sinK_ref[...]
    ones = _ones_rows(n)
    zT = _dot_t(wqvT_ref[...], h)
    d = HEAD_DIM
    for hd in range(WIN_HEADS):
        qc_ref[hd] = (_rope_rows(zT[hd * d:(hd + 1) * d], cosT, sinT) * (ATTN_SCALE * LOG2E)).astype(BF16)
    o1 = WIN_HEADS * d
    for hd in range(GLB_HEADS):
        q = zT[o1 + hd * d:o1 + (hd + 1) * d]
        q = _rope_rows(q * _rms_rows(q) * gqn_ref[...], cosT, sinT)
        qd_ref[hd] = (q * (ATTN_SCALE * LOG2E)).astype(BF16)
    o2 = o1 + GLB_HEADS * d
    for j in range(WIN_KV_HEADS):
        vc_ref[j, 0:d, :] = zT[o2 + j * d:o2 + (j + 1) * d].astype(BF16)
        vc_ref[j, d:, :] = ones
    o3 = o2 + WIN_KV_HEADS * d
    for j in range(GLB_KV_HEADS):
        vd_ref[j, 0:d, :] = zT[o3 + j * d:o3 + (j + 1) * d].astype(BF16)
        vd_ref[j, d:, :] = ones

    zk = jnp.dot(h, wk_ref[...], preferred_element_type=F32)
    zkr = jnp.dot(h, wkr_ref[...], preferred_element_type=F32)
    wc = WIN_KV_HEADS * d
    kc = zk[:, :wc] * cosK + zkr[:, :wc] * sinK
    z, zr = zk[:, wc:], zkr[:, wc:]
    sq = z * z
    hi = sq.astype(BF16)
    lo = (sq - hi.astype(F32)).astype(BF16)
    mean = (jnp.dot(hi, bd_ref[...], preferred_element_type=F32) + jnp.dot(lo, bd_ref[...], preferred_element_type=F32))
    kd = lax.rsqrt(mean + NORM_EPS) * (z * gk_ref[...] * cosK + zr * gkp_ref[...] * sinK)
    k_ref[...] = jnp.concatenate([kc, kd], axis=1).astype(BF16)


def _odd_prep(xa, p, mods, tabs):
    B, S, D = xa.shape
    w_in = p["w_in"]
    d = HEAD_DIM
    sizes = (WIN_HEADS, WIN_KV_HEADS, WIN_KV_HEADS, GLB_HEADS, GLB_KV_HEADS, GLB_KV_HEADS)
    offs = [0]
    for s in sizes:
        offs.append(offs[-1] + s * d)
    col = lambda j: w_in[:, offs[j]:offs[j + 1]]
    bf = lambda a: a.astype(BF16)
    wqvT = bf(jnp.concatenate([col(0), col(3), col(2), col(5)], axis=1).T)
    wk = jnp.concatenate([col(1), col(4)], axis=1)
    wkr = _rot_cols(wk, d)
    gqn = jnp.broadcast_to(p["q_norm"].astype(F32)[:, None], (d, TOK_TILE))
    gk1 = p["k_norm"].astype(F32)
    gk = jnp.tile(gk1, GLB_KV_HEADS).reshape(1, -1)
    gkp = jnp.tile(_perm_cols(gk1, d), GLB_KV_HEADS).reshape(1, -1)
    wd = GLB_KV_HEADS * d
    lane = jnp.arange(wd)
    bd = bf(jnp.where(lane[:, None] // d == lane[None, :] // d, 1.0 / d, 0.0))
    assert WIN_KV_HEADS * d == LANES and wd == LANES
    cosT, sinT, cosK, sinK = tabs
    weights = [wqvT, bf(wk), bf(wkr), gqn, gk, gkp, bd]
    nt = S // TOK_TILE
    ctx_tile = nt - 1
    mod_map = lambda b, i: (b, (i == ctx_tile).astype(jnp.int32), 0, 0)
    dva = d + ONES_ROWS
    return pl.pallas_call(
        _odd_prep_kernel,
        grid=(B, nt),
        in_specs=[pl.BlockSpec((1, TOK_TILE, D), lambda b, i: (b, i, 0)),
                  _full((1, D)),
                  pl.BlockSpec((1, 1, 1, D), mod_map),
                  pl.BlockSpec((1, 1, 1, D), mod_map)]
                 + [_full(w.shape) for w in weights]
                 + [pl.BlockSpec((d, TOK_TILE), lambda b, i: (0, i)),
                    pl.BlockSpec((d, TOK_TILE), lambda b, i: (0, i)),
                    pl.BlockSpec((TOK_TILE, LANES), lambda b, i: (i, 0)),
                    pl.BlockSpec((TOK_TILE, LANES), lambda b, i: (i, 0))],
        out_specs=[pl.BlockSpec((None, WIN_HEADS, d, TOK_TILE), lambda b, i: (b, 0, 0, i)),
                   pl.BlockSpec((None, GLB_HEADS, d, TOK_TILE), lambda b, i: (b, 0, 0, i)),
                   pl.BlockSpec((None, TOK_TILE, 2 * LANES), lambda b, i: (b, i, 0)),
                   pl.BlockSpec((None, WIN_KV_HEADS, dva, TOK_TILE), lambda b, i: (b, 0, 0, i)),
                   pl.BlockSpec((None, GLB_KV_HEADS, dva, TOK_TILE), lambda b, i: (b, 0, 0, i))],
        out_shape=[jax.ShapeDtypeStruct((B, WIN_HEADS, d, S), BF16),
                   jax.ShapeDtypeStruct((B, GLB_HEADS, d, S), BF16),
                   jax.ShapeDtypeStruct((B, S, 2 * LANES), BF16),
                   jax.ShapeDtypeStruct((B, WIN_KV_HEADS, dva, S), BF16),
                   jax.ShapeDtypeStruct((B, GLB_KV_HEADS, dva, S), BF16)],
        compiler_params=_cparams(2),
        name="odd_qkv_prep",
    )(xa, p["norm_mix"].reshape(1, D).astype(F32), mods["sh_a"], mods["sc_a"], *weights, cosT, sinT, cosK, sinK)


class _KeyPack:
    def __init__(self, kw, dk, block_of, slot_of):
        self.kw, self.dk, self.block_of, self.slot_of = kw, dk, block_of, slot_of


def _pad_queries(qT, pack, head):
    reps = pack.kw // qT.shape[0]
    if reps == 1:
        return qT
    rows = lax.broadcasted_iota(jnp.int32, (pack.kw, qT.shape[1]), 0)
    slot = pack.slot_of(head)
    keep = (rows >= slot * pack.dk) & (rows < (slot + 1) * pack.dk)
    return jnp.where(keep, jnp.concatenate([qT] * reps, axis=0), jnp.zeros((), qT.dtype))


def _aligned(start):
    return start if isinstance(start, int) else pl.multiple_of(start, PV_TILE)


def _attn_step(k_ref, vT_ref, acc_ref, q_next, next_offs, s_next, cur_offs, s_cur, m, cmax):
    m_new = jnp.maximum(m, cmax) if cur_offs else m
    pv = None
    cnext = None
    for i in range(max(len(next_offs), len(cur_offs))):
        rows = slice(i * PV_TILE, (i + 1) * PV_TILE)
        if i < len(next_offs):
            s = jnp.dot(k_ref[pl.ds(_aligned(next_offs[i]), PV_TILE), :], q_next, preferred_element_type=F32)
            s_next[rows, :] = s
            cm = jnp.max(s, axis=0, keepdims=True)
            cnext = cm if cnext is None else jnp.maximum(cnext, cm)
        if i < len(cur_offs):
            p = jnp.exp2(s_cur[rows, :] - m_new).astype(BF16)
            d = jnp.dot(vT_ref[:, pl.ds(_aligned(cur_offs[i]), PV_TILE)], p, preferred_element_type=F32)
            pv = d if pv is None else pv + d
    if cur_offs:
        acc_ref[...] = acc_ref[...] * jnp.exp2(m - m_new) + pv
    return m_new, cnext


def _softmax_init(init_ref, dva, dv, tq):
    m0 = jnp.broadcast_to(init_ref[0, 0:1, 0:1], (1, tq))
    row = lax.broadcasted_iota(jnp.int32, (dva, tq), 0)
    acc0 = jnp.where(row == dv, jnp.broadcast_to(init_ref[0, 1:2, 0:1], (dva, tq)), 0.0)
    return m0, acc0


def _flash_kernel(init_ref, qT_ref, qTn_ref, k_ref, vT_ref, o_ref, s0_ref, s1_ref, acc_ref, cm_ref, *,
                  T, n_ctx, kt, dv, pack):
    head = pl.program_id(1)
    qT = _pad_queries(qT_ref[...], pack, head)
    tq = qT.shape[1]
    dva = vT_ref.shape[0]
    pieces = lambda off: [off + kk for kk in range(0, kt, PV_TILE)]
    first = [T + kk for kk in range(0, n_ctx, PV_TILE)] + pieces(0)
    step = functools.partial(_attn_step, k_ref, vT_ref, acc_ref)

    @pl.when(pl.program_id(2) == 0)
    def _():
        _, cm_ref[0:1, :] = step(qT, first, s0_ref, [], None, None, None)

    m, acc0 = _softmax_init(init_ref, dva, dv, tq)
    acc_ref[...] = acc0
    m, cm_b = step(qT, pieces(kt), s1_ref, first, s0_ref, m, cm_ref[0:1, :])

    def body(t, carry):
        m, cm_b = carry
        off = pl.multiple_of(t * (2 * kt), kt)
        m, cm_a = step(qT, pieces(off + 2 * kt), s0_ref, pieces(off + kt), s1_ref, m, cm_b)
        m, cm_b = step(qT, pieces(off + 3 * kt), s1_ref, pieces(off + 2 * kt), s0_ref, m, cm_a)
        return m, cm_b

    m, cm_b = lax.fori_loop(0, T // (2 * kt) - 1, body, (m, cm_b))
    qn = _pad_queries(qTn_ref[...], pack, head)
    m, cm_ref[0:1, :] = step(qn, first, s0_ref, pieces(T - kt), s1_ref, m, cm_b)
    acc = acc_ref[...]
    o_ref[...] = acc[:dv] / acc[dv:dv + 1]


def _flash(qT, k, vT, init, T, pack):
    B, Hq, dk, S = qT.shape
    Hv, dva = vT.shape[1], vT.shape[2]
    dv = dva - ONES_ROWS
    gv = Hq // Hv
    kt = min(KEY_TILE, T // 2)
    tq = min(Q_TILE, T)
    assert T % (2 * kt) == 0 and T % tq == 0 and kt % PV_TILE == 0
    n_ctx = S - T
    assert n_ctx % PV_TILE == 0
    kern = functools.partial(_flash_kernel, T=T, n_ctx=n_ctx, kt=kt, dv=dv, pack=pack)
    nq = T // tq
    return pl.pallas_call(
        kern,
        grid=(B, Hq, nq),
        in_specs=[pl.BlockSpec((1, 2, LANES),lambda b, h, i: (h, 0, 0)),
                  pl.BlockSpec((None, None, dk, tq), lambda b, h, i: (b, h, 0, i)),
                  pl.BlockSpec((None, None, dk, tq), lambda b, h, i: (b, h, 0, jnp.minimum(i + 1, nq - 1))),
                  pl.BlockSpec((None, S, pack.kw), lambda b, h, i: (b, 0, pack.block_of(h))),
                  pl.BlockSpec((None, None, dva, S), lambda b, h, i: (b, h // gv, 0, 0))],
        out_specs=pl.BlockSpec((None, None, dv, tq), lambda b, h, i: (b, h, 0, i)),
        out_shape=jax.ShapeDtypeStruct((B, Hq, dv, T), F32),
        scratch_shapes=[pltpu.VMEM((kt + n_ctx, tq), F32), pltpu.VMEM((kt, tq), F32),
                        pltpu.VMEM((dva, tq), F32), pltpu.VMEM((8, tq), F32)],
        compiler_params=_cparams(3),
        name="dense_attention",
    )(init, qT, qT, k, vT)


def _ctx_attn_kernel(init_ref, qT_ref, k_ref, vT_ref, o_ref, *, dv, pack):
    qT = _pad_queries(qT_ref[...], pack, pl.program_id(1))
    m0, acc0 = _softmax_init(init_ref, vT_ref.shape[0], dv, qT.shape[1])
    s = jnp.dot(k_ref[...], qT, preferred_element_type=F32)
    m = jnp.maximum(m0, jnp.max(s, axis=0, keepdims=True))
    acc = acc0 * jnp.exp2(m0 - m) + jnp.dot(vT_ref[...], jnp.exp2(s - m).astype(BF16), preferred_element_type=F32)
    o_ref[...] = acc[:dv] / acc[dv:dv + 1]


def _ctx_attn(qT, k, vT, init, T, pack):
    B, Hq, dk, S = qT.shape
    Hv, dva = vT.shape[1], vT.shape[2]
    dv = dva - ONES_ROWS
    gv = Hq // Hv
    n = S - T
    blk = T // n
    return pl.pallas_call(
        functools.partial(_ctx_attn_kernel, dv=dv, pack=pack),
        grid=(B, Hq),
        in_specs=[pl.BlockSpec((1, 2, LANES),lambda b, h: (h, 0, 0)),
                  pl.BlockSpec((None, None, dk, n), lambda b, h: (b, h, 0, blk)),
                  pl.BlockSpec((None, n, pack.kw), lambda b, h: (b, blk, pack.block_of(h))),
                  pl.BlockSpec((None, None, dva, n), lambda b, h: (b, h // gv, 0, blk))],
        out_specs=pl.BlockSpec((None, None, dv, n), lambda b, h: (b, h, 0, 0)),
        out_shape=jax.ShapeDtypeStruct((B, Hq, dv, n), F32),
        compiler_params=_cparams(2),
        name="context_attention",
    )(init, qT, k, vT)


def _window_kernel(sink_ref, bias_ref, qT_ref, k_ref, kc_ref, vT_ref, o_ref, *, dv, pack):
    i = pl.program_id(2)
    G, _, tq = qT_ref.shape
    head = pl.program_id(1) * G
    qT = jnp.concatenate([_pad_queries(qT_ref[g], pack, head) for g in range(G)], axis=1)
    snk = jnp.concatenate([jnp.broadcast_to(sink_ref[g, 0:1, 0:1], (1, tq)) for g in range(G)], axis=1)
    W = WINDOW
    kband = jnp.concatenate([k_ref[0][tq - W:tq, :], k_ref[1][...], k_ref[2][0:W, :]], axis=0)
    before_first = jnp.where(i == 0, NEG_INF, 0.0)
    after_last = jnp.where(i == pl.num_programs(2) - 1, NEG_INF, 0.0)
    bias = jnp.concatenate([bias_ref[0:W, :] + before_first, bias_ref[W:W + tq, :],
                            bias_ref[W + tq:, :] + after_last], axis=0)
    s_loc = jnp.dot(kband, qT, preferred_element_type=F32) + bias
    s_ctx = jnp.dot(kc_ref[...], qT, preferred_element_type=F32)
    m = jnp.maximum(jnp.maximum(jnp.max(s_loc, axis=0, keepdims=True), jnp.max(s_ctx, axis=0, keepdims=True)), snk)
    p_loc = jnp.exp2(s_loc - m).astype(BF16)
    p_ctx = jnp.exp2(s_ctx - m).astype(BF16)
    vband = jnp.concatenate([vT_ref[0][:, tq - W:tq], vT_ref[1][...], vT_ref[2][:, 0:W]], axis=1)
    acc = (jnp.dot(vband, p_loc, preferred_element_type=F32)
           + jnp.dot(vT_ref[3][...], p_ctx, preferred_element_type=F32))
    o = acc[:dv] / (acc[dv:dv + 1] + jnp.exp2(snk - m))
    for g in range(G):
        o_ref[g] = o[:, g * tq:(g + 1) * tq]


def _window(qT, k, vT, sink, T, pack):
    B, Hq, dk, S = qT.shape
    Hk, dva = vT.shape[1], vT.shape[2]
    dv = dva - ONES_ROWS
    g = Hq // Hk
    tq = TOK_TILE
    nt = T // tq
    ctx_blk = T // tq
    clip = lambda j: jnp.clip(j, 0, nt - 1)
    kern = functools.partial(_window_kernel, dv=dv, pack=pack)
    kspec = lambda f: pl.BlockSpec((None, tq, pack.kw), lambda b, h, i: (b, f(i), pack.block_of(h * g)))
    vspec = lambda f: pl.BlockSpec((None, None, dva, tq), lambda b, h, i: (b, h, 0, f(i)))
    r = jnp.arange(tq + 2 * WINDOW)[:, None]
    c = jnp.arange(g * tq)[None, :] % tq
    bias = jnp.where(jnp.abs(r - WINDOW - c) <= WINDOW, 0.0, NEG_INF).astype(F32)

    def body(sink_ref, bias_ref, qT_ref, k0, k1, k2, kc, v0, v1, v2, vc, o_ref):
        kern(sink_ref, bias_ref, qT_ref, (k0, k1, k2), kc, (v0, v1, v2, vc), o_ref)

    return pl.pallas_call(
        body,
        grid=(B, Hk, nt),
        in_specs=[pl.BlockSpec((g, 1, LANES), lambda b, h, i: (h, 0, 0)),
                  _full(bias.shape),
                  pl.BlockSpec((None, g, dk, tq), lambda b, h, i: (b, h, 0, i)),
                  kspec(lambda i: clip(i - 1)), kspec(lambda i: i), kspec(lambda i: clip(i + 1)),
                  kspec(lambda i: ctx_blk),
                  vspec(lambda i: clip(i - 1)), vspec(lambda i: i), vspec(lambda i: clip(i + 1)),
                  vspec(lambda i: ctx_blk)],
        out_specs=pl.BlockSpec((None, g, dv, tq), lambda b, h, i: (b, h, 0, i)),
        out_shape=jax.ShapeDtypeStruct((B, Hq, dv, T), F32),
        compiler_params=_cparams(3),
        name="window_attention",
    )(sink, bias, qT, k, k, k, k, vT, vT, vT, vT)


def _project_out(x_ref, oT, w_ref, gate_ref, y_ref):
    y = jnp.dot(oT.T.astype(BF16), w_ref[...], preferred_element_type=F32)
    y_ref[0] = x_ref[0] + gate_ref[0, 0] * y


def _even_out_kernel(x_ref, oa_ref, od_ref, w_ref, gate_ref, lam_ref, sub_ref, y_ref, *, post_scale):
    lam = lam_ref[...]
    parts = [oa_ref[hd] for hd in range(MLA_HEADS)]
    for hd in range(DIFF_HEADS):
        diff = od_ref[2 * hd] - lam * od_ref[2 * hd + 1]
        parts.append(diff * _rms_rows(diff) * sub_ref[...] * post_scale)
    _project_out(x_ref, jnp.concatenate(parts, axis=0), w_ref, gate_ref, y_ref)


def _odd_out_kernel(x_ref, oc_ref, od_ref, w_ref, gate_ref, y_ref):
    parts = [oc_ref[hd] for hd in range(WIN_HEADS)] + [od_ref[hd] for hd in range(GLB_HEADS)]
    _project_out(x_ref, jnp.concatenate(parts, axis=0), w_ref, gate_ref, y_ref)


def _out_proj(kern, x, heads, w, gate, extra, tile0, n_tiles, kind):
    B, S, D = x.shape
    row_map = lambda b, i: (b, i + tile0, 0)
    return pl.pallas_call(
        kern,
        grid=(B, n_tiles),
        in_specs=[pl.BlockSpec((1, TOK_TILE, D), row_map)]
                 + [pl.BlockSpec((None,) + o.shape[1:3] + (TOK_TILE,), lambda b, i: (b, 0, 0, i)) for o in heads]
                 + [_full(w.shape), pl.BlockSpec((1, 1, 1, D), lambda b, i: (b, kind, 0, 0))]
                 + [_full(e.shape) for e in extra],
        out_specs=pl.BlockSpec((1, TOK_TILE, D), row_map),
        out_shape=jax.ShapeDtypeStruct((B, S, D), F32),
        input_output_aliases={0: 0},
        compiler_params=_cparams(2),
        name="merge_out_proj_residual",
    )(x, *heads, w, gate, *extra)


def _first_argmax(vals):
    best, idx = vals[0], jnp.zeros(vals[0].shape, jnp.int32)
    for j in range(1, len(vals)):
        better = vals[j] > best
        idx = jnp.where(better, j, idx)
        best = jnp.where(better, vals[j], best)
    return idx, best


def _pick(idx, vals):
    out = vals[0]
    for j in range(1, len(vals)):
        out = jnp.where(idx == j, vals[j], out)
    return out


def _route(logits, bias):
    s = 1.0 / (1.0 + jnp.exp(-logits))
    sel = s + bias
    srow = [s[e:e + 1] for e in range(N_EXPERTS)]
    row = [sel[e:e + 1] for e in range(N_EXPERTS)]
    scores = []
    for g in range(N_GROUPS):
        a, b, c, d = row[4 * g:4 * g + 4]
        hi1, lo1, hi2, lo2 = jnp.maximum(a, b), jnp.minimum(a, b), jnp.maximum(c, d), jnp.minimum(c, d)
        top1 = jnp.maximum(hi1, hi2)
        top2 = jnp.maximum(jnp.maximum(lo1, lo2), jnp.minimum(hi1, hi2))
        scores.append(top1 + top2)
    gi, _ = _first_argmax(scores)
    v = [_pick(gi, [row[4 * g + j] for g in range(N_GROUPS)]) for j in range(EXPERTS_PER_GROUP)]
    sv = [_pick(gi, [srow[4 * g + j] for g in range(N_GROUPS)]) for j in range(EXPERTS_PER_GROUP)]
    i1, _ = _first_argmax(v)
    i2, _ = _first_argmax([jnp.where(i1 == j, -jnp.inf, v[j]) for j in range(EXPERTS_PER_GROUP)])
    w1, w2 = _pick(i1, sv), _pick(i2, sv)
    tot = w1 + w2
    w1, w2 = w1 / tot, w2 / tot
    rows = []
    for e in range(N_EXPERTS):
        g, j = divmod(e, EXPERTS_PER_GROUP)
        in_g = gi == g
        rows.append(jnp.where(in_g & (i1 == j), w1, 0.0) + jnp.where(in_g & (i2 == j), w2, 0.0))
    return jnp.concatenate(rows, axis=0), gi


def _split(x, terms):
    out = []
    for _ in range(terms):
        part = x.astype(BF16).astype(F32)
        out.append(part)
        x = x - part
    return out


def _to_column(row):
    n = row.shape[1]
    return jnp.concatenate([row, jnp.zeros((LANES - 1, n), F32)], axis=0).T[:, 0:1]


def _moe_kernel(x_ref, g_ref, sh_ref, sc_ref, gate_ref, wrT_ref, br_ref, before_ref, wg_ref, wu_ref, wd_ref, y_ref,
                hs_scr, cs_scr, ys_scr, q_scr, info_ref):
    step = pl.program_id(2)
    per_step = wg_ref.shape[0]
    tm = x_ref.shape[1]
    R = hs_scr.shape[0]

    @pl.when(step == 0)
    def _():
        h = _norm_mod(x_ref[0], g_ref[...], sh_ref[0, 0], sc_ref[0, 0]).astype(BF16)
        comb, gi = _route(_dot_t(wrT_ref[...], h), br_ref[...])
        sel = jnp.concatenate([(gi == g).astype(F32) for g in range(N_GROUPS)]
                              + [jnp.zeros((BF16_ROWS - N_GROUPS, tm), F32)], axis=0)
        rank = jnp.dot(sel.astype(BF16), before_ref[...], preferred_element_type=F32)
        cnt = jnp.sum(sel, axis=1, keepdims=True)
        seg = jnp.ceil(cnt * (1.0 / BF16_ROWS)) * BF16_ROWS
        pos = jnp.zeros((1, tm), F32)
        start = jnp.zeros((1, 1), F32)
        for g in range(N_GROUPS):
            pos = pos + sel[g:g + 1] * (start + rank[g:g + 1])
            info_ref[g] = start[0, 0].astype(jnp.int32)
            info_ref[N_GROUPS + g] = jnp.ceil(cnt[g:g + 1] * (1.0 / MOE_CHUNK))[0, 0].astype(jnp.int32)
            start = start + seg[g:g + 1]
        slot = lax.broadcasted_iota(jnp.int32, (R, tm), 0).astype(F32)
        P = jnp.where(slot == pos, 1.0, 0.0).astype(BF16)
        hs_scr[...] = jnp.dot(P, h, preferred_element_type=F32).astype(BF16)
        parts = _split(comb, COMB_TERMS)
        combT = jnp.concatenate(parts + [jnp.zeros((LANES - COMB_TERMS * N_EXPERTS, tm), F32)], axis=0).T
        cs_scr[...] = jnp.dot(P, combT.astype(BF16), preferred_element_type=F32)
        lane = lax.broadcasted_iota(jnp.int32, (tm, R), 1).astype(F32)
        q_scr[...] = jnp.where(lane == _to_column(pos), 1.0, 0.0).astype(BF16)
        ys_scr[...] = jnp.zeros_like(ys_scr)

    steps_per_group = EXPERTS_PER_GROUP // per_step
    grp = lax.shift_right_logical(step, steps_per_group.bit_length() - 1)
    start = info_ref[grp]
    n_chunks = info_ref[N_GROUPS + grp]

    def experts(r0, rows):
        h = hs_scr[pl.ds(r0, rows), :]
        cs = cs_scr[pl.ds(r0, rows), :]
        lane = lax.broadcasted_iota(jnp.int32, (rows, LANES), 1) & (N_EXPERTS - 1)
        total = None
        for j in range(per_step):
            a = jnp.dot(h, wg_ref[j], preferred_element_type=F32)
            u = jnp.dot(h, wu_ref[j], preferred_element_type=F32)
            act = (a * (1.0 / (1.0 + jnp.exp(-a)))) * u
            y = jnp.dot(act.astype(BF16), wd_ref[j], preferred_element_type=F32)
            w = jnp.sum(jnp.where(lane == step * per_step + j, cs, 0.0), axis=1, keepdims=True)
            total = w * y if total is None else total + w * y
        ys_scr[pl.ds(r0, rows), :] += total

    def pair(c, carry):
        experts(pl.multiple_of(start + c * (2 * MOE_CHUNK), BF16_ROWS), 2 * MOE_CHUNK)
        return carry

    lax.fori_loop(0, lax.shift_right_logical(n_chunks, 1), pair, 0)

    @pl.when((n_chunks & 1) == 1)
    def _():
        experts(pl.multiple_of(start + (n_chunks - 1) * MOE_CHUNK, BF16_ROWS), MOE_CHUNK)

    @pl.when(step == pl.num_programs(2) - 1)
    def _():
        q = q_scr[...]
        out = sum(jnp.dot(q, part.astype(BF16), preferred_element_type=F32) for part in _split(ys_scr[...], 2))
        y_ref[0] = x_ref[0] + gate_ref[0, 0] * out


def _moe(x, g, sh, sc, gate, wrT, br, wg, wu, wd, tm, tile0, n_tiles, kind):
    B, S, D = x.shape
    E, _, F = wg.shape
    R = -(-(tm + BF16_ROWS * N_GROUPS + MOE_CHUNK) // MXU_DIM) * MXU_DIM
    idx = jnp.arange(tm)
    before = (idx[:, None] < idx[None, :]).astype(BF16)
    mod_map = lambda b, i, e: (b, kind, 0, 0)
    row_map = lambda b, i, e: (b, i + tile0, 0)
    return pl.pallas_call(
        _moe_kernel,
        grid=(B, n_tiles, E // MOE_STEP_EXPERTS),
        in_specs=[pl.BlockSpec((1, tm, D), row_map),
                  pl.BlockSpec((1, D), lambda b, i, e: (0, 0)),
                  pl.BlockSpec((1, 1, 1, D), mod_map),
                  pl.BlockSpec((1, 1, 1, D), mod_map),
                  pl.BlockSpec((1, 1, 1, D), mod_map),
                  pl.BlockSpec((E, D), lambda b, i, e: (0, 0)),
                  pl.BlockSpec((E, 1), lambda b, i, e: (0, 0)),
                  pl.BlockSpec((tm, tm), lambda b, i, e: (0, 0)),
                  pl.BlockSpec((MOE_STEP_EXPERTS, D, F), lambda b, i, e: (e, 0, 0)),
                  pl.BlockSpec((MOE_STEP_EXPERTS, D, F), lambda b, i, e: (e, 0, 0)),
                  pl.BlockSpec((MOE_STEP_EXPERTS, F, D), lambda b, i, e: (e, 0, 0))],
        out_specs=pl.BlockSpec((1, tm, D), row_map),
        out_shape=jax.ShapeDtypeStruct((B, S, D), F32),
        scratch_shapes=[pltpu.VMEM((R, D), BF16), pltpu.VMEM((R, LANES), F32), pltpu.VMEM((R, D), F32),
                        pltpu.VMEM((tm, R), BF16), pltpu.SMEM((2 * N_GROUPS,), jnp.int32)],
        input_output_aliases={0: 0},
        compiler_params=_cparams(3),
        name="moe_experts",
    )(x, g.reshape(1, D).astype(F32), sh, sc, gate, wrT, br, before, wg, wu, wd)


def _final_norm_kernel(x_ref, g_ref, o_ref):
    x = x_ref[0]
    o_ref[0] = (x * lax.rsqrt(jnp.mean(x * x, axis=-1, keepdims=True) + NORM_EPS)) * g_ref[...]


def _final_norm(x, g, T):
    B, S, D = x.shape
    return pl.pallas_call(
        _final_norm_kernel,
        grid=(B, T // TOK_TILE),
        in_specs=[pl.BlockSpec((1, TOK_TILE, D), lambda b, i: (b, i, 0)),
                  pl.BlockSpec((1, D), lambda b, i: (0, 0))],
        out_specs=pl.BlockSpec((1, TOK_TILE, D), lambda b, i: (b, i, 0)),
        out_shape=jax.ShapeDtypeStruct((B, T, D), F32),
        compiler_params=_cparams(2),
        name="final_norm",
    )(x, g.reshape(1, D).astype(F32))


def _axial_tables(T, n_ctx, rot_dim):
    half = rot_dim // 2
    inv_freq = ROPE_THETA ** (-jnp.arange(0, half, 2, dtype=F32) / half)
    rows = T // GRID_W
    row = jnp.broadcast_to(jnp.arange(rows, dtype=F32)[:, None], (rows, GRID_W)).reshape(-1)
    col = jnp.broadcast_to(jnp.arange(GRID_W, dtype=F32)[None, :], (rows, GRID_W)).reshape(-1)

    def ang(pos):
        a = pos[:, None] * inv_freq[None, :]
        return jnp.concatenate([a, a], axis=-1)

    a = jnp.concatenate([ang(row), ang(col)], axis=-1)
    cos = jnp.concatenate([jnp.cos(a), jnp.ones((n_ctx, rot_dim), F32)], axis=0)
    sin = jnp.concatenate([jnp.sin(a), jnp.zeros((n_ctx, rot_dim), F32)], axis=0)
    reps = LANES // rot_dim
    return cos.T, sin.T, jnp.tile(cos, (1, reps)), jnp.tile(sin, (1, reps))


def _plain_init(n_heads):
    return jnp.broadcast_to(jnp.array([NEG_INF, 0.0], F32)[None, :, None], (n_heads, 2, LANES))


def _sink_init(sink):
    s = sink.astype(F32) * LOG2E
    return jnp.broadcast_to(jnp.stack([s, jnp.ones_like(s)], axis=1)[:, :, None], (s.shape[0], 2, LANES))


def _even_mixer(xa, p, mods, T, with_ctx, lam_init, tabs):
    B, S, D = xa.shape
    qa, ka, va, qd, kd, vd = _even_prep(xa, p, mods, tabs)
    pack_a = _KeyPack(MLA_PAD, MLA_PAD, lambda h: h, lambda h: 0)
    per = LANES // DIFF_D
    pack_d = _KeyPack(LANES, DIFF_D, lambda h: h // per, lambda h: h % per)
    init_a, init_d = _plain_init(MLA_HEADS), _plain_init(2 * DIFF_HEADS)
    lam = p["lam"].astype(F32)
    lam_full = (jnp.exp(jnp.sum(lam[0] * lam[1])) - jnp.exp(jnp.sum(lam[2] * lam[3])) + lam_init).reshape(1, 1)
    sub = jnp.broadcast_to(p["subln"].astype(F32)[:, None], (DIFF_V, TOK_TILE))
    kern = functools.partial(_even_out_kernel, post_scale=1.0 - lam_init)
    oa = _flash(qa, ka, va, init_a, T, pack_a)
    od = _flash(qd, kd, vd, init_d, T, pack_d)
    xa = _out_proj(kern, xa, [oa, od], p["w_out"], mods["g_a"], [lam_full, sub], 0, T // TOK_TILE, 0)
    if with_ctx:
        oa = _ctx_attn(qa, ka, va, init_a, T, pack_a)
        od = _ctx_attn(qd, kd, vd, init_d, T, pack_d)
        xa = _out_proj(kern, xa, [oa, od], p["w_out"], mods["g_a"], [lam_full, sub], T // TOK_TILE, 1, 1)
    return xa


def _odd_mixer(xa, p, mods, T, with_ctx, tabs):
    B, S, D = xa.shape
    qc, qd, k, vc, vd = _odd_prep(xa, p, mods, tabs)
    gw, gd = WIN_HEADS // WIN_KV_HEADS, GLB_HEADS // GLB_KV_HEADS
    pack_c = _KeyPack(LANES, HEAD_DIM, lambda h: 0, lambda h: h // gw)
    pack_d = _KeyPack(LANES, HEAD_DIM, lambda h: 1, lambda h: h // gd)
    init_d = _plain_init(GLB_HEADS)
    sink = p["sink"].astype(F32) * LOG2E
    sink_w = jnp.broadcast_to(sink[:, None, None], (WIN_HEADS, 1, LANES))
    od = _flash(qd, k, vd, init_d, T, pack_d)
    oc = _window(qc, k, vc, sink_w, T, pack_c)
    xa = _out_proj(_odd_out_kernel, xa, [oc, od], p["w_out"], mods["g_a"], [], 0, T // TOK_TILE, 0)
    if with_ctx:
        oc = _ctx_attn(qc, k, vc, _sink_init(p["sink"]), T, pack_c)
        od = _ctx_attn(qd, k, vd, init_d, T, pack_d)
        xa = _out_proj(_odd_out_kernel, xa, [oc, od], p["w_out"], mods["g_a"], [], T // TOK_TILE, 1, 1)
    return xa


def kernel(x, c, ctx, c_ctx, w_mod, b_mod, norm_mix, norm_ffn, even_w_in, even_norm_q, even_norm_kv, even_w_uq, even_w_ukv, even_lambda, even_subln, even_w_out, odd_w_in, odd_sink, odd_q_norm, odd_k_norm, odd_w_out, w_router, b_router, w_gate, w_up, w_down, norm_final):
    B, T, D = x.shape
    n_ctx = ctx.shape[1]
    depth = w_mod.shape[0]
    assert n_ctx == TOK_TILE == PV_TILE and T % MOE_TILE == 0 and B <= 7
    assert MLA_ROPE == DIFF_D
    S = T + n_ctx
    tabs32 = _axial_tables(T, n_ctx, MLA_ROPE)
    tabs64 = _axial_tables(T, n_ctx, HEAD_DIM)

    cond = jnp.zeros((8, D), F32).at[:B].set(c.astype(F32)).at[B].set(c_ctx.astype(F32))
    mod_all = _mod_vectors(cond, w_mod, b_mod)
    wrT = w_router.astype(BF16).T
    br = b_router.astype(F32).reshape(N_EXPERTS, 1)

    xa = jnp.concatenate([x, ctx], axis=1).astype(F32)
    for l in range(depth):
        with_ctx = l < depth - 1
        i = l // 2
        lat = mod_all[l, :B].reshape(B, 6, D)
        cx = jnp.broadcast_to(mod_all[l, B].reshape(1, 6, D), (B, 6, D))
        both = jnp.stack([lat, cx], axis=1)
        names = ("sh_a", "sc_a", "g_a", "sh_f", "sc_f", "g_f")
        mods = {n: both[:, :, j:j + 1, :] for j, n in enumerate(names)}
        if l % 2 == 0:
            lam_init = 0.8 - 0.6 * math.exp(-0.3 * l)
            p = dict(norm_mix=norm_mix[l], w_in=even_w_in[i], norm_q=even_norm_q[i], norm_kv=even_norm_kv[i],
                     w_uq=even_w_uq[i], w_ukv=even_w_ukv[i], lam=even_lambda[i], subln=even_subln[i],
                     w_out=even_w_out[i].astype(BF16))
            xa = _even_mixer(xa, p, mods, T, with_ctx, lam_init, tabs32)
        else:
            p = dict(norm_mix=norm_mix[l], w_in=odd_w_in[i], sink=odd_sink[i], q_norm=odd_q_norm[i],
                     k_norm=odd_k_norm[i], w_out=odd_w_out[i].astype(BF16))
            xa = _odd_mixer(xa, p, mods, T, with_ctx, tabs64)
        wg, wu, wd = w_gate[l].astype(BF16), w_up[l].astype(BF16), w_down[l].astype(BF16)
        ffn = (norm_ffn[l], mods["sh_f"], mods["sc_f"], mods["g_f"], wrT, br, wg, wu, wd)
        xa = _moe(xa, *ffn, MOE_TILE, 0, T // MOE_TILE, 0)
        if with_ctx:
            xa = _moe(xa, *ffn, TOK_TILE, T // TOK_TILE, 1, 1)
    return _final_norm(xa, norm_final, T)
```

```python
import functools
import math

import jax
import jax.numpy as jnp
from jax import lax
from jax.experimental import pallas as pl
from jax.experimental.pallas import tpu as pltpu

F32 = jnp.float32
BF16 = jnp.bfloat16
LOG2E = 1.4426950408889634

GRID_W = 64
ROPE_THETA = 10000.0
NORM_EPS = 1e-6
NEG_INF = -1e30

MLA_HEADS = 8
MLA_Q_LORA = 384
MLA_KV_LORA = 256
MLA_NOPE = 64
MLA_ROPE = 32
MLA_V = 64
MLA_SCALE = (MLA_NOPE + MLA_ROPE) ** -0.5
MLA_IN_COLS = MLA_Q_LORA + MLA_KV_LORA + MLA_ROPE
MLA_PAD = 128

DIFF_HEADS = 8
DIFF_D = 32
DIFF_V = 2 * DIFF_D
DIFF_SCALE = DIFF_D ** -0.5
DIFF_KW = 64

HEAD_DIM = 64
WIN_HEADS = 8
WIN_KV_HEADS = 2
WINDOW = 128
GLB_HEADS = 8
GLB_KV_HEADS = 2
ATTN_SCALE = HEAD_DIM ** -0.5

N_EXPERTS = 16
N_GROUPS = 4
EXPERTS_PER_GROUP = N_EXPERTS // N_GROUPS

TOK_TILE = 256
Q_TILE = 1024
KEY_TILE = 2048
PV_TILE = 256
MOE_TILE = 1024
MOE_CHUNK = 128
MOE_STEP_EXPERTS = 2
COMB_TERMS = 3
ONES_ROWS = 16
LANES = 128
BF16_ROWS = 16
MXU_DIM = 256
VMEM_LIMIT = 56 * 1024 * 1024


def _cparams(n_axes):
    return pltpu.CompilerParams(dimension_semantics=("arbitrary",) * n_axes,
                                vmem_limit_bytes=VMEM_LIMIT)


def _full(shape):
    return pl.BlockSpec(shape, lambda *_: (0,) * len(shape))


def _mod_kernel(a_ref, w_ref, b_ref, o_ref):
    a = a_ref[...]
    a = a * (1.0 / (1.0 + jnp.exp(-a)))
    o_ref[0] = jnp.dot(a.astype(BF16), w_ref[0].astype(BF16), preferred_element_type=F32) + b_ref[0]


def _mod_vectors(cond, w_mod, b_mod):
    L, D, N = w_mod.shape
    tn = 1536
    return pl.pallas_call(
        _mod_kernel,
        grid=(L, N // tn),
        in_specs=[pl.BlockSpec((8, D), lambda l, j: (0, 0)),
                  pl.BlockSpec((1, D, tn), lambda l, j: (l, 0, j)),
                  pl.BlockSpec((1, 1, tn), lambda l, j: (l, 0, j))],
        out_specs=pl.BlockSpec((1, 8, tn), lambda l, j: (l, 0, j)),
        out_shape=jax.ShapeDtypeStruct((L, 8, N), F32),
        compiler_params=_cparams(2),
        name="mod_vectors",
    )(cond, w_mod, b_mod.reshape(L, 1, N))


def _norm_mod(x, g, sh, sc):
    y = x * lax.rsqrt(jnp.mean(x * x, axis=-1, keepdims=True) + NORM_EPS)
    return (y * g) * (1.0 + sc) + sh


def _dot_t(w, h):
    return lax.dot_general(w, h, (((1,), (1,)), ((), ())), preferred_element_type=F32)


def _rms_rows(x):
    return lax.rsqrt(jnp.mean(x * x, axis=0, keepdims=True) + NORM_EPS)


def _rope_rows(x, cosT, sinT):
    q = x.shape[0] // 4
    rot = jnp.concatenate([-x[q:2 * q], x[0:q], -x[3 * q:4 * q], x[2 * q:3 * q]], axis=0)
    return x * cosT + rot * sinT


def _ones_rows(n):
    row = lax.broadcasted_iota(jnp.int32, (ONES_ROWS, n), 0)
    return jnp.where(row == 0, 1.0, 0.0).astype(BF16)


def _tile_lanes(x, reps):
    return x if reps == 1 else jnp.concatenate([x] * reps, axis=1)


def _even_prep_kernel(x_ref, g_ref, sh_ref, sc_ref, w1T_ref, wckv_ref, wkr_ref, wkrr_ref, wdk_ref, wdkr_ref,
                      wuqT_ref, wuvT_ref, wuk_ref, place_ref, gq_ref, gkvc_ref, gkvr_ref, cosT_ref, sinT_ref,
                      cosK_ref, sinK_ref,
                      qa_ref, ka_ref, va_ref, qd_ref, kd_ref, vd_ref):
    h = _norm_mod(x_ref[0], g_ref[...], sh_ref[0, 0], sc_ref[0, 0]).astype(BF16)
    n = h.shape[0]
    cosT, sinT, cosK, sinK = cosT_ref[...], sinT_ref[...], cosK_ref[...], sinK_ref[...]
    ones = _ones_rows(n)
    zT = _dot_t(w1T_ref[...], h)
    o1, o2 = MLA_Q_LORA, MLA_Q_LORA + MLA_KV_LORA
    o3 = o2 + 2 * DIFF_HEADS * DIFF_D

    c_q = zT[:o1]
    c_q = (c_q * _rms_rows(c_q) * gq_ref[...]).astype(BF16)
    qT = jnp.dot(wuqT_ref[...], c_q, preferred_element_type=F32)
    for hd in range(MLA_HEADS):
        r0 = hd * MLA_PAD
        rope = _rope_rows(qT[r0 + MLA_NOPE:r0 + MLA_NOPE + MLA_ROPE], cosT, sinT)
        head = jnp.concatenate([qT[r0:r0 + MLA_NOPE], rope, qT[r0 + MLA_NOPE + MLA_ROPE:r0 + MLA_PAD]], axis=0)
        qa_ref[hd] = (head * (MLA_SCALE * LOG2E)).astype(BF16)

    c_kvT = zT[o1:o2]
    c_kvT = (c_kvT * _rms_rows(c_kvT) * gkvc_ref[...]).astype(BF16)
    vT = jnp.dot(wuvT_ref[...], c_kvT, preferred_element_type=F32).astype(BF16)
    for hd in range(MLA_HEADS):
        va_ref[hd, 0:MLA_V, :] = vT[hd * MLA_V:(hd + 1) * MLA_V]
        va_ref[hd, MLA_V:, :] = ones

    for j in range(2 * DIFF_HEADS):
        qj = _rope_rows(zT[o2 + j * DIFF_D:o2 + (j + 1) * DIFF_D], cosT, sinT)
        qd_ref[j] = (qj * (DIFF_SCALE * LOG2E)).astype(BF16)
    for hd in range(DIFF_HEADS):
        vd_ref[hd, 0:DIFF_V, :] = zT[o3 + hd * DIFF_V:o3 + (hd + 1) * DIFF_V].astype(BF16)
        vd_ref[hd, DIFF_V:, :] = ones

    c_kv = jnp.dot(h, wckv_ref[...], preferred_element_type=F32)
    c_kv = (c_kv * lax.rsqrt(jnp.mean(c_kv * c_kv, axis=-1, keepdims=True) + NORM_EPS) * gkvr_ref[...]).astype(BF16)
    kr = (jnp.dot(h, wkr_ref[...], preferred_element_type=F32) * cosK
          + jnp.dot(h, wkrr_ref[...], preferred_element_type=F32) * sinK)
    ka = (jnp.dot(c_kv, wuk_ref[...], preferred_element_type=F32)
          + jnp.dot(kr.astype(BF16), place_ref[...], preferred_element_type=F32))
    ka = ka.astype(BF16)
    for hd in range(MLA_HEADS):
        ka_ref[hd] = ka[:, hd * MLA_PAD:(hd + 1) * MLA_PAD]

    reps = wdk_ref.shape[1] // LANES
    kd = (jnp.dot(h, wdk_ref[...], preferred_element_type=F32) * _tile_lanes(cosK, reps)
          + jnp.dot(h, wdkr_ref[...], preferred_element_type=F32) * _tile_lanes(sinK, reps))
    kd = kd.astype(BF16)
    for blk in range(kd_ref.shape[0]):
        kd_ref[blk] = kd[:, blk * DIFF_KW:(blk + 1) * DIFF_KW]


def _rot_cols(w, width):
    d, n = w.shape
    w4 = w.reshape(d, n // width, 4, width // 4)
    return jnp.stack([-w4[:, :, 1], w4[:, :, 0], -w4[:, :, 3], w4[:, :, 2]], axis=2).reshape(d, n)


def _perm_cols(g, width):
    g4 = g.reshape(-1, 4, width // 4)
    return jnp.stack([g4[:, 1], g4[:, 0], g4[:, 3], g4[:, 2]], axis=1).reshape(-1)


def _even_prep(xa, p, mods, tabs):
    B, S, D = xa.shape
    w_in, w_uq, w_ukv = p["w_in"], p["w_uq"], p["w_ukv"]
    o1, o2, o3 = MLA_Q_LORA, MLA_Q_LORA + MLA_KV_LORA, MLA_IN_COLS
    nq = 2 * DIFF_HEADS * DIFF_D
    bf = lambda a: a.astype(BF16)
    w1T = bf(jnp.concatenate([w_in[:, :o2], w_in[:, o3:o3 + nq], w_in[:, o3 + 2 * nq:]], axis=1).T)
    wckv = bf(w_in[:, o1:o2])
    wkr = jnp.pad(w_in[:, o2:o3], ((0, 0), (0, LANES - MLA_ROPE)))
    wkrr = jnp.pad(_rot_cols(w_in[:, o2:o3], MLA_ROPE), ((0, 0), (0, LANES - MLA_ROPE)))
    wdk = w_in[:, o3 + nq:o3 + 2 * nq]
    wdkr = _rot_cols(wdk, DIFF_D)
    uq = w_uq.reshape(MLA_Q_LORA, MLA_HEADS, MLA_NOPE + MLA_ROPE)
    wuqT = bf(jnp.pad(uq, ((0, 0), (0, 0), (0, MLA_PAD - MLA_NOPE - MLA_ROPE))).reshape(MLA_Q_LORA, -1).T)
    ukv = w_ukv.reshape(MLA_KV_LORA, MLA_HEADS, MLA_NOPE + MLA_V)
    wuvT = bf(ukv[:, :, MLA_NOPE:].reshape(MLA_KV_LORA, -1).T)
    wuk = bf(jnp.pad(ukv[:, :, :MLA_NOPE], ((0, 0), (0, 0), (0, MLA_PAD - MLA_NOPE))).reshape(MLA_KV_LORA, -1))
    src = jnp.arange(LANES)[:, None]
    dst = jnp.arange(MLA_HEADS * MLA_PAD)[None, :]
    place = bf((src < MLA_ROPE) & (dst % MLA_PAD == src + MLA_NOPE))
    gq = jnp.broadcast_to(p["norm_q"].astype(F32)[:, None], (MLA_Q_LORA, TOK_TILE))
    gkvc = jnp.broadcast_to(p["norm_kv"].astype(F32)[:, None], (MLA_KV_LORA, TOK_TILE))
    gkvr = p["norm_kv"].astype(F32).reshape(1, MLA_KV_LORA)
    cosT, sinT, cosK, sinK = tabs
    weights = [w1T, wckv, bf(wkr), bf(wkrr), bf(wdk), bf(wdkr), wuqT, wuvT, wuk, place, gq, gkvc, gkvr]
    nt = S // TOK_TILE
    ctx_tile = nt - 1
    mod_map = lambda b, i: (b, (i == ctx_tile).astype(jnp.int32), 0, 0)
    HA, HD = MLA_HEADS, DIFF_HEADS
    dva = MLA_V + ONES_ROWS
    return pl.pallas_call(
        _even_prep_kernel,
        grid=(B, nt),
        in_specs=[pl.BlockSpec((1, TOK_TILE, D), lambda b, i: (b, i, 0)),
                  _full((1, D)),
                  pl.BlockSpec((1, 1, 1, D), mod_map),
                  pl.BlockSpec((1, 1, 1, D), mod_map)]
                 + [_full(w.shape) for w in weights]
                 + [pl.BlockSpec((MLA_ROPE, TOK_TILE), lambda b, i: (0, i)),
                    pl.BlockSpec((MLA_ROPE, TOK_TILE), lambda b, i: (0, i)),
                    pl.BlockSpec((TOK_TILE, LANES), lambda b, i: (i, 0)),
                    pl.BlockSpec((TOK_TILE, LANES), lambda b, i: (i, 0))],
        out_specs=[pl.BlockSpec((None, HA, MLA_PAD, TOK_TILE), lambda b, i: (b, 0, 0, i)),
                   pl.BlockSpec((None, HA, TOK_TILE, MLA_PAD), lambda b, i: (b, 0, i, 0)),
                   pl.BlockSpec((None, HA, dva, TOK_TILE), lambda b, i: (b, 0, 0, i)),
                   pl.BlockSpec((None, 2 * HD, DIFF_D, TOK_TILE), lambda b, i: (b, 0, 0, i)),
                   pl.BlockSpec((None, nq // DIFF_KW, TOK_TILE, DIFF_KW), lambda b, i: (b, 0, i, 0)),
                   pl.BlockSpec((None, HD, dva, TOK_TILE), lambda b, i: (b, 0, 0, i))],
        out_shape=[jax.ShapeDtypeStruct((B, HA, MLA_PAD, S), BF16),
                   jax.ShapeDtypeStruct((B, HA, S, MLA_PAD), BF16),
                   jax.ShapeDtypeStruct((B, HA, dva, S), BF16),
                   jax.ShapeDtypeStruct((B, 2 * HD, DIFF_D, S), BF16),
                   jax.ShapeDtypeStruct((B, nq // DIFF_KW, S, DIFF_KW), BF16),
                   jax.ShapeDtypeStruct((B, HD, dva, S), BF16)],
        compiler_params=_cparams(2),
        name="even_qkv_prep",
    )(xa, p["norm_mix"].reshape(1, D).astype(F32), mods["sh_a"], mods["sc_a"], *weights, cosT, sinT, cosK, sinK)


def _odd_prep_kernel(x_ref, g_ref, sh_ref, sc_ref, wqvT_ref, wk_ref, wkr_ref, gqn_ref, gk_ref, gkp_ref, bd_ref,
                     cosT_ref, sinT_ref, cosK_ref, sinK_ref,
                     qc_ref, qd_ref, k_ref, vc_ref, vd_ref):
    h = _norm_mod(x_ref[0], g_ref[...], sh_ref[0, 0], sc_ref[0, 0]).astype(BF16)
    n = h.shape[0]
    cosT, sinT, cosK, sinK = cosT_ref[...], sinT_ref[...], cosK_ref[...], sinK_ref[...]
    ones = _ones_rows(n)
    zT = _dot_t(wqvT_ref[...], h)
    d = HEAD_DIM
    for hd in range(WIN_HEADS):
        qc_ref[hd] = (_rope_rows(zT[hd * d:(hd + 1) * d], cosT, sinT) * (ATTN_SCALE * LOG2E)).astype(BF16)
    o1 = WIN_HEADS * d
    for hd in range(GLB_HEADS):
        q = zT[o1 + hd * d:o1 + (hd + 1) * d]
        q = _rope_rows(q * _rms_rows(q) * gqn_ref[...], cosT, sinT)
        qd_ref[hd] = (q * (ATTN_SCALE * LOG2E)).astype(BF16)
    o2 = o1 + GLB_HEADS * d
    for j in range(WIN_KV_HEADS):
        vc_ref[j, 0:d, :] = zT[o2 + j * d:o2 + (j + 1) * d].astype(BF16)
        vc_ref[j, d:, :] = ones
    o3 = o2 + WIN_KV_HEADS * d
    for j in range(GLB_KV_HEADS):
        vd_ref[j, 0:d, :] = zT[o3 + j * d:o3 + (j + 1) * d].astype(BF16)
        vd_ref[j, d:, :] = ones

    zk = jnp.dot(h, wk_ref[...], preferred_element_type=F32)
    zkr = jnp.dot(h, wkr_ref[...], preferred_element_type=F32)
    wc = WIN_KV_HEADS * d
    kc = zk[:, :wc] * cosK + zkr[:, :wc] * sinK
    z, zr = zk[:, wc:], zkr[:, wc:]
    sq = z * z
    hi = sq.astype(BF16)
    lo = (sq - hi.astype(F32)).astype(BF16)
    mean = (jnp.dot(hi, bd_ref[...], preferred_element_type=F32) + jnp.dot(lo, bd_ref[...], preferred_element_type=F32))
    kd = lax.rsqrt(mean + NORM_EPS) * (z * gk_ref[...] * cosK + zr * gkp_ref[...] * sinK)
    k_ref[0] = kc.astype(BF16)
    k_ref[1] = kd.astype(BF16)


def _odd_prep(xa, p, mods, tabs):
    B, S, D = xa.shape
    w_in = p["w_in"]
    d = HEAD_DIM
    sizes = (WIN_HEADS, WIN_KV_HEADS, WIN_KV_HEADS, GLB_HEADS, GLB_KV_HEADS, GLB_KV_HEADS)
    offs = [0]
    for s in sizes:
        offs.append(offs[-1] + s * d)
    col = lambda j: w_in[:, offs[j]:offs[j + 1]]
    bf = lambda a: a.astype(BF16)
    wqvT = bf(jnp.concatenate([col(0), col(3), col(2), col(5)], axis=1).T)
    wk = jnp.concatenate([col(1), col(4)], axis=1)
    wkr = _rot_cols(wk, d)
    gqn = jnp.broadcast_to(p["q_norm"].astype(F32)[:, None], (d, TOK_TILE))
    gk1 = p["k_norm"].astype(F32)
    gk = jnp.tile(gk1, GLB_KV_HEADS).reshape(1, -1)
    gkp = jnp.tile(_perm_cols(gk1, d), GLB_KV_HEADS).reshape(1, -1)
    wd = GLB_KV_HEADS * d
    lane = jnp.arange(wd)
    bd = bf(jnp.where(lane[:, None] // d == lane[None, :] // d, 1.0 / d, 0.0))
    assert WIN_KV_HEADS * d == LANES and wd == LANES
    cosT, sinT, cosK, sinK = tabs
    weights = [wqvT, bf(wk), bf(wkr), gqn, gk, gkp, bd]
    nt = S // TOK_TILE
    ctx_tile = nt - 1
    mod_map = lambda b, i: (b, (i == ctx_tile).astype(jnp.int32), 0, 0)
    dva = d + ONES_ROWS
    return pl.pallas_call(
        _odd_prep_kernel,
        grid=(B, nt),
        in_specs=[pl.BlockSpec((1, TOK_TILE, D), lambda b, i: (b, i, 0)),
                  _full((1, D)),
                  pl.BlockSpec((1, 1, 1, D), mod_map),
                  pl.BlockSpec((1, 1, 1, D), mod_map)]
                 + [_full(w.shape) for w in weights]
                 + [pl.BlockSpec((d, TOK_TILE), lambda b, i: (0, i)),
                    pl.BlockSpec((d, TOK_TILE), lambda b, i: (0, i)),
                    pl.BlockSpec((TOK_TILE, LANES), lambda b, i: (i, 0)),
                    pl.BlockSpec((TOK_TILE, LANES), lambda b, i: (i, 0))],
        out_specs=[pl.BlockSpec((None, WIN_HEADS, d, TOK_TILE), lambda b, i: (b, 0, 0, i)),
                   pl.BlockSpec((None, GLB_HEADS, d, TOK_TILE), lambda b, i: (b, 0, 0, i)),
                   pl.BlockSpec((None, 2, TOK_TILE, LANES), lambda b, i: (b, 0, i, 0)),
                   pl.BlockSpec((None, WIN_KV_HEADS, dva, TOK_TILE), lambda b, i: (b, 0, 0, i)),
                   pl.BlockSpec((None, GLB_KV_HEADS, dva, TOK_TILE), lambda b, i: (b, 0, 0, i))],
        out_shape=[jax.ShapeDtypeStruct((B, WIN_HEADS, d, S), BF16),
                   jax.ShapeDtypeStruct((B, GLB_HEADS, d, S), BF16),
                   jax.ShapeDtypeStruct((B, 2, S, LANES), BF16),
                   jax.ShapeDtypeStruct((B, WIN_KV_HEADS, dva, S), BF16),
                   jax.ShapeDtypeStruct((B, GLB_KV_HEADS, dva, S), BF16)],
        compiler_params=_cparams(2),
        name="odd_qkv_prep",
    )(xa, p["norm_mix"].reshape(1, D).astype(F32), mods["sh_a"], mods["sc_a"], *weights, cosT, sinT, cosK, sinK)


class _KeyPack:
    def __init__(self, kw, dk, block_of, slot_of):
        self.kw, self.dk, self.block_of, self.slot_of = kw, dk, block_of, slot_of


def _pad_queries(qT, pack, head):
    reps = pack.kw // qT.shape[0]
    if reps == 1:
        return qT
    rows = lax.broadcasted_iota(jnp.int32, (pack.kw, qT.shape[1]), 0)
    slot = pack.slot_of(head)
    keep = (rows >= slot * pack.dk) & (rows < (slot + 1) * pack.dk)
    return jnp.where(keep, jnp.concatenate([qT] * reps, axis=0), jnp.zeros((), qT.dtype))


def _aligned(start):
    return start if isinstance(start, int) else pl.multiple_of(start, PV_TILE)


def _attn_step(k_ref, vT_ref, acc_ref, q_next, next_offs, s_next, cur_offs, s_cur, m, cmax):
    m_new = jnp.maximum(m, cmax) if cur_offs else m
    pv = None
    cnext = None
    for i in range(max(len(next_offs), len(cur_offs))):
        rows = slice(i * PV_TILE, (i + 1) * PV_TILE)
        if i < len(next_offs):
            s = jnp.dot(k_ref[pl.ds(_aligned(next_offs[i]), PV_TILE), :], q_next, preferred_element_type=F32)
            s_next[rows, :] = s
            cm = jnp.max(s, axis=0, keepdims=True)
            cnext = cm if cnext is None else jnp.maximum(cnext, cm)
        if i < len(cur_offs):
            p = jnp.exp2(s_cur[rows, :] - m_new).astype(BF16)
            d = jnp.dot(vT_ref[:, pl.ds(_aligned(cur_offs[i]), PV_TILE)], p, preferred_element_type=F32)
            pv = d if pv is None else pv + d
    if cur_offs:
        acc_ref[...] = acc_ref[...] * jnp.exp2(m - m_new) + pv
    return m_new, cnext


def _softmax_init(init_ref, dva, dv, tq):
    m0 = jnp.broadcast_to(init_ref[0, 0:1, 0:1], (1, tq))
    row = lax.broadcasted_iota(jnp.int32, (dva, tq), 0)
    acc0 = jnp.where(row == dv, jnp.broadcast_to(init_ref[0, 1:2, 0:1], (dva, tq)), 0.0)
    return m0, acc0


def _flash_kernel(init_ref, qT_ref, qTn_ref, k_ref, vT_ref, o_ref, s0_ref, s1_ref, acc_ref, cm_ref, *,
                  T, n_ctx, kt, dv, pack):
    head = pl.program_id(1)
    qT = _pad_queries(qT_ref[...], pack, head)
    tq = qT.shape[1]
    dva = vT_ref.shape[0]
    pieces = lambda off: [off + kk for kk in range(0, kt, PV_TILE)]
    first = [T + kk for kk in range(0, n_ctx, PV_TILE)] + pieces(0)
    step = functools.partial(_attn_step, k_ref, vT_ref, acc_ref)

    @pl.when(pl.program_id(2) == 0)
    def _():
        _, cm_ref[0:1, :] = step(qT, first, s0_ref, [], None, None, None)

    m, acc0 = _softmax_init(init_ref, dva, dv, tq)
    acc_ref[...] = acc0
    m, cm_b = step(qT, pieces(kt), s1_ref, first, s0_ref, m, cm_ref[0:1, :])

    def body(t, carry):
        m, cm_b = carry
        off = pl.multiple_of(t * (2 * kt), kt)
        m, cm_a = step(qT, pieces(off + 2 * kt), s0_ref, pieces(off + kt), s1_ref, m, cm_b)
        m, cm_b = step(qT, pieces(off + 3 * kt), s1_ref, pieces(off + 2 * kt), s0_ref, m, cm_a)
        return m, cm_b

    m, cm_b = lax.fori_loop(0, T // (2 * kt) - 1, body, (m, cm_b))
    qn = _pad_queries(qTn_ref[...], pack, head)
    m, cm_ref[0:1, :] = step(qn, first, s0_ref, pieces(T - kt), s1_ref, m, cm_b)
    acc = acc_ref[...]
    o_ref[...] = acc[:dv] / acc[dv:dv + 1]


def _flash(qT, k, vT, init, T, pack):
    B, Hq, dk, S = qT.shape
    Hv, dva = vT.shape[1], vT.shape[2]
    dv = dva - ONES_ROWS
    gv = Hq // Hv
    kt = min(KEY_TILE, T // 2)
    tq = min(Q_TILE, T)
    assert T % (2 * kt) == 0 and T % tq == 0 and kt % PV_TILE == 0
    n_ctx = S - T
    assert n_ctx % PV_TILE == 0
    kern = functools.partial(_flash_kernel, T=T, n_ctx=n_ctx, kt=kt, dv=dv, pack=pack)
    nq = T // tq
    return pl.pallas_call(
        kern,
        grid=(B, Hq, nq),
        in_specs=[pl.BlockSpec((1, 2, LANES),lambda b, h, i: (h, 0, 0)),
                  pl.BlockSpec((None, None, dk, tq), lambda b, h, i: (b, h, 0, i)),
                  pl.BlockSpec((None, None, dk, tq), lambda b, h, i: (b, h, 0, jnp.minimum(i + 1, nq - 1))),
                  pl.BlockSpec((None, None, S, pack.kw), lambda b, h, i: (b, pack.block_of(h), 0, 0)),
                  pl.BlockSpec((None, None, dva, S), lambda b, h, i: (b, h // gv, 0, 0))],
        out_specs=pl.BlockSpec((None, None, dv, tq), lambda b, h, i: (b, h, 0, i)),
        out_shape=jax.ShapeDtypeStruct((B, Hq, dv, T), F32),
        scratch_shapes=[pltpu.VMEM((kt + n_ctx, tq), F32), pltpu.VMEM((kt, tq), F32),
                        pltpu.VMEM((dva, tq), F32), pltpu.VMEM((8, tq), F32)],
        compiler_params=_cparams(3),
        name="dense_attention",
    )(init, qT, qT, k, vT)


def _ctx_attn_kernel(init_ref, qT_ref, k_ref, vT_ref, o_ref, *, dv, pack):
    qT = _pad_queries(qT_ref[...], pack, pl.program_id(1))
    m0, acc0 = _softmax_init(init_ref, vT_ref.shape[0], dv, qT.shape[1])
    s = jnp.dot(k_ref[...], qT, preferred_element_type=F32)
    m = jnp.maximum(m0, jnp.max(s, axis=0, keepdims=True))
    acc = acc0 * jnp.exp2(m0 - m) + jnp.dot(vT_ref[...], jnp.exp2(s - m).astype(BF16), preferred_element_type=F32)
    o_ref[...] = acc[:dv] / acc[dv:dv + 1]


def _ctx_attn(qT, k, vT, init, T, pack):
    B, Hq, dk, S = qT.shape
    Hv, dva = vT.shape[1], vT.shape[2]
    dv = dva - ONES_ROWS
    gv = Hq // Hv
    n = S - T
    blk = T // n
    return pl.pallas_call(
        functools.partial(_ctx_attn_kernel, dv=dv, pack=pack),
        grid=(B, Hq),
        in_specs=[pl.BlockSpec((1, 2, LANES),lambda b, h: (h, 0, 0)),
                  pl.BlockSpec((None, None, dk, n), lambda b, h: (b, h, 0, blk)),
                  pl.BlockSpec((None, None, n, pack.kw), lambda b, h: (b, pack.block_of(h), blk, 0)),
                  pl.BlockSpec((None, None, dva, n), lambda b, h: (b, h // gv, 0, blk))],
        out_specs=pl.BlockSpec((None, None, dv, n), lambda b, h: (b, h, 0, 0)),
        out_shape=jax.ShapeDtypeStruct((B, Hq, dv, n), F32),
        compiler_params=_cparams(2),
        name="context_attention",
    )(init, qT, k, vT)


def _window_kernel(sink_ref, bias_ref, qT_ref, k_ref, kc_ref, vT_ref, o_ref, *, dv, pack):
    i = pl.program_id(2)
    G, _, tq = qT_ref.shape
    head = pl.program_id(1) * G
    qT = jnp.concatenate([_pad_queries(qT_ref[g], pack, head) for g in range(G)], axis=1)
    snk = jnp.concatenate([jnp.broadcast_to(sink_ref[g, 0:1, 0:1], (1, tq)) for g in range(G)], axis=1)
    W = WINDOW
    kband = jnp.concatenate([k_ref[0][tq - W:tq, :], k_ref[1][...], k_ref[2][0:W, :]], axis=0)
    before_first = jnp.where(i == 0, NEG_INF, 0.0)
    after_last = jnp.where(i == pl.num_programs(2) - 1, NEG_INF, 0.0)
    bias = jnp.concatenate([bias_ref[0:W, :] + before_first, bias_ref[W:W + tq, :],
                            bias_ref[W + tq:, :] + after_last], axis=0)
    s_loc = jnp.dot(kband, qT, preferred_element_type=F32) + bias
    s_ctx = jnp.dot(kc_ref[...], qT, preferred_element_type=F32)
    m = jnp.maximum(jnp.maximum(jnp.max(s_loc, axis=0, keepdims=True), jnp.max(s_ctx, axis=0, keepdims=True)), snk)
    p_loc = jnp.exp2(s_loc - m).astype(BF16)
    p_ctx = jnp.exp2(s_ctx - m).astype(BF16)
    vband = jnp.concatenate([vT_ref[0][:, tq - W:tq], vT_ref[1][...], vT_ref[2][:, 0:W]], axis=1)
    acc = (jnp.dot(vband, p_loc, preferred_element_type=F32)
           + jnp.dot(vT_ref[3][...], p_ctx, preferred_element_type=F32))
    o = acc[:dv] / (acc[dv:dv + 1] + jnp.exp2(snk - m))
    for g in range(G):
        o_ref[g] = o[:, g * tq:(g + 1) * tq]


def _window(qT, k, vT, sink, T, pack):
    B, Hq, dk, S = qT.shape
    Hk, dva = vT.shape[1], vT.shape[2]
    dv = dva - ONES_ROWS
    g = Hq // Hk
    tq = TOK_TILE
    nt = T // tq
    ctx_blk = T // tq
    clip = lambda j: jnp.clip(j, 0, nt - 1)
    kern = functools.partial(_window_kernel, dv=dv, pack=pack)
    kspec = lambda f: pl.BlockSpec((None, None, tq, pack.kw), lambda b, h, i: (b, pack.block_of(h * g), f(i), 0))
    vspec = lambda f: pl.BlockSpec((None, None, dva, tq), lambda b, h, i: (b, h, 0, f(i)))
    r = jnp.arange(tq + 2 * WINDOW)[:, None]
    c = jnp.arange(g * tq)[None, :] % tq
    bias = jnp.where(jnp.abs(r - WINDOW - c) <= WINDOW, 0.0, NEG_INF).astype(F32)

    def body(sink_ref, bias_ref, qT_ref, k0, k1, k2, kc, v0, v1, v2, vc, o_ref):
        kern(sink_ref, bias_ref, qT_ref, (k0, k1, k2), kc, (v0, v1, v2, vc), o_ref)

    return pl.pallas_call(
        body,
        grid=(B, Hk, nt),
        in_specs=[pl.BlockSpec((g, 1, LANES), lambda b, h, i: (h, 0, 0)),
                  _full(bias.shape),
                  pl.BlockSpec((None, g, dk, tq), lambda b, h, i: (b, h, 0, i)),
                  kspec(lambda i: clip(i - 1)), kspec(lambda i: i), kspec(lambda i: clip(i + 1)),
                  kspec(lambda i: ctx_blk),
                  vspec(lambda i: clip(i - 1)), vspec(lambda i: i), vspec(lambda i: clip(i + 1)),
                  vspec(lambda i: ctx_blk)],
        out_specs=pl.BlockSpec((None, g, dv, tq), lambda b, h, i: (b, h, 0, i)),
        out_shape=jax.ShapeDtypeStruct((B, Hq, dv, T), F32),
        compiler_params=_cparams(3),
        name="window_attention",
    )(sink, bias, qT, k, k, k, k, vT, vT, vT, vT)


def _project_out(x_ref, oT, w_ref, gate_ref, y_ref):
    y = jnp.dot(oT.T.astype(BF16), w_ref[...], preferred_element_type=F32)
    y_ref[0] = x_ref[0] + gate_ref[0, 0] * y


def _even_out_kernel(x_ref, oa_ref, od_ref, w_ref, gate_ref, lam_ref, sub_ref, y_ref, *, post_scale):
    lam = lam_ref[...]
    parts = [oa_ref[hd] for hd in range(MLA_HEADS)]
    for hd in range(DIFF_HEADS):
        diff = od_ref[2 * hd] - lam * od_ref[2 * hd + 1]
        parts.append(diff * _rms_rows(diff) * sub_ref[...] * post_scale)
    _project_out(x_ref, jnp.concatenate(parts, axis=0), w_ref, gate_ref, y_ref)


def _odd_out_kernel(x_ref, oc_ref, od_ref, w_ref, gate_ref, y_ref):
    parts = [oc_ref[hd] for hd in range(WIN_HEADS)] + [od_ref[hd] for hd in range(GLB_HEADS)]
    _project_out(x_ref, jnp.concatenate(parts, axis=0), w_ref, gate_ref, y_ref)


def _out_proj(kern, x, heads, w, gate, extra, tile0, n_tiles, kind):
    B, S, D = x.shape
    row_map = lambda b, i: (b, i + tile0, 0)
    return pl.pallas_call(
        kern,
        grid=(B, n_tiles),
        in_specs=[pl.BlockSpec((1, TOK_TILE, D), row_map)]
                 + [pl.BlockSpec((None,) + o.shape[1:3] + (TOK_TILE,), lambda b, i: (b, 0, 0, i)) for o in heads]
                 + [_full(w.shape), pl.BlockSpec((1, 1, 1, D), lambda b, i: (b, kind, 0, 0))]
                 + [_full(e.shape) for e in extra],
        out_specs=pl.BlockSpec((1, TOK_TILE, D), row_map),
        out_shape=jax.ShapeDtypeStruct((B, S, D), F32),
        input_output_aliases={0: 0},
        compiler_params=_cparams(2),
        name="merge_out_proj_residual",
    )(x, *heads, w, gate, *extra)


def _first_argmax(vals):
    best, idx = vals[0], jnp.zeros(vals[0].shape, jnp.int32)
    for j in range(1, len(vals)):
        better = vals[j] > best
        idx = jnp.where(better, j, idx)
        best = jnp.where(better, vals[j], best)
    return idx, best


def _pick(idx, vals):
    out = vals[0]
    for j in range(1, len(vals)):
        out = jnp.where(idx == j, vals[j], out)
    return out


def _route(logits, bias):
    s = 1.0 / (1.0 + jnp.exp(-logits))
    sel = s + bias
    srow = [s[e:e + 1] for e in range(N_EXPERTS)]
    row = [sel[e:e + 1] for e in range(N_EXPERTS)]
    scores = []
    for g in range(N_GROUPS):
        a, b, c, d = row[4 * g:4 * g + 4]
        hi1, lo1, hi2, lo2 = jnp.maximum(a, b), jnp.minimum(a, b), jnp.maximum(c, d), jnp.minimum(c, d)
        top1 = jnp.maximum(hi1, hi2)
        top2 = jnp.maximum(jnp.maximum(lo1, lo2), jnp.minimum(hi1, hi2))
        scores.append(top1 + top2)
    gi, _ = _first_argmax(scores)
    v = [_pick(gi, [row[4 * g + j] for g in range(N_GROUPS)]) for j in range(EXPERTS_PER_GROUP)]
    sv = [_pick(gi, [srow[4 * g + j] for g in range(N_GROUPS)]) for j in range(EXPERTS_PER_GROUP)]
    i1, _ = _first_argmax(v)
    i2, _ = _first_argmax([jnp.where(i1 == j, -jnp.inf, v[j]) for j in range(EXPERTS_PER_GROUP)])
    w1, w2 = _pick(i1, sv), _pick(i2, sv)
    tot = w1 + w2
    w1, w2 = w1 / tot, w2 / tot
    rows = []
    for e in range(N_EXPERTS):
        g, j = divmod(e, EXPERTS_PER_GROUP)
        in_g = gi == g
        rows.append(jnp.where(in_g & (i1 == j), w1, 0.0) + jnp.where(in_g & (i2 == j), w2, 0.0))
    return jnp.concatenate(rows, axis=0), gi


def _split(x, terms):
    out = []
    for _ in range(terms):
        part = x.astype(BF16).astype(F32)
        out.append(part)
        x = x - part
    return out


def _to_column(row):
    n = row.shape[1]
    return jnp.concatenate([row, jnp.zeros((LANES - 1, n), F32)], axis=0).T[:, 0:1]


def _moe_kernel(x_ref, g_ref, sh_ref, sc_ref, gate_ref, wrT_ref, br_ref, before_ref, wg_ref, wu_ref, wd_ref, y_ref,
                hs_scr, cs_scr, ys_scr, q_scr, info_ref):
    step = pl.program_id(2)
    per_step = wg_ref.shape[0]
    tm = x_ref.shape[1]
    R = hs_scr.shape[0]

    @pl.when(step == 0)
    def _():
        h = _norm_mod(x_ref[0], g_ref[...], sh_ref[0, 0], sc_ref[0, 0]).astype(BF16)
        comb, gi = _route(_dot_t(wrT_ref[...], h), br_ref[...])
        sel = jnp.concatenate([(gi == g).astype(F32) for g in range(N_GROUPS)]
                              + [jnp.zeros((BF16_ROWS - N_GROUPS, tm), F32)], axis=0)
        rank = jnp.dot(sel.astype(BF16), before_ref[...], preferred_element_type=F32)
        cnt = jnp.sum(sel, axis=1, keepdims=True)
        seg = jnp.ceil(cnt * (1.0 / BF16_ROWS)) * BF16_ROWS
        pos = jnp.zeros((1, tm), F32)
        start = jnp.zeros((1, 1), F32)
        for g in range(N_GROUPS):
            pos = pos + sel[g:g + 1] * (start + rank[g:g + 1])
            info_ref[g] = start[0, 0].astype(jnp.int32)
            info_ref[N_GROUPS + g] = jnp.ceil(cnt[g:g + 1] * (1.0 / MOE_CHUNK))[0, 0].astype(jnp.int32)
            start = start + seg[g:g + 1]
        slot = lax.broadcasted_iota(jnp.int32, (R, tm), 0).astype(F32)
        P = jnp.where(slot == pos, 1.0, 0.0).astype(BF16)
        hs_scr[...] = jnp.dot(P, h, preferred_element_type=F32).astype(BF16)
        parts = _split(comb, COMB_TERMS)
        combT = jnp.concatenate(parts + [jnp.zeros((LANES - COMB_TERMS * N_EXPERTS, tm), F32)], axis=0).T
        cs_scr[...] = jnp.dot(P, combT.astype(BF16), preferred_element_type=F32)
        lane = lax.broadcasted_iota(jnp.int32, (tm, R), 1).astype(F32)
        q_scr[...] = jnp.where(lane == _to_column(pos), 1.0, 0.0).astype(BF16)
        ys_scr[...] = jnp.zeros_like(ys_scr)

    steps_per_group = EXPERTS_PER_GROUP // per_step
    grp = lax.shift_right_logical(step, steps_per_group.bit_length() - 1)
    start = info_ref[grp]
    n_chunks = info_ref[N_GROUPS + grp]

    def experts(r0, rows):
        h = hs_scr[pl.ds(r0, rows), :]
        cs = cs_scr[pl.ds(r0, rows), :]
        lane = lax.broadcasted_iota(jnp.int32, (rows, LANES), 1) & (N_EXPERTS - 1)
        total = None
        for j in range(per_step):
            a = jnp.dot(h, wg_ref[j], preferred_element_type=F32)
            u = jnp.dot(h, wu_ref[j], preferred_element_type=F32)
            act = (a * (1.0 / (1.0 + jnp.exp(-a)))) * u
            y = jnp.dot(act.astype(BF16), wd_ref[j], preferred_element_type=F32)
            w = jnp.sum(jnp.where(lane == step * per_step + j, cs, 0.0), axis=1, keepdims=True)
            total = w * y if total is None else total + w * y
        ys_scr[pl.ds(r0, rows), :] += total

    def pair(c, carry):
        experts(pl.multiple_of(start + c * (2 * MOE_CHUNK), BF16_ROWS), 2 * MOE_CHUNK)
        return carry

    lax.fori_loop(0, lax.shift_right_logical(n_chunks, 1), pair, 0)

    @pl.when((n_chunks & 1) == 1)
    def _():
        experts(pl.multiple_of(start + (n_chunks - 1) * MOE_CHUNK, BF16_ROWS), MOE_CHUNK)

    @pl.when(step == pl.num_programs(2) - 1)
    def _():
        q = q_scr[...]
        out = sum(jnp.dot(q, part.astype(BF16), preferred_element_type=F32) for part in _split(ys_scr[...], 2))
        y_ref[0] = x_ref[0] + gate_ref[0, 0] * out


def _moe(x, g, sh, sc, gate, wrT, br, wg, wu, wd, tm, tile0, n_tiles, kind):
    B, S, D = x.shape
    E, _, F = wg.shape
    R = -(-(tm + BF16_ROWS * N_GROUPS + MOE_CHUNK) // MXU_DIM) * MXU_DIM
    idx = jnp.arange(tm)
    before = (idx[:, None] < idx[None, :]).astype(BF16)
    mod_map = lambda b, i, e: (b, kind, 0, 0)
    row_map = lambda b, i, e: (b, i + tile0, 0)
    return pl.pallas_call(
        _moe_kernel,
        grid=(B, n_tiles, E // MOE_STEP_EXPERTS),
        in_specs=[pl.BlockSpec((1, tm, D), row_map),
                  pl.BlockSpec((1, D), lambda b, i, e: (0, 0)),
                  pl.BlockSpec((1, 1, 1, D), mod_map),
                  pl.BlockSpec((1, 1, 1, D), mod_map),
                  pl.BlockSpec((1, 1, 1, D), mod_map),
                  pl.BlockSpec((E, D), lambda b, i, e: (0, 0)),
                  pl.BlockSpec((E, 1), lambda b, i, e: (0, 0)),
                  pl.BlockSpec((tm, tm), lambda b, i, e: (0, 0)),
                  pl.BlockSpec((MOE_STEP_EXPERTS, D, F), lambda b, i, e: (e, 0, 0)),
                  pl.BlockSpec((MOE_STEP_EXPERTS, D, F), lambda b, i, e: (e, 0, 0)),
                  pl.BlockSpec((MOE_STEP_EXPERTS, F, D), lambda b, i, e: (e, 0, 0))],
        out_specs=pl.BlockSpec((1, tm, D), row_map),
        out_shape=jax.ShapeDtypeStruct((B, S, D), F32),
        scratch_shapes=[pltpu.VMEM((R, D), BF16), pltpu.VMEM((R, LANES), F32), pltpu.VMEM((R, D), F32),
                        pltpu.VMEM((tm, R), BF16), pltpu.SMEM((2 * N_GROUPS,), jnp.int32)],
        input_output_aliases={0: 0},
        compiler_params=_cparams(3),
        name="moe_experts",
    )(x, g.reshape(1, D).astype(F32), sh, sc, gate, wrT, br, before, wg, wu, wd)


def _final_norm_kernel(x_ref, g_ref, o_ref):
    x = x_ref[0]
    o_ref[0] = (x * lax.rsqrt(jnp.mean(x * x, axis=-1, keepdims=True) + NORM_EPS)) * g_ref[...]


def _final_norm(x, g, T):
    B, S, D = x.shape
    return pl.pallas_call(
        _final_norm_kernel,
        grid=(B, T // TOK_TILE),
        in_specs=[pl.BlockSpec((1, TOK_TILE, D), lambda b, i: (b, i, 0)),
                  pl.BlockSpec((1, D), lambda b, i: (0, 0))],
        out_specs=pl.BlockSpec((1, TOK_TILE, D), lambda b, i: (b, i, 0)),
        out_shape=jax.ShapeDtypeStruct((B, T, D), F32),
        compiler_params=_cparams(2),
        name="final_norm",
    )(x, g.reshape(1, D).astype(F32))


def _axial_tables(T, n_ctx, rot_dim):
    half = rot_dim // 2
    inv_freq = ROPE_THETA ** (-jnp.arange(0, half, 2, dtype=F32) / half)
    rows = T // GRID_W
    row = jnp.broadcast_to(jnp.arange(rows, dtype=F32)[:, None], (rows, GRID_W)).reshape(-1)
    col = jnp.broadcast_to(jnp.arange(GRID_W, dtype=F32)[None, :], (rows, GRID_W)).reshape(-1)

    def ang(pos):
        a = pos[:, None] * inv_freq[None, :]
        return jnp.concatenate([a, a], axis=-1)

    a = jnp.concatenate([ang(row), ang(col)], axis=-1)
    cos = jnp.concatenate([jnp.cos(a), jnp.ones((n_ctx, rot_dim), F32)], axis=0)
    sin = jnp.concatenate([jnp.sin(a), jnp.zeros((n_ctx, rot_dim), F32)], axis=0)
    reps = LANES // rot_dim
    return cos.T, sin.T, jnp.tile(cos, (1, reps)), jnp.tile(sin, (1, reps))


def _plain_init(n_heads):
    return jnp.broadcast_to(jnp.array([NEG_INF, 0.0], F32)[None, :, None], (n_heads, 2, LANES))


def _sink_init(sink):
    s = sink.astype(F32) * LOG2E
    return jnp.broadcast_to(jnp.stack([s, jnp.ones_like(s)], axis=1)[:, :, None], (s.shape[0], 2, LANES))


def _even_mixer(xa, p, mods, T, with_ctx, lam_init, tabs):
    B, S, D = xa.shape
    qa, ka, va, qd, kd, vd = _even_prep(xa, p, mods, tabs)
    pack_a = _KeyPack(MLA_PAD, MLA_PAD, lambda h: h, lambda h: 0)
    per = DIFF_KW // DIFF_D
    pack_d = _KeyPack(DIFF_KW, DIFF_D, lambda h: h // per, lambda h: h % per)
    init_a, init_d = _plain_init(MLA_HEADS), _plain_init(2 * DIFF_HEADS)
    lam = p["lam"].astype(F32)
    lam_full = (jnp.exp(jnp.sum(lam[0] * lam[1])) - jnp.exp(jnp.sum(lam[2] * lam[3])) + lam_init).reshape(1, 1)
    sub = jnp.broadcast_to(p["subln"].astype(F32)[:, None], (DIFF_V, TOK_TILE))
    kern = functools.partial(_even_out_kernel, post_scale=1.0 - lam_init)
    oa = _flash(qa, ka, va, init_a, T, pack_a)
    od = _flash(qd, kd, vd, init_d, T, pack_d)
    xa = _out_proj(kern, xa, [oa, od], p["w_out"], mods["g_a"], [lam_full, sub], 0, T // TOK_TILE, 0)
    if with_ctx:
        oa = _ctx_attn(qa, ka, va, init_a, T, pack_a)
        od = _ctx_attn(qd, kd, vd, init_d, T, pack_d)
        xa = _out_proj(kern, xa, [oa, od], p["w_out"], mods["g_a"], [lam_full, sub], T // TOK_TILE, 1, 1)
    return xa


def _odd_mixer(xa, p, mods, T, with_ctx, tabs):
    B, S, D = xa.shape
    qc, qd, k, vc, vd = _odd_prep(xa, p, mods, tabs)
    gw, gd = WIN_HEADS // WIN_KV_HEADS, GLB_HEADS // GLB_KV_HEADS
    pack_c = _KeyPack(LANES, HEAD_DIM, lambda h: 0, lambda h: h // gw)
    pack_d = _KeyPack(LANES, HEAD_DIM, lambda h: 1, lambda h: h // gd)
    init_d = _plain_init(GLB_HEADS)
    sink = p["sink"].astype(F32) * LOG2E
    sink_w = jnp.broadcast_to(sink[:, None, None], (WIN_HEADS, 1, LANES))
    od = _flash(qd, k, vd, init_d, T, pack_d)
    oc = _window(qc, k, vc, sink_w, T, pack_c)
    xa = _out_proj(_odd_out_kernel, xa, [oc, od], p["w_out"], mods["g_a"], [], 0, T // TOK_TILE, 0)
    if with_ctx:
        oc = _ctx_attn(qc, k, vc, _sink_init(p["sink"]), T, pack_c)
        od = _ctx_attn(qd, k, vd, init_d, T, pack_d)
        xa = _out_proj(_odd_out_kernel, xa, [oc, od], p["w_out"], mods["g_a"], [], T // TOK_TILE, 1, 1)
    return xa


def kernel(x, c, ctx, c_ctx, w_mod, b_mod, norm_mix, norm_ffn, even_w_in, even_norm_q, even_norm_kv, even_w_uq, even_w_ukv, even_lambda, even_subln, even_w_out, odd_w_in, odd_sink, odd_q_norm, odd_k_norm, odd_w_out, w_router, b_router, w_gate, w_up, w_down, norm_final):
    B, T, D = x.shape
    n_ctx = ctx.shape[1]
    depth = w_mod.shape[0]
    assert n_ctx == TOK_TILE == PV_TILE and T % MOE_TILE == 0 and B <= 7
    assert MLA_ROPE == DIFF_D
    S = T + n_ctx
    tabs32 = _axial_tables(T, n_ctx, MLA_ROPE)
    tabs64 = _axial_tables(T, n_ctx, HEAD_DIM)

    cond = jnp.zeros((8, D), F32).at[:B].set(c.astype(F32)).at[B].set(c_ctx.astype(F32))
    mod_all = _mod_vectors(cond, w_mod, b_mod)
    wrT = w_router.astype(BF16).T
    br = b_router.astype(F32).reshape(N_EXPERTS, 1)

    xa = jnp.concatenate([x, ctx], axis=1).astype(F32)
    for l in range(depth):
        with_ctx = l < depth - 1
        i = l // 2
        lat = mod_all[l, :B].reshape(B, 6, D)
        cx = jnp.broadcast_to(mod_all[l, B].reshape(1, 6, D), (B, 6, D))
        both = jnp.stack([lat, cx], axis=1)
        names = ("sh_a", "sc_a", "g_a", "sh_f", "sc_f", "g_f")
        mods = {n: both[:, :, j:j + 1, :] for j, n in enumerate(names)}
        if l % 2 == 0:
            lam_init = 0.8 - 0.6 * math.exp(-0.3 * l)
            p = dict(norm_mix=norm_mix[l], w_in=even_w_in[i], norm_q=even_norm_q[i], norm_kv=even_norm_kv[i],
                     w_uq=even_w_uq[i], w_ukv=even_w_ukv[i], lam=even_lambda[i], subln=even_subln[i],
                     w_out=even_w_out[i].astype(BF16))
            xa = _even_mixer(xa, p, mods, T, with_ctx, lam_init, tabs32)
        else:
            p = dict(norm_mix=norm_mix[l], w_in=odd_w_in[i], sink=odd_sink[i], q_norm=odd_q_norm[i],
                     k_norm=odd_k_norm[i], w_out=odd_w_out[i].astype(BF16))
            xa = _odd_mixer(xa, p, mods, T, with_ctx, tabs64)
        wg, wu, wd = w_gate[l].astype(BF16), w_up[l].astype(BF16), w_down[l].astype(BF16)
        ffn = (norm_ffn[l], mods["sh_f"], mods["sc_f"], mods["g_f"], wrT, br, wg, wu, wd)
        xa = _moe(xa, *ffn, MOE_TILE, 0, T // MOE_TILE, 0)
        if with_ctx:
            xa = _moe(xa, *ffn, TOK_TILE, T // TOK_TILE, 1, 1)
    return _final_norm(xa, norm_final, T)
```

```python
import functools
import math

import jax
import jax.numpy as jnp
from jax import lax
from jax.experimental import pallas as pl
from jax.experimental.pallas import tpu as pltpu

F32 = jnp.float32
BF16 = jnp.bfloat16
LOG2E = 1.4426950408889634

GRID_W = 64
ROPE_THETA = 10000.0
NORM_EPS = 1e-6
NEG_INF = -1e30

MLA_HEADS = 8
MLA_Q_LORA = 384
MLA_KV_LORA = 256
MLA_NOPE = 64
MLA_ROPE = 32
MLA_V = 64
MLA_SCALE = (MLA_NOPE + MLA_ROPE) ** -0.5
MLA_IN_COLS = MLA_Q_LORA + MLA_KV_LORA + MLA_ROPE
MLA_QK = MLA_NOPE + MLA_ROPE
MLA_PAD = 128

DIFF_HEADS = 8
DIFF_D = 32
DIFF_V = 2 * DIFF_D
DIFF_SCALE = DIFF_D ** -0.5

HEAD_DIM = 64
WIN_HEADS = 8
WIN_KV_HEADS = 2
WINDOW = 128
GLB_HEADS = 8
GLB_KV_HEADS = 2
ATTN_SCALE = HEAD_DIM ** -0.5

N_EXPERTS = 16
N_GROUPS = 4
EXPERTS_PER_GROUP = N_EXPERTS // N_GROUPS

TOK_TILE = 256
Q_TILE = 1024
KEY_TILE = 2048
PV_TILE = 256
MOE_TILE = 1024
MOE_CHUNK = 128
MOE_STEP_EXPERTS = 2
COMB_TERMS = 3
ONES_ROWS = 16
LANES = 128
BF16_ROWS = 16
MXU_DIM = 256
VMEM_LIMIT = 56 * 1024 * 1024


def _cparams(n_axes):
    return pltpu.CompilerParams(dimension_semantics=("arbitrary",) * n_axes,
                                vmem_limit_bytes=VMEM_LIMIT)


def _full(shape):
    return pl.BlockSpec(shape, lambda *_: (0,) * len(shape))


def _mod_kernel(a_ref, w_ref, b_ref, o_ref):
    a = a_ref[...]
    a = a * (1.0 / (1.0 + jnp.exp(-a)))
    o_ref[0] = jnp.dot(a.astype(BF16), w_ref[0].astype(BF16), preferred_element_type=F32) + b_ref[0]


def _mod_vectors(cond, w_mod, b_mod):
    L, D, N = w_mod.shape
    tn = 1536
    return pl.pallas_call(
        _mod_kernel,
        grid=(L, N // tn),
        in_specs=[pl.BlockSpec((8, D), lambda l, j: (0, 0)),
                  pl.BlockSpec((1, D, tn), lambda l, j: (l, 0, j)),
                  pl.BlockSpec((1, 1, tn), lambda l, j: (l, 0, j))],
        out_specs=pl.BlockSpec((1, 8, tn), lambda l, j: (l, 0, j)),
        out_shape=jax.ShapeDtypeStruct((L, 8, N), F32),
        compiler_params=_cparams(2),
        name="mod_vectors",
    )(cond, w_mod, b_mod.reshape(L, 1, N))


def _norm_mod(x, g, sh, sc):
    y = x * lax.rsqrt(jnp.mean(x * x, axis=-1, keepdims=True) + NORM_EPS)
    return (y * g) * (1.0 + sc) + sh


def _dot_t(w, h):
    return lax.dot_general(w, h, (((1,), (1,)), ((), ())), preferred_element_type=F32)


def _rms_rows(x):
    return lax.rsqrt(jnp.mean(x * x, axis=0, keepdims=True) + NORM_EPS)


def _rope_rows(x, cosT, sinT):
    q = x.shape[0] // 4
    rot = jnp.concatenate([-x[q:2 * q], x[0:q], -x[3 * q:4 * q], x[2 * q:3 * q]], axis=0)
    return x * cosT + rot * sinT


def _ones_rows(n):
    row = lax.broadcasted_iota(jnp.int32, (ONES_ROWS, n), 0)
    return jnp.where(row == 0, 1.0, 0.0).astype(BF16)


def _tile_lanes(x, reps):
    return x if reps == 1 else jnp.concatenate([x] * reps, axis=1)


def _even_prep_kernel(x_ref, g_ref, sh_ref, sc_ref, w1T_ref, wckv_ref, wkr_ref, wkrr_ref, wdk_ref, wdkr_ref,
                      wuqT_ref, wuvT_ref, wuk_ref, place_ref, gq_ref, gkvc_ref, gkvr_ref, cosT_ref, sinT_ref,
                      cosK_ref, sinK_ref,
                      qa_ref, ka_ref, va_ref, qd_ref, kd_ref, vd_ref):
    h = _norm_mod(x_ref[0], g_ref[...], sh_ref[0, 0], sc_ref[0, 0]).astype(BF16)
    n = h.shape[0]
    cosT, sinT, cosK, sinK = cosT_ref[...], sinT_ref[...], cosK_ref[...], sinK_ref[...]
    ones = _ones_rows(n)
    zT = _dot_t(w1T_ref[...], h)
    o1, o2 = MLA_Q_LORA, MLA_Q_LORA + MLA_KV_LORA
    o3 = o2 + 2 * DIFF_HEADS * DIFF_D

    c_q = zT[:o1]
    c_q = (c_q * _rms_rows(c_q) * gq_ref[...]).astype(BF16)
    qT = jnp.dot(wuqT_ref[...], c_q, preferred_element_type=F32)
    for hd in range(MLA_HEADS):
        r0 = hd * MLA_QK
        rope = _rope_rows(qT[r0 + MLA_NOPE:r0 + MLA_QK], cosT, sinT)
        head = jnp.concatenate([qT[r0:r0 + MLA_NOPE], rope], axis=0)
        qa_ref[hd] = (head * (MLA_SCALE * LOG2E)).astype(BF16)

    c_kvT = zT[o1:o2]
    c_kvT = (c_kvT * _rms_rows(c_kvT) * gkvc_ref[...]).astype(BF16)
    vT = jnp.dot(wuvT_ref[...], c_kvT, preferred_element_type=F32).astype(BF16)
    for hd in range(MLA_HEADS):
        va_ref[hd, 0:MLA_V, :] = vT[hd * MLA_V:(hd + 1) * MLA_V]
        va_ref[hd, MLA_V:, :] = ones

    for j in range(2 * DIFF_HEADS):
        qj = _rope_rows(zT[o2 + j * DIFF_D:o2 + (j + 1) * DIFF_D], cosT, sinT)
        qd_ref[j] = (qj * (DIFF_SCALE * LOG2E)).astype(BF16)
    for hd in range(DIFF_HEADS):
        vd_ref[hd, 0:DIFF_V, :] = zT[o3 + hd * DIFF_V:o3 + (hd + 1) * DIFF_V].astype(BF16)
        vd_ref[hd, DIFF_V:, :] = ones

    c_kv = jnp.dot(h, wckv_ref[...], preferred_element_type=F32)
    c_kv = (c_kv * lax.rsqrt(jnp.mean(c_kv * c_kv, axis=-1, keepdims=True) + NORM_EPS) * gkvr_ref[...]).astype(BF16)
    kr = (jnp.dot(h, wkr_ref[...], preferred_element_type=F32) * cosK
          + jnp.dot(h, wkrr_ref[...], preferred_element_type=F32) * sinK)
    ka = (jnp.dot(c_kv, wuk_ref[...], preferred_element_type=F32)
          + jnp.dot(kr.astype(BF16), place_ref[...], preferred_element_type=F32))
    ka = ka.astype(BF16)
    for hd in range(MLA_HEADS):
        ka_ref[hd] = ka[:, hd * MLA_PAD:hd * MLA_PAD + MLA_QK]

    reps = wdk_ref.shape[1] // LANES
    kd = (jnp.dot(h, wdk_ref[...], preferred_element_type=F32) * _tile_lanes(cosK, reps)
          + jnp.dot(h, wdkr_ref[...], preferred_element_type=F32) * _tile_lanes(sinK, reps))
    kd = kd.astype(BF16)
    for blk in range(kd_ref.shape[0]):
        kd_ref[blk] = kd[:, blk * DIFF_D:(blk + 1) * DIFF_D]


def _rot_cols(w, width):
    d, n = w.shape
    w4 = w.reshape(d, n // width, 4, width // 4)
    return jnp.stack([-w4[:, :, 1], w4[:, :, 0], -w4[:, :, 3], w4[:, :, 2]], axis=2).reshape(d, n)


def _perm_cols(g, width):
    g4 = g.reshape(-1, 4, width // 4)
    return jnp.stack([g4[:, 1], g4[:, 0], g4[:, 3], g4[:, 2]], axis=1).reshape(-1)


def _even_prep(xa, p, mods, tabs):
    B, S, D = xa.shape
    w_in, w_uq, w_ukv = p["w_in"], p["w_uq"], p["w_ukv"]
    o1, o2, o3 = MLA_Q_LORA, MLA_Q_LORA + MLA_KV_LORA, MLA_IN_COLS
    nq = 2 * DIFF_HEADS * DIFF_D
    bf = lambda a: a.astype(BF16)
    w1T = bf(jnp.concatenate([w_in[:, :o2], w_in[:, o3:o3 + nq], w_in[:, o3 + 2 * nq:]], axis=1).T)
    wckv = bf(w_in[:, o1:o2])
    wkr = jnp.pad(w_in[:, o2:o3], ((0, 0), (0, LANES - MLA_ROPE)))
    wkrr = jnp.pad(_rot_cols(w_in[:, o2:o3], MLA_ROPE), ((0, 0), (0, LANES - MLA_ROPE)))
    wdk = w_in[:, o3 + nq:o3 + 2 * nq]
    wdkr = _rot_cols(wdk, DIFF_D)
    uq = w_uq.reshape(MLA_Q_LORA, MLA_HEADS, MLA_NOPE + MLA_ROPE)
    wuqT = bf(uq.reshape(MLA_Q_LORA, -1).T)
    ukv = w_ukv.reshape(MLA_KV_LORA, MLA_HEADS, MLA_NOPE + MLA_V)
    wuvT = bf(ukv[:, :, MLA_NOPE:].reshape(MLA_KV_LORA, -1).T)
    wuk = bf(jnp.pad(ukv[:, :, :MLA_NOPE], ((0, 0), (0, 0), (0, MLA_PAD - MLA_NOPE))).reshape(MLA_KV_LORA, -1))
    src = jnp.arange(LANES)[:, None]
    dst = jnp.arange(MLA_HEADS * MLA_PAD)[None, :]
    place = bf((src < MLA_ROPE) & (dst % MLA_PAD == src + MLA_NOPE))
    gq = jnp.broadcast_to(p["norm_q"].astype(F32)[:, None], (MLA_Q_LORA, TOK_TILE))
    gkvc = jnp.broadcast_to(p["norm_kv"].astype(F32)[:, None], (MLA_KV_LORA, TOK_TILE))
    gkvr = p["norm_kv"].astype(F32).reshape(1, MLA_KV_LORA)
    cosT, sinT, cosK, sinK = tabs
    weights = [w1T, wckv, bf(wkr), bf(wkrr), bf(wdk), bf(wdkr), wuqT, wuvT, wuk, place, gq, gkvc, gkvr]
    nt = S // TOK_TILE
    ctx_tile = nt - 1
    mod_map = lambda b, i: (b, (i == ctx_tile).astype(jnp.int32), 0, 0)
    HA, HD = MLA_HEADS, DIFF_HEADS
    dva = MLA_V + ONES_ROWS
    return pl.pallas_call(
        _even_prep_kernel,
        grid=(B, nt),
        in_specs=[pl.BlockSpec((1, TOK_TILE, D), lambda b, i: (b, i, 0)),
                  _full((1, D)),
                  pl.BlockSpec((1, 1, 1, D), mod_map),
                  pl.BlockSpec((1, 1, 1, D), mod_map)]
                 + [_full(w.shape) for w in weights]
                 + [pl.BlockSpec((MLA_ROPE, TOK_TILE), lambda b, i: (0, i)),
                    pl.BlockSpec((MLA_ROPE, TOK_TILE), lambda b, i: (0, i)),
                    pl.BlockSpec((TOK_TILE, LANES), lambda b, i: (i, 0)),
                    pl.BlockSpec((TOK_TILE, LANES), lambda b, i: (i, 0))],
        out_specs=[pl.BlockSpec((None, HA, MLA_QK, TOK_TILE), lambda b, i: (b, 0, 0, i)),
                   pl.BlockSpec((None, HA, TOK_TILE, MLA_QK), lambda b, i: (b, 0, i, 0)),
                   pl.BlockSpec((None, HA, dva, TOK_TILE), lambda b, i: (b, 0, 0, i)),
                   pl.BlockSpec((None, 2 * HD, DIFF_D, TOK_TILE), lambda b, i: (b, 0, 0, i)),
                   pl.BlockSpec((None, 2 * HD, TOK_TILE, DIFF_D), lambda b, i: (b, 0, i, 0)),
                   pl.BlockSpec((None, HD, dva, TOK_TILE), lambda b, i: (b, 0, 0, i))],
        out_shape=[jax.ShapeDtypeStruct((B, HA, MLA_QK, S), BF16),
                   jax.ShapeDtypeStruct((B, HA, S, MLA_QK), BF16),
                   jax.ShapeDtypeStruct((B, HA, dva, S), BF16),
                   jax.ShapeDtypeStruct((B, 2 * HD, DIFF_D, S), BF16),
                   jax.ShapeDtypeStruct((B, 2 * HD, S, DIFF_D), BF16),
                   jax.ShapeDtypeStruct((B, HD, dva, S), BF16)],
        compiler_params=_cparams(2),
        name="even_qkv_prep",
    )(xa, p["norm_mix"].reshape(1, D).astype(F32), mods["sh_a"], mods["sc_a"], *weights, cosT, sinT, cosK, sinK)


def _odd_prep_kernel(x_ref, g_ref, sh_ref, sc_ref, wqvT_ref, wk_ref, wkr_ref, gqn_ref, gk_ref, gkp_ref, bd_ref,
                     cosT_ref, sinT_ref, cosK_ref, sinK_ref,
                     qc_ref, qd_ref, k_ref, vc_ref, vd_ref):
    h = _norm_mod(x_ref[0], g_ref[...], sh_ref[0, 0], sc_ref[0, 0]).astype(BF16)
    n = h.shape[0]
    cosT, sinT, cosK, sinK = cosT_ref[...], sinT_ref[...], cosK_ref[...], sinK_ref[...]
    ones = _ones_rows(n)
    zT = _dot_t(wqvT_ref[...], h)
    d = HEAD_DIM
    for hd in range(WIN_HEADS):
        qc_ref[hd] = (_rope_rows(zT[hd * d:(hd + 1) * d], cosT, sinT) * (ATTN_SCALE * LOG2E)).astype(BF16)
    o1 = WIN_HEADS * d
    for hd in range(GLB_HEADS):
        q = zT[o1 + hd * d:o1 + (hd + 1) * d]
        q = _rope_rows(q * _rms_rows(q) * gqn_ref[...], cosT, sinT)
        qd_ref[hd] = (q * (ATTN_SCALE * LOG2E)).astype(BF16)
    o2 = o1 + GLB_HEADS * d
    for j in range(WIN_KV_HEADS):
        vc_ref[j, 0:d, :] = zT[o2 + j * d:o2 + (j + 1) * d].astype(BF16)
        vc_ref[j, d:, :] = ones
    o3 = o2 + WIN_KV_HEADS * d
    for j in range(GLB_KV_HEADS):
        vd_ref[j, 0:d, :] = zT[o3 + j * d:o3 + (j + 1) * d].astype(BF16)
        vd_ref[j, d:, :] = ones

    zk = jnp.dot(h, wk_ref[...], preferred_element_type=F32)
    zkr = jnp.dot(h, wkr_ref[...], preferred_element_type=F32)
    wc = WIN_KV_HEADS * d
    kc = zk[:, :wc] * cosK + zkr[:, :wc] * sinK
    z, zr = zk[:, wc:], zkr[:, wc:]
    sq = z * z
    hi = sq.astype(BF16)
    lo = (sq - hi.astype(F32)).astype(BF16)
    mean = (jnp.dot(hi, bd_ref[...], preferred_element_type=F32) + jnp.dot(lo, bd_ref[...], preferred_element_type=F32))
    kd = lax.rsqrt(mean + NORM_EPS) * (z * gk_ref[...] * cosK + zr * gkp_ref[...] * sinK)
    k = jnp.concatenate([kc, kd], axis=1).astype(BF16)
    for j in range(k_ref.shape[0]):
        k_ref[j] = k[:, j * d:(j + 1) * d]


def _odd_prep(xa, p, mods, tabs):
    B, S, D = xa.shape
    w_in = p["w_in"]
    d = HEAD_DIM
    sizes = (WIN_HEADS, WIN_KV_HEADS, WIN_KV_HEADS, GLB_HEADS, GLB_KV_HEADS, GLB_KV_HEADS)
    offs = [0]
    for s in sizes:
        offs.append(offs[-1] + s * d)
    col = lambda j: w_in[:, offs[j]:offs[j + 1]]
    bf = lambda a: a.astype(BF16)
    wqvT = bf(jnp.concatenate([col(0), col(3), col(2), col(5)], axis=1).T)
    wk = jnp.concatenate([col(1), col(4)], axis=1)
    wkr = _rot_cols(wk, d)
    gqn = jnp.broadcast_to(p["q_norm"].astype(F32)[:, None], (d, TOK_TILE))
    gk1 = p["k_norm"].astype(F32)
    gk = jnp.tile(gk1, GLB_KV_HEADS).reshape(1, -1)
    gkp = jnp.tile(_perm_cols(gk1, d), GLB_KV_HEADS).reshape(1, -1)
    wd = GLB_KV_HEADS * d
    lane = jnp.arange(wd)
    bd = bf(jnp.where(lane[:, None] // d == lane[None, :] // d, 1.0 / d, 0.0))
    assert WIN_KV_HEADS * d == LANES and wd == LANES
    cosT, sinT, cosK, sinK = tabs
    weights = [wqvT, bf(wk), bf(wkr), gqn, gk, gkp, bd]
    nt = S // TOK_TILE
    ctx_tile = nt - 1
    mod_map = lambda b, i: (b, (i == ctx_tile).astype(jnp.int32), 0, 0)
    dva = d + ONES_ROWS
    return pl.pallas_call(
        _odd_prep_kernel,
        grid=(B, nt),
        in_specs=[pl.BlockSpec((1, TOK_TILE, D), lambda b, i: (b, i, 0)),
                  _full((1, D)),
                  pl.BlockSpec((1, 1, 1, D), mod_map),
                  pl.BlockSpec((1, 1, 1, D), mod_map)]
                 + [_full(w.shape) for w in weights]
                 + [pl.BlockSpec((d, TOK_TILE), lambda b, i: (0, i)),
                    pl.BlockSpec((d, TOK_TILE), lambda b, i: (0, i)),
                    pl.BlockSpec((TOK_TILE, LANES), lambda b, i: (i, 0)),
                    pl.BlockSpec((TOK_TILE, LANES), lambda b, i: (i, 0))],
        out_specs=[pl.BlockSpec((None, WIN_HEADS, d, TOK_TILE), lambda b, i: (b, 0, 0, i)),
                   pl.BlockSpec((None, GLB_HEADS, d, TOK_TILE), lambda b, i: (b, 0, 0, i)),
                   pl.BlockSpec((None, WIN_KV_HEADS + GLB_KV_HEADS, TOK_TILE, d), lambda b, i: (b, 0, i, 0)),
                   pl.BlockSpec((None, WIN_KV_HEADS, dva, TOK_TILE), lambda b, i: (b, 0, 0, i)),
                   pl.BlockSpec((None, GLB_KV_HEADS, dva, TOK_TILE), lambda b, i: (b, 0, 0, i))],
        out_shape=[jax.ShapeDtypeStruct((B, WIN_HEADS, d, S), BF16),
                   jax.ShapeDtypeStruct((B, GLB_HEADS, d, S), BF16),
                   jax.ShapeDtypeStruct((B, WIN_KV_HEADS + GLB_KV_HEADS, S, d), BF16),
                   jax.ShapeDtypeStruct((B, WIN_KV_HEADS, dva, S), BF16),
                   jax.ShapeDtypeStruct((B, GLB_KV_HEADS, dva, S), BF16)],
        compiler_params=_cparams(2),
        name="odd_qkv_prep",
    )(xa, p["norm_mix"].reshape(1, D).astype(F32), mods["sh_a"], mods["sc_a"], *weights, cosT, sinT, cosK, sinK)


def _aligned(start):
    return start if isinstance(start, int) else pl.multiple_of(start, PV_TILE)


def _attn_step(k_ref, vT_ref, acc_ref, q_next, next_offs, s_next, cur_offs, s_cur, m, cmax):
    m_new = jnp.maximum(m, cmax) if cur_offs else m
    pv = None
    cnext = None
    for i in range(max(len(next_offs), len(cur_offs))):
        rows = slice(i * PV_TILE, (i + 1) * PV_TILE)
        if i < len(next_offs):
            s = jnp.dot(k_ref[pl.ds(_aligned(next_offs[i]), PV_TILE), :], q_next, preferred_element_type=F32)
            s_next[rows, :] = s
            cm = jnp.max(s, axis=0, keepdims=True)
            cnext = cm if cnext is None else jnp.maximum(cnext, cm)
        if i < len(cur_offs):
            p = jnp.exp2(s_cur[rows, :] - m_new).astype(BF16)
            d = jnp.dot(vT_ref[:, pl.ds(_aligned(cur_offs[i]), PV_TILE)], p, preferred_element_type=F32)
            pv = d if pv is None else pv + d
    if cur_offs:
        acc_ref[...] = acc_ref[...] * jnp.exp2(m - m_new) + pv
    return m_new, cnext


def _softmax_init(init_ref, dva, dv, tq):
    m0 = jnp.broadcast_to(init_ref[0, 0:1, 0:1], (1, tq))
    row = lax.broadcasted_iota(jnp.int32, (dva, tq), 0)
    acc0 = jnp.where(row == dv, jnp.broadcast_to(init_ref[0, 1:2, 0:1], (dva, tq)), 0.0)
    return m0, acc0


def _flash_kernel(init_ref, qT_ref, qTn_ref, k_ref, vT_ref, o_ref, s0_ref, s1_ref, acc_ref, cm_ref, *,
                  T, n_ctx, kt, dv):
    qT = qT_ref[...]
    tq = qT.shape[1]
    dva = vT_ref.shape[0]
    pieces = lambda off: [off + kk for kk in range(0, kt, PV_TILE)]
    first = [T + kk for kk in range(0, n_ctx, PV_TILE)] + pieces(0)
    step = functools.partial(_attn_step, k_ref, vT_ref, acc_ref)

    @pl.when(pl.program_id(2) == 0)
    def _():
        _, cm_ref[0:1, :] = step(qT, first, s0_ref, [], None, None, None)

    m, acc0 = _softmax_init(init_ref, dva, dv, tq)
    acc_ref[...] = acc0
    m, cm_b = step(qT, pieces(kt), s1_ref, first, s0_ref, m, cm_ref[0:1, :])

    def body(t, carry):
        m, cm_b = carry
        off = pl.multiple_of(t * (2 * kt), kt)
        m, cm_a = step(qT, pieces(off + 2 * kt), s0_ref, pieces(off + kt), s1_ref, m, cm_b)
        m, cm_b = step(qT, pieces(off + 3 * kt), s1_ref, pieces(off + 2 * kt), s0_ref, m, cm_a)
        return m, cm_b

    m, cm_b = lax.fori_loop(0, T // (2 * kt) - 1, body, (m, cm_b))
    m, cm_ref[0:1, :] = step(qTn_ref[...], first, s0_ref, pieces(T - kt), s1_ref, m, cm_b)
    acc = acc_ref[...]
    o_ref[...] = acc[:dv] / acc[dv:dv + 1]


def _flash(qT, k, vT, init, T, key_of):
    B, Hq, dk, S = qT.shape
    Hv, dva = vT.shape[1], vT.shape[2]
    dv = dva - ONES_ROWS
    gv = Hq // Hv
    kt = min(KEY_TILE, T // 2)
    tq = min(Q_TILE, T)
    assert T % (2 * kt) == 0 and T % tq == 0 and kt % PV_TILE == 0
    n_ctx = S - T
    assert n_ctx % PV_TILE == 0
    kern = functools.partial(_flash_kernel, T=T, n_ctx=n_ctx, kt=kt, dv=dv)
    nq = T // tq
    return pl.pallas_call(
        kern,
        grid=(B, Hq, nq),
        in_specs=[pl.BlockSpec((1, 2, LANES),lambda b, h, i: (h, 0, 0)),
                  pl.BlockSpec((None, None, dk, tq), lambda b, h, i: (b, h, 0, i)),
                  pl.BlockSpec((None, None, dk, tq), lambda b, h, i: (b, h, 0, jnp.minimum(i + 1, nq - 1))),
                  pl.BlockSpec((None, None, S, dk), lambda b, h, i: (b, key_of(h), 0, 0)),
                  pl.BlockSpec((None, None, dva, S), lambda b, h, i: (b, h // gv, 0, 0))],
        out_specs=pl.BlockSpec((None, None, dv, tq), lambda b, h, i: (b, h, 0, i)),
        out_shape=jax.ShapeDtypeStruct((B, Hq, dv, T), F32),
        scratch_shapes=[pltpu.VMEM((kt + n_ctx, tq), F32), pltpu.VMEM((kt, tq), F32),
                        pltpu.VMEM((dva, tq), F32), pltpu.VMEM((8, tq), F32)],
        compiler_params=_cparams(3),
        name="dense_attention",
    )(init, qT, qT, k, vT)


def _ctx_attn_kernel(init_ref, qT_ref, k_ref, vT_ref, o_ref, *, dv):
    qT = qT_ref[...]
    m0, acc0 = _softmax_init(init_ref, vT_ref.shape[0], dv, qT.shape[1])
    s = jnp.dot(k_ref[...], qT, preferred_element_type=F32)
    m = jnp.maximum(m0, jnp.max(s, axis=0, keepdims=True))
    acc = acc0 * jnp.exp2(m0 - m) + jnp.dot(vT_ref[...], jnp.exp2(s - m).astype(BF16), preferred_element_type=F32)
    o_ref[...] = acc[:dv] / acc[dv:dv + 1]


def _ctx_attn(qT, k, vT, init, T, key_of):
    B, Hq, dk, S = qT.shape
    Hv, dva = vT.shape[1], vT.shape[2]
    dv = dva - ONES_ROWS
    gv = Hq // Hv
    n = S - T
    blk = T // n
    return pl.pallas_call(
        functools.partial(_ctx_attn_kernel, dv=dv),
        grid=(B, Hq),
        in_specs=[pl.BlockSpec((1, 2, LANES),lambda b, h: (h, 0, 0)),
                  pl.BlockSpec((None, None, dk, n), lambda b, h: (b, h, 0, blk)),
                  pl.BlockSpec((None, None, n, dk), lambda b, h: (b, key_of(h), blk, 0)),
                  pl.BlockSpec((None, None, dva, n), lambda b, h: (b, h // gv, 0, blk))],
        out_specs=pl.BlockSpec((None, None, dv, n), lambda b, h: (b, h, 0, 0)),
        out_shape=jax.ShapeDtypeStruct((B, Hq, dv, n), F32),
        compiler_params=_cparams(2),
        name="context_attention",
    )(init, qT, k, vT)


def _window_kernel(sink_ref, bias_ref, qT_ref, k_ref, kc_ref, vT_ref, o_ref, *, dv):
    i = pl.program_id(2)
    G, _, tq = qT_ref.shape
    qT = jnp.concatenate([qT_ref[g] for g in range(G)], axis=1)
    snk = jnp.concatenate([jnp.broadcast_to(sink_ref[g, 0:1, 0:1], (1, tq)) for g in range(G)], axis=1)
    W = WINDOW
    kband = jnp.concatenate([k_ref[0][tq - W:tq, :], k_ref[1][...], k_ref[2][0:W, :]], axis=0)
    before_first = jnp.where(i == 0, NEG_INF, 0.0)
    after_last = jnp.where(i == pl.num_programs(2) - 1, NEG_INF, 0.0)
    bias = jnp.concatenate([bias_ref[0:W, :] + before_first, bias_ref[W:W + tq, :],
                            bias_ref[W + tq:, :] + after_last], axis=0)
    s_loc = jnp.dot(kband, qT, preferred_element_type=F32) + bias
    s_ctx = jnp.dot(kc_ref[...], qT, preferred_element_type=F32)
    m = jnp.maximum(jnp.maximum(jnp.max(s_loc, axis=0, keepdims=True), jnp.max(s_ctx, axis=0, keepdims=True)), snk)
    p_loc = jnp.exp2(s_loc - m).astype(BF16)
    p_ctx = jnp.exp2(s_ctx - m).astype(BF16)
    vband = jnp.concatenate([vT_ref[0][:, tq - W:tq], vT_ref[1][...], vT_ref[2][:, 0:W]], axis=1)
    acc = (jnp.dot(vband, p_loc, preferred_element_type=F32)
           + jnp.dot(vT_ref[3][...], p_ctx, preferred_element_type=F32))
    o = acc[:dv] / (acc[dv:dv + 1] + jnp.exp2(snk - m))
    for g in range(G):
        o_ref[g] = o[:, g * tq:(g + 1) * tq]


def _window(qT, k, vT, sink, T, key_of):
    B, Hq, dk, S = qT.shape
    Hk, dva = vT.shape[1], vT.shape[2]
    dv = dva - ONES_ROWS
    g = Hq // Hk
    tq = TOK_TILE
    nt = T // tq
    ctx_blk = T // tq
    clip = lambda j: jnp.clip(j, 0, nt - 1)
    kern = functools.partial(_window_kernel, dv=dv)
    kspec = lambda f: pl.BlockSpec((None, None, tq, dk), lambda b, h, i: (b, key_of(h * g), f(i), 0))
    vspec = lambda f: pl.BlockSpec((None, None, dva, tq), lambda b, h, i: (b, h, 0, f(i)))
    r = jnp.arange(tq + 2 * WINDOW)[:, None]
    c = jnp.arange(g * tq)[None, :] % tq
    bias = jnp.where(jnp.abs(r - WINDOW - c) <= WINDOW, 0.0, NEG_INF).astype(F32)

    def body(sink_ref, bias_ref, qT_ref, k0, k1, k2, kc, v0, v1, v2, vc, o_ref):
        kern(sink_ref, bias_ref, qT_ref, (k0, k1, k2), kc, (v0, v1, v2, vc), o_ref)

    return pl.pallas_call(
        body,
        grid=(B, Hk, nt),
        in_specs=[pl.BlockSpec((g, 1, LANES), lambda b, h, i: (h, 0, 0)),
                  _full(bias.shape),
                  pl.BlockSpec((None, g, dk, tq), lambda b, h, i: (b, h, 0, i)),
                  kspec(lambda i: clip(i - 1)), kspec(lambda i: i), kspec(lambda i: clip(i + 1)),
                  kspec(lambda i: ctx_blk),
                  vspec(lambda i: clip(i - 1)), vspec(lambda i: i), vspec(lambda i: clip(i + 1)),
                  vspec(lambda i: ctx_blk)],
        out_specs=pl.BlockSpec((None, g, dv, tq), lambda b, h, i: (b, h, 0, i)),
        out_shape=jax.ShapeDtypeStruct((B, Hq, dv, T), F32),
        compiler_params=_cparams(3),
        name="window_attention",
    )(sink, bias, qT, k, k, k, k, vT, vT, vT, vT)


def _project_out(x_ref, oT, w_ref, gate_ref, y_ref):
    y = jnp.dot(oT.T.astype(BF16), w_ref[...], preferred_element_type=F32)
    y_ref[0] = x_ref[0] + gate_ref[0, 0] * y


def _even_out_kernel(x_ref, oa_ref, od_ref, w_ref, gate_ref, lam_ref, sub_ref, y_ref, *, post_scale):
    lam = lam_ref[...]
    parts = [oa_ref[hd] for hd in range(MLA_HEADS)]
    for hd in range(DIFF_HEADS):
        diff = od_ref[2 * hd] - lam * od_ref[2 * hd + 1]
        parts.append(diff * _rms_rows(diff) * sub_ref[...] * post_scale)
    _project_out(x_ref, jnp.concatenate(parts, axis=0), w_ref, gate_ref, y_ref)


def _odd_out_kernel(x_ref, oc_ref, od_ref, w_ref, gate_ref, y_ref):
    parts = [oc_ref[hd] for hd in range(WIN_HEADS)] + [od_ref[hd] for hd in range(GLB_HEADS)]
    _project_out(x_ref, jnp.concatenate(parts, axis=0), w_ref, gate_ref, y_ref)


def _out_proj(kern, x, heads, w, gate, extra, tile0, n_tiles, kind):
    B, S, D = x.shape
    row_map = lambda b, i: (b, i + tile0, 0)
    return pl.pallas_call(
        kern,
        grid=(B, n_tiles),
        in_specs=[pl.BlockSpec((1, TOK_TILE, D), row_map)]
                 + [pl.BlockSpec((None,) + o.shape[1:3] + (TOK_TILE,), lambda b, i: (b, 0, 0, i)) for o in heads]
                 + [_full(w.shape), pl.BlockSpec((1, 1, 1, D), lambda b, i: (b, kind, 0, 0))]
                 + [_full(e.shape) for e in extra],
        out_specs=pl.BlockSpec((1, TOK_TILE, D), row_map),
        out_shape=jax.ShapeDtypeStruct((B, S, D), F32),
        input_output_aliases={0: 0},
        compiler_params=_cparams(2),
        name="merge_out_proj_residual",
    )(x, *heads, w, gate, *extra)


def _first_argmax(vals):
    best, idx = vals[0], jnp.zeros(vals[0].shape, jnp.int32)
    for j in range(1, len(vals)):
        better = vals[j] > best
        idx = jnp.where(better, j, idx)
        best = jnp.where(better, vals[j], best)
    return idx, best


def _pick(idx, vals):
    out = vals[0]
    for j in range(1, len(vals)):
        out = jnp.where(idx == j, vals[j], out)
    return out


def _route(logits, bias):
    s = 1.0 / (1.0 + jnp.exp(-logits))
    sel = s + bias
    srow = [s[e:e + 1] for e in range(N_EXPERTS)]
    row = [sel[e:e + 1] for e in range(N_EXPERTS)]
    scores = []
    for g in range(N_GROUPS):
        a, b, c, d = row[4 * g:4 * g + 4]
        hi1, lo1, hi2, lo2 = jnp.maximum(a, b), jnp.minimum(a, b), jnp.maximum(c, d), jnp.minimum(c, d)
        top1 = jnp.maximum(hi1, hi2)
        top2 = jnp.maximum(jnp.maximum(lo1, lo2), jnp.minimum(hi1, hi2))
        scores.append(top1 + top2)
    gi, _ = _first_argmax(scores)
    v = [_pick(gi, [row[4 * g + j] for g in range(N_GROUPS)]) for j in range(EXPERTS_PER_GROUP)]
    sv = [_pick(gi, [srow[4 * g + j] for g in range(N_GROUPS)]) for j in range(EXPERTS_PER_GROUP)]
    i1, _ = _first_argmax(v)
    i2, _ = _first_argmax([jnp.where(i1 == j, -jnp.inf, v[j]) for j in range(EXPERTS_PER_GROUP)])
    w1, w2 = _pick(i1, sv), _pick(i2, sv)
    tot = w1 + w2
    w1, w2 = w1 / tot, w2 / tot
    rows = []
    for e in range(N_EXPERTS):
        g, j = divmod(e, EXPERTS_PER_GROUP)
        in_g = gi == g
        rows.append(jnp.where(in_g & (i1 == j), w1, 0.0) + jnp.where(in_g & (i2 == j), w2, 0.0))
    return jnp.concatenate(rows, axis=0), gi


def _split(x, terms):
    out = []
    for _ in range(terms):
        part = x.astype(BF16).astype(F32)
        out.append(part)
        x = x - part
    return out


def _to_column(row):
    n = row.shape[1]
    return jnp.concatenate([row, jnp.zeros((LANES - 1, n), F32)], axis=0).T[:, 0:1]


def _moe_kernel(x_ref, g_ref, sh_ref, sc_ref, gate_ref, wrT_ref, br_ref, before_ref, wg_ref, wu_ref, wd_ref, y_ref,
                hs_scr, cs_scr, ys_scr, q_scr, info_ref):
    step = pl.program_id(2)
    per_step = wg_ref.shape[0]
    tm = x_ref.shape[1]
    R = hs_scr.shape[0]

    @pl.when(step == 0)
    def _():
        h = _norm_mod(x_ref[0], g_ref[...], sh_ref[0, 0], sc_ref[0, 0]).astype(BF16)
        comb, gi = _route(_dot_t(wrT_ref[...], h), br_ref[...])
        sel = jnp.concatenate([(gi == g).astype(F32) for g in range(N_GROUPS)]
                              + [jnp.zeros((BF16_ROWS - N_GROUPS, tm), F32)], axis=0)
        rank = jnp.dot(sel.astype(BF16), before_ref[...], preferred_element_type=F32)
        cnt = jnp.sum(sel, axis=1, keepdims=True)
        seg = jnp.ceil(cnt * (1.0 / BF16_ROWS)) * BF16_ROWS
        pos = jnp.zeros((1, tm), F32)
        start = jnp.zeros((1, 1), F32)
        for g in range(N_GROUPS):
            pos = pos + sel[g:g + 1] * (start + rank[g:g + 1])
            info_ref[g] = start[0, 0].astype(jnp.int32)
            info_ref[N_GROUPS + g] = jnp.ceil(cnt[g:g + 1] * (1.0 / MOE_CHUNK))[0, 0].astype(jnp.int32)
            start = start + seg[g:g + 1]
        slot = lax.broadcasted_iota(jnp.int32, (R, tm), 0).astype(F32)
        P = jnp.where(slot == pos, 1.0, 0.0).astype(BF16)
        hs_scr[...] = jnp.dot(P, h, preferred_element_type=F32).astype(BF16)
        parts = _split(comb, COMB_TERMS)
        combT = jnp.concatenate(parts + [jnp.zeros((LANES - COMB_TERMS * N_EXPERTS, tm), F32)], axis=0).T
        cs_scr[...] = jnp.dot(P, combT.astype(BF16), preferred_element_type=F32)
        lane = lax.broadcasted_iota(jnp.int32, (tm, R), 1).astype(F32)
        q_scr[...] = jnp.where(lane == _to_column(pos), 1.0, 0.0).astype(BF16)
        ys_scr[...] = jnp.zeros_like(ys_scr)

    steps_per_group = EXPERTS_PER_GROUP // per_step
    grp = lax.shift_right_logical(step, steps_per_group.bit_length() - 1)
    start = info_ref[grp]
    n_chunks = info_ref[N_GROUPS + grp]

    def experts(r0, rows):
        h = hs_scr[pl.ds(r0, rows), :]
        cs = cs_scr[pl.ds(r0, rows), :]
        lane = lax.broadcasted_iota(jnp.int32, (rows, LANES), 1) & (N_EXPERTS - 1)
        total = None
        for j in range(per_step):
            a = jnp.dot(h, wg_ref[j], preferred_element_type=F32)
            u = jnp.dot(h, wu_ref[j], preferred_element_type=F32)
            act = (a * (1.0 / (1.0 + jnp.exp(-a)))) * u
            y = jnp.dot(act.astype(BF16), wd_ref[j], preferred_element_type=F32)
            w = jnp.sum(jnp.where(lane == step * per_step + j, cs, 0.0), axis=1, keepdims=True)
            total = w * y if total is None else total + w * y
        ys_scr[pl.ds(r0, rows), :] += total

    def pair(c, carry):
        experts(pl.multiple_of(start + c * (2 * MOE_CHUNK), BF16_ROWS), 2 * MOE_CHUNK)
        return carry

    lax.fori_loop(0, lax.shift_right_logical(n_chunks, 1), pair, 0)

    @pl.when((n_chunks & 1) == 1)
    def _():
        experts(pl.multiple_of(start + (n_chunks - 1) * MOE_CHUNK, BF16_ROWS), MOE_CHUNK)

    @pl.when(step == pl.num_programs(2) - 1)
    def _():
        q = q_scr[...]
        out = sum(jnp.dot(q, part.astype(BF16), preferred_element_type=F32) for part in _split(ys_scr[...], 2))
        y_ref[0] = x_ref[0] + gate_ref[0, 0] * out


def _moe(x, g, sh, sc, gate, wrT, br, wg, wu, wd, tm, tile0, n_tiles, kind):
    B, S, D = x.shape
    E, _, F = wg.shape
    R = -(-(tm + BF16_ROWS * N_GROUPS + MOE_CHUNK) // MXU_DIM) * MXU_DIM
    idx = jnp.arange(tm)
    before = (idx[:, None] < idx[None, :]).astype(BF16)
    mod_map = lambda b, i, e: (b, kind, 0, 0)
    row_map = lambda b, i, e: (b, i + tile0, 0)
    return pl.pallas_call(
        _moe_kernel,
        grid=(B, n_tiles, E // MOE_STEP_EXPERTS),
        in_specs=[pl.BlockSpec((1, tm, D), row_map),
                  pl.BlockSpec((1, D), lambda b, i, e: (0, 0)),
                  pl.BlockSpec((1, 1, 1, D), mod_map),
                  pl.BlockSpec((1, 1, 1, D), mod_map),
                  pl.BlockSpec((1, 1, 1, D), mod_map),
                  pl.BlockSpec((E, D), lambda b, i, e: (0, 0)),
                  pl.BlockSpec((E, 1), lambda b, i, e: (0, 0)),
                  pl.BlockSpec((tm, tm), lambda b, i, e: (0, 0)),
                  pl.BlockSpec((MOE_STEP_EXPERTS, D, F), lambda b, i, e: (e, 0, 0)),
                  pl.BlockSpec((MOE_STEP_EXPERTS, D, F), lambda b, i, e: (e, 0, 0)),
                  pl.BlockSpec((MOE_STEP_EXPERTS, F, D), lambda b, i, e: (e, 0, 0))],
        out_specs=pl.BlockSpec((1, tm, D), row_map),
        out_shape=jax.ShapeDtypeStruct((B, S, D), F32),
        scratch_shapes=[pltpu.VMEM((R, D), BF16), pltpu.VMEM((R, LANES), F32), pltpu.VMEM((R, D), F32),
                        pltpu.VMEM((tm, R), BF16), pltpu.SMEM((2 * N_GROUPS,), jnp.int32)],
        input_output_aliases={0: 0},
        compiler_params=_cparams(3),
        name="moe_experts",
    )(x, g.reshape(1, D).astype(F32), sh, sc, gate, wrT, br, before, wg, wu, wd)


def _final_norm_kernel(x_ref, g_ref, o_ref):
    x = x_ref[0]
    o_ref[0] = (x * lax.rsqrt(jnp.mean(x * x, axis=-1, keepdims=True) + NORM_EPS)) * g_ref[...]


def _final_norm(x, g, T):
    B, S, D = x.shape
    return pl.pallas_call(
        _final_norm_kernel,
        grid=(B, T // TOK_TILE),
        in_specs=[pl.BlockSpec((1, TOK_TILE, D), lambda b, i: (b, i, 0)),
                  pl.BlockSpec((1, D), lambda b, i: (0, 0))],
        out_specs=pl.BlockSpec((1, TOK_TILE, D), lambda b, i: (b, i, 0)),
        out_shape=jax.ShapeDtypeStruct((B, T, D), F32),
        compiler_params=_cparams(2),
        name="final_norm",
    )(x, g.reshape(1, D).astype(F32))


def _axial_tables(T, n_ctx, rot_dim):
    half = rot_dim // 2
    inv_freq = ROPE_THETA ** (-jnp.arange(0, half, 2, dtype=F32) / half)
    rows = T // GRID_W
    row = jnp.broadcast_to(jnp.arange(rows, dtype=F32)[:, None], (rows, GRID_W)).reshape(-1)
    col = jnp.broadcast_to(jnp.arange(GRID_W, dtype=F32)[None, :], (rows, GRID_W)).reshape(-1)

    def ang(pos):
        a = pos[:, None] * inv_freq[None, :]
        return jnp.concatenate([a, a], axis=-1)

    a = jnp.concatenate([ang(row), ang(col)], axis=-1)
    cos = jnp.concatenate([jnp.cos(a), jnp.ones((n_ctx, rot_dim), F32)], axis=0)
    sin = jnp.concatenate([jnp.sin(a), jnp.zeros((n_ctx, rot_dim), F32)], axis=0)
    reps = LANES // rot_dim
    return cos.T, sin.T, jnp.tile(cos, (1, reps)), jnp.tile(sin, (1, reps))


def _plain_init(n_heads):
    return jnp.broadcast_to(jnp.array([NEG_INF, 0.0], F32)[None, :, None], (n_heads, 2, LANES))


def _sink_init(sink):
    s = sink.astype(F32) * LOG2E
    return jnp.broadcast_to(jnp.stack([s, jnp.ones_like(s)], axis=1)[:, :, None], (s.shape[0], 2, LANES))


def _even_mixer(xa, p, mods, T, with_ctx, lam_init, tabs):
    B, S, D = xa.shape
    qa, ka, va, qd, kd, vd = _even_prep(xa, p, mods, tabs)
    own = lambda h: h
    init_a, init_d = _plain_init(MLA_HEADS), _plain_init(2 * DIFF_HEADS)
    lam = p["lam"].astype(F32)
    lam_full = (jnp.exp(jnp.sum(lam[0] * lam[1])) - jnp.exp(jnp.sum(lam[2] * lam[3])) + lam_init).reshape(1, 1)
    sub = jnp.broadcast_to(p["subln"].astype(F32)[:, None], (DIFF_V, TOK_TILE))
    kern = functools.partial(_even_out_kernel, post_scale=1.0 - lam_init)
    oa = _flash(qa, ka, va, init_a, T, own)
    od = _flash(qd, kd, vd, init_d, T, own)
    xa = _out_proj(kern, xa, [oa, od], p["w_out"], mods["g_a"], [lam_full, sub], 0, T // TOK_TILE, 0)
    if with_ctx:
        oa = _ctx_attn(qa, ka, va, init_a, T, own)
        od = _ctx_attn(qd, kd, vd, init_d, T, own)
        xa = _out_proj(kern, xa, [oa, od], p["w_out"], mods["g_a"], [lam_full, sub], T // TOK_TILE, 1, 1)
    return xa


def _odd_mixer(xa, p, mods, T, with_ctx, tabs):
    B, S, D = xa.shape
    qc, qd, k, vc, vd = _odd_prep(xa, p, mods, tabs)
    gw, gd = WIN_HEADS // WIN_KV_HEADS, GLB_HEADS // GLB_KV_HEADS
    key_c = lambda h: h // gw
    key_d = lambda h: WIN_KV_HEADS + h // gd
    init_d = _plain_init(GLB_HEADS)
    sink = p["sink"].astype(F32) * LOG2E
    sink_w = jnp.broadcast_to(sink[:, None, None], (WIN_HEADS, 1, LANES))
    od = _flash(qd, k, vd, init_d, T, key_d)
    oc = _window(qc, k, vc, sink_w, T, key_c)
    xa = _out_proj(_odd_out_kernel, xa, [oc, od], p["w_out"], mods["g_a"], [], 0, T // TOK_TILE, 0)
    if with_ctx:
        oc = _ctx_attn(qc, k, vc, _sink_init(p["sink"]), T, key_c)
        od = _ctx_attn(qd, k, vd, init_d, T, key_d)
        xa = _out_proj(_odd_out_kernel, xa, [oc, od], p["w_out"], mods["g_a"], [], T // TOK_TILE, 1, 1)
    return xa


def kernel(x, c, ctx, c_ctx, w_mod, b_mod, norm_mix, norm_ffn, even_w_in, even_norm_q, even_norm_kv, even_w_uq, even_w_ukv, even_lambda, even_subln, even_w_out, odd_w_in, odd_sink, odd_q_norm, odd_k_norm, odd_w_out, w_router, b_router, w_gate, w_up, w_down, norm_final):
    B, T, D = x.shape
    n_ctx = ctx.shape[1]
    depth = w_mod.shape[0]
    assert n_ctx == TOK_TILE == PV_TILE and T % MOE_TILE == 0 and B <= 7
    assert MLA_ROPE == DIFF_D
    S = T + n_ctx
    tabs32 = _axial_tables(T, n_ctx, MLA_ROPE)
    tabs64 = _axial_tables(T, n_ctx, HEAD_DIM)

    cond = jnp.zeros((8, D), F32).at[:B].set(c.astype(F32)).at[B].set(c_ctx.astype(F32))
    mod_all = _mod_vectors(cond, w_mod, b_mod)
    wrT = w_router.astype(BF16).T
    br = b_router.astype(F32).reshape(N_EXPERTS, 1)

    xa = jnp.concatenate([x, ctx], axis=1).astype(F32)
    for l in range(depth):
        with_ctx = l < depth - 1
        i = l // 2
        lat = mod_all[l, :B].reshape(B, 6, D)
        cx = jnp.broadcast_to(mod_all[l, B].reshape(1, 6, D), (B, 6, D))
        both = jnp.stack([lat, cx], axis=1)
        names = ("sh_a", "sc_a", "g_a", "sh_f", "sc_f", "g_f")
        mods = {n: both[:, :, j:j + 1, :] for j, n in enumerate(names)}
        if l % 2 == 0:
            lam_init = 0.8 - 0.6 * math.exp(-0.3 * l)
            p = dict(norm_mix=norm_mix[l], w_in=even_w_in[i], norm_q=even_norm_q[i], norm_kv=even_norm_kv[i],
                     w_uq=even_w_uq[i], w_ukv=even_w_ukv[i], lam=even_lambda[i], subln=even_subln[i],
                     w_out=even_w_out[i].astype(BF16))
            xa = _even_mixer(xa, p, mods, T, with_ctx, lam_init, tabs32)
        else:
            p = dict(norm_mix=norm_mix[l], w_in=odd_w_in[i], sink=odd_sink[i], q_norm=odd_q_norm[i],
                     k_norm=odd_k_norm[i], w_out=odd_w_out[i].astype(BF16))
            xa = _odd_mixer(xa, p, mods, T, with_ctx, tabs64)
        wg, wu, wd = w_gate[l].astype(BF16), w_up[l].astype(BF16), w_down[l].astype(BF16)
        ffn = (norm_ffn[l], mods["sh_f"], mods["sc_f"], mods["g_f"], wrT, br, wg, wu, wd)
        xa = _moe(xa, *ffn, MOE_TILE, 0, T // MOE_TILE, 0)
        if with_ctx:
            xa = _moe(xa, *ffn, TOK_TILE, T // TOK_TILE, 1, 1)
    return _final_norm(xa, norm_final, T)
```

```python
import functools
import math

import jax
import jax.numpy as jnp
from jax import lax
from jax.experimental import pallas as pl
from jax.experimental.pallas import tpu as pltpu

F32 = jnp.float32
BF16 = jnp.bfloat16
LOG2E = 1.4426950408889634

GRID_W = 64
ROPE_THETA = 10000.0
NORM_EPS = 1e-6
NEG_INF = -1e30

MLA_HEADS = 8
MLA_Q_LORA = 384
MLA_KV_LORA = 256
MLA_NOPE = 64
MLA_ROPE = 32
MLA_V = 64
MLA_SCALE = (MLA_NOPE + MLA_ROPE) ** -0.5
MLA_IN_COLS = MLA_Q_LORA + MLA_KV_LORA + MLA_ROPE
MLA_QK = MLA_NOPE + MLA_ROPE
MLA_PAD = 128

DIFF_HEADS = 8
DIFF_D = 32
DIFF_V = 2 * DIFF_D
DIFF_SCALE = DIFF_D ** -0.5

HEAD_DIM = 64
WIN_HEADS = 8
WIN_KV_HEADS = 2
WINDOW = 128
GLB_HEADS = 8
GLB_KV_HEADS = 2
ATTN_SCALE = HEAD_DIM ** -0.5

N_EXPERTS = 16
N_GROUPS = 4
EXPERTS_PER_GROUP = N_EXPERTS // N_GROUPS

TOK_TILE = 256
Q_TILE = 1024
KEY_TILE = 2048
PV_TILE = 256
MOE_TILE = 1024
MOE_CHUNK = 128
MOE_STEP_EXPERTS = 2
COMB_TERMS = 3
ONES_ROWS = 16
LANES = 128
BF16_ROWS = 16
MXU_DIM = 256
VMEM_LIMIT = 56 * 1024 * 1024


def _cparams(n_axes):
    return pltpu.CompilerParams(dimension_semantics=("arbitrary",) * n_axes,
                                vmem_limit_bytes=VMEM_LIMIT)


def _full(shape):
    return pl.BlockSpec(shape, lambda *_: (0,) * len(shape))


def _mod_kernel(a_ref, w_ref, b_ref, o_ref):
    a = a_ref[...]
    a = a * (1.0 / (1.0 + jnp.exp(-a)))
    o_ref[0] = jnp.dot(a.astype(BF16), w_ref[0].astype(BF16), preferred_element_type=F32) + b_ref[0]


def _mod_vectors(cond, w_mod, b_mod):
    L, D, N = w_mod.shape
    tn = 1536
    return pl.pallas_call(
        _mod_kernel,
        grid=(L, N // tn),
        in_specs=[pl.BlockSpec((8, D), lambda l, j: (0, 0)),
                  pl.BlockSpec((1, D, tn), lambda l, j: (l, 0, j)),
                  pl.BlockSpec((1, 1, tn), lambda l, j: (l, 0, j))],
        out_specs=pl.BlockSpec((1, 8, tn), lambda l, j: (l, 0, j)),
        out_shape=jax.ShapeDtypeStruct((L, 8, N), F32),
        compiler_params=_cparams(2),
        name="mod_vectors",
    )(cond, w_mod, b_mod.reshape(L, 1, N))


def _norm_mod(x, g, sh, sc):
    y = x * lax.rsqrt(jnp.mean(x * x, axis=-1, keepdims=True) + NORM_EPS)
    return (y * g) * (1.0 + sc) + sh


def _dot_t(w, h):
    return lax.dot_general(w, h, (((1,), (1,)), ((), ())), preferred_element_type=F32)


def _rms_rows(x):
    return lax.rsqrt(jnp.mean(x * x, axis=0, keepdims=True) + NORM_EPS)


def _rope_rows(x, cosT, sinT):
    q = x.shape[0] // 4
    rot = jnp.concatenate([-x[q:2 * q], x[0:q], -x[3 * q:4 * q], x[2 * q:3 * q]], axis=0)
    return x * cosT + rot * sinT


def _ones_rows(n):
    row = lax.broadcasted_iota(jnp.int32, (ONES_ROWS, n), 0)
    return jnp.where(row == 0, 1.0, 0.0).astype(BF16)


def _tile_lanes(x, reps):
    return x if reps == 1 else jnp.concatenate([x] * reps, axis=1)


def _even_prep_kernel(x_ref, g_ref, sh_ref, sc_ref, w1T_ref, wckv_ref, wkr_ref, wkrr_ref, wdk_ref, wdkr_ref,
                      wuqT_ref, wuvT_ref, wuk_ref, place_ref, gq_ref, gkvc_ref, gkvr_ref, cosT_ref, sinT_ref,
                      cosK_ref, sinK_ref,
                      qa_ref, ka_ref, va_ref, qd_ref, kd_ref, vd_ref):
    h = _norm_mod(x_ref[0], g_ref[...], sh_ref[0, 0], sc_ref[0, 0]).astype(BF16)
    n = h.shape[0]
    cosT, sinT, cosK, sinK = cosT_ref[...], sinT_ref[...], cosK_ref[...], sinK_ref[...]
    ones = _ones_rows(n)
    zT = _dot_t(w1T_ref[...], h)
    o1, o2 = MLA_Q_LORA, MLA_Q_LORA + MLA_KV_LORA
    o3 = o2 + 2 * DIFF_HEADS * DIFF_D

    c_q = zT[:o1]
    c_q = (c_q * _rms_rows(c_q) * gq_ref[...]).astype(BF16)
    qT = jnp.dot(wuqT_ref[...], c_q, preferred_element_type=F32)
    for hd in range(MLA_HEADS):
        r0 = hd * MLA_QK
        rope = _rope_rows(qT[r0 + MLA_NOPE:r0 + MLA_QK], cosT, sinT)
        head = jnp.concatenate([qT[r0:r0 + MLA_NOPE], rope], axis=0)
        qa_ref[hd] = (head * (MLA_SCALE * LOG2E)).astype(BF16)

    c_kvT = zT[o1:o2]
    c_kvT = (c_kvT * _rms_rows(c_kvT) * gkvc_ref[...]).astype(BF16)
    vT = jnp.dot(wuvT_ref[...], c_kvT, preferred_element_type=F32).astype(BF16)
    for hd in range(MLA_HEADS):
        va_ref[hd, 0:MLA_V, :] = vT[hd * MLA_V:(hd + 1) * MLA_V]
        va_ref[hd, MLA_V:, :] = ones

    for j in range(2 * DIFF_HEADS):
        qj = _rope_rows(zT[o2 + j * DIFF_D:o2 + (j + 1) * DIFF_D], cosT, sinT)
        qd_ref[j] = (qj * (DIFF_SCALE * LOG2E)).astype(BF16)
    for hd in range(DIFF_HEADS):
        vd_ref[hd, 0:DIFF_V, :] = zT[o3 + hd * DIFF_V:o3 + (hd + 1) * DIFF_V].astype(BF16)
        vd_ref[hd, DIFF_V:, :] = ones

    c_kv = jnp.dot(h, wckv_ref[...], preferred_element_type=F32)
    c_kv = (c_kv * lax.rsqrt(jnp.mean(c_kv * c_kv, axis=-1, keepdims=True) + NORM_EPS) * gkvr_ref[...]).astype(BF16)
    kr = (jnp.dot(h, wkr_ref[...], preferred_element_type=F32) * cosK
          + jnp.dot(h, wkrr_ref[...], preferred_element_type=F32) * sinK)
    ka = (jnp.dot(c_kv, wuk_ref[...], preferred_element_type=F32)
          + jnp.dot(kr.astype(BF16), place_ref[...], preferred_element_type=F32))
    ka = ka.astype(BF16)
    for hd in range(MLA_HEADS):
        ka_ref[hd] = ka[:, hd * MLA_PAD:hd * MLA_PAD + MLA_QK]

    reps = wdk_ref.shape[1] // LANES
    kd = (jnp.dot(h, wdk_ref[...], preferred_element_type=F32) * _tile_lanes(cosK, reps)
          + jnp.dot(h, wdkr_ref[...], preferred_element_type=F32) * _tile_lanes(sinK, reps))
    kd = kd.astype(BF16)
    for blk in range(kd_ref.shape[0]):
        kd_ref[blk] = kd[:, blk * DIFF_D:(blk + 1) * DIFF_D]


def _rot_cols(w, width):
    d, n = w.shape
    w4 = w.reshape(d, n // width, 4, width // 4)
    return jnp.stack([-w4[:, :, 1], w4[:, :, 0], -w4[:, :, 3], w4[:, :, 2]], axis=2).reshape(d, n)


def _perm_cols(g, width):
    g4 = g.reshape(-1, 4, width // 4)
    return jnp.stack([g4[:, 1], g4[:, 0], g4[:, 3], g4[:, 2]], axis=1).reshape(-1)


def _even_prep(xa, p, mods, tabs):
    B, S, D = xa.shape
    w_in, w_uq, w_ukv = p["w_in"], p["w_uq"], p["w_ukv"]
    o1, o2, o3 = MLA_Q_LORA, MLA_Q_LORA + MLA_KV_LORA, MLA_IN_COLS
    nq = 2 * DIFF_HEADS * DIFF_D
    bf = lambda a: a.astype(BF16)
    w1T = bf(jnp.concatenate([w_in[:, :o2], w_in[:, o3:o3 + nq], w_in[:, o3 + 2 * nq:]], axis=1).T)
    wckv = bf(w_in[:, o1:o2])
    wkr = jnp.pad(w_in[:, o2:o3], ((0, 0), (0, LANES - MLA_ROPE)))
    wkrr = jnp.pad(_rot_cols(w_in[:, o2:o3], MLA_ROPE), ((0, 0), (0, LANES - MLA_ROPE)))
    wdk = w_in[:, o3 + nq:o3 + 2 * nq]
    wdkr = _rot_cols(wdk, DIFF_D)
    uq = w_uq.reshape(MLA_Q_LORA, MLA_HEADS, MLA_NOPE + MLA_ROPE)
    wuqT = bf(uq.reshape(MLA_Q_LORA, -1).T)
    ukv = w_ukv.reshape(MLA_KV_LORA, MLA_HEADS, MLA_NOPE + MLA_V)
    wuvT = bf(ukv[:, :, MLA_NOPE:].reshape(MLA_KV_LORA, -1).T)
    wuk = bf(jnp.pad(ukv[:, :, :MLA_NOPE], ((0, 0), (0, 0), (0, MLA_PAD - MLA_NOPE))).reshape(MLA_KV_LORA, -1))
    src = jnp.arange(LANES)[:, None]
    dst = jnp.arange(MLA_HEADS * MLA_PAD)[None, :]
    place = bf((src < MLA_ROPE) & (dst % MLA_PAD == src + MLA_NOPE))
    gq = jnp.broadcast_to(p["norm_q"].astype(F32)[:, None], (MLA_Q_LORA, TOK_TILE))
    gkvc = jnp.broadcast_to(p["norm_kv"].astype(F32)[:, None], (MLA_KV_LORA, TOK_TILE))
    gkvr = p["norm_kv"].astype(F32).reshape(1, MLA_KV_LORA)
    cosT, sinT, cosK, sinK = tabs
    weights = [w1T, wckv, bf(wkr), bf(wkrr), bf(wdk), bf(wdkr), wuqT, wuvT, wuk, place, gq, gkvc, gkvr]
    nt = S // TOK_TILE
    ctx_tile = nt - 1
    mod_map = lambda b, i: (b, (i == ctx_tile).astype(jnp.int32), 0, 0)
    HA, HD = MLA_HEADS, DIFF_HEADS
    dva = MLA_V + ONES_ROWS
    return pl.pallas_call(
        _even_prep_kernel,
        grid=(B, nt),
        in_specs=[pl.BlockSpec((1, TOK_TILE, D), lambda b, i: (b, i, 0)),
                  _full((1, D)),
                  pl.BlockSpec((1, 1, 1, D), mod_map),
                  pl.BlockSpec((1, 1, 1, D), mod_map)]
                 + [_full(w.shape) for w in weights]
                 + [pl.BlockSpec((MLA_ROPE, TOK_TILE), lambda b, i: (0, i)),
                    pl.BlockSpec((MLA_ROPE, TOK_TILE), lambda b, i: (0, i)),
                    pl.BlockSpec((TOK_TILE, LANES), lambda b, i: (i, 0)),
                    pl.BlockSpec((TOK_TILE, LANES), lambda b, i: (i, 0))],
        out_specs=[pl.BlockSpec((None, HA, MLA_QK, TOK_TILE), lambda b, i: (b, 0, 0, i)),
                   pl.BlockSpec((None, HA, TOK_TILE, MLA_QK), lambda b, i: (b, 0, i, 0)),
                   pl.BlockSpec((None, HA, dva, TOK_TILE), lambda b, i: (b, 0, 0, i)),
                   pl.BlockSpec((None, 2 * HD, DIFF_D, TOK_TILE), lambda b, i: (b, 0, 0, i)),
                   pl.BlockSpec((None, 2 * HD, TOK_TILE, DIFF_D), lambda b, i: (b, 0, i, 0)),
                   pl.BlockSpec((None, HD, dva, TOK_TILE), lambda b, i: (b, 0, 0, i))],
        out_shape=[jax.ShapeDtypeStruct((B, HA, MLA_QK, S), BF16),
                   jax.ShapeDtypeStruct((B, HA, S, MLA_QK), BF16),
                   jax.ShapeDtypeStruct((B, HA, dva, S), BF16),
                   jax.ShapeDtypeStruct((B, 2 * HD, DIFF_D, S), BF16),
                   jax.ShapeDtypeStruct((B, 2 * HD, S, DIFF_D), BF16),
                   jax.ShapeDtypeStruct((B, HD, dva, S), BF16)],
        compiler_params=_cparams(2),
        name="even_qkv_prep",
    )(xa, p["norm_mix"].reshape(1, D).astype(F32), mods["sh_a"], mods["sc_a"], *weights, cosT, sinT, cosK, sinK)


def _odd_prep_kernel(x_ref, g_ref, sh_ref, sc_ref, wqvT_ref, wk_ref, wkr_ref, gqn_ref, gk_ref, gkp_ref, bd_ref,
                     cosT_ref, sinT_ref, cosK_ref, sinK_ref,
                     qc_ref, qd_ref, k_ref, vc_ref, vd_ref):
    h = _norm_mod(x_ref[0], g_ref[...], sh_ref[0, 0], sc_ref[0, 0]).astype(BF16)
    n = h.shape[0]
    cosT, sinT, cosK, sinK = cosT_ref[...], sinT_ref[...], cosK_ref[...], sinK_ref[...]
    ones = _ones_rows(n)
    zT = _dot_t(wqvT_ref[...], h)
    d = HEAD_DIM
    for hd in range(WIN_HEADS):
        qc_ref[hd] = (_rope_rows(zT[hd * d:(hd + 1) * d], cosT, sinT) * (ATTN_SCALE * LOG2E)).astype(BF16)
    o1 = WIN_HEADS * d
    for hd in range(GLB_HEADS):
        q = zT[o1 + hd * d:o1 + (hd + 1) * d]
        q = _rope_rows(q * _rms_rows(q) * gqn_ref[...], cosT, sinT)
        qd_ref[hd] = (q * (ATTN_SCALE * LOG2E)).astype(BF16)
    o2 = o1 + GLB_HEADS * d
    for j in range(WIN_KV_HEADS):
        vc_ref[j, 0:d, :] = zT[o2 + j * d:o2 + (j + 1) * d].astype(BF16)
        vc_ref[j, d:, :] = ones
    o3 = o2 + WIN_KV_HEADS * d
    for j in range(GLB_KV_HEADS):
        vd_ref[j, 0:d, :] = zT[o3 + j * d:o3 + (j + 1) * d].astype(BF16)
        vd_ref[j, d:, :] = ones

    zk = jnp.dot(h, wk_ref[...], preferred_element_type=F32)
    zkr = jnp.dot(h, wkr_ref[...], preferred_element_type=F32)
    wc = WIN_KV_HEADS * d
    kc = zk[:, :wc] * cosK + zkr[:, :wc] * sinK
    z, zr = zk[:, wc:], zkr[:, wc:]
    sq = z * z
    hi = sq.astype(BF16)
    lo = (sq - hi.astype(F32)).astype(BF16)
    mean = (jnp.dot(hi, bd_ref[...], preferred_element_type=F32) + jnp.dot(lo, bd_ref[...], preferred_element_type=F32))
    kd = lax.rsqrt(mean + NORM_EPS) * (z * gk_ref[...] * cosK + zr * gkp_ref[...] * sinK)
    k = jnp.concatenate([kc, kd], axis=1).astype(BF16)
    for j in range(k_ref.shape[0]):
        k_ref[j] = k[:, j * d:(j + 1) * d]


def _odd_prep(xa, p, mods, tabs):
    B, S, D = xa.shape
    w_in = p["w_in"]
    d = HEAD_DIM
    sizes = (WIN_HEADS, WIN_KV_HEADS, WIN_KV_HEADS, GLB_HEADS, GLB_KV_HEADS, GLB_KV_HEADS)
    offs = [0]
    for s in sizes:
        offs.append(offs[-1] + s * d)
    col = lambda j: w_in[:, offs[j]:offs[j + 1]]
    bf = lambda a: a.astype(BF16)
    wqvT = bf(jnp.concatenate([col(0), col(3), col(2), col(5)], axis=1).T)
    wk = jnp.concatenate([col(1), col(4)], axis=1)
    wkr = _rot_cols(wk, d)
    gqn = jnp.broadcast_to(p["q_norm"].astype(F32)[:, None], (d, TOK_TILE))
    gk1 = p["k_norm"].astype(F32)
    gk = jnp.tile(gk1, GLB_KV_HEADS).reshape(1, -1)
    gkp = jnp.tile(_perm_cols(gk1, d), GLB_KV_HEADS).reshape(1, -1)
    wd = GLB_KV_HEADS * d
    lane = jnp.arange(wd)
    bd = bf(jnp.where(lane[:, None] // d == lane[None, :] // d, 1.0 / d, 0.0))
    assert WIN_KV_HEADS * d == LANES and wd == LANES
    cosT, sinT, cosK, sinK = tabs
    weights = [wqvT, bf(wk), bf(wkr), gqn, gk, gkp, bd]
    nt = S // TOK_TILE
    ctx_tile = nt - 1
    mod_map = lambda b, i: (b, (i == ctx_tile).astype(jnp.int32), 0, 0)
    dva = d + ONES_ROWS
    return pl.pallas_call(
        _odd_prep_kernel,
        grid=(B, nt),
        in_specs=[pl.BlockSpec((1, TOK_TILE, D), lambda b, i: (b, i, 0)),
                  _full((1, D)),
                  pl.BlockSpec((1, 1, 1, D), mod_map),
                  pl.BlockSpec((1, 1, 1, D), mod_map)]
                 + [_full(w.shape) for w in weights]
                 + [pl.BlockSpec((d, TOK_TILE), lambda b, i: (0, i)),
                    pl.BlockSpec((d, TOK_TILE), lambda b, i: (0, i)),
                    pl.BlockSpec((TOK_TILE, LANES), lambda b, i: (i, 0)),
                    pl.BlockSpec((TOK_TILE, LANES), lambda b, i: (i, 0))],
        out_specs=[pl.BlockSpec((None, WIN_HEADS, d, TOK_TILE), lambda b, i: (b, 0, 0, i)),
                   pl.BlockSpec((None, GLB_HEADS, d, TOK_TILE), lambda b, i: (b, 0, 0, i)),
                   pl.BlockSpec((None, WIN_KV_HEADS + GLB_KV_HEADS, TOK_TILE, d), lambda b, i: (b, 0, i, 0)),
                   pl.BlockSpec((None, WIN_KV_HEADS, dva, TOK_TILE), lambda b, i: (b, 0, 0, i)),
                   pl.BlockSpec((None, GLB_KV_HEADS, dva, TOK_TILE), lambda b, i: (b, 0, 0, i))],
        out_shape=[jax.ShapeDtypeStruct((B, WIN_HEADS, d, S), BF16),
                   jax.ShapeDtypeStruct((B, GLB_HEADS, d, S), BF16),
                   jax.ShapeDtypeStruct((B, WIN_KV_HEADS + GLB_KV_HEADS, S, d), BF16),
                   jax.ShapeDtypeStruct((B, WIN_KV_HEADS, dva, S), BF16),
                   jax.ShapeDtypeStruct((B, GLB_KV_HEADS, dva, S), BF16)],
        compiler_params=_cparams(2),
        name="odd_qkv_prep",
    )(xa, p["norm_mix"].reshape(1, D).astype(F32), mods["sh_a"], mods["sc_a"], *weights, cosT, sinT, cosK, sinK)


def _aligned(start):
    return start if isinstance(start, int) else pl.multiple_of(start, PV_TILE)


def _attn_step(k_ref, vT_ref, acc_ref, q_next, next_offs, s_next, cur_offs, s_cur, m, cmax):
    m_new = jnp.maximum(m, cmax) if cur_offs else m
    pv = None
    cnext = None
    for i in range(max(len(next_offs), len(cur_offs))):
        rows = slice(i * PV_TILE, (i + 1) * PV_TILE)
        if i < len(next_offs):
            s = jnp.dot(k_ref[pl.ds(_aligned(next_offs[i]), PV_TILE), :], q_next, preferred_element_type=F32)
            s_next[rows, :] = s
            cm = jnp.max(s, axis=0, keepdims=True)
            cnext = cm if cnext is None else jnp.maximum(cnext, cm)
        if i < len(cur_offs):
            p = jnp.exp2(s_cur[rows, :] - m_new).astype(BF16)
            d = jnp.dot(vT_ref[:, pl.ds(_aligned(cur_offs[i]), PV_TILE)], p, preferred_element_type=F32)
            pv = d if pv is None else pv + d
    if cur_offs:
        acc_ref[...] = acc_ref[...] * jnp.exp2(m - m_new) + pv
    return m_new, cnext


def _softmax_init(init_ref, dva, dv, tq):
    m0 = jnp.broadcast_to(init_ref[0, 0:1, 0:1], (1, tq))
    row = lax.broadcasted_iota(jnp.int32, (dva, tq), 0)
    acc0 = jnp.where(row == dv, jnp.broadcast_to(init_ref[0, 1:2, 0:1], (dva, tq)), 0.0)
    return m0, acc0


def _flash_kernel(init_ref, qT_ref, qTn_ref, k_ref, vT_ref, o_ref, s0_ref, s1_ref, acc_ref, cm_ref, *,
                  T, n_ctx, kt, dv):
    qT = qT_ref[...]
    tq = qT.shape[1]
    dva = vT_ref.shape[0]
    pieces = lambda off: [off + kk for kk in range(0, kt, PV_TILE)]
    first = [T + kk for kk in range(0, n_ctx, PV_TILE)] + pieces(0)
    step = functools.partial(_attn_step, k_ref, vT_ref, acc_ref)

    @pl.when(pl.program_id(2) == 0)
    def _():
        _, cm_ref[0:1, :] = step(qT, first, s0_ref, [], None, None, None)

    m, acc0 = _softmax_init(init_ref, dva, dv, tq)
    acc_ref[...] = acc0
    m, cm_b = step(qT, pieces(kt), s1_ref, first, s0_ref, m, cm_ref[0:1, :])

    def body(t, carry):
        m, cm_b = carry
        off = pl.multiple_of(t * (2 * kt), kt)
        m, cm_a = step(qT, pieces(off + 2 * kt), s0_ref, pieces(off + kt), s1_ref, m, cm_b)
        m, cm_b = step(qT, pieces(off + 3 * kt), s1_ref, pieces(off + 2 * kt), s0_ref, m, cm_a)
        return m, cm_b

    m, cm_b = lax.fori_loop(0, T // (2 * kt) - 1, body, (m, cm_b))
    m, cm_ref[0:1, :] = step(qTn_ref[...], first, s0_ref, pieces(T - kt), s1_ref, m, cm_b)
    acc = acc_ref[...]
    o_ref[...] = acc[:dv] / acc[dv:dv + 1]


def _flash(qT, k, vT, init, T, key_of):
    B, Hq, dk, S = qT.shape
    Hv, dva = vT.shape[1], vT.shape[2]
    dv = dva - ONES_ROWS
    gv = Hq // Hv
    kt = min(KEY_TILE, T // 2)
    tq = min(Q_TILE, T)
    assert T % (2 * kt) == 0 and T % tq == 0 and kt % PV_TILE == 0
    n_ctx = S - T
    assert n_ctx % PV_TILE == 0
    kern = functools.partial(_flash_kernel, T=T, n_ctx=n_ctx, kt=kt, dv=dv)
    nq = T // tq
    return pl.pallas_call(
        kern,
        grid=(B, Hq, nq),
        in_specs=[pl.BlockSpec((1, 2, LANES),lambda b, h, i: (h, 0, 0)),
                  pl.BlockSpec((None, None, dk, tq), lambda b, h, i: (b, h, 0, i)),
                  pl.BlockSpec((None, None, dk, tq), lambda b, h, i: (b, h, 0, jnp.minimum(i + 1, nq - 1))),
                  pl.BlockSpec((None, None, S, dk), lambda b, h, i: (b, key_of(h), 0, 0)),
                  pl.BlockSpec((None, None, dva, S), lambda b, h, i: (b, h // gv, 0, 0))],
        out_specs=pl.BlockSpec((None, None, dv, tq), lambda b, h, i: (b, h, 0, i)),
        out_shape=jax.ShapeDtypeStruct((B, Hq, dv, T), F32),
        scratch_shapes=[pltpu.VMEM((kt + n_ctx, tq), F32), pltpu.VMEM((kt, tq), F32),
                        pltpu.VMEM((dva, tq), F32), pltpu.VMEM((8, tq), F32)],
        compiler_params=_cparams(3),
        name="dense_attention",
    )(init, qT, qT, k, vT)


def _ctx_attn_kernel(init_ref, qT_ref, k_ref, vT_ref, o_ref, *, dv):
    qT = qT_ref[...]
    m0, acc0 = _softmax_init(init_ref, vT_ref.shape[0], dv, qT.shape[1])
    s = jnp.dot(k_ref[...], qT, preferred_element_type=F32)
    m = jnp.maximum(m0, jnp.max(s, axis=0, keepdims=True))
    acc = acc0 * jnp.exp2(m0 - m) + jnp.dot(vT_ref[...], jnp.exp2(s - m).astype(BF16), preferred_element_type=F32)
    o_ref[...] = acc[:dv] / acc[dv:dv + 1]


def _ctx_attn(qT, k, vT, init, T, key_of):
    B, Hq, dk, S = qT.shape
    Hv, dva = vT.shape[1], vT.shape[2]
    dv = dva - ONES_ROWS
    gv = Hq // Hv
    n = S - T
    blk = T // n
    return pl.pallas_call(
        functools.partial(_ctx_attn_kernel, dv=dv),
        grid=(B, Hq),
        in_specs=[pl.BlockSpec((1, 2, LANES),lambda b, h: (h, 0, 0)),
                  pl.BlockSpec((None, None, dk, n), lambda b, h: (b, h, 0, blk)),
                  pl.BlockSpec((None, None, n, dk), lambda b, h: (b, key_of(h), blk, 0)),
                  pl.BlockSpec((None, None, dva, n), lambda b, h: (b, h // gv, 0, blk))],
        out_specs=pl.BlockSpec((None, None, dv, n), lambda b, h: (b, h, 0, 0)),
        out_shape=jax.ShapeDtypeStruct((B, Hq, dv, n), F32),
        compiler_params=_cparams(2),
        name="context_attention",
    )(init, qT, k, vT)


def _window_kernel(sink_ref, bias_ref, qT_ref, k_ref, kc_ref, vT_ref, o_ref, *, dv):
    i = pl.program_id(2)
    G, _, tq = qT_ref.shape
    qT = jnp.concatenate([qT_ref[g] for g in range(G)], axis=1)
    snk = jnp.concatenate([jnp.broadcast_to(sink_ref[g, 0:1, 0:1], (1, tq)) for g in range(G)], axis=1)
    W = WINDOW
    kband = jnp.concatenate([k_ref[0][tq - W:tq, :], k_ref[1][...], k_ref[2][0:W, :]], axis=0)
    before_first = jnp.where(i == 0, NEG_INF, 0.0)
    after_last = jnp.where(i == pl.num_programs(2) - 1, NEG_INF, 0.0)
    bias = jnp.concatenate([bias_ref[0:W, :] + before_first, bias_ref[W:W + tq, :],
                            bias_ref[W + tq:, :] + after_last], axis=0)
    s_loc = jnp.dot(kband, qT, preferred_element_type=F32) + bias
    s_ctx = jnp.dot(kc_ref[...], qT, preferred_element_type=F32)
    m = jnp.maximum(jnp.maximum(jnp.max(s_loc, axis=0, keepdims=True), jnp.max(s_ctx, axis=0, keepdims=True)), snk)
    p_loc = jnp.exp2(s_loc - m).astype(BF16)
    p_ctx = jnp.exp2(s_ctx - m).astype(BF16)
    vband = jnp.concatenate([vT_ref[0][:, tq - W:tq], vT_ref[1][...], vT_ref[2][:, 0:W]], axis=1)
    acc = (jnp.dot(vband, p_loc, preferred_element_type=F32)
           + jnp.dot(vT_ref[3][...], p_ctx, preferred_element_type=F32))
    o = acc[:dv] / (acc[dv:dv + 1] + jnp.exp2(snk - m))
    for g in range(G):
        o_ref[g] = o[:, g * tq:(g + 1) * tq]


def _window(qT, k, vT, sink, T, key_of):
    B, Hq, dk, S = qT.shape
    Hk, dva = vT.shape[1], vT.shape[2]
    dv = dva - ONES_ROWS
    g = Hq // Hk
    tq = TOK_TILE
    nt = T // tq
    ctx_blk = T // tq
    clip = lambda j: jnp.clip(j, 0, nt - 1)
    kern = functools.partial(_window_kernel, dv=dv)
    kspec = lambda f: pl.BlockSpec((None, None, tq, dk), lambda b, h, i: (b, key_of(h * g), f(i), 0))
    vspec = lambda f: pl.BlockSpec((None, None, dva, tq), lambda b, h, i: (b, h, 0, f(i)))
    r = jnp.arange(tq + 2 * WINDOW)[:, None]
    c = jnp.arange(g * tq)[None, :] % tq
    bias = jnp.where(jnp.abs(r - WINDOW - c) <= WINDOW, 0.0, NEG_INF).astype(F32)

    def body(sink_ref, bias_ref, qT_ref, k0, k1, k2, kc, v0, v1, v2, vc, o_ref):
        kern(sink_ref, bias_ref, qT_ref, (k0, k1, k2), kc, (v0, v1, v2, vc), o_ref)

    return pl.pallas_call(
        body,
        grid=(B, Hk, nt),
        in_specs=[pl.BlockSpec((g, 1, LANES), lambda b, h, i: (h, 0, 0)),
                  _full(bias.shape),
                  pl.BlockSpec((None, g, dk, tq), lambda b, h, i: (b, h, 0, i)),
                  kspec(lambda i: clip(i - 1)), kspec(lambda i: i), kspec(lambda i: clip(i + 1)),
                  kspec(lambda i: ctx_blk),
                  vspec(lambda i: clip(i - 1)), vspec(lambda i: i), vspec(lambda i: clip(i + 1)),
                  vspec(lambda i: ctx_blk)],
        out_specs=pl.BlockSpec((None, g, dv, tq), lambda b, h, i: (b, h, 0, i)),
        out_shape=jax.ShapeDtypeStruct((B, Hq, dv, T), F32),
        compiler_params=_cparams(3),
        name="window_attention",
    )(sink, bias, qT, k, k, k, k, vT, vT, vT, vT)


def _project_out(x_ref, oT, w_ref, gate_ref, y_ref):
    y = jnp.dot(oT.T.astype(BF16), w_ref[...], preferred_element_type=F32)
    y_ref[0] = x_ref[0] + gate_ref[0, 0] * y


def _even_out_kernel(x_ref, oa_ref, od_ref, w_ref, gate_ref, lam_ref, sub_ref, y_ref, *, post_scale):
    lam = lam_ref[...]
    parts = [oa_ref[hd] for hd in range(MLA_HEADS)]
    for hd in range(DIFF_HEADS):
        diff = od_ref[2 * hd] - lam * od_ref[2 * hd + 1]
        parts.append(diff * _rms_rows(diff) * sub_ref[...] * post_scale)
    _project_out(x_ref, jnp.concatenate(parts, axis=0), w_ref, gate_ref, y_ref)


def _odd_out_kernel(x_ref, oc_ref, od_ref, w_ref, gate_ref, y_ref):
    parts = [oc_ref[hd] for hd in range(WIN_HEADS)] + [od_ref[hd] for hd in range(GLB_HEADS)]
    _project_out(x_ref, jnp.concatenate(parts, axis=0), w_ref, gate_ref, y_ref)


def _out_proj(kern, x, heads, w, gate, extra, tile0, n_tiles, kind):
    B, S, D = x.shape
    row_map = lambda b, i: (b, i + tile0, 0)
    return pl.pallas_call(
        kern,
        grid=(B, n_tiles),
        in_specs=[pl.BlockSpec((1, TOK_TILE, D), row_map)]
                 + [pl.BlockSpec((None,) + o.shape[1:3] + (TOK_TILE,), lambda b, i: (b, 0, 0, i)) for o in heads]
                 + [_full(w.shape), pl.BlockSpec((1, 1, 1, D), lambda b, i: (b, kind, 0, 0))]
                 + [_full(e.shape) for e in extra],
        out_specs=pl.BlockSpec((1, TOK_TILE, D), row_map),
        out_shape=jax.ShapeDtypeStruct((B, S, D), F32),
        input_output_aliases={0: 0},
        compiler_params=_cparams(2),
        name="merge_out_proj_residual",
    )(x, *heads, w, gate, *extra)


def _first_argmax(vals):
    best, idx = vals[0], jnp.zeros(vals[0].shape, jnp.int32)
    for j in range(1, len(vals)):
        better = vals[j] > best
        idx = jnp.where(better, j, idx)
        best = jnp.where(better, vals[j], best)
    return idx, best


def _pick(idx, vals):
    out = vals[0]
    for j in range(1, len(vals)):
        out = jnp.where(idx == j, vals[j], out)
    return out


def _route(logits, bias):
    s = 1.0 / (1.0 + jnp.exp(-logits))
    sel = s + bias
    srow = [s[e:e + 1] for e in range(N_EXPERTS)]
    row = [sel[e:e + 1] for e in range(N_EXPERTS)]
    scores = []
    for g in range(N_GROUPS):
        a, b, c, d = row[4 * g:4 * g + 4]
        hi1, lo1, hi2, lo2 = jnp.maximum(a, b), jnp.minimum(a, b), jnp.maximum(c, d), jnp.minimum(c, d)
        top1 = jnp.maximum(hi1, hi2)
        top2 = jnp.maximum(jnp.maximum(lo1, lo2), jnp.minimum(hi1, hi2))
        scores.append(top1 + top2)
    gi, _ = _first_argmax(scores)
    v = [_pick(gi, [row[4 * g + j] for g in range(N_GROUPS)]) for j in range(EXPERTS_PER_GROUP)]
    sv = [_pick(gi, [srow[4 * g + j] for g in range(N_GROUPS)]) for j in range(EXPERTS_PER_GROUP)]
    i1, _ = _first_argmax(v)
    i2, _ = _first_argmax([jnp.where(i1 == j, -jnp.inf, v[j]) for j in range(EXPERTS_PER_GROUP)])
    w1, w2 = _pick(i1, sv), _pick(i2, sv)
    tot = w1 + w2
    w1, w2 = w1 / tot, w2 / tot
    rows = []
    for e in range(N_EXPERTS):
        g, j = divmod(e, EXPERTS_PER_GROUP)
        in_g = gi == g
        rows.append(jnp.where(in_g & (i1 == j), w1, 0.0) + jnp.where(in_g & (i2 == j), w2, 0.0))
    return jnp.concatenate(rows, axis=0), gi


def _split(x, terms):
    out = []
    for _ in range(terms):
        part = x.astype(BF16).astype(F32)
        out.append(part)
        x = x - part
    return out


def _to_column(row):
    n = row.shape[1]
    return jnp.concatenate([row, jnp.zeros((LANES - 1, n), F32)], axis=0).T[:, 0:1]


def _moe_kernel(x_ref, g_ref, sh_ref, sc_ref, gate_ref, wrT_ref, br_ref, before_ref, wg_ref, wu_ref, wd_ref, fin_ref,
                y_ref, hs_scr, cs_scr, ys_scr, q_scr, info_ref, *, final):
    step = pl.program_id(2)
    per_step = wg_ref.shape[0]
    tm = x_ref.shape[1]
    R = hs_scr.shape[0]

    @pl.when(step == 0)
    def _():
        h = _norm_mod(x_ref[0], g_ref[...], sh_ref[0, 0], sc_ref[0, 0]).astype(BF16)
        comb, gi = _route(_dot_t(wrT_ref[...], h), br_ref[...])
        sel = jnp.concatenate([(gi == g).astype(F32) for g in range(N_GROUPS)]
                              + [jnp.zeros((BF16_ROWS - N_GROUPS, tm), F32)], axis=0)
        rank = jnp.dot(sel.astype(BF16), before_ref[...], preferred_element_type=F32)
        cnt = jnp.sum(sel, axis=1, keepdims=True)
        seg = jnp.ceil(cnt * (1.0 / BF16_ROWS)) * BF16_ROWS
        pos = jnp.zeros((1, tm), F32)
        start = jnp.zeros((1, 1), F32)
        for g in range(N_GROUPS):
            pos = pos + sel[g:g + 1] * (start + rank[g:g + 1])
            info_ref[g] = start[0, 0].astype(jnp.int32)
            info_ref[N_GROUPS + g] = jnp.ceil(cnt[g:g + 1] * (1.0 / MOE_CHUNK))[0, 0].astype(jnp.int32)
            start = start + seg[g:g + 1]
        slot = lax.broadcasted_iota(jnp.int32, (R, tm), 0).astype(F32)
        P = jnp.where(slot == pos, 1.0, 0.0).astype(BF16)
        hs_scr[...] = jnp.dot(P, h, preferred_element_type=F32).astype(BF16)
        parts = _split(comb, COMB_TERMS)
        combT = jnp.concatenate(parts + [jnp.zeros((LANES - COMB_TERMS * N_EXPERTS, tm), F32)], axis=0).T
        cs_scr[...] = jnp.dot(P, combT.astype(BF16), preferred_element_type=F32)
        lane = lax.broadcasted_iota(jnp.int32, (tm, R), 1).astype(F32)
        q_scr[...] = jnp.where(lane == _to_column(pos), 1.0, 0.0).astype(BF16)
        ys_scr[...] = jnp.zeros_like(ys_scr)

    steps_per_group = EXPERTS_PER_GROUP // per_step
    grp = lax.shift_right_logical(step, steps_per_group.bit_length() - 1)
    start = info_ref[grp]
    n_chunks = info_ref[N_GROUPS + grp]

    def experts(r0, rows):
        h = hs_scr[pl.ds(r0, rows), :]
        cs = cs_scr[pl.ds(r0, rows), :]
        lane = lax.broadcasted_iota(jnp.int32, (rows, LANES), 1) & (N_EXPERTS - 1)
        total = None
        for j in range(per_step):
            a = jnp.dot(h, wg_ref[j], preferred_element_type=F32)
            u = jnp.dot(h, wu_ref[j], preferred_element_type=F32)
            act = (a * (1.0 / (1.0 + jnp.exp(-a)))) * u
            y = jnp.dot(act.astype(BF16), wd_ref[j], preferred_element_type=F32)
            w = jnp.sum(jnp.where(lane == step * per_step + j, cs, 0.0), axis=1, keepdims=True)
            total = w * y if total is None else total + w * y
        ys_scr[pl.ds(r0, rows), :] += total

    def pair(c, carry):
        experts(pl.multiple_of(start + c * (2 * MOE_CHUNK), BF16_ROWS), 2 * MOE_CHUNK)
        return carry

    lax.fori_loop(0, lax.shift_right_logical(n_chunks, 1), pair, 0)

    @pl.when((n_chunks & 1) == 1)
    def _():
        experts(pl.multiple_of(start + (n_chunks - 1) * MOE_CHUNK, BF16_ROWS), MOE_CHUNK)

    @pl.when(step == pl.num_programs(2) - 1)
    def _():
        q = q_scr[...]
        out = sum(jnp.dot(q, part.astype(BF16), preferred_element_type=F32) for part in _split(ys_scr[...], 2))
        y = x_ref[0] + gate_ref[0, 0] * out
        if final:
            y = (y * lax.rsqrt(jnp.mean(y * y, axis=-1, keepdims=True) + NORM_EPS)) * fin_ref[...]
        y_ref[0] = y


def _moe(x, g, sh, sc, gate, wrT, br, wg, wu, wd, fin, tm, tile0, n_tiles, kind, final):
    B, S, D = x.shape
    E, _, F = wg.shape
    R = -(-(tm + BF16_ROWS * N_GROUPS + MOE_CHUNK) // MXU_DIM) * MXU_DIM
    idx = jnp.arange(tm)
    before = (idx[:, None] < idx[None, :]).astype(BF16)
    mod_map = lambda b, i, e: (b, kind, 0, 0)
    row_map = lambda b, i, e: (b, i + tile0, 0)
    assert not (final and tile0)
    return pl.pallas_call(
        functools.partial(_moe_kernel, final=final),
        grid=(B, n_tiles, E // MOE_STEP_EXPERTS),
        in_specs=[pl.BlockSpec((1, tm, D), row_map),
                  pl.BlockSpec((1, D), lambda b, i, e: (0, 0)),
                  pl.BlockSpec((1, 1, 1, D), mod_map),
                  pl.BlockSpec((1, 1, 1, D), mod_map),
                  pl.BlockSpec((1, 1, 1, D), mod_map),
                  pl.BlockSpec((E, D), lambda b, i, e: (0, 0)),
                  pl.BlockSpec((E, 1), lambda b, i, e: (0, 0)),
                  pl.BlockSpec((tm, tm), lambda b, i, e: (0, 0)),
                  pl.BlockSpec((MOE_STEP_EXPERTS, D, F), lambda b, i, e: (e, 0, 0)),
                  pl.BlockSpec((MOE_STEP_EXPERTS, D, F), lambda b, i, e: (e, 0, 0)),
                  pl.BlockSpec((MOE_STEP_EXPERTS, F, D), lambda b, i, e: (e, 0, 0)),
                  pl.BlockSpec((1, D), lambda b, i, e: (0, 0))],
        out_specs=pl.BlockSpec((1, tm, D), row_map),
        out_shape=jax.ShapeDtypeStruct((B, n_tiles * tm if final else S, D), F32),
        scratch_shapes=[pltpu.VMEM((R, D), BF16), pltpu.VMEM((R, LANES), F32), pltpu.VMEM((R, D), F32),
                        pltpu.VMEM((tm, R), BF16), pltpu.SMEM((2 * N_GROUPS,), jnp.int32)],
        input_output_aliases={} if final else {0: 0},
        compiler_params=_cparams(3),
        name="moe_experts",
    )(x, g.reshape(1, D).astype(F32), sh, sc, gate, wrT, br, before, wg, wu, wd, fin.reshape(1, D).astype(F32))


def _axial_tables(T, n_ctx, rot_dim):
    half = rot_dim // 2
    inv_freq = ROPE_THETA ** (-jnp.arange(0, half, 2, dtype=F32) / half)
    rows = T // GRID_W
    row = jnp.broadcast_to(jnp.arange(rows, dtype=F32)[:, None], (rows, GRID_W)).reshape(-1)
    col = jnp.broadcast_to(jnp.arange(GRID_W, dtype=F32)[None, :], (rows, GRID_W)).reshape(-1)

    def ang(pos):
        a = pos[:, None] * inv_freq[None, :]
        return jnp.concatenate([a, a], axis=-1)

    a = jnp.concatenate([ang(row), ang(col)], axis=-1)
    cos = jnp.concatenate([jnp.cos(a), jnp.ones((n_ctx, rot_dim), F32)], axis=0)
    sin = jnp.concatenate([jnp.sin(a), jnp.zeros((n_ctx, rot_dim), F32)], axis=0)
    reps = LANES // rot_dim
    return cos.T, sin.T, jnp.tile(cos, (1, reps)), jnp.tile(sin, (1, reps))


def _plain_init(n_heads):
    return jnp.broadcast_to(jnp.array([NEG_INF, 0.0], F32)[None, :, None], (n_heads, 2, LANES))


def _sink_init(sink):
    s = sink.astype(F32) * LOG2E
    return jnp.broadcast_to(jnp.stack([s, jnp.ones_like(s)], axis=1)[:, :, None], (s.shape[0], 2, LANES))


def _even_mixer(xa, p, mods, T, with_ctx, lam_init, tabs):
    B, S, D = xa.shape
    qa, ka, va, qd, kd, vd = _even_prep(xa, p, mods, tabs)
    own = lambda h: h
    init_a, init_d = _plain_init(MLA_HEADS), _plain_init(2 * DIFF_HEADS)
    lam = p["lam"].astype(F32)
    lam_full = (jnp.exp(jnp.sum(lam[0] * lam[1])) - jnp.exp(jnp.sum(lam[2] * lam[3])) + lam_init).reshape(1, 1)
    sub = jnp.broadcast_to(p["subln"].astype(F32)[:, None], (DIFF_V, TOK_TILE))
    kern = functools.partial(_even_out_kernel, post_scale=1.0 - lam_init)
    oa = _flash(qa, ka, va, init_a, T, own)
    od = _flash(qd, kd, vd, init_d, T, own)
    xa = _out_proj(kern, xa, [oa, od], p["w_out"], mods["g_a"], [lam_full, sub], 0, T // TOK_TILE, 0)
    if with_ctx:
        oa = _ctx_attn(qa, ka, va, init_a, T, own)
        od = _ctx_attn(qd, kd, vd, init_d, T, own)
        xa = _out_proj(kern, xa, [oa, od], p["w_out"], mods["g_a"], [lam_full, sub], T // TOK_TILE, 1, 1)
    return xa


def _odd_mixer(xa, p, mods, T, with_ctx, tabs):
    B, S, D = xa.shape
    qc, qd, k, vc, vd = _odd_prep(xa, p, mods, tabs)
    gw, gd = WIN_HEADS // WIN_KV_HEADS, GLB_HEADS // GLB_KV_HEADS
    key_c = lambda h: h // gw
    key_d = lambda h: WIN_KV_HEADS + h // gd
    init_d = _plain_init(GLB_HEADS)
    sink = p["sink"].astype(F32) * LOG2E
    sink_w = jnp.broadcast_to(sink[:, None, None], (WIN_HEADS, 1, LANES))
    od = _flash(qd, k, vd, init_d, T, key_d)
    oc = _window(qc, k, vc, sink_w, T, key_c)
    xa = _out_proj(_odd_out_kernel, xa, [oc, od], p["w_out"], mods["g_a"], [], 0, T // TOK_TILE, 0)
    if with_ctx:
        oc = _ctx_attn(qc, k, vc, _sink_init(p["sink"]), T, key_c)
        od = _ctx_attn(qd, k, vd, init_d, T, key_d)
        xa = _out_proj(_odd_out_kernel, xa, [oc, od], p["w_out"], mods["g_a"], [], T // TOK_TILE, 1, 1)
    return xa


def kernel(x, c, ctx, c_ctx, w_mod, b_mod, norm_mix, norm_ffn, even_w_in, even_norm_q, even_norm_kv, even_w_uq, even_w_ukv, even_lambda, even_subln, even_w_out, odd_w_in, odd_sink, odd_q_norm, odd_k_norm, odd_w_out, w_router, b_router, w_gate, w_up, w_down, norm_final):
    B, T, D = x.shape
    n_ctx = ctx.shape[1]
    depth = w_mod.shape[0]
    assert n_ctx == TOK_TILE == PV_TILE and T % MOE_TILE == 0 and B <= 7
    assert MLA_ROPE == DIFF_D
    S = T + n_ctx
    tabs32 = _axial_tables(T, n_ctx, MLA_ROPE)
    tabs64 = _axial_tables(T, n_ctx, HEAD_DIM)

    cond = jnp.zeros((8, D), F32).at[:B].set(c.astype(F32)).at[B].set(c_ctx.astype(F32))
    mod_all = _mod_vectors(cond, w_mod, b_mod)
    wrT = w_router.astype(BF16).T
    br = b_router.astype(F32).reshape(N_EXPERTS, 1)

    xa = jnp.concatenate([x, ctx], axis=1).astype(F32)
    for l in range(depth):
        with_ctx = l < depth - 1
        i = l // 2
        lat = mod_all[l, :B].reshape(B, 6, D)
        cx = jnp.broadcast_to(mod_all[l, B].reshape(1, 6, D), (B, 6, D))
        both = jnp.stack([lat, cx], axis=1)
        names = ("sh_a", "sc_a", "g_a", "sh_f", "sc_f", "g_f")
        mods = {n: both[:, :, j:j + 1, :] for j, n in enumerate(names)}
        if l % 2 == 0:
            lam_init = 0.8 - 0.6 * math.exp(-0.3 * l)
            p = dict(norm_mix=norm_mix[l], w_in=even_w_in[i], norm_q=even_norm_q[i], norm_kv=even_norm_kv[i],
                     w_uq=even_w_uq[i], w_ukv=even_w_ukv[i], lam=even_lambda[i], subln=even_subln[i],
                     w_out=even_w_out[i].astype(BF16))
            xa = _even_mixer(xa, p, mods, T, with_ctx, lam_init, tabs32)
        else:
            p = dict(norm_mix=norm_mix[l], w_in=odd_w_in[i], sink=odd_sink[i], q_norm=odd_q_norm[i],
                     k_norm=odd_k_norm[i], w_out=odd_w_out[i].astype(BF16))
            xa = _odd_mixer(xa, p, mods, T, with_ctx, tabs64)
        wg, wu, wd = w_gate[l].astype(BF16), w_up[l].astype(BF16), w_down[l].astype(BF16)
        ffn = (norm_ffn[l], mods["sh_f"], mods["sc_f"], mods["g_f"], wrT, br, wg, wu, wd, norm_final)
        xa = _moe(xa, *ffn, MOE_TILE, 0, T // MOE_TILE, 0, final=not with_ctx)
        if with_ctx:
            xa = _moe(xa, *ffn, TOK_TILE, T // TOK_TILE, 1, 1, final=False)
    return xa
```

```python
import functools
import math

import jax
import jax.numpy as jnp
from jax import lax
from jax.experimental import pallas as pl
from jax.experimental.pallas import tpu as pltpu

F32 = jnp.float32
BF16 = jnp.bfloat16
LOG2E = 1.4426950408889634

GRID_W = 64
ROPE_THETA = 10000.0
NORM_EPS = 1e-6
NEG_INF = -1e30

MLA_HEADS = 8
MLA_Q_LORA = 384
MLA_KV_LORA = 256
MLA_NOPE = 64
MLA_ROPE = 32
MLA_V = 64
MLA_SCALE = (MLA_NOPE + MLA_ROPE) ** -0.5
MLA_IN_COLS = MLA_Q_LORA + MLA_KV_LORA + MLA_ROPE
MLA_QK = MLA_NOPE + MLA_ROPE
MLA_PAD = 128

DIFF_HEADS = 8
DIFF_D = 32
DIFF_V = 2 * DIFF_D
DIFF_SCALE = DIFF_D ** -0.5

HEAD_DIM = 64
WIN_HEADS = 8
WIN_KV_HEADS = 2
WINDOW = 128
GLB_HEADS = 8
GLB_KV_HEADS = 2
ATTN_SCALE = HEAD_DIM ** -0.5

N_EXPERTS = 16
N_GROUPS = 4
EXPERTS_PER_GROUP = N_EXPERTS // N_GROUPS

TOK_TILE = 256
Q_TILE = 1024
KEY_TILE = 2048
PV_TILE = 256
MOE_TILE = 1024
MOE_CHUNK = 128
MOE_STEP_EXPERTS = 2
WEIGHT_SLOTS = 3
COMB_TERMS = 3
ONES_ROWS = 16
LANES = 128
BF16_ROWS = 16
MXU_DIM = 256
VMEM_LIMIT = 56 * 1024 * 1024


def _cparams(n_axes):
    return pltpu.CompilerParams(dimension_semantics=("arbitrary",) * n_axes,
                                vmem_limit_bytes=VMEM_LIMIT)


def _full(shape):
    return pl.BlockSpec(shape, lambda *_: (0,) * len(shape))


def _mod_kernel(a_ref, w_ref, b_ref, o_ref):
    a = a_ref[...]
    a = a * (1.0 / (1.0 + jnp.exp(-a)))
    o_ref[0] = jnp.dot(a.astype(BF16), w_ref[0].astype(BF16), preferred_element_type=F32) + b_ref[0]


def _mod_vectors(cond, w_mod, b_mod):
    L, D, N = w_mod.shape
    tn = 1536
    return pl.pallas_call(
        _mod_kernel,
        grid=(L, N // tn),
        in_specs=[pl.BlockSpec((8, D), lambda l, j: (0, 0)),
                  pl.BlockSpec((1, D, tn), lambda l, j: (l, 0, j)),
                  pl.BlockSpec((1, 1, tn), lambda l, j: (l, 0, j))],
        out_specs=pl.BlockSpec((1, 8, tn), lambda l, j: (l, 0, j)),
        out_shape=jax.ShapeDtypeStruct((L, 8, N), F32),
        compiler_params=_cparams(2),
        name="mod_vectors",
    )(cond, w_mod, b_mod.reshape(L, 1, N))


def _norm_mod(x, g, sh, sc):
    y = x * lax.rsqrt(jnp.mean(x * x, axis=-1, keepdims=True) + NORM_EPS)
    return (y * g) * (1.0 + sc) + sh


def _dot_t(w, h):
    return lax.dot_general(w, h, (((1,), (1,)), ((), ())), preferred_element_type=F32)


def _rms_rows(x):
    return lax.rsqrt(jnp.mean(x * x, axis=0, keepdims=True) + NORM_EPS)


def _rope_rows(x, cosT, sinT):
    q = x.shape[0] // 4
    rot = jnp.concatenate([-x[q:2 * q], x[0:q], -x[3 * q:4 * q], x[2 * q:3 * q]], axis=0)
    return x * cosT + rot * sinT


def _ones_rows(n):
    row = lax.broadcasted_iota(jnp.int32, (ONES_ROWS, n), 0)
    return jnp.where(row == 0, 1.0, 0.0).astype(BF16)


def _tile_lanes(x, reps):
    return x if reps == 1 else jnp.concatenate([x] * reps, axis=1)


def _even_prep_kernel(x_ref, g_ref, sh_ref, sc_ref, w1T_ref, wckv_ref, wkr_ref, wkrr_ref, wdk_ref, wdkr_ref,
                      wuqT_ref, wuvT_ref, wuk_ref, place_ref, gq_ref, gkvc_ref, gkvr_ref, cosT_ref, sinT_ref,
                      cosK_ref, sinK_ref,
                      qa_ref, ka_ref, va_ref, qd_ref, kd_ref, vd_ref):
    h = _norm_mod(x_ref[0], g_ref[...], sh_ref[0, 0], sc_ref[0, 0]).astype(BF16)
    n = h.shape[0]
    cosT, sinT, cosK, sinK = cosT_ref[...], sinT_ref[...], cosK_ref[...], sinK_ref[...]
    ones = _ones_rows(n)
    zT = _dot_t(w1T_ref[...], h)
    o1, o2 = MLA_Q_LORA, MLA_Q_LORA + MLA_KV_LORA
    o3 = o2 + 2 * DIFF_HEADS * DIFF_D

    c_q = zT[:o1]
    c_q = (c_q * _rms_rows(c_q) * gq_ref[...]).astype(BF16)
    qT = jnp.dot(wuqT_ref[...], c_q, preferred_element_type=F32)
    for hd in range(MLA_HEADS):
        r0 = hd * MLA_QK
        rope = _rope_rows(qT[r0 + MLA_NOPE:r0 + MLA_QK], cosT, sinT)
        head = jnp.concatenate([qT[r0:r0 + MLA_NOPE], rope], axis=0)
        qa_ref[hd] = (head * (MLA_SCALE * LOG2E)).astype(BF16)

    c_kvT = zT[o1:o2]
    c_kvT = (c_kvT * _rms_rows(c_kvT) * gkvc_ref[...]).astype(BF16)
    vT = jnp.dot(wuvT_ref[...], c_kvT, preferred_element_type=F32).astype(BF16)
    for hd in range(MLA_HEADS):
        va_ref[hd, 0:MLA_V, :] = vT[hd * MLA_V:(hd + 1) * MLA_V]
        va_ref[hd, MLA_V:, :] = ones

    for j in range(2 * DIFF_HEADS):
        qj = _rope_rows(zT[o2 + j * DIFF_D:o2 + (j + 1) * DIFF_D], cosT, sinT)
        qd_ref[j] = (qj * (DIFF_SCALE * LOG2E)).astype(BF16)
    for hd in range(DIFF_HEADS):
        vd_ref[hd, 0:DIFF_V, :] = zT[o3 + hd * DIFF_V:o3 + (hd + 1) * DIFF_V].astype(BF16)
        vd_ref[hd, DIFF_V:, :] = ones

    c_kv = jnp.dot(h, wckv_ref[...], preferred_element_type=F32)
    c_kv = (c_kv * lax.rsqrt(jnp.mean(c_kv * c_kv, axis=-1, keepdims=True) + NORM_EPS) * gkvr_ref[...]).astype(BF16)
    kr = (jnp.dot(h, wkr_ref[...], preferred_element_type=F32) * cosK
          + jnp.dot(h, wkrr_ref[...], preferred_element_type=F32) * sinK)
    ka = (jnp.dot(c_kv, wuk_ref[...], preferred_element_type=F32)
          + jnp.dot(kr.astype(BF16), place_ref[...], preferred_element_type=F32))
    ka = ka.astype(BF16)
    for hd in range(MLA_HEADS):
        ka_ref[hd] = ka[:, hd * MLA_PAD:hd * MLA_PAD + MLA_QK]

    reps = wdk_ref.shape[1] // LANES
    kd = (jnp.dot(h, wdk_ref[...], preferred_element_type=F32) * _tile_lanes(cosK, reps)
          + jnp.dot(h, wdkr_ref[...], preferred_element_type=F32) * _tile_lanes(sinK, reps))
    kd = kd.astype(BF16)
    for blk in range(kd_ref.shape[0]):
        kd_ref[blk] = kd[:, blk * DIFF_D:(blk + 1) * DIFF_D]


def _rot_cols(w, width):
    d, n = w.shape
    w4 = w.reshape(d, n // width, 4, width // 4)
    return jnp.stack([-w4[:, :, 1], w4[:, :, 0], -w4[:, :, 3], w4[:, :, 2]], axis=2).reshape(d, n)


def _perm_cols(g, width):
    g4 = g.reshape(-1, 4, width // 4)
    return jnp.stack([g4[:, 1], g4[:, 0], g4[:, 3], g4[:, 2]], axis=1).reshape(-1)


def _even_prep(xa, p, mods, tabs):
    B, S, D = xa.shape
    w_in, w_uq, w_ukv = p["w_in"], p["w_uq"], p["w_ukv"]
    o1, o2, o3 = MLA_Q_LORA, MLA_Q_LORA + MLA_KV_LORA, MLA_IN_COLS
    nq = 2 * DIFF_HEADS * DIFF_D
    bf = lambda a: a.astype(BF16)
    w1T = bf(jnp.concatenate([w_in[:, :o2], w_in[:, o3:o3 + nq], w_in[:, o3 + 2 * nq:]], axis=1).T)
    wckv = bf(w_in[:, o1:o2])
    wkr = jnp.pad(w_in[:, o2:o3], ((0, 0), (0, LANES - MLA_ROPE)))
    wkrr = jnp.pad(_rot_cols(w_in[:, o2:o3], MLA_ROPE), ((0, 0), (0, LANES - MLA_ROPE)))
    wdk = w_in[:, o3 + nq:o3 + 2 * nq]
    wdkr = _rot_cols(wdk, DIFF_D)
    uq = w_uq.reshape(MLA_Q_LORA, MLA_HEADS, MLA_NOPE + MLA_ROPE)
    wuqT = bf(uq.reshape(MLA_Q_LORA, -1).T)
    ukv = w_ukv.reshape(MLA_KV_LORA, MLA_HEADS, MLA_NOPE + MLA_V)
    wuvT = bf(ukv[:, :, MLA_NOPE:].reshape(MLA_KV_LORA, -1).T)
    wuk = bf(jnp.pad(ukv[:, :, :MLA_NOPE], ((0, 0), (0, 0), (0, MLA_PAD - MLA_NOPE))).reshape(MLA_KV_LORA, -1))
    src = jnp.arange(LANES)[:, None]
    dst = jnp.arange(MLA_HEADS * MLA_PAD)[None, :]
    place = bf((src < MLA_ROPE) & (dst % MLA_PAD == src + MLA_NOPE))
    gq = jnp.broadcast_to(p["norm_q"].astype(F32)[:, None], (MLA_Q_LORA, TOK_TILE))
    gkvc = jnp.broadcast_to(p["norm_kv"].astype(F32)[:, None], (MLA_KV_LORA, TOK_TILE))
    gkvr = p["norm_kv"].astype(F32).reshape(1, MLA_KV_LORA)
    cosT, sinT, cosK, sinK = tabs
    weights = [w1T, wckv, bf(wkr), bf(wkrr), bf(wdk), bf(wdkr), wuqT, wuvT, wuk, place, gq, gkvc, gkvr]
    nt = S // TOK_TILE
    ctx_tile = nt - 1
    mod_map = lambda b, i: (b, (i == ctx_tile).astype(jnp.int32), 0, 0)
    HA, HD = MLA_HEADS, DIFF_HEADS
    dva = MLA_V + ONES_ROWS
    return pl.pallas_call(
        _even_prep_kernel,
        grid=(B, nt),
        in_specs=[pl.BlockSpec((1, TOK_TILE, D), lambda b, i: (b, i, 0)),
                  _full((1, D)),
                  pl.BlockSpec((1, 1, 1, D), mod_map),
                  pl.BlockSpec((1, 1, 1, D), mod_map)]
                 + [_full(w.shape) for w in weights]
                 + [pl.BlockSpec((MLA_ROPE, TOK_TILE), lambda b, i: (0, i)),
                    pl.BlockSpec((MLA_ROPE, TOK_TILE), lambda b, i: (0, i)),
                    pl.BlockSpec((TOK_TILE, LANES), lambda b, i: (i, 0)),
                    pl.BlockSpec((TOK_TILE, LANES), lambda b, i: (i, 0))],
        out_specs=[pl.BlockSpec((None, HA, MLA_QK, TOK_TILE), lambda b, i: (b, 0, 0, i)),
                   pl.BlockSpec((None, HA, TOK_TILE, MLA_QK), lambda b, i: (b, 0, i, 0)),
                   pl.BlockSpec((None, HA, dva, TOK_TILE), lambda b, i: (b, 0, 0, i)),
                   pl.BlockSpec((None, 2 * HD, DIFF_D, TOK_TILE), lambda b, i: (b, 0, 0, i)),
                   pl.BlockSpec((None, 2 * HD, TOK_TILE, DIFF_D), lambda b, i: (b, 0, i, 0)),
                   pl.BlockSpec((None, HD, dva, TOK_TILE), lambda b, i: (b, 0, 0, i))],
        out_shape=[jax.ShapeDtypeStruct((B, HA, MLA_QK, S), BF16),
                   jax.ShapeDtypeStruct((B, HA, S, MLA_QK), BF16),
                   jax.ShapeDtypeStruct((B, HA, dva, S), BF16),
                   jax.ShapeDtypeStruct((B, 2 * HD, DIFF_D, S), BF16),
                   jax.ShapeDtypeStruct((B, 2 * HD, S, DIFF_D), BF16),
                   jax.ShapeDtypeStruct((B, HD, dva, S), BF16)],
        compiler_params=_cparams(2),
        name="even_qkv_prep",
    )(xa, p["norm_mix"].reshape(1, D).astype(F32), mods["sh_a"], mods["sc_a"], *weights, cosT, sinT, cosK, sinK)


def _odd_prep_kernel(x_ref, g_ref, sh_ref, sc_ref, wqvT_ref, wk_ref, wkr_ref, gqn_ref, gk_ref, gkp_ref, bd_ref,
                     cosT_ref, sinT_ref, cosK_ref, sinK_ref,
                     qc_ref, qd_ref, k_ref, vc_ref, vd_ref):
    h = _norm_mod(x_ref[0], g_ref[...], sh_ref[0, 0], sc_ref[0, 0]).astype(BF16)
    n = h.shape[0]
    cosT, sinT, cosK, sinK = cosT_ref[...], sinT_ref[...], cosK_ref[...], sinK_ref[...]
    ones = _ones_rows(n)
    zT = _dot_t(wqvT_ref[...], h)
    d = HEAD_DIM
    for hd in range(WIN_HEADS):
        qc_ref[hd] = (_rope_rows(zT[hd * d:(hd + 1) * d], cosT, sinT) * (ATTN_SCALE * LOG2E)).astype(BF16)
    o1 = WIN_HEADS * d
    for hd in range(GLB_HEADS):
        q = zT[o1 + hd * d:o1 + (hd + 1) * d]
        q = _rope_rows(q * _rms_rows(q) * gqn_ref[...], cosT, sinT)
        qd_ref[hd] = (q * (ATTN_SCALE * LOG2E)).astype(BF16)
    o2 = o1 + GLB_HEADS * d
    for j in range(WIN_KV_HEADS):
        vc_ref[j, 0:d, :] = zT[o2 + j * d:o2 + (j + 1) * d].astype(BF16)
        vc_ref[j, d:, :] = ones
    o3 = o2 + WIN_KV_HEADS * d
    for j in range(GLB_KV_HEADS):
        vd_ref[j, 0:d, :] = zT[o3 + j * d:o3 + (j + 1) * d].astype(BF16)
        vd_ref[j, d:, :] = ones

    zk = jnp.dot(h, wk_ref[...], preferred_element_type=F32)
    zkr = jnp.dot(h, wkr_ref[...], preferred_element_type=F32)
    wc = WIN_KV_HEADS * d
    kc = zk[:, :wc] * cosK + zkr[:, :wc] * sinK
    z, zr = zk[:, wc:], zkr[:, wc:]
    sq = z * z
    hi = sq.astype(BF16)
    lo = (sq - hi.astype(F32)).astype(BF16)
    mean = (jnp.dot(hi, bd_ref[...], preferred_element_type=F32) + jnp.dot(lo, bd_ref[...], preferred_element_type=F32))
    kd = lax.rsqrt(mean + NORM_EPS) * (z * gk_ref[...] * cosK + zr * gkp_ref[...] * sinK)
    k = jnp.concatenate([kc, kd], axis=1).astype(BF16)
    for j in range(k_ref.shape[0]):
        k_ref[j] = k[:, j * d:(j + 1) * d]


def _odd_prep(xa, p, mods, tabs):
    B, S, D = xa.shape
    w_in = p["w_in"]
    d = HEAD_DIM
    sizes = (WIN_HEADS, WIN_KV_HEADS, WIN_KV_HEADS, GLB_HEADS, GLB_KV_HEADS, GLB_KV_HEADS)
    offs = [0]
    for s in sizes:
        offs.append(offs[-1] + s * d)
    col = lambda j: w_in[:, offs[j]:offs[j + 1]]
    bf = lambda a: a.astype(BF16)
    wqvT = bf(jnp.concatenate([col(0), col(3), col(2), col(5)], axis=1).T)
    wk = jnp.concatenate([col(1), col(4)], axis=1)
    wkr = _rot_cols(wk, d)
    gqn = jnp.broadcast_to(p["q_norm"].astype(F32)[:, None], (d, TOK_TILE))
    gk1 = p["k_norm"].astype(F32)
    gk = jnp.tile(gk1, GLB_KV_HEADS).reshape(1, -1)
    gkp = jnp.tile(_perm_cols(gk1, d), GLB_KV_HEADS).reshape(1, -1)
    wd = GLB_KV_HEADS * d
    lane = jnp.arange(wd)
    bd = bf(jnp.where(lane[:, None] // d == lane[None, :] // d, 1.0 / d, 0.0))
    assert WIN_KV_HEADS * d == LANES and wd == LANES
    cosT, sinT, cosK, sinK = tabs
    weights = [wqvT, bf(wk), bf(wkr), gqn, gk, gkp, bd]
    nt = S // TOK_TILE
    ctx_tile = nt - 1
    mod_map = lambda b, i: (b, (i == ctx_tile).astype(jnp.int32), 0, 0)
    dva = d + ONES_ROWS
    return pl.pallas_call(
        _odd_prep_kernel,
        grid=(B, nt),
        in_specs=[pl.BlockSpec((1, TOK_TILE, D), lambda b, i: (b, i, 0)),
                  _full((1, D)),
                  pl.BlockSpec((1, 1, 1, D), mod_map),
                  pl.BlockSpec((1, 1, 1, D), mod_map)]
                 + [_full(w.shape) for w in weights]
                 + [pl.BlockSpec((d, TOK_TILE), lambda b, i: (0, i)),
                    pl.BlockSpec((d, TOK_TILE), lambda b, i: (0, i)),
                    pl.BlockSpec((TOK_TILE, LANES), lambda b, i: (i, 0)),
                    pl.BlockSpec((TOK_TILE, LANES), lambda b, i: (i, 0))],
        out_specs=[pl.BlockSpec((None, WIN_HEADS, d, TOK_TILE), lambda b, i: (b, 0, 0, i)),
                   pl.BlockSpec((None, GLB_HEADS, d, TOK_TILE), lambda b, i: (b, 0, 0, i)),
                   pl.BlockSpec((None, WIN_KV_HEADS + GLB_KV_HEADS, TOK_TILE, d), lambda b, i: (b, 0, i, 0)),
                   pl.BlockSpec((None, WIN_KV_HEADS, dva, TOK_TILE), lambda b, i: (b, 0, 0, i)),
                   pl.BlockSpec((None, GLB_KV_HEADS, dva, TOK_TILE), lambda b, i: (b, 0, 0, i))],
        out_shape=[jax.ShapeDtypeStruct((B, WIN_HEADS, d, S), BF16),
                   jax.ShapeDtypeStruct((B, GLB_HEADS, d, S), BF16),
                   jax.ShapeDtypeStruct((B, WIN_KV_HEADS + GLB_KV_HEADS, S, d), BF16),
                   jax.ShapeDtypeStruct((B, WIN_KV_HEADS, dva, S), BF16),
                   jax.ShapeDtypeStruct((B, GLB_KV_HEADS, dva, S), BF16)],
        compiler_params=_cparams(2),
        name="odd_qkv_prep",
    )(xa, p["norm_mix"].reshape(1, D).astype(F32), mods["sh_a"], mods["sc_a"], *weights, cosT, sinT, cosK, sinK)


def _aligned(start):
    return start if isinstance(start, int) else pl.multiple_of(start, PV_TILE)


def _attn_step(k_ref, vT_ref, acc_ref, q_next, next_offs, s_next, cur_offs, s_cur, m, cmax):
    m_new = jnp.maximum(m, cmax) if cur_offs else m
    pv = None
    cnext = None
    for i in range(max(len(next_offs), len(cur_offs))):
        rows = slice(i * PV_TILE, (i + 1) * PV_TILE)
        if i < len(next_offs):
            s = jnp.dot(k_ref[pl.ds(_aligned(next_offs[i]), PV_TILE), :], q_next, preferred_element_type=F32)
            s_next[rows, :] = s
            cm = jnp.max(s, axis=0, keepdims=True)
            cnext = cm if cnext is None else jnp.maximum(cnext, cm)
        if i < len(cur_offs):
            p = jnp.exp2(s_cur[rows, :] - m_new).astype(BF16)
            d = jnp.dot(vT_ref[:, pl.ds(_aligned(cur_offs[i]), PV_TILE)], p, preferred_element_type=F32)
            pv = d if pv is None else pv + d
    if cur_offs:
        acc_ref[...] = acc_ref[...] * jnp.exp2(m - m_new) + pv
    return m_new, cnext


def _softmax_init(init_ref, dva, dv, tq):
    m0 = jnp.broadcast_to(init_ref[0, 0:1, 0:1], (1, tq))
    row = lax.broadcasted_iota(jnp.int32, (dva, tq), 0)
    acc0 = jnp.where(row == dv, jnp.broadcast_to(init_ref[0, 1:2, 0:1], (dva, tq)), 0.0)
    return m0, acc0


def _flash_kernel(init_ref, qT_ref, qTn_ref, k_ref, vT_ref, o_ref, s0_ref, s1_ref, acc_ref, cm_ref, *,
                  T, n_ctx, kt, dv):
    qT = qT_ref[...]
    tq = qT.shape[1]
    dva = vT_ref.shape[0]
    pieces = lambda off: [off + kk for kk in range(0, kt, PV_TILE)]
    first = [T + kk for kk in range(0, n_ctx, PV_TILE)] + pieces(0)
    step = functools.partial(_attn_step, k_ref, vT_ref, acc_ref)

    @pl.when(pl.program_id(2) == 0)
    def _():
        _, cm_ref[0:1, :] = step(qT, first, s0_ref, [], None, None, None)

    m, acc0 = _softmax_init(init_ref, dva, dv, tq)
    acc_ref[...] = acc0
    m, cm_b = step(qT, pieces(kt), s1_ref, first, s0_ref, m, cm_ref[0:1, :])

    def body(t, carry):
        m, cm_b = carry
        off = pl.multiple_of(t * (2 * kt), kt)
        m, cm_a = step(qT, pieces(off + 2 * kt), s0_ref, pieces(off + kt), s1_ref, m, cm_b)
        m, cm_b = step(qT, pieces(off + 3 * kt), s1_ref, pieces(off + 2 * kt), s0_ref, m, cm_a)
        return m, cm_b

    m, cm_b = lax.fori_loop(0, T // (2 * kt) - 1, body, (m, cm_b))
    m, cm_ref[0:1, :] = step(qTn_ref[...], first, s0_ref, pieces(T - kt), s1_ref, m, cm_b)
    acc = acc_ref[...]
    o_ref[...] = acc[:dv] / acc[dv:dv + 1]


def _flash(qT, k, vT, init, T, key_of):
    B, Hq, dk, S = qT.shape
    Hv, dva = vT.shape[1], vT.shape[2]
    dv = dva - ONES_ROWS
    gv = Hq // Hv
    kt = min(KEY_TILE, T // 2)
    tq = min(Q_TILE, T)
    assert T % (2 * kt) == 0 and T % tq == 0 and kt % PV_TILE == 0
    n_ctx = S - T
    assert n_ctx % PV_TILE == 0
    kern = functools.partial(_flash_kernel, T=T, n_ctx=n_ctx, kt=kt, dv=dv)
    nq = T // tq
    return pl.pallas_call(
        kern,
        grid=(B, Hq, nq),
        in_specs=[pl.BlockSpec((1, 2, LANES),lambda b, h, i: (h, 0, 0)),
                  pl.BlockSpec((None, None, dk, tq), lambda b, h, i: (b, h, 0, i)),
                  pl.BlockSpec((None, None, dk, tq), lambda b, h, i: (b, h, 0, jnp.minimum(i + 1, nq - 1))),
                  pl.BlockSpec((None, None, S, dk), lambda b, h, i: (b, key_of(h), 0, 0)),
                  pl.BlockSpec((None, None, dva, S), lambda b, h, i: (b, h // gv, 0, 0))],
        out_specs=pl.BlockSpec((None, None, dv, tq), lambda b, h, i: (b, h, 0, i)),
        out_shape=jax.ShapeDtypeStruct((B, Hq, dv, T), F32),
        scratch_shapes=[pltpu.VMEM((kt + n_ctx, tq), F32), pltpu.VMEM((kt, tq), F32),
                        pltpu.VMEM((dva, tq), F32), pltpu.VMEM((8, tq), F32)],
        compiler_params=_cparams(3),
        name="dense_attention",
    )(init, qT, qT, k, vT)


def _ctx_attn_kernel(init_ref, qT_ref, k_ref, vT_ref, o_ref, *, dv):
    qT = qT_ref[...]
    m0, acc0 = _softmax_init(init_ref, vT_ref.shape[0], dv, qT.shape[1])
    s = jnp.dot(k_ref[...], qT, preferred_element_type=F32)
    m = jnp.maximum(m0, jnp.max(s, axis=0, keepdims=True))
    acc = acc0 * jnp.exp2(m0 - m) + jnp.dot(vT_ref[...], jnp.exp2(s - m).astype(BF16), preferred_element_type=F32)
    o_ref[...] = acc[:dv] / acc[dv:dv + 1]


def _ctx_attn(qT, k, vT, init, T, key_of):
    B, Hq, dk, S = qT.shape
    Hv, dva = vT.shape[1], vT.shape[2]
    dv = dva - ONES_ROWS
    gv = Hq // Hv
    n = S - T
    blk = T // n
    return pl.pallas_call(
        functools.partial(_ctx_attn_kernel, dv=dv),
        grid=(B, Hq),
        in_specs=[pl.BlockSpec((1, 2, LANES),lambda b, h: (h, 0, 0)),
                  pl.BlockSpec((None, None, dk, n), lambda b, h: (b, h, 0, blk)),
                  pl.BlockSpec((None, None, n, dk), lambda b, h: (b, key_of(h), blk, 0)),
                  pl.BlockSpec((None, None, dva, n), lambda b, h: (b, h // gv, 0, blk))],
        out_specs=pl.BlockSpec((None, None, dv, n), lambda b, h: (b, h, 0, 0)),
        out_shape=jax.ShapeDtypeStruct((B, Hq, dv, n), F32),
        compiler_params=_cparams(2),
        name="context_attention",
    )(init, qT, k, vT)


def _window_kernel(sink_ref, bias_ref, qT_ref, k_ref, kc_ref, vT_ref, o_ref, *, dv):
    i = pl.program_id(2)
    G, _, tq = qT_ref.shape
    qT = jnp.concatenate([qT_ref[g] for g in range(G)], axis=1)
    snk = jnp.concatenate([jnp.broadcast_to(sink_ref[g, 0:1, 0:1], (1, tq)) for g in range(G)], axis=1)
    W = WINDOW
    kband = jnp.concatenate([k_ref[0][tq - W:tq, :], k_ref[1][...], k_ref[2][0:W, :]], axis=0)
    before_first = jnp.where(i == 0, NEG_INF, 0.0)
    after_last = jnp.where(i == pl.num_programs(2) - 1, NEG_INF, 0.0)
    bias = jnp.concatenate([bias_ref[0:W, :] + before_first, bias_ref[W:W + tq, :],
                            bias_ref[W + tq:, :] + after_last], axis=0)
    s_loc = jnp.dot(kband, qT, preferred_element_type=F32) + bias
    s_ctx = jnp.dot(kc_ref[...], qT, preferred_element_type=F32)
    m = jnp.maximum(jnp.maximum(jnp.max(s_loc, axis=0, keepdims=True), jnp.max(s_ctx, axis=0, keepdims=True)), snk)
    p_loc = jnp.exp2(s_loc - m).astype(BF16)
    p_ctx = jnp.exp2(s_ctx - m).astype(BF16)
    vband = jnp.concatenate([vT_ref[0][:, tq - W:tq], vT_ref[1][...], vT_ref[2][:, 0:W]], axis=1)
    acc = (jnp.dot(vband, p_loc, preferred_element_type=F32)
           + jnp.dot(vT_ref[3][...], p_ctx, preferred_element_type=F32))
    o = acc[:dv] / (acc[dv:dv + 1] + jnp.exp2(snk - m))
    for g in range(G):
        o_ref[g] = o[:, g * tq:(g + 1) * tq]


def _window(qT, k, vT, sink, T, key_of):
    B, Hq, dk, S = qT.shape
    Hk, dva = vT.shape[1], vT.shape[2]
    dv = dva - ONES_ROWS
    g = Hq // Hk
    tq = TOK_TILE
    nt = T // tq
    ctx_blk = T // tq
    clip = lambda j: jnp.clip(j, 0, nt - 1)
    kern = functools.partial(_window_kernel, dv=dv)
    kspec = lambda f: pl.BlockSpec((None, None, tq, dk), lambda b, h, i: (b, key_of(h * g), f(i), 0))
    vspec = lambda f: pl.BlockSpec((None, None, dva, tq), lambda b, h, i: (b, h, 0, f(i)))
    r = jnp.arange(tq + 2 * WINDOW)[:, None]
    c = jnp.arange(g * tq)[None, :] % tq
    bias = jnp.where(jnp.abs(r - WINDOW - c) <= WINDOW, 0.0, NEG_INF).astype(F32)

    def body(sink_ref, bias_ref, qT_ref, k0, k1, k2, kc, v0, v1, v2, vc, o_ref):
        kern(sink_ref, bias_ref, qT_ref, (k0, k1, k2), kc, (v0, v1, v2, vc), o_ref)

    return pl.pallas_call(
        body,
        grid=(B, Hk, nt),
        in_specs=[pl.BlockSpec((g, 1, LANES), lambda b, h, i: (h, 0, 0)),
                  _full(bias.shape),
                  pl.BlockSpec((None, g, dk, tq), lambda b, h, i: (b, h, 0, i)),
                  kspec(lambda i: clip(i - 1)), kspec(lambda i: i), kspec(lambda i: clip(i + 1)),
                  kspec(lambda i: ctx_blk),
                  vspec(lambda i: clip(i - 1)), vspec(lambda i: i), vspec(lambda i: clip(i + 1)),
                  vspec(lambda i: ctx_blk)],
        out_specs=pl.BlockSpec((None, g, dv, tq), lambda b, h, i: (b, h, 0, i)),
        out_shape=jax.ShapeDtypeStruct((B, Hq, dv, T), F32),
        compiler_params=_cparams(3),
        name="window_attention",
    )(sink, bias, qT, k, k, k, k, vT, vT, vT, vT)


def _project_out(x_ref, oT, w_ref, gate_ref, y_ref):
    y = jnp.dot(oT.T.astype(BF16), w_ref[...], preferred_element_type=F32)
    y_ref[0] = x_ref[0] + gate_ref[0, 0] * y


def _even_out_kernel(x_ref, oa_ref, od_ref, w_ref, gate_ref, lam_ref, sub_ref, y_ref, *, post_scale):
    lam = lam_ref[...]
    parts = [oa_ref[hd] for hd in range(MLA_HEADS)]
    for hd in range(DIFF_HEADS):
        diff = od_ref[2 * hd] - lam * od_ref[2 * hd + 1]
        parts.append(diff * _rms_rows(diff) * sub_ref[...] * post_scale)
    _project_out(x_ref, jnp.concatenate(parts, axis=0), w_ref, gate_ref, y_ref)


def _odd_out_kernel(x_ref, oc_ref, od_ref, w_ref, gate_ref, y_ref):
    parts = [oc_ref[hd] for hd in range(WIN_HEADS)] + [od_ref[hd] for hd in range(GLB_HEADS)]
    _project_out(x_ref, jnp.concatenate(parts, axis=0), w_ref, gate_ref, y_ref)


def _out_proj(kern, x, heads, w, gate, extra, tile0, n_tiles, kind):
    B, S, D = x.shape
    row_map = lambda b, i: (b, i + tile0, 0)
    return pl.pallas_call(
        kern,
        grid=(B, n_tiles),
        in_specs=[pl.BlockSpec((1, TOK_TILE, D), row_map)]
                 + [pl.BlockSpec((None,) + o.shape[1:3] + (TOK_TILE,), lambda b, i: (b, 0, 0, i)) for o in heads]
                 + [_full(w.shape), pl.BlockSpec((1, 1, 1, D), lambda b, i: (b, kind, 0, 0))]
                 + [_full(e.shape) for e in extra],
        out_specs=pl.BlockSpec((1, TOK_TILE, D), row_map),
        out_shape=jax.ShapeDtypeStruct((B, S, D), F32),
        input_output_aliases={0: 0},
        compiler_params=_cparams(2),
        name="merge_out_proj_residual",
    )(x, *heads, w, gate, *extra)


def _first_argmax(vals):
    best, idx = vals[0], jnp.zeros(vals[0].shape, jnp.int32)
    for j in range(1, len(vals)):
        better = vals[j] > best
        idx = jnp.where(better, j, idx)
        best = jnp.where(better, vals[j], best)
    return idx, best


def _pick(idx, vals):
    out = vals[0]
    for j in range(1, len(vals)):
        out = jnp.where(idx == j, vals[j], out)
    return out


def _route(logits, bias):
    s = 1.0 / (1.0 + jnp.exp(-logits))
    sel = s + bias
    srow = [s[e:e + 1] for e in range(N_EXPERTS)]
    row = [sel[e:e + 1] for e in range(N_EXPERTS)]
    scores = []
    for g in range(N_GROUPS):
        a, b, c, d = row[4 * g:4 * g + 4]
        hi1, lo1, hi2, lo2 = jnp.maximum(a, b), jnp.minimum(a, b), jnp.maximum(c, d), jnp.minimum(c, d)
        top1 = jnp.maximum(hi1, hi2)
        top2 = jnp.maximum(jnp.maximum(lo1, lo2), jnp.minimum(hi1, hi2))
        scores.append(top1 + top2)
    gi, _ = _first_argmax(scores)
    v = [_pick(gi, [row[4 * g + j] for g in range(N_GROUPS)]) for j in range(EXPERTS_PER_GROUP)]
    sv = [_pick(gi, [srow[4 * g + j] for g in range(N_GROUPS)]) for j in range(EXPERTS_PER_GROUP)]
    i1, _ = _first_argmax(v)
    i2, _ = _first_argmax([jnp.where(i1 == j, -jnp.inf, v[j]) for j in range(EXPERTS_PER_GROUP)])
    w1, w2 = _pick(i1, sv), _pick(i2, sv)
    tot = w1 + w2
    w1, w2 = w1 / tot, w2 / tot
    rows = []
    for e in range(N_EXPERTS):
        g, j = divmod(e, EXPERTS_PER_GROUP)
        in_g = gi == g
        rows.append(jnp.where(in_g & (i1 == j), w1, 0.0) + jnp.where(in_g & (i2 == j), w2, 0.0))
    return jnp.concatenate(rows, axis=0), gi


def _split(x, terms):
    out = []
    for _ in range(terms):
        part = x.astype(BF16).astype(F32)
        out.append(part)
        x = x - part
    return out


def _to_column(row):
    n = row.shape[1]
    return jnp.concatenate([row, jnp.zeros((LANES - 1, n), F32)], axis=0).T[:, 0:1]


def _moe_kernel(x_ref, g_ref, sh_ref, sc_ref, gate_ref, wrT_ref, br_ref, before_ref, wg_ref, wu_ref, wd_ref, fin_ref,
                y_ref, hs_scr, cs_scr, ys_scr, q_scr, info_ref, wg_buf, wu_buf, wd_buf, w_sem, *, final):
    step = pl.program_id(2)
    per_step = wg_buf.shape[1]
    tm = x_ref.shape[1]
    R = hs_scr.shape[0]

    n_steps = pl.num_programs(2)
    lin = (pl.program_id(0) * pl.num_programs(1) + pl.program_id(1)) * n_steps + step
    total_steps = pl.num_programs(0) * pl.num_programs(1) * n_steps

    def weight_copies(at_lin):
        slot = lax.rem(at_lin, WEIGHT_SLOTS)
        first = lax.rem(at_lin, n_steps) * per_step
        return [pltpu.make_async_copy(w.at[pl.ds(first, per_step)], buf.at[slot], w_sem.at[k, slot])
                for k, (w, buf) in enumerate(((wg_ref, wg_buf), (wu_ref, wu_buf), (wd_ref, wd_buf)))]

    @pl.when(lin == 0)
    def _():
        for ahead in range(WEIGHT_SLOTS - 1):
            for cp in weight_copies(lin + ahead):
                cp.start()

    @pl.when(lin + (WEIGHT_SLOTS - 1) < total_steps)
    def _():
        for cp in weight_copies(lin + (WEIGHT_SLOTS - 1)):
            cp.start()

    @pl.when(step == 0)
    def _():
        h = _norm_mod(x_ref[0], g_ref[...], sh_ref[0, 0], sc_ref[0, 0]).astype(BF16)
        comb, gi = _route(_dot_t(wrT_ref[...], h), br_ref[...])
        sel = jnp.concatenate([(gi == g).astype(F32) for g in range(N_GROUPS)]
                              + [jnp.zeros((BF16_ROWS - N_GROUPS, tm), F32)], axis=0)
        rank = jnp.dot(sel.astype(BF16), before_ref[...], preferred_element_type=F32)
        cnt = jnp.sum(sel, axis=1, keepdims=True)
        seg = jnp.ceil(cnt * (1.0 / BF16_ROWS)) * BF16_ROWS
        pos = jnp.zeros((1, tm), F32)
        start = jnp.zeros((1, 1), F32)
        for g in range(N_GROUPS):
            pos = pos + sel[g:g + 1] * (start + rank[g:g + 1])
            info_ref[g] = start[0, 0].astype(jnp.int32)
            info_ref[N_GROUPS + g] = jnp.ceil(cnt[g:g + 1] * (1.0 / MOE_CHUNK))[0, 0].astype(jnp.int32)
            start = start + seg[g:g + 1]
        slot = lax.broadcasted_iota(jnp.int32, (R, tm), 0).astype(F32)
        P = jnp.where(slot == pos, 1.0, 0.0).astype(BF16)
        hs_scr[...] = jnp.dot(P, h, preferred_element_type=F32).astype(BF16)
        parts = _split(comb, COMB_TERMS)
        combT = jnp.concatenate(parts + [jnp.zeros((LANES - COMB_TERMS * N_EXPERTS, tm), F32)], axis=0).T
        cs_scr[...] = jnp.dot(P, combT.astype(BF16), preferred_element_type=F32)
        lane = lax.broadcasted_iota(jnp.int32, (tm, R), 1).astype(F32)
        q_scr[...] = jnp.where(lane == _to_column(pos), 1.0, 0.0).astype(BF16)
        ys_scr[...] = jnp.zeros_like(ys_scr)

    for cp in weight_copies(lin):
        cp.wait()
    slot = lax.rem(lin, WEIGHT_SLOTS)
    steps_per_group = EXPERTS_PER_GROUP // per_step
    grp = lax.shift_right_logical(step, steps_per_group.bit_length() - 1)
    start = info_ref[grp]
    n_chunks = info_ref[N_GROUPS + grp]

    def experts(r0, rows):
        h = hs_scr[pl.ds(r0, rows), :]
        cs = cs_scr[pl.ds(r0, rows), :]
        lane = lax.broadcasted_iota(jnp.int32, (rows, LANES), 1) & (N_EXPERTS - 1)
        total = None
        for j in range(per_step):
            a = jnp.dot(h, wg_buf[slot, j], preferred_element_type=F32)
            u = jnp.dot(h, wu_buf[slot, j], preferred_element_type=F32)
            act = (a * (1.0 / (1.0 + jnp.exp(-a)))) * u
            y = jnp.dot(act.astype(BF16), wd_buf[slot, j], preferred_element_type=F32)
            w = jnp.sum(jnp.where(lane == step * per_step + j, cs, 0.0), axis=1, keepdims=True)
            total = w * y if total is None else total + w * y
        ys_scr[pl.ds(r0, rows), :] += total

    def pair(c, carry):
        experts(pl.multiple_of(start + c * (2 * MOE_CHUNK), BF16_ROWS), 2 * MOE_CHUNK)
        return carry

    lax.fori_loop(0, lax.shift_right_logical(n_chunks, 1), pair, 0)

    @pl.when((n_chunks & 1) == 1)
    def _():
        experts(pl.multiple_of(start + (n_chunks - 1) * MOE_CHUNK, BF16_ROWS), MOE_CHUNK)

    @pl.when(step == pl.num_programs(2) - 1)
    def _():
        q = q_scr[...]
        out = sum(jnp.dot(q, part.astype(BF16), preferred_element_type=F32) for part in _split(ys_scr[...], 2))
        y = x_ref[0] + gate_ref[0, 0] * out
        if final:
            y = (y * lax.rsqrt(jnp.mean(y * y, axis=-1, keepdims=True) + NORM_EPS)) * fin_ref[...]
        y_ref[0] = y


def _moe(x, g, sh, sc, gate, wrT, br, wg, wu, wd, fin, tm, tile0, n_tiles, kind, final):
    B, S, D = x.shape
    E, _, F = wg.shape
    R = -(-(tm + BF16_ROWS * N_GROUPS + MOE_CHUNK) // MXU_DIM) * MXU_DIM
    idx = jnp.arange(tm)
    before = (idx[:, None] < idx[None, :]).astype(BF16)
    mod_map = lambda b, i, e: (b, kind, 0, 0)
    row_map = lambda b, i, e: (b, i + tile0, 0)
    assert not (final and tile0)
    assert B * n_tiles * (E // MOE_STEP_EXPERTS) >= WEIGHT_SLOTS and EXPERTS_PER_GROUP % MOE_STEP_EXPERTS == 0
    return pl.pallas_call(
        functools.partial(_moe_kernel, final=final),
        grid=(B, n_tiles, E // MOE_STEP_EXPERTS),
        in_specs=[pl.BlockSpec((1, tm, D), row_map),
                  pl.BlockSpec((1, D), lambda b, i, e: (0, 0)),
                  pl.BlockSpec((1, 1, 1, D), mod_map),
                  pl.BlockSpec((1, 1, 1, D), mod_map),
                  pl.BlockSpec((1, 1, 1, D), mod_map),
                  pl.BlockSpec((E, D), lambda b, i, e: (0, 0)),
                  pl.BlockSpec((E, 1), lambda b, i, e: (0, 0)),
                  pl.BlockSpec((tm, tm), lambda b, i, e: (0, 0)),
                  pl.BlockSpec(memory_space=pl.ANY),
                  pl.BlockSpec(memory_space=pl.ANY),
                  pl.BlockSpec(memory_space=pl.ANY),
                  pl.BlockSpec((1, D), lambda b, i, e: (0, 0))],
        out_specs=pl.BlockSpec((1, tm, D), row_map),
        out_shape=jax.ShapeDtypeStruct((B, n_tiles * tm if final else S, D), F32),
        scratch_shapes=[pltpu.VMEM((R, D), BF16), pltpu.VMEM((R, LANES), F32), pltpu.VMEM((R, D), F32),
                        pltpu.VMEM((tm, R), BF16), pltpu.SMEM((2 * N_GROUPS,), jnp.int32),
                        pltpu.VMEM((WEIGHT_SLOTS, MOE_STEP_EXPERTS, D, F), BF16),
                        pltpu.VMEM((WEIGHT_SLOTS, MOE_STEP_EXPERTS, D, F), BF16),
                        pltpu.VMEM((WEIGHT_SLOTS, MOE_STEP_EXPERTS, F, D), BF16),
                        pltpu.SemaphoreType.DMA((3, WEIGHT_SLOTS))],
        input_output_aliases={} if final else {0: 0},
        compiler_params=_cparams(3),
        name="moe_experts",
    )(x, g.reshape(1, D).astype(F32), sh, sc, gate, wrT, br, before, wg, wu, wd, fin.reshape(1, D).astype(F32))


def _axial_tables(T, n_ctx, rot_dim):
    half = rot_dim // 2
    inv_freq = ROPE_THETA ** (-jnp.arange(0, half, 2, dtype=F32) / half)
    rows = T // GRID_W
    row = jnp.broadcast_to(jnp.arange(rows, dtype=F32)[:, None], (rows, GRID_W)).reshape(-1)
    col = jnp.broadcast_to(jnp.arange(GRID_W, dtype=F32)[None, :], (rows, GRID_W)).reshape(-1)

    def ang(pos):
        a = pos[:, None] * inv_freq[None, :]
        return jnp.concatenate([a, a], axis=-1)

    a = jnp.concatenate([ang(row), ang(col)], axis=-1)
    cos = jnp.concatenate([jnp.cos(a), jnp.ones((n_ctx, rot_dim), F32)], axis=0)
    sin = jnp.concatenate([jnp.sin(a), jnp.zeros((n_ctx, rot_dim), F32)], axis=0)
    reps = LANES // rot_dim
    return cos.T, sin.T, jnp.tile(cos, (1, reps)), jnp.tile(sin, (1, reps))


def _plain_init(n_heads):
    return jnp.broadcast_to(jnp.array([NEG_INF, 0.0], F32)[None, :, None], (n_heads, 2, LANES))


def _sink_init(sink):
    s = sink.astype(F32) * LOG2E
    return jnp.broadcast_to(jnp.stack([s, jnp.ones_like(s)], axis=1)[:, :, None], (s.shape[0], 2, LANES))


def _even_mixer(xa, p, mods, T, with_ctx, lam_init, tabs):
    B, S, D = xa.shape
    qa, ka, va, qd, kd, vd = _even_prep(xa, p, mods, tabs)
    own = lambda h: h
    init_a, init_d = _plain_init(MLA_HEADS), _plain_init(2 * DIFF_HEADS)
    lam = p["lam"].astype(F32)
    lam_full = (jnp.exp(jnp.sum(lam[0] * lam[1])) - jnp.exp(jnp.sum(lam[2] * lam[3])) + lam_init).reshape(1, 1)
    sub = jnp.broadcast_to(p["subln"].astype(F32)[:, None], (DIFF_V, TOK_TILE))
    kern = functools.partial(_even_out_kernel, post_scale=1.0 - lam_init)
    oa = _flash(qa, ka, va, init_a, T, own)
    od = _flash(qd, kd, vd, init_d, T, own)
    xa = _out_proj(kern, xa, [oa, od], p["w_out"], mods["g_a"], [lam_full, sub], 0, T // TOK_TILE, 0)
    if with_ctx:
        oa = _ctx_attn(qa, ka, va, init_a, T, own)
        od = _ctx_attn(qd, kd, vd, init_d, T, own)
        xa = _out_proj(kern, xa, [oa, od], p["w_out"], mods["g_a"], [lam_full, sub], T // TOK_TILE, 1, 1)
    return xa


def _odd_mixer(xa, p, mods, T, with_ctx, tabs):
    B, S, D = xa.shape
    qc, qd, k, vc, vd = _odd_prep(xa, p, mods, tabs)
    gw, gd = WIN_HEADS // WIN_KV_HEADS, GLB_HEADS // GLB_KV_HEADS
    key_c = lambda h: h // gw
    key_d = lambda h: WIN_KV_HEADS + h // gd
    init_d = _plain_init(GLB_HEADS)
    sink = p["sink"].astype(F32) * LOG2E
    sink_w = jnp.broadcast_to(sink[:, None, None], (WIN_HEADS, 1, LANES))
    od = _flash(qd, k, vd, init_d, T, key_d)
    oc = _window(qc, k, vc, sink_w, T, key_c)
    xa = _out_proj(_odd_out_kernel, xa, [oc, od], p["w_out"], mods["g_a"], [], 0, T // TOK_TILE, 0)
    if with_ctx:
        oc = _ctx_attn(qc, k, vc, _sink_init(p["sink"]), T, key_c)
        od = _ctx_attn(qd, k, vd, init_d, T, key_d)
        xa = _out_proj(_odd_out_kernel, xa, [oc, od], p["w_out"], mods["g_a"], [], T // TOK_TILE, 1, 1)
    return xa


def kernel(x, c, ctx, c_ctx, w_mod, b_mod, norm_mix, norm_ffn, even_w_in, even_norm_q, even_norm_kv, even_w_uq, even_w_ukv, even_lambda, even_subln, even_w_out, odd_w_in, odd_sink, odd_q_norm, odd_k_norm, odd_w_out, w_router, b_router, w_gate, w_up, w_down, norm_final):
    B, T, D = x.shape
    n_ctx = ctx.shape[1]
    depth = w_mod.shape[0]
    assert n_ctx == TOK_TILE == PV_TILE and T % MOE_TILE == 0 and B <= 7
    assert MLA_ROPE == DIFF_D
    S = T + n_ctx
    tabs32 = _axial_tables(T, n_ctx, MLA_ROPE)
    tabs64 = _axial_tables(T, n_ctx, HEAD_DIM)

    cond = jnp.zeros((8, D), F32).at[:B].set(c.astype(F32)).at[B].set(c_ctx.astype(F32))
    mod_all = _mod_vectors(cond, w_mod, b_mod)
    wrT = w_router.astype(BF16).T
    br = b_router.astype(F32).reshape(N_EXPERTS, 1)

    xa = jnp.concatenate([x, ctx], axis=1).astype(F32)
    for l in range(depth):
        with_ctx = l < depth - 1
        i = l // 2
        lat = mod_all[l, :B].reshape(B, 6, D)
        cx = jnp.broadcast_to(mod_all[l, B].reshape(1, 6, D), (B, 6, D))
        both = jnp.stack([lat, cx], axis=1)
        names = ("sh_a", "sc_a", "g_a", "sh_f", "sc_f", "g_f")
        mods = {n: both[:, :, j:j + 1, :] for j, n in enumerate(names)}
        if l % 2 == 0:
            lam_init = 0.8 - 0.6 * math.exp(-0.3 * l)
            p = dict(norm_mix=norm_mix[l], w_in=even_w_in[i], norm_q=even_norm_q[i], norm_kv=even_norm_kv[i],
                     w_uq=even_w_uq[i], w_ukv=even_w_ukv[i], lam=even_lambda[i], subln=even_subln[i],
                     w_out=even_w_out[i].astype(BF16))
            xa = _even_mixer(xa, p, mods, T, with_ctx, lam_init, tabs32)
        else:
            p = dict(norm_mix=norm_mix[l], w_in=odd_w_in[i], sink=odd_sink[i], q_norm=odd_q_norm[i],
                     k_norm=odd_k_norm[i], w_out=odd_w_out[i].astype(BF16))
            xa = _odd_mixer(xa, p, mods, T, with_ctx, tabs64)
        wg, wu, wd = w_gate[l].astype(BF16), w_up[l].astype(BF16), w_down[l].astype(BF16)
        ffn = (norm_ffn[l], mods["sh_f"], mods["sc_f"], mods["g_f"], wrT, br, wg, wu, wd, norm_final)
        xa = _moe(xa, *ffn, MOE_TILE, 0, T // MOE_TILE, 0, final=not with_ctx)
        if with_ctx:
            xa = _moe(xa, *ffn, TOK_TILE, T // TOK_TILE, 1, 1, final=False)
    return xa
```

```python
import functools
import math

import jax
import jax.numpy as jnp
from jax import lax
from jax.experimental import pallas as pl
from jax.experimental.pallas import tpu as pltpu

F32 = jnp.float32
BF16 = jnp.bfloat16
LOG2E = 1.4426950408889634

GRID_W = 64
ROPE_THETA = 10000.0
NORM_EPS = 1e-6
NEG_INF = -1e30

MLA_HEADS = 8
MLA_Q_LORA = 384
MLA_KV_LORA = 256
MLA_NOPE = 64
MLA_ROPE = 32
MLA_V = 64
MLA_SCALE = (MLA_NOPE + MLA_ROPE) ** -0.5
MLA_IN_COLS = MLA_Q_LORA + MLA_KV_LORA + MLA_ROPE
MLA_QK = MLA_NOPE + MLA_ROPE
MLA_PAD = 128

DIFF_HEADS = 8
DIFF_D = 32
DIFF_V = 2 * DIFF_D
DIFF_SCALE = DIFF_D ** -0.5

HEAD_DIM = 64
WIN_HEADS = 8
WIN_KV_HEADS = 2
WINDOW = 128
GLB_HEADS = 8
GLB_KV_HEADS = 2
ATTN_SCALE = HEAD_DIM ** -0.5

N_EXPERTS = 16
N_GROUPS = 4
EXPERTS_PER_GROUP = N_EXPERTS // N_GROUPS

TOK_TILE = 256
Q_TILE = 1024
KEY_TILE = 2048
PV_TILE = 256
MOE_TILE = 1024
MOE_CHUNK = 128
MOE_STEP_EXPERTS = 2
COMB_TERMS = 3
ONES_ROWS = 16
LANES = 128
BF16_ROWS = 16
MXU_DIM = 256
VMEM_LIMIT = 56 * 1024 * 1024


def _cparams(n_axes):
    return pltpu.CompilerParams(dimension_semantics=("arbitrary",) * n_axes,
                                vmem_limit_bytes=VMEM_LIMIT)


def _full(shape):
    return pl.BlockSpec(shape, lambda *_: (0,) * len(shape))


def _mod_kernel(a_ref, w_ref, b_ref, o_ref):
    a = a_ref[...]
    a = a * (1.0 / (1.0 + jnp.exp(-a)))
    o_ref[0] = jnp.dot(a.astype(BF16), w_ref[0].astype(BF16), preferred_element_type=F32) + b_ref[0]


def _mod_vectors(cond, w_mod, b_mod):
    L, D, N = w_mod.shape
    tn = 1536
    return pl.pallas_call(
        _mod_kernel,
        grid=(L, N // tn),
        in_specs=[pl.BlockSpec((8, D), lambda l, j: (0, 0)),
                  pl.BlockSpec((1, D, tn), lambda l, j: (l, 0, j)),
                  pl.BlockSpec((1, 1, tn), lambda l, j: (l, 0, j))],
        out_specs=pl.BlockSpec((1, 8, tn), lambda l, j: (l, 0, j)),
        out_shape=jax.ShapeDtypeStruct((L, 8, N), F32),
        compiler_params=_cparams(2),
        name="mod_vectors",
    )(cond, w_mod, b_mod.reshape(L, 1, N))


def _norm_mod(x, g, sh, sc):
    y = x * lax.rsqrt(jnp.mean(x * x, axis=-1, keepdims=True) + NORM_EPS)
    return (y * g) * (1.0 + sc) + sh


def _dot_t(w, h):
    return lax.dot_general(w, h, (((1,), (1,)), ((), ())), preferred_element_type=F32)


def _rms_rows(x):
    return lax.rsqrt(jnp.mean(x * x, axis=0, keepdims=True) + NORM_EPS)


def _rope_rows(x, cosT, sinT):
    q = x.shape[0] // 4
    rot = jnp.concatenate([-x[q:2 * q], x[0:q], -x[3 * q:4 * q], x[2 * q:3 * q]], axis=0)
    return x * cosT + rot * sinT


def _ones_rows(n):
    row = lax.broadcasted_iota(jnp.int32, (ONES_ROWS, n), 0)
    return jnp.where(row == 0, 1.0, 0.0).astype(BF16)


def _tile_lanes(x, reps):
    return x if reps == 1 else jnp.concatenate([x] * reps, axis=1)


def _even_prep_kernel(x_ref, g_ref, sh_ref, sc_ref, w1T_ref, wckv_ref, wkr_ref, wkrr_ref, wdk_ref, wdkr_ref,
                      wuqT_ref, wuvT_ref, wuk_ref, place_ref, gq_ref, gkvc_ref, gkvr_ref, cosT_ref, sinT_ref,
                      cosK_ref, sinK_ref,
                      qa_ref, ka_ref, va_ref, qd_ref, kd_ref, vd_ref):
    h = _norm_mod(x_ref[0], g_ref[...], sh_ref[0, 0], sc_ref[0, 0]).astype(BF16)
    n = h.shape[0]
    cosT, sinT, cosK, sinK = cosT_ref[...], sinT_ref[...], cosK_ref[...], sinK_ref[...]
    ones = _ones_rows(n)
    zT = _dot_t(w1T_ref[...], h)
    o1, o2 = MLA_Q_LORA, MLA_Q_LORA + MLA_KV_LORA
    o3 = o2 + 2 * DIFF_HEADS * DIFF_D

    c_q = zT[:o1]
    c_q = (c_q * _rms_rows(c_q) * gq_ref[...]).astype(BF16)
    qT = jnp.dot(wuqT_ref[...], c_q, preferred_element_type=F32)
    for hd in range(MLA_HEADS):
        r0 = hd * MLA_QK
        rope = _rope_rows(qT[r0 + MLA_NOPE:r0 + MLA_QK], cosT, sinT)
        head = jnp.concatenate([qT[r0:r0 + MLA_NOPE], rope], axis=0)
        qa_ref[hd] = (head * (MLA_SCALE * LOG2E)).astype(BF16)

    c_kvT = zT[o1:o2]
    c_kvT = (c_kvT * _rms_rows(c_kvT) * gkvc_ref[...]).astype(BF16)
    vT = jnp.dot(wuvT_ref[...], c_kvT, preferred_element_type=F32).astype(BF16)
    for hd in range(MLA_HEADS):
        va_ref[hd, 0:MLA_V, :] = vT[hd * MLA_V:(hd + 1) * MLA_V]
        va_ref[hd, MLA_V:, :] = ones

    for j in range(2 * DIFF_HEADS):
        qj = _rope_rows(zT[o2 + j * DIFF_D:o2 + (j + 1) * DIFF_D], cosT, sinT)
        qd_ref[j] = (qj * (DIFF_SCALE * LOG2E)).astype(BF16)
    for hd in range(DIFF_HEADS):
        vd_ref[hd, 0:DIFF_V, :] = zT[o3 + hd * DIFF_V:o3 + (hd + 1) * DIFF_V].astype(BF16)
        vd_ref[hd, DIFF_V:, :] = ones

    c_kv = jnp.dot(h, wckv_ref[...], preferred_element_type=F32)
    c_kv = (c_kv * lax.rsqrt(jnp.mean(c_kv * c_kv, axis=-1, keepdims=True) + NORM_EPS) * gkvr_ref[...]).astype(BF16)
    kr = (jnp.dot(h, wkr_ref[...], preferred_element_type=F32) * cosK
          + jnp.dot(h, wkrr_ref[...], preferred_element_type=F32) * sinK)
    ka = (jnp.dot(c_kv, wuk_ref[...], preferred_element_type=F32)
          + jnp.dot(kr.astype(BF16), place_ref[...], preferred_element_type=F32))
    ka = ka.astype(BF16)
    for hd in range(MLA_HEADS):
        ka_ref[hd] = ka[:, hd * MLA_PAD:hd * MLA_PAD + MLA_QK]

    reps = wdk_ref.shape[1] // LANES
    kd = (jnp.dot(h, wdk_ref[...], preferred_element_type=F32) * _tile_lanes(cosK, reps)
          + jnp.dot(h, wdkr_ref[...], preferred_element_type=F32) * _tile_lanes(sinK, reps))
    kd = kd.astype(BF16)
    for blk in range(kd_ref.shape[0]):
        kd_ref[blk] = kd[:, blk * DIFF_D:(blk + 1) * DIFF_D]


def _rot_cols(w, width):
    d, n = w.shape
    w4 = w.reshape(d, n // width, 4, width // 4)
    return jnp.stack([-w4[:, :, 1], w4[:, :, 0], -w4[:, :, 3], w4[:, :, 2]], axis=2).reshape(d, n)


def _perm_cols(g, width):
    g4 = g.reshape(-1, 4, width // 4)
    return jnp.stack([g4[:, 1], g4[:, 0], g4[:, 3], g4[:, 2]], axis=1).reshape(-1)


def _even_prep(xa, p, mods, tabs):
    B, S, D = xa.shape
    w_in, w_uq, w_ukv = p["w_in"], p["w_uq"], p["w_ukv"]
    o1, o2, o3 = MLA_Q_LORA, MLA_Q_LORA + MLA_KV_LORA, MLA_IN_COLS
    nq = 2 * DIFF_HEADS * DIFF_D
    bf = lambda a: a.astype(BF16)
    w1T = bf(jnp.concatenate([w_in[:, :o2], w_in[:, o3:o3 + nq], w_in[:, o3 + 2 * nq:]], axis=1).T)
    wckv = bf(w_in[:, o1:o2])
    wkr = jnp.pad(w_in[:, o2:o3], ((0, 0), (0, LANES - MLA_ROPE)))
    wkrr = jnp.pad(_rot_cols(w_in[:, o2:o3], MLA_ROPE), ((0, 0), (0, LANES - MLA_ROPE)))
    wdk = w_in[:, o3 + nq:o3 + 2 * nq]
    wdkr = _rot_cols(wdk, DIFF_D)
    uq = w_uq.reshape(MLA_Q_LORA, MLA_HEADS, MLA_NOPE + MLA_ROPE)
    wuqT = bf(uq.reshape(MLA_Q_LORA, -1).T)
    ukv = w_ukv.reshape(MLA_KV_LORA, MLA_HEADS, MLA_NOPE + MLA_V)
    wuvT = bf(ukv[:, :, MLA_NOPE:].reshape(MLA_KV_LORA, -1).T)
    wuk = bf(jnp.pad(ukv[:, :, :MLA_NOPE], ((0, 0), (0, 0), (0, MLA_PAD - MLA_NOPE))).reshape(MLA_KV_LORA, -1))
    src = jnp.arange(LANES)[:, None]
    dst = jnp.arange(MLA_HEADS * MLA_PAD)[None, :]
    place = bf((src < MLA_ROPE) & (dst % MLA_PAD == src + MLA_NOPE))
    gq = jnp.broadcast_to(p["norm_q"].astype(F32)[:, None], (MLA_Q_LORA, TOK_TILE))
    gkvc = jnp.broadcast_to(p["norm_kv"].astype(F32)[:, None], (MLA_KV_LORA, TOK_TILE))
    gkvr = p["norm_kv"].astype(F32).reshape(1, MLA_KV_LORA)
    cosT, sinT, cosK, sinK = tabs
    weights = [w1T, wckv, bf(wkr), bf(wkrr), bf(wdk), bf(wdkr), wuqT, wuvT, wuk, place, gq, gkvc, gkvr]
    nt = S // TOK_TILE
    ctx_tile = nt - 1
    mod_map = lambda b, i: (b, (i == ctx_tile).astype(jnp.int32), 0, 0)
    HA, HD = MLA_HEADS, DIFF_HEADS
    dva = MLA_V + ONES_ROWS
    return pl.pallas_call(
        _even_prep_kernel,
        grid=(B, nt),
        in_specs=[pl.BlockSpec((1, TOK_TILE, D), lambda b, i: (b, i, 0)),
                  _full((1, D)),
                  pl.BlockSpec((1, 1, 1, D), mod_map),
                  pl.BlockSpec((1, 1, 1, D), mod_map)]
                 + [_full(w.shape) for w in weights]
                 + [pl.BlockSpec((MLA_ROPE, TOK_TILE), lambda b, i: (0, i)),
                    pl.BlockSpec((MLA_ROPE, TOK_TILE), lambda b, i: (0, i)),
                    pl.BlockSpec((TOK_TILE, LANES), lambda b, i: (i, 0)),
                    pl.BlockSpec((TOK_TILE, LANES), lambda b, i: (i, 0))],
        out_specs=[pl.BlockSpec((None, HA, MLA_QK, TOK_TILE), lambda b, i: (b, 0, 0, i)),
                   pl.BlockSpec((None, HA, TOK_TILE, MLA_QK), lambda b, i: (b, 0, i, 0)),
                   pl.BlockSpec((None, HA, dva, TOK_TILE), lambda b, i: (b, 0, 0, i)),
                   pl.BlockSpec((None, 2 * HD, DIFF_D, TOK_TILE), lambda b, i: (b, 0, 0, i)),
                   pl.BlockSpec((None, 2 * HD, TOK_TILE, DIFF_D), lambda b, i: (b, 0, i, 0)),
                   pl.BlockSpec((None, HD, dva, TOK_TILE), lambda b, i: (b, 0, 0, i))],
        out_shape=[jax.ShapeDtypeStruct((B, HA, MLA_QK, S), BF16),
                   jax.ShapeDtypeStruct((B, HA, S, MLA_QK), BF16),
                   jax.ShapeDtypeStruct((B, HA, dva, S), BF16),
                   jax.ShapeDtypeStruct((B, 2 * HD, DIFF_D, S), BF16),
                   jax.ShapeDtypeStruct((B, 2 * HD, S, DIFF_D), BF16),
                   jax.ShapeDtypeStruct((B, HD, dva, S), BF16)],
        compiler_params=_cparams(2),
        name="even_qkv_prep",
    )(xa, p["norm_mix"].reshape(1, D).astype(F32), mods["sh_a"], mods["sc_a"], *weights, cosT, sinT, cosK, sinK)


def _odd_prep_kernel(x_ref, g_ref, sh_ref, sc_ref, wqvT_ref, wk_ref, wkr_ref, gqn_ref, gk_ref, gkp_ref, bd_ref,
                     cosT_ref, sinT_ref, cosK_ref, sinK_ref,
                     qc_ref, qd_ref, k_ref, vc_ref, vd_ref):
    h = _norm_mod(x_ref[0], g_ref[...], sh_ref[0, 0], sc_ref[0, 0]).astype(BF16)
    n = h.shape[0]
    cosT, sinT, cosK, sinK = cosT_ref[...], sinT_ref[...], cosK_ref[...], sinK_ref[...]
    ones = _ones_rows(n)
    zT = _dot_t(wqvT_ref[...], h)
    d = HEAD_DIM
    for hd in range(WIN_HEADS):
        qc_ref[hd] = (_rope_rows(zT[hd * d:(hd + 1) * d], cosT, sinT) * (ATTN_SCALE * LOG2E)).astype(BF16)
    o1 = WIN_HEADS * d
    for hd in range(GLB_HEADS):
        q = zT[o1 + hd * d:o1 + (hd + 1) * d]
        q = _rope_rows(q * _rms_rows(q) * gqn_ref[...], cosT, sinT)
        qd_ref[hd] = (q * (ATTN_SCALE * LOG2E)).astype(BF16)
    o2 = o1 + GLB_HEADS * d
    for j in range(WIN_KV_HEADS):
        vc_ref[j, 0:d, :] = zT[o2 + j * d:o2 + (j + 1) * d].astype(BF16)
        vc_ref[j, d:, :] = ones
    o3 = o2 + WIN_KV_HEADS * d
    for j in range(GLB_KV_HEADS):
        vd_ref[j, 0:d, :] = zT[o3 + j * d:o3 + (j + 1) * d].astype(BF16)
        vd_ref[j, d:, :] = ones

    zk = jnp.dot(h, wk_ref[...], preferred_element_type=F32)
    zkr = jnp.dot(h, wkr_ref[...], preferred_element_type=F32)
    wc = WIN_KV_HEADS * d
    kc = zk[:, :wc] * cosK + zkr[:, :wc] * sinK
    z, zr = zk[:, wc:], zkr[:, wc:]
    sq = z * z
    hi = sq.astype(BF16)
    lo = (sq - hi.astype(F32)).astype(BF16)
    mean = (jnp.dot(hi, bd_ref[...], preferred_element_type=F32) + jnp.dot(lo, bd_ref[...], preferred_element_type=F32))
    kd = lax.rsqrt(mean + NORM_EPS) * (z * gk_ref[...] * cosK + zr * gkp_ref[...] * sinK)
    k = jnp.concatenate([kc, kd], axis=1).astype(BF16)
    for j in range(k_ref.shape[0]):
        k_ref[j] = k[:, j * d:(j + 1) * d]


def _odd_prep(xa, p, mods, tabs):
    B, S, D = xa.shape
    w_in = p["w_in"]
    d = HEAD_DIM
    sizes = (WIN_HEADS, WIN_KV_HEADS, WIN_KV_HEADS, GLB_HEADS, GLB_KV_HEADS, GLB_KV_HEADS)
    offs = [0]
    for s in sizes:
        offs.append(offs[-1] + s * d)
    col = lambda j: w_in[:, offs[j]:offs[j + 1]]
    bf = lambda a: a.astype(BF16)
    wqvT = bf(jnp.concatenate([col(0), col(3), col(2), col(5)], axis=1).T)
    wk = jnp.concatenate([col(1), col(4)], axis=1)
    wkr = _rot_cols(wk, d)
    gqn = jnp.broadcast_to(p["q_norm"].astype(F32)[:, None], (d, TOK_TILE))
    gk1 = p["k_norm"].astype(F32)
    gk = jnp.tile(gk1, GLB_KV_HEADS).reshape(1, -1)
    gkp = jnp.tile(_perm_cols(gk1, d), GLB_KV_HEADS).reshape(1, -1)
    wd = GLB_KV_HEADS * d
    lane = jnp.arange(wd)
    bd = bf(jnp.where(lane[:, None] // d == lane[None, :] // d, 1.0 / d, 0.0))
    assert WIN_KV_HEADS * d == LANES and wd == LANES
    cosT, sinT, cosK, sinK = tabs
    weights = [wqvT, bf(wk), bf(wkr), gqn, gk, gkp, bd]
    nt = S // TOK_TILE
    ctx_tile = nt - 1
    mod_map = lambda b, i: (b, (i == ctx_tile).astype(jnp.int32), 0, 0)
    dva = d + ONES_ROWS
    return pl.pallas_call(
        _odd_prep_kernel,
        grid=(B, nt),
        in_specs=[pl.BlockSpec((1, TOK_TILE, D), lambda b, i: (b, i, 0)),
                  _full((1, D)),
                  pl.BlockSpec((1, 1, 1, D), mod_map),
                  pl.BlockSpec((1, 1, 1, D), mod_map)]
                 + [_full(w.shape) for w in weights]
                 + [pl.BlockSpec((d, TOK_TILE), lambda b, i: (0, i)),
                    pl.BlockSpec((d, TOK_TILE), lambda b, i: (0, i)),
                    pl.BlockSpec((TOK_TILE, LANES), lambda b, i: (i, 0)),
                    pl.BlockSpec((TOK_TILE, LANES), lambda b, i: (i, 0))],
        out_specs=[pl.BlockSpec((None, WIN_HEADS, d, TOK_TILE), lambda b, i: (b, 0, 0, i)),
                   pl.BlockSpec((None, GLB_HEADS, d, TOK_TILE), lambda b, i: (b, 0, 0, i)),
                   pl.BlockSpec((None, WIN_KV_HEADS + GLB_KV_HEADS, TOK_TILE, d), lambda b, i: (b, 0, i, 0)),
                   pl.BlockSpec((None, WIN_KV_HEADS, dva, TOK_TILE), lambda b, i: (b, 0, 0, i)),
                   pl.BlockSpec((None, GLB_KV_HEADS, dva, TOK_TILE), lambda b, i: (b, 0, 0, i))],
        out_shape=[jax.ShapeDtypeStruct((B, WIN_HEADS, d, S), BF16),
                   jax.ShapeDtypeStruct((B, GLB_HEADS, d, S), BF16),
                   jax.ShapeDtypeStruct((B, WIN_KV_HEADS + GLB_KV_HEADS, S, d), BF16),
                   jax.ShapeDtypeStruct((B, WIN_KV_HEADS, dva, S), BF16),
                   jax.ShapeDtypeStruct((B, GLB_KV_HEADS, dva, S), BF16)],
        compiler_params=_cparams(2),
        name="odd_qkv_prep",
    )(xa, p["norm_mix"].reshape(1, D).astype(F32), mods["sh_a"], mods["sc_a"], *weights, cosT, sinT, cosK, sinK)


def _aligned(start):
    return start if isinstance(start, int) else pl.multiple_of(start, PV_TILE)


def _attn_step(k_ref, vT_ref, acc_ref, q_next, next_offs, s_next, cur_offs, s_cur, m, cmax):
    m_new = jnp.maximum(m, cmax) if cur_offs else m
    pv = None
    cnext = None
    for i in range(max(len(next_offs), len(cur_offs))):
        rows = slice(i * PV_TILE, (i + 1) * PV_TILE)
        if i < len(next_offs):
            s = jnp.dot(k_ref[pl.ds(_aligned(next_offs[i]), PV_TILE), :], q_next, preferred_element_type=F32)
            s_next[rows, :] = s
            cm = jnp.max(s, axis=0, keepdims=True)
            cnext = cm if cnext is None else jnp.maximum(cnext, cm)
        if i < len(cur_offs):
            p = jnp.exp2(s_cur[rows, :] - m_new).astype(BF16)
            d = jnp.dot(vT_ref[:, pl.ds(_aligned(cur_offs[i]), PV_TILE)], p, preferred_element_type=F32)
            pv = d if pv is None else pv + d
    if cur_offs:
        acc_ref[...] = acc_ref[...] * jnp.exp2(m - m_new) + pv
    return m_new, cnext


def _softmax_init(init_ref, dva, dv, tq):
    m0 = jnp.broadcast_to(init_ref[0, 0:1, 0:1], (1, tq))
    row = lax.broadcasted_iota(jnp.int32, (dva, tq), 0)
    acc0 = jnp.where(row == dv, jnp.broadcast_to(init_ref[0, 1:2, 0:1], (dva, tq)), 0.0)
    return m0, acc0


def _flash_kernel(init_ref, qT_ref, qTn_ref, k_ref, vT_ref, o_ref, s0_ref, s1_ref, acc_ref, cm_ref, *,
                  T, n_ctx, kt, dv):
    qT = qT_ref[...]
    tq = qT.shape[1]
    dva = vT_ref.shape[0]
    pieces = lambda off: [off + kk for kk in range(0, kt, PV_TILE)]
    first = [T + kk for kk in range(0, n_ctx, PV_TILE)] + pieces(0)
    step = functools.partial(_attn_step, k_ref, vT_ref, acc_ref)

    @pl.when(pl.program_id(2) == 0)
    def _():
        _, cm_ref[0:1, :] = step(qT, first, s0_ref, [], None, None, None)

    m, acc0 = _softmax_init(init_ref, dva, dv, tq)
    acc_ref[...] = acc0
    m, cm_b = step(qT, pieces(kt), s1_ref, first, s0_ref, m, cm_ref[0:1, :])

    def body(t, carry):
        m, cm_b = carry
        off = pl.multiple_of(t * (2 * kt), kt)
        m, cm_a = step(qT, pieces(off + 2 * kt), s0_ref, pieces(off + kt), s1_ref, m, cm_b)
        m, cm_b = step(qT, pieces(off + 3 * kt), s1_ref, pieces(off + 2 * kt), s0_ref, m, cm_a)
        return m, cm_b

    m, cm_b = lax.fori_loop(0, T // (2 * kt) - 1, body, (m, cm_b))
    m, cm_ref[0:1, :] = step(qTn_ref[...], first, s0_ref, pieces(T - kt), s1_ref, m, cm_b)
    acc = acc_ref[...]
    o_ref[...] = (acc[:dv] / acc[dv:dv + 1]).astype(o_ref.dtype)


def _flash(qT, k, vT, init, T, key_of, out_dtype=BF16):
    B, Hq, dk, S = qT.shape
    Hv, dva = vT.shape[1], vT.shape[2]
    dv = dva - ONES_ROWS
    gv = Hq // Hv
    kt = min(KEY_TILE, T // 2)
    tq = min(Q_TILE, T)
    assert T % (2 * kt) == 0 and T % tq == 0 and kt % PV_TILE == 0
    n_ctx = S - T
    assert n_ctx % PV_TILE == 0
    kern = functools.partial(_flash_kernel, T=T, n_ctx=n_ctx, kt=kt, dv=dv)
    nq = T // tq
    return pl.pallas_call(
        kern,
        grid=(B, Hq, nq),
        in_specs=[pl.BlockSpec((1, 2, LANES),lambda b, h, i: (h, 0, 0)),
                  pl.BlockSpec((None, None, dk, tq), lambda b, h, i: (b, h, 0, i)),
                  pl.BlockSpec((None, None, dk, tq), lambda b, h, i: (b, h, 0, jnp.minimum(i + 1, nq - 1))),
                  pl.BlockSpec((None, None, S, dk), lambda b, h, i: (b, key_of(h), 0, 0)),
                  pl.BlockSpec((None, None, dva, S), lambda b, h, i: (b, h // gv, 0, 0))],
        out_specs=pl.BlockSpec((None, None, dv, tq), lambda b, h, i: (b, h, 0, i)),
        out_shape=jax.ShapeDtypeStruct((B, Hq, dv, T), out_dtype),
        scratch_shapes=[pltpu.VMEM((kt + n_ctx, tq), F32), pltpu.VMEM((kt, tq), F32),
                        pltpu.VMEM((dva, tq), F32), pltpu.VMEM((8, tq), F32)],
        compiler_params=_cparams(3),
        name="dense_attention",
    )(init, qT, qT, k, vT)


def _ctx_attn_kernel(init_ref, qT_ref, k_ref, vT_ref, o_ref, *, dv):
    qT = qT_ref[...]
    m0, acc0 = _softmax_init(init_ref, vT_ref.shape[0], dv, qT.shape[1])
    s = jnp.dot(k_ref[...], qT, preferred_element_type=F32)
    m = jnp.maximum(m0, jnp.max(s, axis=0, keepdims=True))
    acc = acc0 * jnp.exp2(m0 - m) + jnp.dot(vT_ref[...], jnp.exp2(s - m).astype(BF16), preferred_element_type=F32)
    o_ref[...] = (acc[:dv] / acc[dv:dv + 1]).astype(o_ref.dtype)


def _ctx_attn(qT, k, vT, init, T, key_of, out_dtype=BF16):
    B, Hq, dk, S = qT.shape
    Hv, dva = vT.shape[1], vT.shape[2]
    dv = dva - ONES_ROWS
    gv = Hq // Hv
    n = S - T
    blk = T // n
    return pl.pallas_call(
        functools.partial(_ctx_attn_kernel, dv=dv),
        grid=(B, Hq),
        in_specs=[pl.BlockSpec((1, 2, LANES),lambda b, h: (h, 0, 0)),
                  pl.BlockSpec((None, None, dk, n), lambda b, h: (b, h, 0, blk)),
                  pl.BlockSpec((None, None, n, dk), lambda b, h: (b, key_of(h), blk, 0)),
                  pl.BlockSpec((None, None, dva, n), lambda b, h: (b, h // gv, 0, blk))],
        out_specs=pl.BlockSpec((None, None, dv, n), lambda b, h: (b, h, 0, 0)),
        out_shape=jax.ShapeDtypeStruct((B, Hq, dv, n), out_dtype),
        compiler_params=_cparams(2),
        name="context_attention",
    )(init, qT, k, vT)


def _window_kernel(sink_ref, bias_ref, qT_ref, k_ref, kc_ref, vT_ref, o_ref, *, dv):
    i = pl.program_id(2)
    G, _, tq = qT_ref.shape
    qT = jnp.concatenate([qT_ref[g] for g in range(G)], axis=1)
    snk = jnp.concatenate([jnp.broadcast_to(sink_ref[g, 0:1, 0:1], (1, tq)) for g in range(G)], axis=1)
    W = WINDOW
    kband = jnp.concatenate([k_ref[0][tq - W:tq, :], k_ref[1][...], k_ref[2][0:W, :]], axis=0)
    before_first = jnp.where(i == 0, NEG_INF, 0.0)
    after_last = jnp.where(i == pl.num_programs(2) - 1, NEG_INF, 0.0)
    bias = jnp.concatenate([bias_ref[0:W, :] + before_first, bias_ref[W:W + tq, :],
                            bias_ref[W + tq:, :] + after_last], axis=0)
    s_loc = jnp.dot(kband, qT, preferred_element_type=F32) + bias
    s_ctx = jnp.dot(kc_ref[...], qT, preferred_element_type=F32)
    m = jnp.maximum(jnp.maximum(jnp.max(s_loc, axis=0, keepdims=True), jnp.max(s_ctx, axis=0, keepdims=True)), snk)
    p_loc = jnp.exp2(s_loc - m).astype(BF16)
    p_ctx = jnp.exp2(s_ctx - m).astype(BF16)
    vband = jnp.concatenate([vT_ref[0][:, tq - W:tq], vT_ref[1][...], vT_ref[2][:, 0:W]], axis=1)
    acc = (jnp.dot(vband, p_loc, preferred_element_type=F32)
           + jnp.dot(vT_ref[3][...], p_ctx, preferred_element_type=F32))
    o = acc[:dv] / (acc[dv:dv + 1] + jnp.exp2(snk - m))
    for g in range(G):
        o_ref[g] = o[:, g * tq:(g + 1) * tq].astype(o_ref.dtype)


def _window(qT, k, vT, sink, T, key_of):
    B, Hq, dk, S = qT.shape
    Hk, dva = vT.shape[1], vT.shape[2]
    dv = dva - ONES_ROWS
    g = Hq // Hk
    tq = TOK_TILE
    nt = T // tq
    ctx_blk = T // tq
    clip = lambda j: jnp.clip(j, 0, nt - 1)
    kern = functools.partial(_window_kernel, dv=dv)
    kspec = lambda f: pl.BlockSpec((None, None, tq, dk), lambda b, h, i: (b, key_of(h * g), f(i), 0))
    vspec = lambda f: pl.BlockSpec((None, None, dva, tq), lambda b, h, i: (b, h, 0, f(i)))
    r = jnp.arange(tq + 2 * WINDOW)[:, None]
    c = jnp.arange(g * tq)[None, :] % tq
    bias = jnp.where(jnp.abs(r - WINDOW - c) <= WINDOW, 0.0, NEG_INF).astype(F32)

    def body(sink_ref, bias_ref, qT_ref, k0, k1, k2, kc, v0, v1, v2, vc, o_ref):
        kern(sink_ref, bias_ref, qT_ref, (k0, k1, k2), kc, (v0, v1, v2, vc), o_ref)

    return pl.pallas_call(
        body,
        grid=(B, Hk, nt),
        in_specs=[pl.BlockSpec((g, 1, LANES), lambda b, h, i: (h, 0, 0)),
                  _full(bias.shape),
                  pl.BlockSpec((None, g, dk, tq), lambda b, h, i: (b, h, 0, i)),
                  kspec(lambda i: clip(i - 1)), kspec(lambda i: i), kspec(lambda i: clip(i + 1)),
                  kspec(lambda i: ctx_blk),
                  vspec(lambda i: clip(i - 1)), vspec(lambda i: i), vspec(lambda i: clip(i + 1)),
                  vspec(lambda i: ctx_blk)],
        out_specs=pl.BlockSpec((None, g, dv, tq), lambda b, h, i: (b, h, 0, i)),
        out_shape=jax.ShapeDtypeStruct((B, Hq, dv, T), BF16),
        compiler_params=_cparams(3),
        name="window_attention",
    )(sink, bias, qT, k, k, k, k, vT, vT, vT, vT)


def _project_out(x_ref, oT, w_ref, gate_ref, y_ref):
    y = jnp.dot(oT.T.astype(BF16), w_ref[...], preferred_element_type=F32)
    y_ref[0] = x_ref[0] + gate_ref[0, 0] * y


def _even_out_kernel(x_ref, oa_ref, od_ref, w_ref, gate_ref, lam_ref, sub_ref, y_ref, *, post_scale):
    lam = lam_ref[...]
    parts = [oa_ref[hd].astype(F32) for hd in range(MLA_HEADS)]
    for hd in range(DIFF_HEADS):
        diff = od_ref[2 * hd] - lam * od_ref[2 * hd + 1]
        parts.append(diff * _rms_rows(diff) * sub_ref[...] * post_scale)
    _project_out(x_ref, jnp.concatenate(parts, axis=0), w_ref, gate_ref, y_ref)


def _odd_out_kernel(x_ref, oc_ref, od_ref, w_ref, gate_ref, y_ref):
    parts = ([oc_ref[hd].astype(F32) for hd in range(WIN_HEADS)]
             + [od_ref[hd].astype(F32) for hd in range(GLB_HEADS)])
    _project_out(x_ref, jnp.concatenate(parts, axis=0), w_ref, gate_ref, y_ref)


def _out_proj(kern, x, heads, w, gate, extra, tile0, n_tiles, kind):
    B, S, D = x.shape
    row_map = lambda b, i: (b, i + tile0, 0)
    return pl.pallas_call(
        kern,
        grid=(B, n_tiles),
        in_specs=[pl.BlockSpec((1, TOK_TILE, D), row_map)]
                 + [pl.BlockSpec((None,) + o.shape[1:3] + (TOK_TILE,), lambda b, i: (b, 0, 0, i)) for o in heads]
                 + [_full(w.shape), pl.BlockSpec((1, 1, 1, D), lambda b, i: (b, kind, 0, 0))]
                 + [_full(e.shape) for e in extra],
        out_specs=pl.BlockSpec((1, TOK_TILE, D), row_map),
        out_shape=jax.ShapeDtypeStruct((B, S, D), F32),
        input_output_aliases={0: 0},
        compiler_params=_cparams(2),
        name="merge_out_proj_residual",
    )(x, *heads, w, gate, *extra)


def _first_argmax(vals):
    best, idx = vals[0], jnp.zeros(vals[0].shape, jnp.int32)
    for j in range(1, len(vals)):
        better = vals[j] > best
        idx = jnp.where(better, j, idx)
        best = jnp.where(better, vals[j], best)
    return idx, best


def _pick(idx, vals):
    out = vals[0]
    for j in range(1, len(vals)):
        out = jnp.where(idx == j, vals[j], out)
    return out


def _route(logits, bias):
    s = 1.0 / (1.0 + jnp.exp(-logits))
    sel = s + bias
    srow = [s[e:e + 1] for e in range(N_EXPERTS)]
    row = [sel[e:e + 1] for e in range(N_EXPERTS)]
    scores = []
    for g in range(N_GROUPS):
        a, b, c, d = row[4 * g:4 * g + 4]
        hi1, lo1, hi2, lo2 = jnp.maximum(a, b), jnp.minimum(a, b), jnp.maximum(c, d), jnp.minimum(c, d)
        top1 = jnp.maximum(hi1, hi2)
        top2 = jnp.maximum(jnp.maximum(lo1, lo2), jnp.minimum(hi1, hi2))
        scores.append(top1 + top2)
    gi, _ = _first_argmax(scores)
    v = [_pick(gi, [row[4 * g + j] for g in range(N_GROUPS)]) for j in range(EXPERTS_PER_GROUP)]
    sv = [_pick(gi, [srow[4 * g + j] for g in range(N_GROUPS)]) for j in range(EXPERTS_PER_GROUP)]
    i1, _ = _first_argmax(v)
    i2, _ = _first_argmax([jnp.where(i1 == j, -jnp.inf, v[j]) for j in range(EXPERTS_PER_GROUP)])
    w1, w2 = _pick(i1, sv), _pick(i2, sv)
    tot = w1 + w2
    w1, w2 = w1 / tot, w2 / tot
    rows = []
    for e in range(N_EXPERTS):
        g, j = divmod(e, EXPERTS_PER_GROUP)
        in_g = gi == g
        rows.append(jnp.where(in_g & (i1 == j), w1, 0.0) + jnp.where(in_g & (i2 == j), w2, 0.0))
    return jnp.concatenate(rows, axis=0), gi


def _split(x, terms):
    out = []
    for _ in range(terms):
        part = x.astype(BF16).astype(F32)
        out.append(part)
        x = x - part
    return out


def _to_column(row):
    n = row.shape[1]
    return jnp.concatenate([row, jnp.zeros((LANES - 1, n), F32)], axis=0).T[:, 0:1]


def _moe_kernel(x_ref, g_ref, sh_ref, sc_ref, gate_ref, wrT_ref, br_ref, before_ref, wg_ref, wu_ref, wd_ref, fin_ref,
                y_ref, hs_scr, cs_scr, ys_scr, q_scr, info_ref, *, final):
    step = pl.program_id(2)
    per_step = wg_ref.shape[0]
    tm = x_ref.shape[1]
    R = hs_scr.shape[0]

    @pl.when(step == 0)
    def _():
        h = _norm_mod(x_ref[0], g_ref[...], sh_ref[0, 0], sc_ref[0, 0]).astype(BF16)
        comb, gi = _route(_dot_t(wrT_ref[...], h), br_ref[...])
        sel = jnp.concatenate([(gi == g).astype(F32) for g in range(N_GROUPS)]
                              + [jnp.zeros((BF16_ROWS - N_GROUPS, tm), F32)], axis=0)
        rank = jnp.dot(sel.astype(BF16), before_ref[...], preferred_element_type=F32)
        cnt = jnp.sum(sel, axis=1, keepdims=True)
        seg = jnp.ceil(cnt * (1.0 / BF16_ROWS)) * BF16_ROWS
        pos = jnp.zeros((1, tm), F32)
        start = jnp.zeros((1, 1), F32)
        for g in range(N_GROUPS):
            pos = pos + sel[g:g + 1] * (start + rank[g:g + 1])
            info_ref[g] = start[0, 0].astype(jnp.int32)
            info_ref[N_GROUPS + g] = jnp.ceil(cnt[g:g + 1] * (1.0 / MOE_CHUNK))[0, 0].astype(jnp.int32)
            start = start + seg[g:g + 1]
        slot = lax.broadcasted_iota(jnp.int32, (R, tm), 0).astype(F32)
        P = jnp.where(slot == pos, 1.0, 0.0).astype(BF16)
        hs_scr[...] = jnp.dot(P, h, preferred_element_type=F32).astype(BF16)
        parts = _split(comb, COMB_TERMS)
        combT = jnp.concatenate(parts + [jnp.zeros((LANES - COMB_TERMS * N_EXPERTS, tm), F32)], axis=0).T
        cs_scr[...] = jnp.dot(P, combT.astype(BF16), preferred_element_type=F32)
        lane = lax.broadcasted_iota(jnp.int32, (tm, R), 1).astype(F32)
        q_scr[...] = jnp.where(lane == _to_column(pos), 1.0, 0.0).astype(BF16)
        ys_scr[...] = jnp.zeros_like(ys_scr)

    steps_per_group = EXPERTS_PER_GROUP // per_step
    grp = lax.shift_right_logical(step, steps_per_group.bit_length() - 1)
    start = info_ref[grp]
    n_chunks = info_ref[N_GROUPS + grp]

    def experts(r0, rows):
        h = hs_scr[pl.ds(r0, rows), :]
        cs = cs_scr[pl.ds(r0, rows), :]
        lane = lax.broadcasted_iota(jnp.int32, (rows, LANES), 1) & (N_EXPERTS - 1)
        total = None
        for j in range(per_step):
            a = jnp.dot(h, wg_ref[j], preferred_element_type=F32)
            u = jnp.dot(h, wu_ref[j], preferred_element_type=F32)
            act = (a * (1.0 / (1.0 + jnp.exp(-a)))) * u
            y = jnp.dot(act.astype(BF16), wd_ref[j], preferred_element_type=F32)
            w = jnp.sum(jnp.where(lane == step * per_step + j, cs, 0.0), axis=1, keepdims=True)
            total = w * y if total is None else total + w * y
        ys_scr[pl.ds(r0, rows), :] += total

    def pair(c, carry):
        experts(pl.multiple_of(start + c * (2 * MOE_CHUNK), BF16_ROWS), 2 * MOE_CHUNK)
        return carry

    lax.fori_loop(0, lax.shift_right_logical(n_chunks, 1), pair, 0)

    @pl.when((n_chunks & 1) == 1)
    def _():
        experts(pl.multiple_of(start + (n_chunks - 1) * MOE_CHUNK, BF16_ROWS), MOE_CHUNK)

    @pl.when(step == pl.num_programs(2) - 1)
    def _():
        q = q_scr[...]
        out = sum(jnp.dot(q, part.astype(BF16), preferred_element_type=F32) for part in _split(ys_scr[...], 2))
        y = x_ref[0] + gate_ref[0, 0] * out
        if final:
            y = (y * lax.rsqrt(jnp.mean(y * y, axis=-1, keepdims=True) + NORM_EPS)) * fin_ref[...]
        y_ref[0] = y


def _moe(x, g, sh, sc, gate, wrT, br, wg, wu, wd, fin, tm, tile0, n_tiles, kind, final):
    B, S, D = x.shape
    E, _, F = wg.shape
    R = -(-(tm + BF16_ROWS * N_GROUPS + MOE_CHUNK) // MXU_DIM) * MXU_DIM
    idx = jnp.arange(tm)
    before = (idx[:, None] < idx[None, :]).astype(BF16)
    mod_map = lambda b, i, e: (b, kind, 0, 0)
    row_map = lambda b, i, e: (b, i + tile0, 0)
    assert not (final and tile0)
    return pl.pallas_call(
        functools.partial(_moe_kernel, final=final),
        grid=(B, n_tiles, E // MOE_STEP_EXPERTS),
        in_specs=[pl.BlockSpec((1, tm, D), row_map),
                  pl.BlockSpec((1, D), lambda b, i, e: (0, 0)),
                  pl.BlockSpec((1, 1, 1, D), mod_map),
                  pl.BlockSpec((1, 1, 1, D), mod_map),
                  pl.BlockSpec((1, 1, 1, D), mod_map),
                  pl.BlockSpec((E, D), lambda b, i, e: (0, 0)),
                  pl.BlockSpec((E, 1), lambda b, i, e: (0, 0)),
                  pl.BlockSpec((tm, tm), lambda b, i, e: (0, 0)),
                  pl.BlockSpec((MOE_STEP_EXPERTS, D, F), lambda b, i, e: (e, 0, 0)),
                  pl.BlockSpec((MOE_STEP_EXPERTS, D, F), lambda b, i, e: (e, 0, 0)),
                  pl.BlockSpec((MOE_STEP_EXPERTS, F, D), lambda b, i, e: (e, 0, 0)),
                  pl.BlockSpec((1, D), lambda b, i, e: (0, 0))],
        out_specs=pl.BlockSpec((1, tm, D), row_map),
        out_shape=jax.ShapeDtypeStruct((B, n_tiles * tm if final else S, D), F32),
        scratch_shapes=[pltpu.VMEM((R, D), BF16), pltpu.VMEM((R, LANES), F32), pltpu.VMEM((R, D), F32),
                        pltpu.VMEM((tm, R), BF16), pltpu.SMEM((2 * N_GROUPS,), jnp.int32)],
        input_output_aliases={} if final else {0: 0},
        compiler_params=_cparams(3),
        name="moe_experts",
    )(x, g.reshape(1, D).astype(F32), sh, sc, gate, wrT, br, before, wg, wu, wd, fin.reshape(1, D).astype(F32))


def _axial_tables(T, n_ctx, rot_dim):
    half = rot_dim // 2
    inv_freq = ROPE_THETA ** (-jnp.arange(0, half, 2, dtype=F32) / half)
    rows = T // GRID_W
    row = jnp.broadcast_to(jnp.arange(rows, dtype=F32)[:, None], (rows, GRID_W)).reshape(-1)
    col = jnp.broadcast_to(jnp.arange(GRID_W, dtype=F32)[None, :], (rows, GRID_W)).reshape(-1)

    def ang(pos):
        a = pos[:, None] * inv_freq[None, :]
        return jnp.concatenate([a, a], axis=-1)

    a = jnp.concatenate([ang(row), ang(col)], axis=-1)
    cos = jnp.concatenate([jnp.cos(a), jnp.ones((n_ctx, rot_dim), F32)], axis=0)
    sin = jnp.concatenate([jnp.sin(a), jnp.zeros((n_ctx, rot_dim), F32)], axis=0)
    reps = LANES // rot_dim
    return cos.T, sin.T, jnp.tile(cos, (1, reps)), jnp.tile(sin, (1, reps))


def _plain_init(n_heads):
    return jnp.broadcast_to(jnp.array([NEG_INF, 0.0], F32)[None, :, None], (n_heads, 2, LANES))


def _sink_init(sink):
    s = sink.astype(F32) * LOG2E
    return jnp.broadcast_to(jnp.stack([s, jnp.ones_like(s)], axis=1)[:, :, None], (s.shape[0], 2, LANES))


def _even_mixer(xa, p, mods, T, with_ctx, lam_init, tabs):
    B, S, D = xa.shape
    qa, ka, va, qd, kd, vd = _even_prep(xa, p, mods, tabs)
    own = lambda h: h
    init_a, init_d = _plain_init(MLA_HEADS), _plain_init(2 * DIFF_HEADS)
    lam = p["lam"].astype(F32)
    lam_full = (jnp.exp(jnp.sum(lam[0] * lam[1])) - jnp.exp(jnp.sum(lam[2] * lam[3])) + lam_init).reshape(1, 1)
    sub = jnp.broadcast_to(p["subln"].astype(F32)[:, None], (DIFF_V, TOK_TILE))
    kern = functools.partial(_even_out_kernel, post_scale=1.0 - lam_init)
    oa = _flash(qa, ka, va, init_a, T, own)
    od = _flash(qd, kd, vd, init_d, T, own, F32)
    xa = _out_proj(kern, xa, [oa, od], p["w_out"], mods["g_a"], [lam_full, sub], 0, T // TOK_TILE, 0)
    if with_ctx:
        oa = _ctx_attn(qa, ka, va, init_a, T, own)
        od = _ctx_attn(qd, kd, vd, init_d, T, own, F32)
        xa = _out_proj(kern, xa, [oa, od], p["w_out"], mods["g_a"], [lam_full, sub], T // TOK_TILE, 1, 1)
    return xa


def _odd_mixer(xa, p, mods, T, with_ctx, tabs):
    B, S, D = xa.shape
    qc, qd, k, vc, vd = _odd_prep(xa, p, mods, tabs)
    gw, gd = WIN_HEADS // WIN_KV_HEADS, GLB_HEADS // GLB_KV_HEADS
    key_c = lambda h: h // gw
    key_d = lambda h: WIN_KV_HEADS + h // gd
    init_d = _plain_init(GLB_HEADS)
    sink = p["sink"].astype(F32) * LOG2E
    sink_w = jnp.broadcast_to(sink[:, None, None], (WIN_HEADS, 1, LANES))
    od = _flash(qd, k, vd, init_d, T, key_d)
    oc = _window(qc, k, vc, sink_w, T, key_c)
    xa = _out_proj(_odd_out_kernel, xa, [oc, od], p["w_out"], mods["g_a"], [], 0, T // TOK_TILE, 0)
    if with_ctx:
        oc = _ctx_attn(qc, k, vc, _sink_init(p["sink"]), T, key_c)
        od = _ctx_attn(qd, k, vd, init_d, T, key_d)
        xa = _out_proj(_odd_out_kernel, xa, [oc, od], p["w_out"], mods["g_a"], [], T // TOK_TILE, 1, 1)
    return xa


def kernel(x, c, ctx, c_ctx, w_mod, b_mod, norm_mix, norm_ffn, even_w_in, even_norm_q, even_norm_kv, even_w_uq, even_w_ukv, even_lambda, even_subln, even_w_out, odd_w_in, odd_sink, odd_q_norm, odd_k_norm, odd_w_out, w_router, b_router, w_gate, w_up, w_down, norm_final):
    B, T, D = x.shape
    n_ctx = ctx.shape[1]
    depth = w_mod.shape[0]
    assert n_ctx == TOK_TILE == PV_TILE and T % MOE_TILE == 0 and B <= 7
    assert MLA_ROPE == DIFF_D
    S = T + n_ctx
    tabs32 = _axial_tables(T, n_ctx, MLA_ROPE)
    tabs64 = _axial_tables(T, n_ctx, HEAD_DIM)

    cond = jnp.zeros((8, D), F32).at[:B].set(c.astype(F32)).at[B].set(c_ctx.astype(F32))
    mod_all = _mod_vectors(cond, w_mod, b_mod)
    wrT = w_router.astype(BF16).T
    br = b_router.astype(F32).reshape(N_EXPERTS, 1)

    xa = jnp.concatenate([x, ctx], axis=1).astype(F32)
    for l in range(depth):
        with_ctx = l < depth - 1
        i = l // 2
        lat = mod_all[l, :B].reshape(B, 6, D)
        cx = jnp.broadcast_to(mod_all[l, B].reshape(1, 6, D), (B, 6, D))
        both = jnp.stack([lat, cx], axis=1)
        names = ("sh_a", "sc_a", "g_a", "sh_f", "sc_f", "g_f")
        mods = {n: both[:, :, j:j + 1, :] for j, n in enumerate(names)}
        if l % 2 == 0:
            lam_init = 0.8 - 0.6 * math.exp(-0.3 * l)
            p = dict(norm_mix=norm_mix[l], w_in=even_w_in[i], norm_q=even_norm_q[i], norm_kv=even_norm_kv[i],
                     w_uq=even_w_uq[i], w_ukv=even_w_ukv[i], lam=even_lambda[i], subln=even_subln[i],
                     w_out=even_w_out[i].astype(BF16))
            xa = _even_mixer(xa, p, mods, T, with_ctx, lam_init, tabs32)
        else:
            p = dict(norm_mix=norm_mix[l], w_in=odd_w_in[i], sink=odd_sink[i], q_norm=odd_q_norm[i],
                     k_norm=odd_k_norm[i], w_out=odd_w_out[i].astype(BF16))
            xa = _odd_mixer(xa, p, mods, T, with_ctx, tabs64)
        wg, wu, wd = w_gate[l].astype(BF16), w_up[l].astype(BF16), w_down[l].astype(BF16)
        ffn = (norm_ffn[l], mods["sh_f"], mods["sc_f"], mods["g_f"], wrT, br, wg, wu, wd, norm_final)
        xa = _moe(xa, *ffn, MOE_TILE, 0, T // MOE_TILE, 0, final=not with_ctx)
        if with_ctx:
            xa = _moe(xa, *ffn, TOK_TILE, T // TOK_TILE, 1, 1, final=False)
    return xa
```
